```python
import jax, jax.numpy as jnp
from jax import lax
import numpy as np

D_MODEL = 1024
BATCH = 4
SEQ = 4096
DEPTH = 1

CTX_LEN = 256
GRID_W = 64
D_INNER = 2 * D_MODEL
W_POOL = D_INNER // 2
W_SSD = D_INNER - W_POOL
POOL_WINDOWS = (2, 4, 8, 16)
N_POOL_GROUPS = len(POOL_WINDOWS)
POOL_GROUP_W = W_POOL // N_POOL_GROUPS
SSD_HEADDIM = 64
SSD_HEADS = W_SSD // SSD_HEADDIM
SSD_GROUPS = 4
SSD_HEADS_PER_GROUP = SSD_HEADS // SSD_GROUPS
D_STATE = 128
D_CONV = 4
CONV_LEFT = D_CONV // 2
CHUNK = 128
N_DIR = 2
GN = SSD_GROUPS * D_STATE
CONV_DIM = W_SSD + 2 * GN
OFF_POOL_Z = W_POOL
OFF_SSD_Z = 2 * W_POOL
OFF_XBC = 2 * W_POOL + W_SSD
OFF_DT = OFF_XBC + CONV_DIM
PROJ_DIM = OFF_DT + N_DIR * SSD_HEADS
EPS = 1e-6

kernel_name = "hybrid_pool_ssd_prefix_dit_block"


def rmsnorm(x, w):
    xf = x.astype(jnp.float32)
    y = xf * lax.rsqrt(jnp.mean(xf * xf, axis=-1, keepdims=True) + EPS)
    return (y * w.astype(jnp.float32)).astype(x.dtype)


def box_mean(x, window, axis):
    n = x.shape[axis]
    s = jnp.cumsum(x.astype(jnp.float32), axis=axis)
    pad = [(0, 0)] * x.ndim
    pad[axis] = (1, 0)
    s = jnp.pad(s, pad)
    t = jnp.arange(n)
    lo = jnp.clip(t - window // 2, 0, n)
    hi = jnp.clip(t + window - window // 2, 0, n)
    total = jnp.take(s, hi, axis=axis) - jnp.take(s, lo, axis=axis)
    shape = [1] * x.ndim
    shape[axis] = n
    count = (hi - lo).astype(jnp.float32).reshape(shape)
    return (total / count).astype(x.dtype)


def pool_mixer(u, w_lin, scale, grid):
    b, n, _ = u.shape
    groups = u.reshape(b, n, N_POOL_GROUPS, POOL_GROUP_W)
    outs = []
    for g, w in enumerate(POOL_WINDOWS):
        ug = groups[:, :, g]
        if grid:
            rows = n // GRID_W
            img = ug.reshape(b, rows, GRID_W, POOL_GROUP_W)
            m = box_mean(box_mean(img, w, 1), w, 2).reshape(b, n, POOL_GROUP_W)
        else:
            m = box_mean(ug, w, 1)
        outs.append(m - ug)
    d = jnp.stack(outs, axis=2)
    y = jnp.einsum('bngc,gcd->bngd', d, w_lin).reshape(b, n, W_POOL)
    return y * scale


def centred_dwconv(u, w, bias):
    n = u.shape[1]
    up = jnp.pad(u, ((0, 0), (CONV_LEFT, D_CONV - 1 - CONV_LEFT), (0, 0)))
    y = sum(up[:, k:k + n] * w[k] for k in range(D_CONV))
    return jax.nn.silu(y + bias)


def ssd_scan(x, dt, a, b_in, c_in, h0):
    bsz, n = x.shape[:2]
    nc = n // CHUNK
    G, R = SSD_GROUPS, SSD_HEADS_PER_GROUP
    xd = (x * dt[..., None]).reshape(bsz, nc, CHUNK, G, R, SSD_HEADDIM)
    adt = (dt * a).reshape(bsz, nc, CHUNK, G, R)
    bc = b_in.reshape(bsz, nc, CHUNK, G, D_STATE)
    cc = c_in.reshape(bsz, nc, CHUNK, G, D_STATE)
    acum = jnp.cumsum(adt, axis=2)
    seg = acum[:, :, :, None] - acum[:, :, None, :]
    lower = jnp.tril(jnp.ones((CHUNK, CHUNK), dtype=bool))[:, :, None, None]
    decay = jnp.exp(jnp.where(lower, seg, -jnp.inf))
    cb = jnp.einsum('bclgn,bcsgn->bclsg', cc, bc)
    y_diag = jnp.einsum('bclsg,bclsgr,bcsgrp->bclgrp', cb, decay, xd)
    decay_to_end = jnp.exp(acum[:, :, -1:] - acum)
    chunk_states = jnp.einsum('bclgn,bclgr,bclgrp->bcgrpn', bc, decay_to_end, xd)
    chunk_decay = jnp.exp(acum[:, :, -1])

    def step(h, inp):
        s_c, d_c = inp
        return h * d_c[..., None, None] + s_c, h

    h_final, h_starts = lax.scan(
        step, h0, (jnp.moveaxis(chunk_states, 1, 0), jnp.moveaxis(chunk_decay, 1, 0)))
    h_starts = jnp.moveaxis(h_starts, 0, 1)
    y_off = jnp.einsum('bclgn,bcgrpn,bclgr->bclgrp', cc, h_starts, jnp.exp(acum))
    y = (y_diag + y_off).reshape(bsz, n, SSD_HEADS, SSD_HEADDIM)
    return y, h_final


def ssd_bidir(xbc_raw, dt_raw, conv_w, conv_b, a_log, dt_bias, d_skip, h0):
    bsz, n, _ = xbc_raw.shape
    xbc = centred_dwconv(xbc_raw, conv_w, conv_b)
    xs = xbc[..., :W_SSD].reshape(bsz, n, SSD_HEADS, SSD_HEADDIM)
    bs = xbc[..., W_SSD:W_SSD + GN].reshape(bsz, n, SSD_GROUPS, D_STATE)
    cs = xbc[..., W_SSD + GN:].reshape(bsz, n, SSD_GROUPS, D_STATE)
    dt = jax.nn.softplus(dt_raw.reshape(bsz, n, N_DIR, SSD_HEADS).astype(jnp.float32)
                         + dt_bias.astype(jnp.float32))
    a = -jnp.exp(a_log.astype(jnp.float32))
    flip = lambda t: jnp.flip(t, axis=1)
    y_f, h_f = ssd_scan(xs, dt[:, :, 0], a[0], bs, cs, h0[0])
    y_b, h_b = ssd_scan(flip(xs), flip(dt[:, :, 1]), a[1], flip(bs), flip(cs), h0[1])
    y = y_f + flip(y_b) + d_skip[:, None] * xs
    return y.reshape(bsz, n, W_SSD).astype(xbc_raw.dtype), jnp.stack([h_f, h_b])


def modulated_projection(h, norm_w, shift, scale, w_in):
    hm = rmsnorm(h, norm_w) * (1.0 + scale) + shift
    return hm @ w_in


def mixer_output(p, y_ssd, grid, pool_w, pool_scale, ssd_norm_w, w_out):
    bsz, n, _ = p.shape
    u_pool = p[..., :W_POOL]
    z_pool = p[..., OFF_POOL_Z:OFF_SSD_Z]
    z_ssd = p[..., OFF_SSD_Z:OFF_XBC]
    y_pool = pool_mixer(u_pool, pool_w, pool_scale, grid) * jax.nn.silu(z_pool)
    gated = (y_ssd * jax.nn.silu(z_ssd)).reshape(bsz, n, SSD_GROUPS, W_SSD // SSD_GROUPS)
    y_s = rmsnorm(gated, ssd_norm_w.reshape(SSD_GROUPS, W_SSD // SSD_GROUPS)).reshape(bsz, n, W_SSD)
    return jnp.concatenate([y_pool, y_s], axis=-1) @ w_out


def setup_inputs(seed: int = 0) -> dict:
    key = jax.random.key(seed)
    ks = jax.random.split(key, 20)
    nrm = jax.random.normal
    x = nrm(ks[0], (BATCH, SEQ, D_MODEL), jnp.float32)
    c = nrm(ks[1], (BATCH, D_MODEL), jnp.float32)
    ctx = nrm(ks[2], (BATCH, CTX_LEN, D_MODEL), jnp.float32)
    c_ctx = nrm(ks[3], (D_MODEL,), jnp.float32)
    norm_w = 1.0 + 0.02 * nrm(ks[4], (DEPTH, D_MODEL), jnp.float32)
    w_ada = 0.5 * D_MODEL ** -0.5 * nrm(ks[5], (DEPTH, D_MODEL, 3 * D_MODEL), jnp.float32)
    b_ada = 0.02 * nrm(ks[6], (DEPTH, 3 * D_MODEL), jnp.float32)
    w_in = D_MODEL ** -0.5 * nrm(ks[7], (DEPTH, D_MODEL, PROJ_DIM), jnp.float32)
    conv_w = D_CONV ** -0.5 * nrm(ks[8], (DEPTH, D_CONV, CONV_DIM), jnp.float32)
    conv_b = 0.02 * nrm(ks[9], (DEPTH, CONV_DIM), jnp.float32)
    a_log = jnp.log(jax.random.uniform(ks[10], (DEPTH, N_DIR, SSD_HEADS), jnp.float32, 1.0, 16.0))
    dt0 = jnp.exp(jax.random.uniform(ks[11], (DEPTH, N_DIR, SSD_HEADS), jnp.float32,
                                     float(np.log(1e-3)), float(np.log(1e-1))))
    dt_bias = dt0 + jnp.log(-jnp.expm1(-dt0))
    d_skip = 1.0 + 0.02 * nrm(ks[12], (DEPTH, SSD_HEADS), jnp.float32)
    ssd_norm_w = 1.0 + 0.02 * nrm(ks[13], (DEPTH, W_SSD), jnp.float32)
    pool_w = POOL_GROUP_W ** -0.5 * nrm(ks[14], (DEPTH, N_POOL_GROUPS, POOL_GROUP_W, POOL_GROUP_W), jnp.float32)
    pool_scale = 1.0 + 0.02 * nrm(ks[15], (DEPTH, W_POOL), jnp.float32)
    w_out = D_INNER ** -0.5 * nrm(ks[16], (DEPTH, D_INNER, D_MODEL), jnp.float32)
    final_norm_w = 1.0 + 0.02 * nrm(ks[17], (D_MODEL,), jnp.float32)
    return {"x": x, "c": c, "ctx": ctx, "c_ctx": c_ctx, "norm_w": norm_w,
            "w_ada": w_ada, "b_ada": b_ada, "w_in": w_in, "conv_w": conv_w,
            "conv_b": conv_b, "a_log": a_log, "dt_bias": dt_bias, "d_skip": d_skip,
            "ssd_norm_w": ssd_norm_w, "pool_w": pool_w, "pool_scale": pool_scale,
            "w_out": w_out, "final_norm_w": final_norm_w}


def reference(x, c, ctx, c_ctx, norm_w, w_ada, b_ada, w_in, conv_w, conv_b, a_log,
              dt_bias, d_skip, ssd_norm_w, pool_w, pool_scale, w_out, final_norm_w):
    bsz = x.shape[0]
    h_lat, h_ctx = x, ctx
    for i in range(DEPTH):
        mod_lat = jax.nn.silu(c) @ w_ada[i] + b_ada[i]
        mod_ctx = jax.nn.silu(c_ctx) @ w_ada[i] + b_ada[i]
        sh_l, sc_l, g_l = jnp.split(mod_lat[:, None, :], 3, axis=-1)
        sh_c, sc_c, g_c = jnp.split(mod_ctx, 3, axis=-1)

        p_ctx = modulated_projection(h_ctx, norm_w[i], sh_c, sc_c, w_in[i])
        h0 = jnp.zeros((N_DIR, bsz, SSD_GROUPS, SSD_HEADS_PER_GROUP, SSD_HEADDIM, D_STATE), jnp.float32)
        y_ssd_ctx, h_ctx_end = ssd_bidir(p_ctx[..., OFF_XBC:OFF_DT], p_ctx[..., OFF_DT:],
                                         conv_w[i], conv_b[i], a_log[i], dt_bias[i], d_skip[i], h0)

        p_lat = modulated_projection(h_lat, norm_w[i], sh_l, sc_l, w_in[i])
        y_ssd_lat, _ = ssd_bidir(p_lat[..., OFF_XBC:OFF_DT], p_lat[..., OFF_DT:],
                                 conv_w[i], conv_b[i], a_log[i], dt_bias[i], d_skip[i], h_ctx_end)
        h_lat = h_lat + g_l * mixer_output(p_lat, y_ssd_lat, True, pool_w[i], pool_scale[i],
                                           ssd_norm_w[i], w_out[i])
        if i < DEPTH - 1:
            h_ctx = h_ctx + g_c * mixer_output(p_ctx, y_ssd_ctx, False, pool_w[i], pool_scale[i],
                                               ssd_norm_w[i], w_out[i])
    return rmsnorm(h_lat, final_norm_w)
```

```python
import functools

import numpy as np
import jax
import jax.numpy as jnp
from jax.experimental import pallas as pl
from jax.experimental.pallas import tpu as pltpu

D_MODEL = 1024
GRID_W = 64
W_POOL = 1024
W_SSD = 1024
POOL_WINDOWS = (2, 4, 8, 16)
POOL_GROUP_W = 256
HEADDIM = 64
HEADS = 16
GROUPS = 4
HEADS_PER_GROUP = 4
D_STATE = 128
D_CONV = 4
CONV_LEFT = 2
CHUNK = 128
GN = GROUPS * D_STATE
CONV_DIM = W_SSD + 2 * GN
OFF_XBC = 2 * W_POOL + W_SSD
OFF_DT = OFF_XBC + CONV_DIM
DT_PAD = 128
EPS = 1e-6
SUBLANES = 8
VMEM_LIMIT = 56 * 1024 * 1024


def _silu(v):
    return v * (1.0 / (1.0 + jnp.exp(-v)))


def _softplus(v):
    return jnp.maximum(v, 0.0) + jnp.log1p(jnp.exp(-jnp.abs(v)))


def _bf(v):
    return v.astype(jnp.bfloat16)


def _dot(a, b):
    return jnp.dot(a, b, preferred_element_type=jnp.float32)


def _mod_kernel(c_ref, w_ref, b_ref, o_ref):
    s = _silu(c_ref[...])
    o_ref[...] = jnp.dot(s, w_ref[...], preferred_element_type=jnp.float32,
                         precision=jax.lax.Precision.HIGHEST) + b_ref[...]


def _modulation(cond_rows, w_ada, b_ada):
    rows = cond_rows.shape[0]
    n_out = w_ada.shape[1]
    tn = 1024
    return pl.pallas_call(
        _mod_kernel,
        grid=(n_out // tn,),
        in_specs=[pl.BlockSpec((rows, D_MODEL), lambda j: (0, 0)),
                  pl.BlockSpec((D_MODEL, tn), lambda j: (0, j)),
                  pl.BlockSpec((1, tn), lambda j: (0, j))],
        out_specs=pl.BlockSpec((rows, tn), lambda j: (0, j)),
        out_shape=jax.ShapeDtypeStruct((rows, n_out), jnp.float32),
        compiler_params=pltpu.CompilerParams(vmem_limit_bytes=VMEM_LIMIT),
        name="mod",
    )(cond_rows, w_ada, b_ada.reshape(1, n_out))


def _proj_kernel(x_ref, nw_ref, sh_ref, sc_ref, w_ref, *o_refs, widths, col_tile):
    x = x_ref[...]
    ms = jnp.mean(x * x, axis=-1, keepdims=True)
    hm = x * jax.lax.rsqrt(ms + EPS) * nw_ref[...]
    hm = _bf(hm * (1.0 + sc_ref[0]) + sh_ref[0])
    off = 0
    for o_ref, width in zip(o_refs, widths):
        for j in range(0, width, col_tile):
            tw = min(col_tile, width - j)
            o_ref[:, j:j + tw] = _dot(hm, w_ref[:, off + j:off + j + tw])
        off += width


def _projection(x2d, norm_w, shift, scale, w_bf, widths, rows_per_mod, tm):
    n_tok = x2d.shape[0]
    tiles_per_mod = rows_per_mod // tm
    n_cols = w_bf.shape[1]
    kern = functools.partial(_proj_kernel, widths=tuple(widths), col_tile=512)
    mod_spec = pl.BlockSpec((1, 1, D_MODEL), lambda i: (i // tiles_per_mod, 0, 0))
    return pl.pallas_call(
        kern,
        grid=(n_tok // tm,),
        in_specs=[pl.BlockSpec((tm, D_MODEL), lambda i: (i, 0)),
                  pl.BlockSpec((1, D_MODEL), lambda i: (0, 0)),
                  mod_spec, mod_spec,
                  pl.BlockSpec((D_MODEL, n_cols), lambda i: (0, 0))],
        out_specs=[pl.BlockSpec((tm, w), lambda i: (i, 0)) for w in widths],
        out_shape=[jax.ShapeDtypeStruct((n_tok, w), jnp.float32) for w in widths],
        compiler_params=pltpu.CompilerParams(vmem_limit_bytes=VMEM_LIMIT),
        name="proj",
    )(x2d, norm_w.reshape(1, D_MODEL), shift, scale, w_bf)


def _lane_cumsum(v):
    lane = jax.lax.broadcasted_iota(jnp.int32, v.shape, 1)
    shift = 1
    while shift < CHUNK:
        v = v + jnp.where(lane >= shift, pltpu.roll(v, shift, 1), 0.0)
        shift *= 2
    return v


def _dt_terms(dt_blk, a_col, bias_col):
    dt_t = dt_blk.T[:2 * HEADS]
    dt = _softplus(dt_t + bias_col)
    cum = _lane_cumsum(dt * a_col)
    return dt, cum, cum[:, CHUNK - 1:CHUNK]


def _ssd_fwd_kernel(cur_ref, prev_ref, next_ref, dt_ref, cw_ref, cb_ref, alog_ref, bias_ref,
                    dskip_ref, h0_ref, *rest, n_chunks, with_y):
    if with_y:
        y_ref, xs_t_ref, c_t_ref, b_ref, hout_ref, ext_ref, h_ref = rest
    else:
        xs_t_ref, b_ref, hout_ref, ext_ref, h_ref = rest
    c = pl.program_id(1)

    @pl.when(c == 0)
    def _():
        h_ref[...] = h0_ref[0]

    ext_ref[0:SUBLANES] = jnp.where(c > 0, prev_ref[...], 0.0)
    ext_ref[SUBLANES:SUBLANES + CHUNK] = cur_ref[...]
    ext_ref[SUBLANES + CHUNK:] = jnp.where(c < n_chunks - 1, next_ref[...], 0.0)
    acc = cb_ref[...]
    for k in range(D_CONV):
        lo = SUBLANES - CONV_LEFT + k
        acc = acc + ext_ref[lo:lo + CHUNK] * cw_ref[k:k + 1]
    xc = _silu(acc)

    xs_t = xc[:, :W_SSD].T
    b_tok = _bf(xc[:, W_SSD:W_SSD + GN])
    xs_t_ref[0, 0] = _bf(xs_t)
    b_ref[0, 0] = b_tok

    a_col = -jnp.exp(alog_ref[...])
    dt, cum, tot = _dt_terms(dt_ref[...], a_col, bias_ref[...])
    dt_f, cum_f, tot_f = dt[:HEADS], cum[:HEADS], tot[:HEADS]
    scale_in = dt_f * jnp.exp(tot_f - cum_f)
    chunk_decay = jnp.exp(tot_f)

    if with_y:
        dt_b, cum_b = dt[HEADS:], cum[HEADS:]
        cumx_b = cum_b - dt_b * a_col[HEADS:]
        c_t = _bf(xc[:, W_SSD + GN:].T)
        c_t_ref[0, 0] = c_t
        col_terms = jnp.concatenate(
            [jnp.log(dt_f) - cum_f, jnp.log(dt_b) + cumx_b,
             jnp.zeros((CHUNK - 2 * HEADS, CHUNK), jnp.float32)], axis=0).T
        row_f = cum_f
        row_b = -cumx_b
        decay_out_f = jnp.exp(cum_f)
        src = jax.lax.broadcasted_iota(jnp.int32, (CHUNK, CHUNK), 0)
        dst = jax.lax.broadcasted_iota(jnp.int32, (CHUNK, CHUNK), 1)
        causal = src <= dst
        is_diag = src == dst

    y_parts = []
    for g in range(GROUPS):
        bg = b_tok[:, g * D_STATE:(g + 1) * D_STATE]
        rows = slice(g * HEADS_PER_GROUP * HEADDIM, (g + 1) * HEADS_PER_GROUP * HEADDIM)
        if with_y:
            cg_t = c_t[g * D_STATE:(g + 1) * D_STATE]
            g_t = _dot(bg, cg_t)
            g_diag = jnp.sum(jnp.where(is_diag, g_t, 0.0), axis=0, keepdims=True)
            y_off = _dot(_bf(h_ref[rows]), cg_t)
        xd_parts = []
        for r in range(HEADS_PER_GROUP):
            h = g * HEADS_PER_GROUP + r
            hr = slice(h * HEADDIM, (h + 1) * HEADDIM)
            x_h = xs_t[hr]
            xd_parts.append(_bf(x_h * scale_in[h:h + 1]))
            if with_y:
                col_f = jnp.broadcast_to(col_terms[:, h:h + 1], (CHUNK, CHUNK))
                col_b = jnp.broadcast_to(col_terms[:, HEADS + h:HEADS + h + 1], (CHUNK, CHUNK))
                expo = jnp.where(causal, col_f + row_f[h:h + 1], col_b + row_b[h:h + 1])
                w_t = _bf(g_t * jnp.exp(expo))
                y_h = _dot(_bf(x_h), w_t)
                y_h = y_h + y_off[r * HEADDIM:(r + 1) * HEADDIM] * decay_out_f[h:h + 1]
                skip = dskip_ref[hr] + g_diag * dt_b[h:h + 1]
                y_parts.append(y_h + skip * x_h)
        xd = jnp.concatenate(xd_parts, axis=0)
        s_new = _dot(xd, bg)
        for r in range(HEADS_PER_GROUP):
            h = g * HEADS_PER_GROUP + r
            hr = slice(h * HEADDIM, (h + 1) * HEADDIM)
            h_ref[hr] = h_ref[hr] * chunk_decay[h:h + 1] + s_new[r * HEADDIM:(r + 1) * HEADDIM]
    if with_y:
        y_ref[...] = jnp.concatenate(y_parts, axis=0).T

    @pl.when(c == n_chunks - 1)
    def _():
        hout_ref[0] = h_ref[...]


def _ssd_forward(xbc, dt_raw, conv_w, conv_b, alog_col, bias_col, dskip_b, h0, bsz, n_chunks,
                 with_y):
    n_tok = xbc.shape[0]
    blocks8 = n_tok // SUBLANES
    per = CHUNK // SUBLANES

    def cur_map(b, c):
        return (b * n_chunks + c, 0)

    def prev_map(b, c):
        return (jnp.maximum((b * n_chunks + c) * per - 1, 0), 0)

    def next_map(b, c):
        return (jnp.minimum((b * n_chunks + c + 1) * per, blocks8 - 1), 0)

    const2 = lambda b, c: (0, 0)
    chunk4 = lambda b, c: (b, c, 0, 0)
    state3 = lambda b, c: (b, 0, 0)
    xs_t_shape = jax.ShapeDtypeStruct((bsz, n_chunks, W_SSD, CHUNK), jnp.bfloat16)
    gn_t_shape = jax.ShapeDtypeStruct((bsz, n_chunks, GN, CHUNK), jnp.bfloat16)
    b_shape = jax.ShapeDtypeStruct((bsz, n_chunks, CHUNK, GN), jnp.bfloat16)
    h_shape = jax.ShapeDtypeStruct((bsz, W_SSD, D_STATE), jnp.float32)
    xs_t_spec = pl.BlockSpec((1, 1, W_SSD, CHUNK), chunk4)
    gn_t_spec = pl.BlockSpec((1, 1, GN, CHUNK), chunk4)
    b_spec = pl.BlockSpec((1, 1, CHUNK, GN), chunk4)
    h_spec = pl.BlockSpec((1, W_SSD, D_STATE), state3)
    if with_y:
        out_shape = [jax.ShapeDtypeStruct((n_tok, W_SSD), jnp.float32),
                     xs_t_shape, gn_t_shape, b_shape, h_shape]
        out_specs = [pl.BlockSpec((CHUNK, W_SSD), cur_map), xs_t_spec, gn_t_spec, b_spec, h_spec]
    else:
        out_shape = [xs_t_shape, b_shape, h_shape]
        out_specs = [xs_t_spec, b_spec, h_spec]
    kern = functools.partial(_ssd_fwd_kernel, n_chunks=n_chunks, with_y=with_y)
    return pl.pallas_call(
        kern,
        grid=(bsz, n_chunks),
        in_specs=[pl.BlockSpec((CHUNK, CONV_DIM), cur_map),
                  pl.BlockSpec((SUBLANES, CONV_DIM), prev_map),
                  pl.BlockSpec((SUBLANES, CONV_DIM), next_map),
                  pl.BlockSpec((CHUNK, DT_PAD), cur_map),
                  pl.BlockSpec((D_CONV, CONV_DIM), const2),
                  pl.BlockSpec((1, CONV_DIM), const2),
                  pl.BlockSpec((2 * HEADS, 1), const2),
                  pl.BlockSpec((2 * HEADS, 1), const2),
                  pl.BlockSpec((W_SSD, CHUNK), const2),
                  h_spec],
        out_specs=out_specs,
        out_shape=out_shape,
        scratch_shapes=[pltpu.VMEM((CHUNK + 2 * SUBLANES, CONV_DIM), jnp.float32),
                        pltpu.VMEM((W_SSD, D_STATE), jnp.float32)],
        compiler_params=pltpu.CompilerParams(
            dimension_semantics=("arbitrary", "arbitrary"), vmem_limit_bytes=VMEM_LIMIT),
        name="ssd_fwd" if with_y else "ssd_fwd_state",
    )(xbc, xbc, xbc, dt_raw, conv_w, conv_b, alog_col, bias_col, dskip_b, h0)


def _ssd_bwd_kernel(xs_t_ref, b_ref, dt_ref, alog_ref, bias_ref, h0_ref, *rest, n_chunks, with_y):
    if with_y:
        c_t_ref, ypart_ref, y_ref, hout_ref, h_ref = rest
    else:
        hout_ref, h_ref = rest
    c = pl.program_id(1)

    @pl.when(c == 0)
    def _():
        h_ref[...] = h0_ref[0]

    a_col = -jnp.exp(alog_ref[...])
    dt, cum, tot = _dt_terms(dt_ref[...], a_col, bias_ref[...])
    dt_b, cum_b, tot_b = dt[HEADS:], cum[HEADS:], tot[HEADS:]
    cumx_b = cum_b - dt_b * a_col[HEADS:]
    scale_in = dt_b * jnp.exp(cumx_b)
    chunk_decay = jnp.exp(tot_b)
    if with_y:
        decay_out = jnp.exp(tot_b - cumx_b)

    y_parts = []
    for g in range(GROUPS):
        bg = b_ref[0, 0, :, g * D_STATE:(g + 1) * D_STATE]
        rows = slice(g * HEADS_PER_GROUP * HEADDIM, (g + 1) * HEADS_PER_GROUP * HEADDIM)
        if with_y:
            cg_t = c_t_ref[0, 0, g * D_STATE:(g + 1) * D_STATE]
            y_off = _dot(_bf(h_ref[rows]), cg_t)
        xd_parts = []
        for r in range(HEADS_PER_GROUP):
            h = g * HEADS_PER_GROUP + r
            hr = slice(h * HEADDIM, (h + 1) * HEADDIM)
            x_h = xs_t_ref[0, 0, hr].astype(jnp.float32)
            xd_parts.append(_bf(x_h * scale_in[h:h + 1]))
            if with_y:
                y_parts.append(y_off[r * HEADDIM:(r + 1) * HEADDIM] * decay_out[h:h + 1])
        xd = jnp.concatenate(xd_parts, axis=0)
        s_new = _dot(xd, bg)
        for r in range(HEADS_PER_GROUP):
            h = g * HEADS_PER_GROUP + r
            hr = slice(h * HEADDIM, (h + 1) * HEADDIM)
            h_ref[hr] = h_ref[hr] * chunk_decay[h:h + 1] + s_new[r * HEADDIM:(r + 1) * HEADDIM]
    if with_y:
        y_ref[...] = ypart_ref[...] + jnp.concatenate(y_parts, axis=0).T

    @pl.when(c == n_chunks - 1)
    def _():
        hout_ref[0] = h_ref[...]


def _ssd_backward(xs_t, b_tok, dt_raw, alog_col, bias_col, h0, bsz, n_chunks, c_t=None,
                  y_part=None):
    with_y = c_t is not None
    n_tok = dt_raw.shape[0]

    def tok_map(b, c):
        return (b * n_chunks + (n_chunks - 1 - c), 0)

    const2 = lambda b, c: (0, 0)
    chunk4 = lambda b, c: (b, n_chunks - 1 - c, 0, 0)
    state3 = lambda b, c: (b, 0, 0)
    h_spec = pl.BlockSpec((1, W_SSD, D_STATE), state3)
    h_shape = jax.ShapeDtypeStruct((bsz, W_SSD, D_STATE), jnp.float32)
    in_specs = [pl.BlockSpec((1, 1, W_SSD, CHUNK), chunk4),
                pl.BlockSpec((1, 1, CHUNK, GN), chunk4),
                pl.BlockSpec((CHUNK, DT_PAD), tok_map),
                pl.BlockSpec((2 * HEADS, 1), const2),
                pl.BlockSpec((2 * HEADS, 1), const2),
                h_spec]
    args = [xs_t, b_tok, dt_raw, alog_col, bias_col, h0]
    if with_y:
        in_specs += [pl.BlockSpec((1, 1, GN, CHUNK), chunk4),
                     pl.BlockSpec((CHUNK, W_SSD), tok_map)]
        args += [c_t, y_part]
        out_shape = [jax.ShapeDtypeStruct((n_tok, W_SSD), jnp.float32), h_shape]
        out_specs = [pl.BlockSpec((CHUNK, W_SSD), tok_map), h_spec]
    else:
        out_shape = [h_shape]
        out_specs = [h_spec]
    kern = functools.partial(_ssd_bwd_kernel, n_chunks=n_chunks, with_y=with_y)
    return pl.pallas_call(
        kern,
        grid=(bsz, n_chunks),
        in_specs=in_specs,
        out_specs=out_specs,
        out_shape=out_shape,
        scratch_shapes=[pltpu.VMEM((W_SSD, D_STATE), jnp.float32)],
        compiler_params=pltpu.CompilerParams(
            dimension_semantics=("arbitrary", "arbitrary"), vmem_limit_bytes=VMEM_LIMIT),
        name="ssd_bwd" if with_y else "ssd_bwd_state",
    )(*args)


POOL_TILE_ROWS = 4
POOL_TILE = POOL_TILE_ROWS * GRID_W
POOL_PAD_ROWS = max(POOL_WINDOWS) // 2


def _pool_constants(window, n_rows):
    lo_off, hi_off = -(window // 2), window - window // 2
    col = np.arange(GRID_W)
    lo = np.clip(col + lo_off, 0, GRID_W)
    hi = np.clip(col + hi_off, 0, GRID_W)
    band = ((col[None, :] >= lo[:, None]) & (col[None, :] < hi[:, None])).astype(np.float32)
    band_tile = np.kron(np.eye(POOL_TILE_ROWS, dtype=np.float32), band)
    row = np.arange(n_rows)
    cnt_r = np.clip(row + hi_off, 0, n_rows) - np.clip(row + lo_off, 0, n_rows)
    inv = 1.0 / (cnt_r[:, None] * (hi - lo)[None, :]).astype(np.float64)
    inv = np.broadcast_to(inv.reshape(-1, 1), (n_rows * GRID_W, 128)).astype(np.float32)
    return jnp.asarray(band_tile, jnp.bfloat16), jnp.asarray(inv)


def _pool_kernel(u_ref, z_ref, band_ref, inv_ref, w_ref, scale_ref, o_ref, pad_ref, *, window,
                 n_rows):
    pad_tok = POOL_PAD_ROWS * GRID_W
    n_tok = n_rows * GRID_W
    zeros = jnp.zeros((pad_tok, POOL_GROUP_W), jnp.float32)
    pad_ref[0:pad_tok] = zeros
    pad_ref[pad_tok + n_tok:] = zeros
    pad_ref[pad_tok:pad_tok + n_tok] = u_ref[...]
    first = (POOL_PAD_ROWS - window // 2) * GRID_W

    def tile(t, carry):
        base = pl.multiple_of(t * POOL_TILE, POOL_TILE)
        rsum = pad_ref[pl.ds(base + first, POOL_TILE)]
        for k in range(1, window):
            rsum = rsum + pad_ref[pl.ds(base + first + k * GRID_W, POOL_TILE)]
        hi = _bf(rsum)
        lo = _bf(rsum - hi.astype(jnp.float32))
        band = band_ref[...]
        box = _dot(band, hi) + _dot(band, lo)
        inv = inv_ref[pl.ds(base, POOL_TILE)]
        mean = box * jnp.concatenate([inv, inv], axis=1)
        d = mean - u_ref[pl.ds(base, POOL_TILE)]
        y = _dot(_bf(d), w_ref[0]) * scale_ref[...]
        o_ref[pl.ds(base, POOL_TILE)] = _bf(y * _silu(z_ref[pl.ds(base, POOL_TILE)]))
        return carry

    jax.lax.fori_loop(0, n_tok // POOL_TILE, tile, 0)


def _pool_group(p_pool, pool_w_bf, pool_scale, g, bsz, n_img_tok):
    window = POOL_WINDOWS[g]
    n_rows = n_img_tok // GRID_W
    band, inv = _pool_constants(window, n_rows)
    kern = functools.partial(_pool_kernel, window=window, n_rows=n_rows)
    n_groups = len(POOL_WINDOWS)
    return pl.pallas_call(
        kern,
        grid=(bsz,),
        in_specs=[pl.BlockSpec((n_img_tok, POOL_GROUP_W), lambda b: (b, g)),
                  pl.BlockSpec((n_img_tok, POOL_GROUP_W), lambda b: (b, n_groups + g)),
                  pl.BlockSpec((POOL_TILE, POOL_TILE), lambda b: (0, 0)),
                  pl.BlockSpec((n_img_tok, 128), lambda b: (0, 0)),
                  pl.BlockSpec((1, POOL_GROUP_W, POOL_GROUP_W), lambda b: (g, 0, 0)),
                  pl.BlockSpec((1, POOL_GROUP_W), lambda b: (0, g))],
        out_specs=pl.BlockSpec((n_img_tok, POOL_GROUP_W), lambda b: (b, 0)),
        out_shape=jax.ShapeDtypeStruct((bsz * n_img_tok, POOL_GROUP_W), jnp.bfloat16),
        scratch_shapes=[pltpu.VMEM((n_img_tok + 2 * POOL_PAD_ROWS * GRID_W, POOL_GROUP_W),
                                   jnp.float32)],
        compiler_params=pltpu.CompilerParams(vmem_limit_bytes=VMEM_LIMIT),
        name=f"pool{window}",
    )(p_pool, p_pool, band, inv, pool_w_bf, pool_scale)


def _out_kernel(yp0_ref, yp1_ref, yp2_ref, yp3_ref, ys_ref, zs_ref, x_ref, gate_ref, snw_ref,
                wout_ref, fnw_ref, o_ref):
    acc = None
    for g, yp_ref in enumerate((yp0_ref, yp1_ref, yp2_ref, yp3_ref)):
        part = _dot(yp_ref[...], wout_ref[g * POOL_GROUP_W:(g + 1) * POOL_GROUP_W])
        acc = part if acc is None else acc + part
    gw = W_SSD // GROUPS
    for g in range(GROUPS):
        cols = slice(g * gw, (g + 1) * gw)
        gated = ys_ref[:, cols] * _silu(zs_ref[:, cols])
        ms = jnp.mean(gated * gated, axis=-1, keepdims=True)
        yn = gated * jax.lax.rsqrt(ms + EPS) * snw_ref[:, cols]
        acc = acc + _dot(_bf(yn), wout_ref[W_POOL + g * gw:W_POOL + (g + 1) * gw])
    h = x_ref[...] + gate_ref[0] * acc
    ms = jnp.mean(h * h, axis=-1, keepdims=True)
    o_ref[...] = h * jax.lax.rsqrt(ms + EPS) * fnw_ref[...]


def _output(y_pool, y_ssd, z_ssd, x2d, gate, ssd_norm_w, w_out_bf, final_norm_w, rows_per_mod, tm):
    n_tok = x2d.shape[0]
    tiles_per_mod = rows_per_mod // tm
    tok = lambda i: (i, 0)
    const = lambda i: (0, 0)
    return pl.pallas_call(
        _out_kernel,
        grid=(n_tok // tm,),
        in_specs=[pl.BlockSpec((tm, POOL_GROUP_W), tok)] * 4 + [
            pl.BlockSpec((tm, W_SSD), tok),
            pl.BlockSpec((tm, W_SSD), tok),
            pl.BlockSpec((tm, D_MODEL), tok),
            pl.BlockSpec((1, 1, D_MODEL), lambda i: (i // tiles_per_mod, 0, 0)),
            pl.BlockSpec((1, W_SSD), const),
            pl.BlockSpec((W_POOL + W_SSD, D_MODEL), const),
            pl.BlockSpec((1, D_MODEL), const)],
        out_specs=pl.BlockSpec((tm, D_MODEL), tok),
        out_shape=jax.ShapeDtypeStruct((n_tok, D_MODEL), jnp.float32),
        compiler_params=pltpu.CompilerParams(vmem_limit_bytes=VMEM_LIMIT),
        name="out",
    )(*y_pool, y_ssd, z_ssd, x2d, gate, ssd_norm_w.reshape(1, W_SSD), w_out_bf,
      final_norm_w.reshape(1, D_MODEL))


def kernel(x, c, ctx, c_ctx, norm_w, w_ada, b_ada, w_in, conv_w, conv_b, a_log, dt_bias, d_skip,
           ssd_norm_w, pool_w, pool_scale, w_out, final_norm_w):
    bsz, seq, _ = x.shape
    ctx_len = ctx.shape[1]
    depth = norm_w.shape[0]
    assert depth == 1, "single-layer block: the context stream update is never consumed"
    assert seq % CHUNK == 0 and ctx_len % CHUNK == 0 and seq % GRID_W == 0

    mod_rows = -(-(bsz + 1) // SUBLANES) * SUBLANES
    cond = jnp.concatenate([c, c_ctx[None], jnp.zeros((mod_rows - bsz - 1, D_MODEL), c.dtype)])
    mod = _modulation(cond, w_ada[0], b_ada[0])
    shift = mod[:, :D_MODEL].reshape(mod_rows, 1, D_MODEL)
    scale = mod[:, D_MODEL:2 * D_MODEL].reshape(mod_rows, 1, D_MODEL)
    gate = mod[:, 2 * D_MODEL:].reshape(mod_rows, 1, D_MODEL)

    w_in_bf = _bf(jnp.pad(w_in[0], ((0, 0), (0, DT_PAD - 2 * HEADS))))
    alog_col = a_log[0].reshape(2 * HEADS, 1)
    bias_col = dt_bias[0].reshape(2 * HEADS, 1)
    dskip_b = jnp.broadcast_to(jnp.repeat(d_skip[0], HEADDIM)[:, None], (W_SSD, CHUNK))
    conv_b2 = conv_b[0].reshape(1, CONV_DIM)
    zero_state = jnp.zeros((bsz, W_SSD, D_STATE), jnp.float32)

    ctx2d = ctx.reshape(bsz * ctx_len, D_MODEL)
    xbc_c, dt_c = _projection(ctx2d, norm_w[0], shift[bsz:bsz + 1], scale[bsz:bsz + 1],
                              w_in_bf[:, OFF_XBC:], (CONV_DIM, DT_PAD), bsz * ctx_len, 256)
    nc_ctx = ctx_len // CHUNK
    xs_t_c, b_c, h_fwd = _ssd_forward(xbc_c, dt_c, conv_w[0], conv_b2, alog_col, bias_col, dskip_b,
                                      zero_state, bsz, nc_ctx, with_y=False)
    (h_bwd,) = _ssd_backward(xs_t_c, b_c, dt_c, alog_col, bias_col, zero_state, bsz, nc_ctx)

    x2d = x.reshape(bsz * seq, D_MODEL)
    p_pool, z_ssd, xbc, dt_raw = _projection(
        x2d, norm_w[0], shift, scale, w_in_bf, (2 * W_POOL, W_SSD, CONV_DIM, DT_PAD), seq, 256)
    nc = seq // CHUNK
    y_part, xs_t, c_t, b_tok, _ = _ssd_forward(xbc, dt_raw, conv_w[0], conv_b2, alog_col, bias_col,
                                               dskip_b, h_fwd, bsz, nc, with_y=True)
    y_ssd, _ = _ssd_backward(xs_t, b_tok, dt_raw, alog_col, bias_col, h_bwd, bsz, nc, c_t=c_t,
                             y_part=y_part)

    pool_w_bf = _bf(pool_w[0])
    y_pool = [_pool_group(p_pool, pool_w_bf, pool_scale, g, bsz, seq)
              for g in range(len(POOL_WINDOWS))]
    out = _output(y_pool, y_ssd, z_ssd, x2d, gate, ssd_norm_w[0], _bf(w_out[0]), final_norm_w, seq,
                  256)
    return out.reshape(bsz, seq, D_MODEL)
```

```python
import functools

import numpy as np
import jax
import jax.numpy as jnp
from jax.experimental import pallas as pl
from jax.experimental.pallas import tpu as pltpu

D_MODEL = 1024
GRID_W = 64
W_POOL = 1024
W_SSD = 1024
POOL_WINDOWS = (2, 4, 8, 16)
N_POOL_GROUPS = len(POOL_WINDOWS)
POOL_GROUP_W = 256
HEADDIM = 64
HEADS = 16
GROUPS = 4
HEADS_PER_GROUP = 4
D_STATE = 128
D_CONV = 4
CONV_LEFT = 2
CHUNK = 128
GN = GROUPS * D_STATE
CONV_DIM = W_SSD + 2 * GN
OFF_POOL_Z = W_POOL
OFF_SSD_Z = 2 * W_POOL
OFF_XBC = 2 * W_POOL + W_SSD
OFF_DT = OFF_XBC + CONV_DIM
DT_PAD = 128
EPS = 1e-6
SUBLANES = 8
HALO = 16
VMEM_LIMIT = 56 * 1024 * 1024


def _silu(v):
    return v * (1.0 / (1.0 + jnp.exp(-v)))


def _softplus(v):
    return jnp.maximum(v, 0.0) + jnp.log1p(jnp.exp(-jnp.abs(v)))


def _bf(v):
    return v.astype(jnp.bfloat16)


def _dot(a, b):
    return jnp.dot(a, b, preferred_element_type=jnp.float32)


def _mod_kernel(c_ref, w_ref, b_ref, o_ref):
    s = _silu(c_ref[...])
    o_ref[...] = jnp.dot(s, w_ref[...], preferred_element_type=jnp.float32,
                         precision=jax.lax.Precision.HIGHEST) + b_ref[...]


def _modulation(cond_rows, w_ada, b_ada):
    rows = cond_rows.shape[0]
    n_out = w_ada.shape[1]
    tn = 1024
    return pl.pallas_call(
        _mod_kernel,
        grid=(n_out // tn,),
        in_specs=[pl.BlockSpec((rows, D_MODEL), lambda j: (0, 0)),
                  pl.BlockSpec((D_MODEL, tn), lambda j: (0, j)),
                  pl.BlockSpec((1, tn), lambda j: (0, j))],
        out_specs=pl.BlockSpec((rows, tn), lambda j: (0, j)),
        out_shape=jax.ShapeDtypeStruct((rows, n_out), jnp.float32),
        compiler_params=pltpu.CompilerParams(vmem_limit_bytes=VMEM_LIMIT),
        name="mod",
    )(cond_rows, w_ada, b_ada.reshape(1, n_out))


def _lane_cumsum(v):
    lane = jax.lax.broadcasted_iota(jnp.int32, v.shape, 1)
    shift = 1
    while shift < CHUNK:
        v = v + jnp.where(lane >= shift, pltpu.roll(v, shift, 1), 0.0)
        shift *= 2
    return v


def _proj_kernel(x_ref, xp_ref, xn_ref, nw_ref, sh_ref, sc_ref, w_ref, cw_ref, cb_ref, alog_ref,
                 bias_ref, *rest, tm, tiles_per_seq, full):
    if full:
        (u0, u1, u2, u3, zp0, zp1, zp2, zp3, zs_ref, xs_t_ref, b_ref, c_t_ref, dt_ref, cum_ref,
         pe_ref) = rest
        u_refs, zp_refs = (u0, u1, u2, u3), (zp0, zp1, zp2, zp3)
    else:
        xs_t_ref, b_ref, dt_ref, cum_ref, pe_ref = rest
    i = pl.program_id(0)
    pos = i % tiles_per_seq
    has_prev = pos > 0
    has_next = pos < tiles_per_seq - 1

    def modulated(v):
        ms = jnp.mean(v * v, axis=-1, keepdims=True)
        y = v * jax.lax.rsqrt(ms + EPS) * nw_ref[...]
        return y * (1.0 + sc_ref[0]) + sh_ref[0]

    top = jnp.where(has_prev, modulated(xp_ref[...]), 0.0)
    bot = jnp.where(has_next, modulated(xn_ref[...]), 0.0)
    hm_ext = _bf(jnp.concatenate([top, modulated(x_ref[...]), bot], axis=0))
    hm = hm_ext[HALO:HALO + tm]
    n_chunks = tm // CHUNK

    if full:
        for g in range(N_POOL_GROUPS):
            cols = slice(g * POOL_GROUP_W, (g + 1) * POOL_GROUP_W)
            u_refs[g][...] = _dot(hm, w_ref[:, cols])
            zcols = slice(OFF_POOL_Z + g * POOL_GROUP_W, OFF_POOL_Z + (g + 1) * POOL_GROUP_W)
            zp_refs[g][...] = _bf(_silu(_dot(hm, w_ref[:, zcols])))
        seg = 512
        for j in range(0, W_SSD, seg):
            zs_ref[:, j:j + seg] = _bf(_silu(_dot(hm, w_ref[:, OFF_SSD_Z + j:OFF_SSD_Z + j + seg])))

    seg = pe_ref.shape[1]
    for j in range(0, CONV_DIM, seg):
        is_x = j < W_SSD
        is_b = W_SSD <= j < W_SSD + GN
        if not full and not (is_x or is_b):
            continue
        pe_ref[...] = _dot(hm_ext, w_ref[:, OFF_XBC + j:OFF_XBC + j + seg])
        acc = cb_ref[:, j:j + seg]
        for k in range(D_CONV):
            lo = HALO - CONV_LEFT + k
            acc = acc + pe_ref[lo:lo + tm] * cw_ref[k:k + 1, j:j + seg]
        xc = _silu(acc)
        if is_b:
            b_ref[:, j - W_SSD:j - W_SSD + seg] = _bf(xc)
        else:
            dst, off = (xs_t_ref, j) if is_x else (c_t_ref, j - W_SSD - GN)
            for q in range(n_chunks):
                dst[q, off:off + seg] = _bf(xc[q * CHUNK:(q + 1) * CHUNK].T)

    p_dt = _dot(hm, w_ref[:, OFF_DT:OFF_DT + DT_PAD])
    a_col = -jnp.exp(alog_ref[...])
    for q in range(n_chunks):
        dt = _softplus(p_dt[q * CHUNK:(q + 1) * CHUNK].T[:2 * HEADS] + bias_ref[...])
        dt_ref[q] = dt
        cum_ref[q] = _lane_cumsum(dt * a_col)


def _projection(x2d, norm_w, shift, scale, w_bf, conv_w, conv_b, alog_col, bias_col, seq_len, tm,
                full):
    n_tok = x2d.shape[0]
    tiles_per_seq = seq_len // tm
    n_mod = shift.shape[0]
    nct = n_tok // CHUNK
    per = tm // HALO
    last_halo = n_tok // HALO - 1
    kern = functools.partial(_proj_kernel, tm=tm, tiles_per_seq=tiles_per_seq, full=full)
    mod_map = (lambda i: (i // tiles_per_seq, 0, 0)) if n_mod > 1 else (lambda i: (0, 0, 0))
    mod_spec = pl.BlockSpec((1, 1, D_MODEL), mod_map)
    const = lambda i: (0, 0)
    tok = lambda i: (i, 0)
    chunk3 = lambda i: (i, 0, 0)
    q = tm // CHUNK
    xs_t = (jax.ShapeDtypeStruct((nct, W_SSD, CHUNK), jnp.bfloat16),
            pl.BlockSpec((q, W_SSD, CHUNK), chunk3))
    b_tok = (jax.ShapeDtypeStruct((n_tok, GN), jnp.bfloat16), pl.BlockSpec((tm, GN), tok))
    c_t = (jax.ShapeDtypeStruct((nct, GN, CHUNK), jnp.bfloat16), pl.BlockSpec((q, GN, CHUNK), chunk3))
    dt = (jax.ShapeDtypeStruct((nct, 2 * HEADS, CHUNK), jnp.float32),
          pl.BlockSpec((q, 2 * HEADS, CHUNK), chunk3))
    if full:
        u = (jax.ShapeDtypeStruct((n_tok, POOL_GROUP_W), jnp.float32),
             pl.BlockSpec((tm, POOL_GROUP_W), tok))
        zp = (jax.ShapeDtypeStruct((n_tok, POOL_GROUP_W), jnp.bfloat16),
              pl.BlockSpec((tm, POOL_GROUP_W), tok))
        zs = (jax.ShapeDtypeStruct((n_tok, W_SSD), jnp.bfloat16), pl.BlockSpec((tm, W_SSD), tok))
        outs = [u] * N_POOL_GROUPS + [zp] * N_POOL_GROUPS + [zs, xs_t, b_tok, c_t, dt, dt]
    else:
        outs = [xs_t, b_tok, dt, dt]
    return pl.pallas_call(
        kern,
        grid=(n_tok // tm,),
        in_specs=[pl.BlockSpec((tm, D_MODEL), tok),
                  pl.BlockSpec((HALO, D_MODEL), lambda i: (jnp.maximum(i * per - 1, 0), 0)),
                  pl.BlockSpec((HALO, D_MODEL), lambda i: (jnp.minimum((i + 1) * per, last_halo), 0)),
                  pl.BlockSpec((1, D_MODEL), const),
                  mod_spec, mod_spec,
                  pl.BlockSpec(w_bf.shape, const),
                  pl.BlockSpec((D_CONV, CONV_DIM), const),
                  pl.BlockSpec((1, CONV_DIM), const),
                  pl.BlockSpec((2 * HEADS, 1), const),
                  pl.BlockSpec((2 * HEADS, 1), const)],
        out_specs=[o[1] for o in outs],
        out_shape=[o[0] for o in outs],
        scratch_shapes=[pltpu.VMEM((tm + 2 * HALO, 512), jnp.float32)],
        compiler_params=pltpu.CompilerParams(vmem_limit_bytes=VMEM_LIMIT),
        name="proj" if full else "proj_ctx",
    )(x2d, x2d, x2d, norm_w.reshape(1, D_MODEL), shift, scale, w_bf, conv_w, conv_b, alog_col,
      bias_col)


def _state_update(h_ref, xs_t_ref, b_ref, scale_in, chunk_decay):
    for g in range(GROUPS):
        bg = b_ref[:, g * D_STATE:(g + 1) * D_STATE]
        xd = []
        for r in range(HEADS_PER_GROUP):
            h = g * HEADS_PER_GROUP + r
            x_h = xs_t_ref[0, h * HEADDIM:(h + 1) * HEADDIM].astype(jnp.float32)
            xd.append(_bf(x_h * scale_in[h:h + 1]))
        s_new = _dot(jnp.concatenate(xd, axis=0), bg)
        for r in range(HEADS_PER_GROUP):
            h = g * HEADS_PER_GROUP + r
            hr = slice(h * HEADDIM, (h + 1) * HEADDIM)
            h_ref[hr] = h_ref[hr] * chunk_decay[h:h + 1] + s_new[r * HEADDIM:(r + 1) * HEADDIM]


def _ssd_fwd_kernel(xs_t_ref, b_ref, dt_ref, cum_ref, alog_ref, h0_ref, *rest, n_chunks, with_y):
    if with_y:
        c_t_ref, dskip_ref, y_ref, hout_ref, h_ref = rest
    else:
        hout_ref, h_ref = rest
    c = pl.program_id(1)

    @pl.when(c == 0)
    def _():
        h_ref[...] = h0_ref[0]

    dt_f, cum_f = dt_ref[0, :HEADS], cum_ref[0, :HEADS]
    tot_f = cum_f[:, CHUNK - 1:CHUNK]
    scale_in = dt_f * jnp.exp(tot_f - cum_f)
    chunk_decay = jnp.exp(tot_f)

    if with_y:
        a_b = -jnp.exp(alog_ref[HEADS:])
        dt_b, cum_b = dt_ref[0, HEADS:], cum_ref[0, HEADS:]
        cumx_b = cum_b - dt_b * a_b
        col_terms = jnp.concatenate(
            [jnp.log(dt_f) - cum_f, jnp.log(dt_b) + cumx_b,
             jnp.zeros((CHUNK - 2 * HEADS, CHUNK), jnp.float32)], axis=0).T
        row_f = cum_f
        row_b = -cumx_b
        decay_out_f = jnp.exp(cum_f)
        src = jax.lax.broadcasted_iota(jnp.int32, (CHUNK, CHUNK), 0)
        dst = jax.lax.broadcasted_iota(jnp.int32, (CHUNK, CHUNK), 1)
        causal = src <= dst
        is_diag = src == dst
        y_parts = []
        for g in range(GROUPS):
            bg = b_ref[:, g * D_STATE:(g + 1) * D_STATE]
            cg_t = c_t_ref[0, g * D_STATE:(g + 1) * D_STATE]
            rows = slice(g * HEADS_PER_GROUP * HEADDIM, (g + 1) * HEADS_PER_GROUP * HEADDIM)
            g_t = _dot(bg, cg_t)
            g_diag = jnp.sum(jnp.where(is_diag, g_t, 0.0), axis=0, keepdims=True)
            y_off = _dot(_bf(h_ref[rows]), cg_t)
            for r in range(HEADS_PER_GROUP):
                h = g * HEADS_PER_GROUP + r
                hr = slice(h * HEADDIM, (h + 1) * HEADDIM)
                x_bf = xs_t_ref[0, hr]
                col_f = jnp.broadcast_to(col_terms[:, h:h + 1], (CHUNK, CHUNK))
                col_b = jnp.broadcast_to(col_terms[:, HEADS + h:HEADS + h + 1], (CHUNK, CHUNK))
                expo = jnp.where(causal, col_f + row_f[h:h + 1], col_b + row_b[h:h + 1])
                w_t = _bf(g_t * jnp.exp(expo))
                y_h = _dot(x_bf, w_t)
                y_h = y_h + y_off[r * HEADDIM:(r + 1) * HEADDIM] * decay_out_f[h:h + 1]
                skip = dskip_ref[hr] + g_diag * dt_b[h:h + 1]
                y_parts.append(y_h + skip * x_bf.astype(jnp.float32))
        y_ref[...] = jnp.concatenate(y_parts, axis=0).T

    _state_update(h_ref, xs_t_ref, b_ref, scale_in, chunk_decay)

    @pl.when(c == n_chunks - 1)
    def _():
        hout_ref[0] = h_ref[...]


def _ssd_bwd_kernel(xs_t_ref, b_ref, dt_ref, cum_ref, alog_ref, h0_ref, *rest, n_chunks, with_y):
    if with_y:
        c_t_ref, ypart_ref, y_ref, hout_ref, h_ref = rest
    else:
        hout_ref, h_ref = rest
    c = pl.program_id(1)

    @pl.when(c == 0)
    def _():
        h_ref[...] = h0_ref[0]

    a_b = -jnp.exp(alog_ref[HEADS:])
    dt_b, cum_b = dt_ref[0, HEADS:], cum_ref[0, HEADS:]
    tot_b = cum_b[:, CHUNK - 1:CHUNK]
    cumx_b = cum_b - dt_b * a_b
    scale_in = dt_b * jnp.exp(cumx_b)
    chunk_decay = jnp.exp(tot_b)

    if with_y:
        decay_out = jnp.exp(tot_b - cumx_b)
        y_parts = []
        for g in range(GROUPS):
            cg_t = c_t_ref[0, g * D_STATE:(g + 1) * D_STATE]
            rows = slice(g * HEADS_PER_GROUP * HEADDIM, (g + 1) * HEADS_PER_GROUP * HEADDIM)
            y_off = _dot(_bf(h_ref[rows]), cg_t)
            for r in range(HEADS_PER_GROUP):
                h = g * HEADS_PER_GROUP + r
                y_parts.append(y_off[r * HEADDIM:(r + 1) * HEADDIM] * decay_out[h:h + 1])
        y_ref[...] = ypart_ref[...] + jnp.concatenate(y_parts, axis=0).T

    _state_update(h_ref, xs_t_ref, b_ref, scale_in, chunk_decay)

    @pl.when(c == n_chunks - 1)
    def _():
        hout_ref[0] = h_ref[...]


def _ssd_sweep(reverse, xs_t, b_tok, dt, cum, alog_col, h0, bsz, n_chunks, c_t=None, dskip_b=None,
               y_part=None):
    with_y = c_t is not None
    n_tok = b_tok.shape[0]

    def chunk_of(b, c):
        return b * n_chunks + (n_chunks - 1 - c if reverse else c)

    tok = lambda b, c: (chunk_of(b, c), 0)
    chunk3 = lambda b, c: (chunk_of(b, c), 0, 0)
    const2 = lambda b, c: (0, 0)
    state3 = lambda b, c: (b, 0, 0)
    h_spec = pl.BlockSpec((1, W_SSD, D_STATE), state3)
    h_shape = jax.ShapeDtypeStruct((bsz, W_SSD, D_STATE), jnp.float32)
    head_spec = pl.BlockSpec((1, 2 * HEADS, CHUNK), chunk3)
    in_specs = [pl.BlockSpec((1, W_SSD, CHUNK), chunk3),
                pl.BlockSpec((CHUNK, GN), tok),
                head_spec, head_spec,
                pl.BlockSpec((2 * HEADS, 1), const2),
                h_spec]
    args = [xs_t, b_tok, dt, cum, alog_col, h0]
    y_spec = pl.BlockSpec((CHUNK, W_SSD), tok)
    if with_y:
        in_specs.append(pl.BlockSpec((1, GN, CHUNK), chunk3))
        args.append(c_t)
        if reverse:
            in_specs.append(y_spec)
            args.append(y_part)
        else:
            in_specs.append(pl.BlockSpec((W_SSD, CHUNK), const2))
            args.append(dskip_b)
        out_shape = [jax.ShapeDtypeStruct((n_tok, W_SSD), jnp.float32), h_shape]
        out_specs = [y_spec, h_spec]
    else:
        out_shape = [h_shape]
        out_specs = [h_spec]
    body = _ssd_bwd_kernel if reverse else _ssd_fwd_kernel
    name = ("ssd_bwd" if reverse else "ssd_fwd") + ("" if with_y else "_state")
    return pl.pallas_call(
        functools.partial(body, n_chunks=n_chunks, with_y=with_y),
        grid=(bsz, n_chunks),
        in_specs=in_specs,
        out_specs=out_specs,
        out_shape=out_shape,
        scratch_shapes=[pltpu.VMEM((W_SSD, D_STATE), jnp.float32)],
        compiler_params=pltpu.CompilerParams(
            dimension_semantics=("arbitrary", "arbitrary"), vmem_limit_bytes=VMEM_LIMIT),
        name=name,
    )(*args)


POOL_TILE_ROWS = 4
POOL_TILE = POOL_TILE_ROWS * GRID_W
POOL_PAD_ROWS = max(POOL_WINDOWS) // 2


def _pool_constants(window, n_rows):
    lo_off, hi_off = -(window // 2), window - window // 2
    col = np.arange(GRID_W)
    lo = np.clip(col + lo_off, 0, GRID_W)
    hi = np.clip(col + hi_off, 0, GRID_W)
    band = ((col[None, :] >= lo[:, None]) & (col[None, :] < hi[:, None])).astype(np.float32)
    band_tile = np.kron(np.eye(POOL_TILE_ROWS, dtype=np.float32), band)
    row = np.arange(n_rows)
    cnt_r = np.clip(row + hi_off, 0, n_rows) - np.clip(row + lo_off, 0, n_rows)
    inv = 1.0 / (cnt_r[:, None] * (hi - lo)[None, :]).astype(np.float64)
    inv = np.broadcast_to(inv.reshape(-1, 1), (n_rows * GRID_W, 128)).astype(np.float32)
    return jnp.asarray(band_tile, jnp.bfloat16), jnp.asarray(inv)


def _pool_kernel(u_ref, z_ref, band_ref, inv_ref, w_ref, scale_ref, o_ref, pad_ref, *, window,
                 n_rows):
    pad_tok = POOL_PAD_ROWS * GRID_W
    n_tok = n_rows * GRID_W
    zeros = jnp.zeros((pad_tok, POOL_GROUP_W), jnp.float32)
    pad_ref[0:pad_tok] = zeros
    pad_ref[pad_tok + n_tok:] = zeros
    pad_ref[pad_tok:pad_tok + n_tok] = u_ref[...]
    first = (POOL_PAD_ROWS - window // 2) * GRID_W

    def tile(t, carry):
        base = pl.multiple_of(t * POOL_TILE, POOL_TILE)
        rsum = pad_ref[pl.ds(base + first, POOL_TILE)]
        for k in range(1, window):
            rsum = rsum + pad_ref[pl.ds(base + first + k * GRID_W, POOL_TILE)]
        hi = _bf(rsum)
        lo = _bf(rsum - hi.astype(jnp.float32))
        band = band_ref[...]
        box = _dot(band, hi) + _dot(band, lo)
        inv = inv_ref[pl.ds(base, POOL_TILE)]
        mean = box * jnp.concatenate([inv, inv], axis=1)
        d = mean - u_ref[pl.ds(base, POOL_TILE)]
        y = _dot(_bf(d), w_ref[0]) * scale_ref[...]
        gate = z_ref[pl.ds(base, POOL_TILE)].astype(jnp.float32)
        o_ref[pl.ds(base, POOL_TILE)] = _bf(y * gate)
        return carry

    jax.lax.fori_loop(0, n_tok // POOL_TILE, tile, 0)


def _pool_group(u, gate, pool_w_bf, pool_scale, g, bsz, n_img_tok):
    window = POOL_WINDOWS[g]
    n_rows = n_img_tok // GRID_W
    band, inv = _pool_constants(window, n_rows)
    kern = functools.partial(_pool_kernel, window=window, n_rows=n_rows)
    img = pl.BlockSpec((n_img_tok, POOL_GROUP_W), lambda b: (b, 0))
    return pl.pallas_call(
        kern,
        grid=(bsz,),
        in_specs=[img, img,
                  pl.BlockSpec((POOL_TILE, POOL_TILE), lambda b: (0, 0)),
                  pl.BlockSpec((n_img_tok, 128), lambda b: (0, 0)),
                  pl.BlockSpec((1, POOL_GROUP_W, POOL_GROUP_W), lambda b: (g, 0, 0)),
                  pl.BlockSpec((1, POOL_GROUP_W), lambda b: (0, g))],
        out_specs=img,
        out_shape=jax.ShapeDtypeStruct((bsz * n_img_tok, POOL_GROUP_W), jnp.bfloat16),
        scratch_shapes=[pltpu.VMEM((n_img_tok + 2 * POOL_PAD_ROWS * GRID_W, POOL_GROUP_W),
                                   jnp.float32)],
        compiler_params=pltpu.CompilerParams(vmem_limit_bytes=VMEM_LIMIT),
        name=f"pool{window}",
    )(u, gate, band, inv, pool_w_bf, pool_scale)


def _out_kernel(yp0_ref, yp1_ref, yp2_ref, yp3_ref, ys_ref, zs_ref, x_ref, gate_ref, snw_ref,
                wout_ref, fnw_ref, o_ref):
    acc = None
    for g, yp_ref in enumerate((yp0_ref, yp1_ref, yp2_ref, yp3_ref)):
        part = _dot(yp_ref[...], wout_ref[g * POOL_GROUP_W:(g + 1) * POOL_GROUP_W])
        acc = part if acc is None else acc + part
    gw = W_SSD // GROUPS
    for g in range(GROUPS):
        cols = slice(g * gw, (g + 1) * gw)
        gated = ys_ref[:, cols] * zs_ref[:, cols].astype(jnp.float32)
        ms = jnp.mean(gated * gated, axis=-1, keepdims=True)
        yn = gated * jax.lax.rsqrt(ms + EPS) * snw_ref[:, cols]
        acc = acc + _dot(_bf(yn), wout_ref[W_POOL + g * gw:W_POOL + (g + 1) * gw])
    h = x_ref[...] + gate_ref[0] * acc
    ms = jnp.mean(h * h, axis=-1, keepdims=True)
    o_ref[...] = h * jax.lax.rsqrt(ms + EPS) * fnw_ref[...]


def _output(y_pool, y_ssd, z_ssd, x2d, gate, ssd_norm_w, w_out_bf, final_norm_w, rows_per_mod, tm):
    n_tok = x2d.shape[0]
    tiles_per_mod = rows_per_mod // tm
    tok = lambda i: (i, 0)
    const = lambda i: (0, 0)
    return pl.pallas_call(
        _out_kernel,
        grid=(n_tok // tm,),
        in_specs=[pl.BlockSpec((tm, POOL_GROUP_W), tok)] * 4 + [
            pl.BlockSpec((tm, W_SSD), tok),
            pl.BlockSpec((tm, W_SSD), tok),
            pl.BlockSpec((tm, D_MODEL), tok),
            pl.BlockSpec((1, 1, D_MODEL), lambda i: (i // tiles_per_mod, 0, 0)),
            pl.BlockSpec((1, W_SSD), const),
            pl.BlockSpec((W_POOL + W_SSD, D_MODEL), const),
            pl.BlockSpec((1, D_MODEL), const)],
        out_specs=pl.BlockSpec((tm, D_MODEL), tok),
        out_shape=jax.ShapeDtypeStruct((n_tok, D_MODEL), jnp.float32),
        compiler_params=pltpu.CompilerParams(vmem_limit_bytes=VMEM_LIMIT),
        name="out",
    )(*y_pool, y_ssd, z_ssd, x2d, gate, ssd_norm_w.reshape(1, W_SSD), w_out_bf,
      final_norm_w.reshape(1, D_MODEL))


def kernel(x, c, ctx, c_ctx, norm_w, w_ada, b_ada, w_in, conv_w, conv_b, a_log, dt_bias, d_skip,
           ssd_norm_w, pool_w, pool_scale, w_out, final_norm_w):
    bsz, seq, _ = x.shape
    ctx_len = ctx.shape[1]
    depth = norm_w.shape[0]
    assert depth == 1, "single-layer block: the context stream update is never consumed"
    assert seq % 512 == 0 and ctx_len % CHUNK == 0 and seq % GRID_W == 0

    mod_rows = -(-(bsz + 1) // SUBLANES) * SUBLANES
    cond = jnp.concatenate([c, c_ctx[None], jnp.zeros((mod_rows - bsz - 1, D_MODEL), c.dtype)])
    mod = _modulation(cond, w_ada[0], b_ada[0])
    shift = mod[:, :D_MODEL].reshape(mod_rows, 1, D_MODEL)
    scale = mod[:, D_MODEL:2 * D_MODEL].reshape(mod_rows, 1, D_MODEL)
    gate = mod[:, 2 * D_MODEL:].reshape(mod_rows, 1, D_MODEL)

    w_in_bf = _bf(jnp.pad(w_in[0], ((0, 0), (0, DT_PAD - 2 * HEADS))))
    alog_col = a_log[0].reshape(2 * HEADS, 1)
    bias_col = dt_bias[0].reshape(2 * HEADS, 1)
    dskip_b = jnp.broadcast_to(jnp.repeat(d_skip[0], HEADDIM)[:, None], (W_SSD, CHUNK))
    conv_b2 = conv_b[0].reshape(1, CONV_DIM)
    zero_state = jnp.zeros((bsz, W_SSD, D_STATE), jnp.float32)

    ctx2d = ctx.reshape(bsz * ctx_len, D_MODEL)
    xs_t_c, b_c, dt_c, cum_c = _projection(
        ctx2d, norm_w[0], shift[bsz:bsz + 1], scale[bsz:bsz + 1], w_in_bf, conv_w[0], conv_b2,
        alog_col, bias_col, ctx_len, ctx_len, full=False)
    nc_ctx = ctx_len // CHUNK
    (h_fwd,) = _ssd_sweep(False, xs_t_c, b_c, dt_c, cum_c, alog_col, zero_state, bsz, nc_ctx)
    (h_bwd,) = _ssd_sweep(True, xs_t_c, b_c, dt_c, cum_c, alog_col, zero_state, bsz, nc_ctx)

    x2d = x.reshape(bsz * seq, D_MODEL)
    outs = _projection(x2d, norm_w[0], shift, scale, w_in_bf, conv_w[0], conv_b2, alog_col,
                       bias_col, seq, 512, full=True)
    u_pool, gate_pool = outs[:N_POOL_GROUPS], outs[N_POOL_GROUPS:2 * N_POOL_GROUPS]
    gate_ssd, xs_t, b_tok, c_t, dt, cum = outs[2 * N_POOL_GROUPS:]
    nc = seq // CHUNK
    y_part, _ = _ssd_sweep(False, xs_t, b_tok, dt, cum, alog_col, h_fwd, bsz, nc, c_t=c_t,
                           dskip_b=dskip_b)
    y_ssd, _ = _ssd_sweep(True, xs_t, b_tok, dt, cum, alog_col, h_bwd, bsz, nc, c_t=c_t,
                          y_part=y_part)

    pool_w_bf = _bf(pool_w[0])
    y_pool = [_pool_group(u_pool[g], gate_pool[g], pool_w_bf, pool_scale, g, bsz, seq)
              for g in range(N_POOL_GROUPS)]
    out = _output(y_pool, y_ssd, gate_ssd, x2d, gate, ssd_norm_w[0], _bf(w_out[0]), final_norm_w,
                  seq, 256)
    return out.reshape(bsz, seq, D_MODEL)
```

```python
import functools

import numpy as np
import jax
import jax.numpy as jnp
from jax.experimental import pallas as pl
from jax.experimental.pallas import tpu as pltpu

D_MODEL = 1024
GRID_W = 64
W_POOL = 1024
W_SSD = 1024
POOL_WINDOWS = (2, 4, 8, 16)
N_POOL_GROUPS = len(POOL_WINDOWS)
POOL_GROUP_W = 256
HEADDIM = 64
HEADS = 16
GROUPS = 4
HEADS_PER_GROUP = 4
D_STATE = 128
D_CONV = 4
CONV_LEFT = 2
CHUNK = 128
GN = GROUPS * D_STATE
CONV_DIM = W_SSD + 2 * GN
OFF_POOL_Z = W_POOL
OFF_SSD_Z = 2 * W_POOL
OFF_XBC = 2 * W_POOL + W_SSD
OFF_DT = OFF_XBC + CONV_DIM
DT_PAD = 128
EPS = 1e-6
SUBLANES = 8
LANES = 128
IL_GROUPS = CHUNK // SUBLANES
CONV_SEG = 512
VMEM_LIMIT = 56 * 1024 * 1024


def _silu(v):
    h = 0.5 * v
    return h + h * jnp.tanh(h)


def _softplus(v):
    return jnp.maximum(v, 0.0) + jnp.log1p(jnp.exp(-jnp.abs(v)))


def _bf(v):
    return v.astype(jnp.bfloat16)


def _dot(a, b):
    return jnp.dot(a, b, preferred_element_type=jnp.float32)


def _mod_kernel(c_ref, w_ref, b_ref, o_ref):
    s = _silu(c_ref[...])
    o_ref[...] = jnp.dot(s, w_ref[...], preferred_element_type=jnp.float32,
                         precision=jax.lax.Precision.HIGHEST) + b_ref[...]


def _modulation(cond_rows, w_ada, b_ada):
    rows = cond_rows.shape[0]
    n_out = w_ada.shape[1]
    tn = 1024
    return pl.pallas_call(
        _mod_kernel,
        grid=(n_out // tn,),
        in_specs=[pl.BlockSpec((rows, D_MODEL), lambda j: (0, 0)),
                  pl.BlockSpec((D_MODEL, tn), lambda j: (0, j)),
                  pl.BlockSpec((1, tn), lambda j: (0, j))],
        out_specs=pl.BlockSpec((rows, tn), lambda j: (0, j)),
        out_shape=jax.ShapeDtypeStruct((rows, n_out), jnp.float32),
        compiler_params=pltpu.CompilerParams(vmem_limit_bytes=VMEM_LIMIT),
        name="mod",
    )(cond_rows, w_ada, b_ada.reshape(1, n_out))


def _lane_cumsum(v):
    lane = jax.lax.broadcasted_iota(jnp.int32, v.shape, 1)
    shift = 1
    while shift < CHUNK:
        v = v + jnp.where(lane >= shift, pltpu.roll(v, shift, 1), 0.0)
        shift *= 2
    return v


def _proj_kernel(x_ref, xp_ref, xn_ref, nw_ref, sh_ref, sc_ref, w_ref, cw_ref, cb_ref, alog_ref,
                 bias_ref, *rest, tm, tiles_per_seq, full):
    if full:
        (u0, u1, u2, u3, zp0, zp1, zp2, zp3, zs_ref, xs_t_ref, b_ref, c_t_ref, dt_ref, cum_ref,
         pe_ref, xc_ref, mn_ref) = rest
        u_refs, zp_refs = (u0, u1, u2, u3), (zp0, zp1, zp2, zp3)
    else:
        xs_t_ref, b_ref, dt_ref, cum_ref, pe_ref, xc_ref, mn_ref = rest
    i = pl.program_id(0)
    pos = i % tiles_per_seq
    has_prev = pos > 0
    has_next = pos < tiles_per_seq - 1
    n_chunks = tm // CHUNK

    def modulated(v):
        ms = jnp.mean(v * v, axis=-1, keepdims=True)
        y = v * jax.lax.rsqrt(ms + EPS) * nw_ref[...]
        return y * (1.0 + sc_ref[0]) + sh_ref[0]

    m_tok = modulated(x_ref[...])
    hm = _bf(m_tok)
    for t in range(D_MODEL // LANES):
        mn_ref[t] = m_tok[:, t * LANES:(t + 1) * LANES]

    if full:
        for g in range(N_POOL_GROUPS):
            cols = slice(g * POOL_GROUP_W, (g + 1) * POOL_GROUP_W)
            u_refs[g][...] = _dot(hm, w_ref[:, cols])

    p_dt = _dot(hm, w_ref[:, OFF_DT:OFF_DT + DT_PAD])
    a_col = -jnp.exp(alog_ref[...])
    for q in range(n_chunks):
        dt = _softplus(p_dt[q * CHUNK:(q + 1) * CHUNK].T[:2 * HEADS] + bias_ref[...])
        dt_ref[q] = dt
        cum_ref[q] = _lane_cumsum(dt * a_col)

    rows = [jnp.concatenate([mn_ref[t, pl.ds(q * CHUNK + b, SUBLANES, stride=IL_GROUPS), :]
                             for t in range(D_MODEL // LANES)], axis=1)
            for q in range(n_chunks) for b in range(IL_GROUPS)]
    halo = [jnp.where(has_prev, modulated(xp_ref[...]), 0.0),
            jnp.where(has_next, modulated(xn_ref[...]), 0.0)]
    hm_il = _bf(jnp.concatenate(halo + rows, axis=0))
    seg = pe_ref.shape[1]
    sub = jax.lax.broadcasted_iota(jnp.int32, (SUBLANES, seg), 0)

    def group(q, b):
        lo = 2 * SUBLANES + q * CHUNK + b * SUBLANES
        return pe_ref[lo:lo + SUBLANES]

    def shifted(q, b, delta):
        bb = b + delta
        if 0 <= bb < IL_GROUPS:
            return group(q, bb)
        if bb < 0:
            bb += IL_GROUPS
            if q == 0:
                first = pe_ref[bb - SUBLANES:bb - SUBLANES + 1]
            else:
                first = pe_ref[2 * SUBLANES + (q - 1) * CHUNK + bb * SUBLANES + SUBLANES - 1:
                               2 * SUBLANES + (q - 1) * CHUNK + (bb + 1) * SUBLANES]
            return jnp.where(sub == 0, first, pltpu.roll(group(q, bb), 1, 0))
        bb -= IL_GROUPS
        if q == n_chunks - 1:
            last = pe_ref[SUBLANES + bb:SUBLANES + bb + 1]
        else:
            nxt = 2 * SUBLANES + (q + 1) * CHUNK + bb * SUBLANES
            last = pe_ref[nxt:nxt + 1]
        return jnp.where(sub == SUBLANES - 1, last, pltpu.roll(group(q, bb), SUBLANES - 1, 0))

    for j in range(0, CONV_DIM, seg):
        is_x = j < W_SSD
        is_b = W_SSD <= j < W_SSD + GN
        if not full and not (is_x or is_b):
            continue
        pe_ref[...] = _dot(hm_il, w_ref[:, OFF_XBC + j:OFF_XBC + j + seg])
        taps = [cw_ref[k:k + 1, j:j + seg] for k in range(D_CONV)]
        bias = cb_ref[:, j:j + seg]
        for q in range(n_chunks):
            for b in range(IL_GROUPS):
                acc = bias
                for k in range(D_CONV):
                    acc = acc + shifted(q, b, k - CONV_LEFT) * taps[k]
                lo = q * CHUNK + b * SUBLANES
                act = _silu(acc)
                for t in range(seg // LANES):
                    xc_ref[t, lo:lo + SUBLANES] = act[:, t * LANES:(t + 1) * LANES]
        for q in range(n_chunks):
            xc = jnp.concatenate(
                [jnp.concatenate(
                    [xc_ref[t, pl.ds(q * CHUNK + (m % 2) * (CHUNK // 2) + m // 2, SUBLANES,
                                     stride=SUBLANES), :] for t in range(seg // LANES)], axis=1)
                 for m in range(IL_GROUPS)], axis=0)
            if is_b:
                b_ref[q * CHUNK:(q + 1) * CHUNK, j - W_SSD:j - W_SSD + seg] = _bf(xc)
            else:
                dst, off = (xs_t_ref, j) if is_x else (c_t_ref, j - W_SSD - GN)
                dst[q, off:off + seg] = _bf(xc.T)

    if full:
        for g in range(N_POOL_GROUPS):
            zcols = slice(OFF_POOL_Z + g * POOL_GROUP_W, OFF_POOL_Z + (g + 1) * POOL_GROUP_W)
            zp_refs[g][...] = _bf(_silu(_dot(hm, w_ref[:, zcols])))
        for j in range(0, W_SSD, CONV_SEG):
            zcols = slice(OFF_SSD_Z + j, OFF_SSD_Z + j + CONV_SEG)
            zs_ref[:, j:j + CONV_SEG] = _bf(_silu(_dot(hm, w_ref[:, zcols])))


def _projection(x2d, norm_w, shift, scale, w_bf, conv_w, conv_b, alog_col, bias_col, seq_len, tm,
                full):
    n_tok = x2d.shape[0]
    tiles_per_seq = seq_len // tm
    n_mod = shift.shape[0]
    nct = n_tok // CHUNK
    per = tm // SUBLANES
    last_halo = n_tok // SUBLANES - 1
    kern = functools.partial(_proj_kernel, tm=tm, tiles_per_seq=tiles_per_seq, full=full)
    mod_map = (lambda i: (i // tiles_per_seq, 0, 0)) if n_mod > 1 else (lambda i: (0, 0, 0))
    mod_spec = pl.BlockSpec((1, 1, D_MODEL), mod_map)
    const = lambda i: (0, 0)
    tok = lambda i: (i, 0)
    chunk3 = lambda i: (i, 0, 0)
    q = tm // CHUNK
    xs_t = (jax.ShapeDtypeStruct((nct, W_SSD, CHUNK), jnp.bfloat16),
            pl.BlockSpec((q, W_SSD, CHUNK), chunk3))
    b_tok = (jax.ShapeDtypeStruct((n_tok, GN), jnp.bfloat16), pl.BlockSpec((tm, GN), tok))
    c_t = (jax.ShapeDtypeStruct((nct, GN, CHUNK), jnp.bfloat16), pl.BlockSpec((q, GN, CHUNK), chunk3))
    dt = (jax.ShapeDtypeStruct((nct, 2 * HEADS, CHUNK), jnp.float32),
          pl.BlockSpec((q, 2 * HEADS, CHUNK), chunk3))
    if full:
        u = (jax.ShapeDtypeStruct((n_tok, POOL_GROUP_W), jnp.float32),
             pl.BlockSpec((tm, POOL_GROUP_W), tok))
        zp = (jax.ShapeDtypeStruct((n_tok, POOL_GROUP_W), jnp.bfloat16),
              pl.BlockSpec((tm, POOL_GROUP_W), tok))
        zs = (jax.ShapeDtypeStruct((n_tok, W_SSD), jnp.bfloat16), pl.BlockSpec((tm, W_SSD), tok))
        outs = [u] * N_POOL_GROUPS + [zp] * N_POOL_GROUPS + [zs, xs_t, b_tok, c_t, dt, dt]
    else:
        outs = [xs_t, b_tok, dt, dt]
    return pl.pallas_call(
        kern,
        grid=(n_tok // tm,),
        in_specs=[pl.BlockSpec((tm, D_MODEL), tok),
                  pl.BlockSpec((SUBLANES, D_MODEL), lambda i: (jnp.maximum(i * per - 1, 0), 0)),
                  pl.BlockSpec((SUBLANES, D_MODEL), lambda i: (jnp.minimum((i + 1) * per, last_halo), 0)),
                  pl.BlockSpec((1, D_MODEL), const),
                  mod_spec, mod_spec,
                  pl.BlockSpec(w_bf.shape, const),
                  pl.BlockSpec((D_CONV, CONV_DIM), const),
                  pl.BlockSpec((1, CONV_DIM), const),
                  pl.BlockSpec((2 * HEADS, 1), const),
                  pl.BlockSpec((2 * HEADS, 1), const)],
        out_specs=[o[1] for o in outs],
        out_shape=[o[0] for o in outs],
        scratch_shapes=[pltpu.VMEM((tm + 2 * SUBLANES, CONV_SEG), jnp.float32),
                        pltpu.VMEM((CONV_SEG // LANES, tm, LANES), jnp.float32),
                        pltpu.VMEM((D_MODEL // LANES, tm, LANES), jnp.float32)],
        compiler_params=pltpu.CompilerParams(vmem_limit_bytes=VMEM_LIMIT),
        name="proj" if full else "proj_ctx",
    )(x2d, x2d, x2d, norm_w.reshape(1, D_MODEL), shift, scale, w_bf, conv_w, conv_b, alog_col,
      bias_col)


def _state_update(h_ref, xs_t_ref, b_ref, scale_in, chunk_decay):
    for g in range(GROUPS):
        bg = b_ref[:, g * D_STATE:(g + 1) * D_STATE]
        xd = []
        for r in range(HEADS_PER_GROUP):
            h = g * HEADS_PER_GROUP + r
            x_h = xs_t_ref[0, h * HEADDIM:(h + 1) * HEADDIM].astype(jnp.float32)
            xd.append(_bf(x_h * scale_in[h:h + 1]))
        s_new = _dot(jnp.concatenate(xd, axis=0), bg)
        for r in range(HEADS_PER_GROUP):
            h = g * HEADS_PER_GROUP + r
            hr = slice(h * HEADDIM, (h + 1) * HEADDIM)
            h_ref[hr] = h_ref[hr] * chunk_decay[h:h + 1] + s_new[r * HEADDIM:(r + 1) * HEADDIM]


def _ssd_fwd_kernel(xs_t_ref, b_ref, dt_ref, cum_ref, alog_ref, h0_ref, *rest, n_chunks, with_y):
    if with_y:
        c_t_ref, dskip_ref, y_ref, hout_ref, h_ref = rest
    else:
        hout_ref, h_ref = rest
    c = pl.program_id(1)

    @pl.when(c == 0)
    def _():
        h_ref[...] = h0_ref[0]

    dt_f, cum_f = dt_ref[0, :HEADS], cum_ref[0, :HEADS]
    tot_f = cum_f[:, CHUNK - 1:CHUNK]
    scale_in = dt_f * jnp.exp(tot_f - cum_f)
    chunk_decay = jnp.exp(tot_f)

    if with_y:
        a_b = -jnp.exp(alog_ref[HEADS:])
        dt_b, cum_b = dt_ref[0, HEADS:], cum_ref[0, HEADS:]
        cumx_b = cum_b - dt_b * a_b
        col_terms = jnp.concatenate(
            [jnp.log(dt_f) - cum_f, jnp.log(dt_b) + cumx_b,
             jnp.zeros((CHUNK - 2 * HEADS, CHUNK), jnp.float32)], axis=0).T
        row_f = cum_f
        row_b = -cumx_b
        decay_out_f = jnp.exp(cum_f)
        src = jax.lax.broadcasted_iota(jnp.int32, (CHUNK, CHUNK), 0)
        dst = jax.lax.broadcasted_iota(jnp.int32, (CHUNK, CHUNK), 1)
        causal = src <= dst
        is_diag = src == dst
        y_parts = []
        for g in range(GROUPS):
            bg = b_ref[:, g * D_STATE:(g + 1) * D_STATE]
            cg_t = c_t_ref[0, g * D_STATE:(g + 1) * D_STATE]
            rows = slice(g * HEADS_PER_GROUP * HEADDIM, (g + 1) * HEADS_PER_GROUP * HEADDIM)
            g_t = _dot(bg, cg_t)
            g_diag = jnp.sum(jnp.where(is_diag, g_t, 0.0), axis=0, keepdims=True)
            y_off = _dot(_bf(h_ref[rows]), cg_t)
            for r in range(HEADS_PER_GROUP):
                h = g * HEADS_PER_GROUP + r
                hr = slice(h * HEADDIM, (h + 1) * HEADDIM)
                x_bf = xs_t_ref[0, hr]
                col_f = jnp.broadcast_to(col_terms[:, h:h + 1], (CHUNK, CHUNK))
                col_b = jnp.broadcast_to(col_terms[:, HEADS + h:HEADS + h + 1], (CHUNK, CHUNK))
                expo = jnp.where(causal, col_f + row_f[h:h + 1], col_b + row_b[h:h + 1])
                w_t = _bf(g_t * jnp.exp(expo))
                y_h = _dot(x_bf, w_t)
                y_h = y_h + y_off[r * HEADDIM:(r + 1) * HEADDIM] * decay_out_f[h:h + 1]
                skip = dskip_ref[hr] + g_diag * dt_b[h:h + 1]
                y_parts.append(y_h + skip * x_bf.astype(jnp.float32))
        y_ref[...] = jnp.concatenate(y_parts, axis=0).T

    _state_update(h_ref, xs_t_ref, b_ref, scale_in, chunk_decay)

    @pl.when(c == n_chunks - 1)
    def _():
        hout_ref[0] = h_ref[...]


def _ssd_bwd_kernel(xs_t_ref, b_ref, dt_ref, cum_ref, alog_ref, h0_ref, *rest, n_chunks, with_y):
    if with_y:
        c_t_ref, ypart_ref, y_ref, hout_ref, h_ref = rest
    else:
        hout_ref, h_ref = rest
    c = pl.program_id(1)

    @pl.when(c == 0)
    def _():
        h_ref[...] = h0_ref[0]

    a_b = -jnp.exp(alog_ref[HEADS:])
    dt_b, cum_b = dt_ref[0, HEADS:], cum_ref[0, HEADS:]
    tot_b = cum_b[:, CHUNK - 1:CHUNK]
    cumx_b = cum_b - dt_b * a_b
    scale_in = dt_b * jnp.exp(cumx_b)
    chunk_decay = jnp.exp(tot_b)

    if with_y:
        decay_out = jnp.exp(tot_b - cumx_b)
        y_parts = []
        for g in range(GROUPS):
            cg_t = c_t_ref[0, g * D_STATE:(g + 1) * D_STATE]
            rows = slice(g * HEADS_PER_GROUP * HEADDIM, (g + 1) * HEADS_PER_GROUP * HEADDIM)
            y_off = _dot(_bf(h_ref[rows]), cg_t)
            for r in range(HEADS_PER_GROUP):
                h = g * HEADS_PER_GROUP + r
                y_parts.append(y_off[r * HEADDIM:(r + 1) * HEADDIM] * decay_out[h:h + 1])
        y_ref[...] = ypart_ref[...] + jnp.concatenate(y_parts, axis=0).T

    _state_update(h_ref, xs_t_ref, b_ref, scale_in, chunk_decay)

    @pl.when(c == n_chunks - 1)
    def _():
        hout_ref[0] = h_ref[...]


def _ssd_sweep(reverse, xs_t, b_tok, dt, cum, alog_col, h0, bsz, n_chunks, c_t=None, dskip_b=None,
               y_part=None):
    with_y = c_t is not None
    n_tok = b_tok.shape[0]

    def chunk_of(b, c):
        return b * n_chunks + (n_chunks - 1 - c if reverse else c)

    tok = lambda b, c: (chunk_of(b, c), 0)
    chunk3 = lambda b, c: (chunk_of(b, c), 0, 0)
    const2 = lambda b, c: (0, 0)
    state3 = lambda b, c: (b, 0, 0)
    h_spec = pl.BlockSpec((1, W_SSD, D_STATE), state3)
    h_shape = jax.ShapeDtypeStruct((bsz, W_SSD, D_STATE), jnp.float32)
    head_spec = pl.BlockSpec((1, 2 * HEADS, CHUNK), chunk3)
    in_specs = [pl.BlockSpec((1, W_SSD, CHUNK), chunk3),
                pl.BlockSpec((CHUNK, GN), tok),
                head_spec, head_spec,
                pl.BlockSpec((2 * HEADS, 1), const2),
                h_spec]
    args = [xs_t, b_tok, dt, cum, alog_col, h0]
    y_spec = pl.BlockSpec((CHUNK, W_SSD), tok)
    if with_y:
        in_specs.append(pl.BlockSpec((1, GN, CHUNK), chunk3))
        args.append(c_t)
        if reverse:
            in_specs.append(y_spec)
            args.append(y_part)
        else:
            in_specs.append(pl.BlockSpec((W_SSD, CHUNK), const2))
            args.append(dskip_b)
        out_shape = [jax.ShapeDtypeStruct((n_tok, W_SSD), jnp.float32), h_shape]
        out_specs = [y_spec, h_spec]
    else:
        out_shape = [h_shape]
        out_specs = [h_spec]
    body = _ssd_bwd_kernel if reverse else _ssd_fwd_kernel
    name = ("ssd_bwd" if reverse else "ssd_fwd") + ("" if with_y else "_state")
    return pl.pallas_call(
        functools.partial(body, n_chunks=n_chunks, with_y=with_y),
        grid=(bsz, n_chunks),
        in_specs=in_specs,
        out_specs=out_specs,
        out_shape=out_shape,
        scratch_shapes=[pltpu.VMEM((W_SSD, D_STATE), jnp.float32)],
        compiler_params=pltpu.CompilerParams(
            dimension_semantics=("arbitrary", "arbitrary"), vmem_limit_bytes=VMEM_LIMIT),
        name=name,
    )(*args)


POOL_TILE_ROWS = 4
POOL_TILE = POOL_TILE_ROWS * GRID_W
POOL_PAD_ROWS = max(POOL_WINDOWS) // 2


def _pool_constants(window, n_rows):
    lo_off, hi_off = -(window // 2), window - window // 2
    col = np.arange(GRID_W)
    lo = np.clip(col + lo_off, 0, GRID_W)
    hi = np.clip(col + hi_off, 0, GRID_W)
    band = ((col[None, :] >= lo[:, None]) & (col[None, :] < hi[:, None])).astype(np.float32)
    band_tile = np.kron(np.eye(POOL_TILE_ROWS, dtype=np.float32), band)
    row = np.arange(n_rows)
    cnt_r = np.clip(row + hi_off, 0, n_rows) - np.clip(row + lo_off, 0, n_rows)
    inv = 1.0 / (cnt_r[:, None] * (hi - lo)[None, :]).astype(np.float64)
    inv = np.broadcast_to(inv.reshape(-1, 1), (n_rows * GRID_W, 128)).astype(np.float32)
    return jnp.asarray(band_tile, jnp.bfloat16), jnp.asarray(inv)


def _pool_kernel(u_ref, z_ref, band_ref, inv_ref, w_ref, scale_ref, o_ref, pad_ref, *, window,
                 n_rows):
    pad_tok = POOL_PAD_ROWS * GRID_W
    n_tok = n_rows * GRID_W
    zeros = jnp.zeros((pad_tok, POOL_GROUP_W), jnp.float32)
    pad_ref[0:pad_tok] = zeros
    pad_ref[pad_tok + n_tok:] = zeros
    pad_ref[pad_tok:pad_tok + n_tok] = u_ref[...]
    first = (POOL_PAD_ROWS - window // 2) * GRID_W

    def tile(t, carry):
        base = pl.multiple_of(t * POOL_TILE, POOL_TILE)
        rsum = pad_ref[pl.ds(base + first, POOL_TILE)]
        for k in range(1, window):
            rsum = rsum + pad_ref[pl.ds(base + first + k * GRID_W, POOL_TILE)]
        hi = _bf(rsum)
        lo = _bf(rsum - hi.astype(jnp.float32))
        band = band_ref[...]
        box = _dot(band, hi) + _dot(band, lo)
        inv = inv_ref[pl.ds(base, POOL_TILE)]
        mean = box * jnp.concatenate([inv, inv], axis=1)
        d = mean - u_ref[pl.ds(base, POOL_TILE)]
        y = _dot(_bf(d), w_ref[0]) * scale_ref[...]
        gate = z_ref[pl.ds(base, POOL_TILE)].astype(jnp.float32)
        o_ref[pl.ds(base, POOL_TILE)] = _bf(y * gate)
        return carry

    jax.lax.fori_loop(0, n_tok // POOL_TILE, tile, 0)


def _pool_group(u, gate, pool_w_bf, pool_scale, g, bsz, n_img_tok):
    window = POOL_WINDOWS[g]
    n_rows = n_img_tok // GRID_W
    band, inv = _pool_constants(window, n_rows)
    kern = functools.partial(_pool_kernel, window=window, n_rows=n_rows)
    img = pl.BlockSpec((n_img_tok, POOL_GROUP_W), lambda b: (b, 0))
    return pl.pallas_call(
        kern,
        grid=(bsz,),
        in_specs=[img, img,
                  pl.BlockSpec((POOL_TILE, POOL_TILE), lambda b: (0, 0)),
                  pl.BlockSpec((n_img_tok, 128), lambda b: (0, 0)),
                  pl.BlockSpec((1, POOL_GROUP_W, POOL_GROUP_W), lambda b: (g, 0, 0)),
                  pl.BlockSpec((1, POOL_GROUP_W), lambda b: (0, g))],
        out_specs=img,
        out_shape=jax.ShapeDtypeStruct((bsz * n_img_tok, POOL_GROUP_W), jnp.bfloat16),
        scratch_shapes=[pltpu.VMEM((n_img_tok + 2 * POOL_PAD_ROWS * GRID_W, POOL_GROUP_W),
                                   jnp.float32)],
        compiler_params=pltpu.CompilerParams(vmem_limit_bytes=VMEM_LIMIT),
        name=f"pool{window}",
    )(u, gate, band, inv, pool_w_bf, pool_scale)


def _out_kernel(yp0_ref, yp1_ref, yp2_ref, yp3_ref, ys_ref, zs_ref, x_ref, gate_ref, snw_ref,
                wout_ref, fnw_ref, o_ref):
    acc = None
    for g, yp_ref in enumerate((yp0_ref, yp1_ref, yp2_ref, yp3_ref)):
        part = _dot(yp_ref[...], wout_ref[g * POOL_GROUP_W:(g + 1) * POOL_GROUP_W])
        acc = part if acc is None else acc + part
    gw = W_SSD // GROUPS
    for g in range(GROUPS):
        cols = slice(g * gw, (g + 1) * gw)
        gated = ys_ref[:, cols] * zs_ref[:, cols].astype(jnp.float32)
        ms = jnp.mean(gated * gated, axis=-1, keepdims=True)
        yn = gated * jax.lax.rsqrt(ms + EPS) * snw_ref[:, cols]
        acc = acc + _dot(_bf(yn), wout_ref[W_POOL + g * gw:W_POOL + (g + 1) * gw])
    h = x_ref[...] + gate_ref[0] * acc
    ms = jnp.mean(h * h, axis=-1, keepdims=True)
    o_ref[...] = h * jax.lax.rsqrt(ms + EPS) * fnw_ref[...]


def _output(y_pool, y_ssd, z_ssd, x2d, gate, ssd_norm_w, w_out_bf, final_norm_w, rows_per_mod, tm):
    n_tok = x2d.shape[0]
    tiles_per_mod = rows_per_mod // tm
    tok = lambda i: (i, 0)
    const = lambda i: (0, 0)
    return pl.pallas_call(
        _out_kernel,
        grid=(n_tok // tm,),
        in_specs=[pl.BlockSpec((tm, POOL_GROUP_W), tok)] * 4 + [
            pl.BlockSpec((tm, W_SSD), tok),
            pl.BlockSpec((tm, W_SSD), tok),
            pl.BlockSpec((tm, D_MODEL), tok),
            pl.BlockSpec((1, 1, D_MODEL), lambda i: (i // tiles_per_mod, 0, 0)),
            pl.BlockSpec((1, W_SSD), const),
            pl.BlockSpec((W_POOL + W_SSD, D_MODEL), const),
            pl.BlockSpec((1, D_MODEL), const)],
        out_specs=pl.BlockSpec((tm, D_MODEL), tok),
        out_shape=jax.ShapeDtypeStruct((n_tok, D_MODEL), jnp.float32),
        compiler_params=pltpu.CompilerParams(vmem_limit_bytes=VMEM_LIMIT),
        name="out",
    )(*y_pool, y_ssd, z_ssd, x2d, gate, ssd_norm_w.reshape(1, W_SSD), w_out_bf,
      final_norm_w.reshape(1, D_MODEL))


def kernel(x, c, ctx, c_ctx, norm_w, w_ada, b_ada, w_in, conv_w, conv_b, a_log, dt_bias, d_skip,
           ssd_norm_w, pool_w, pool_scale, w_out, final_norm_w):
    bsz, seq, _ = x.shape
    ctx_len = ctx.shape[1]
    depth = norm_w.shape[0]
    assert depth == 1, "single-layer block: the context stream update is never consumed"
    assert seq % 512 == 0 and ctx_len % CHUNK == 0 and seq % GRID_W == 0

    mod_rows = -(-(bsz + 1) // SUBLANES) * SUBLANES
    cond = jnp.concatenate([c, c_ctx[None], jnp.zeros((mod_rows - bsz - 1, D_MODEL), c.dtype)])
    mod = _modulation(cond, w_ada[0], b_ada[0])
    shift = mod[:, :D_MODEL].reshape(mod_rows, 1, D_MODEL)
    scale = mod[:, D_MODEL:2 * D_MODEL].reshape(mod_rows, 1, D_MODEL)
    gate = mod[:, 2 * D_MODEL:].reshape(mod_rows, 1, D_MODEL)

    w_in_bf = _bf(jnp.pad(w_in[0], ((0, 0), (0, DT_PAD - 2 * HEADS))))
    alog_col = a_log[0].reshape(2 * HEADS, 1)
    bias_col = dt_bias[0].reshape(2 * HEADS, 1)
    dskip_b = jnp.broadcast_to(jnp.repeat(d_skip[0], HEADDIM)[:, None], (W_SSD, CHUNK))
    conv_b2 = conv_b[0].reshape(1, CONV_DIM)
    zero_state = jnp.zeros((bsz, W_SSD, D_STATE), jnp.float32)

    ctx2d = ctx.reshape(bsz * ctx_len, D_MODEL)
    xs_t_c, b_c, dt_c, cum_c = _projection(
        ctx2d, norm_w[0], shift[bsz:bsz + 1], scale[bsz:bsz + 1], w_in_bf, conv_w[0], conv_b2,
        alog_col, bias_col, ctx_len, ctx_len, full=False)
    nc_ctx = ctx_len // CHUNK
    (h_fwd,) = _ssd_sweep(False, xs_t_c, b_c, dt_c, cum_c, alog_col, zero_state, bsz, nc_ctx)
    (h_bwd,) = _ssd_sweep(True, xs_t_c, b_c, dt_c, cum_c, alog_col, zero_state, bsz, nc_ctx)

    x2d = x.reshape(bsz * seq, D_MODEL)
    outs = _projection(x2d, norm_w[0], shift, scale, w_in_bf, conv_w[0], conv_b2, alog_col,
                       bias_col, seq, 512, full=True)
    u_pool, gate_pool = outs[:N_POOL_GROUPS], outs[N_POOL_GROUPS:2 * N_POOL_GROUPS]
    gate_ssd, xs_t, b_tok, c_t, dt, cum = outs[2 * N_POOL_GROUPS:]
    nc = seq // CHUNK
    y_part, _ = _ssd_sweep(False, xs_t, b_tok, dt, cum, alog_col, h_fwd, bsz, nc, c_t=c_t,
                           dskip_b=dskip_b)
    y_ssd, _ = _ssd_sweep(True, xs_t, b_tok, dt, cum, alog_col, h_bwd, bsz, nc, c_t=c_t,
                          y_part=y_part)

    pool_w_bf = _bf(pool_w[0])
    y_pool = [_pool_group(u_pool[g], gate_pool[g], pool_w_bf, pool_scale, g, bsz, seq)
              for g in range(N_POOL_GROUPS)]
    out = _output(y_pool, y_ssd, gate_ssd, x2d, gate, ssd_norm_w[0], _bf(w_out[0]), final_norm_w,
                  seq, 256)
    return out.reshape(bsz, seq, D_MODEL)
```

```python
import functools

import numpy as np
import jax
import jax.numpy as jnp
from jax.experimental import pallas as pl
from jax.experimental.pallas import tpu as pltpu

D_MODEL = 1024
GRID_W = 64
W_POOL = 1024
W_SSD = 1024
POOL_WINDOWS = (2, 4, 8, 16)
N_POOL_GROUPS = len(POOL_WINDOWS)
POOL_GROUP_W = 256
HEADDIM = 64
HEADS = 16
GROUPS = 4
HEADS_PER_GROUP = 4
D_STATE = 128
D_CONV = 4
CONV_LEFT = 2
CHUNK = 128
GN = GROUPS * D_STATE
CONV_DIM = W_SSD + 2 * GN
OFF_POOL_Z = W_POOL
OFF_SSD_Z = 2 * W_POOL
OFF_XBC = 2 * W_POOL + W_SSD
OFF_DT = OFF_XBC + CONV_DIM
DT_PAD = 128
EPS = 1e-6
SUBLANES = 8
LANES = 128
IL_GROUPS = CHUNK // SUBLANES
CONV_SEG = 512
SSD_CHUNKS_PER_STEP = 4
VMEM_LIMIT = 56 * 1024 * 1024


def _silu(v):
    h = 0.5 * v
    return h + h * jnp.tanh(h)


def _softplus(v):
    return jnp.maximum(v, 0.0) + jnp.log1p(jnp.exp(-jnp.abs(v)))


def _bf(v):
    return v.astype(jnp.bfloat16)


def _dot(a, b):
    return jnp.dot(a, b, preferred_element_type=jnp.float32)


def _mod_kernel(c_ref, w_ref, b_ref, o_ref):
    s = _silu(c_ref[...])
    o_ref[...] = jnp.dot(s, w_ref[...], preferred_element_type=jnp.float32,
                         precision=jax.lax.Precision.HIGHEST) + b_ref[...]


def _modulation(cond_rows, w_ada, b_ada):
    rows = cond_rows.shape[0]
    n_out = w_ada.shape[1]
    tn = 1024
    return pl.pallas_call(
        _mod_kernel,
        grid=(n_out // tn,),
        in_specs=[pl.BlockSpec((rows, D_MODEL), lambda j: (0, 0)),
                  pl.BlockSpec((D_MODEL, tn), lambda j: (0, j)),
                  pl.BlockSpec((1, tn), lambda j: (0, j))],
        out_specs=pl.BlockSpec((rows, tn), lambda j: (0, j)),
        out_shape=jax.ShapeDtypeStruct((rows, n_out), jnp.float32),
        compiler_params=pltpu.CompilerParams(vmem_limit_bytes=VMEM_LIMIT),
        name="mod",
    )(cond_rows, w_ada, b_ada.reshape(1, n_out))


def _lane_cumsum(v):
    lane = jax.lax.broadcasted_iota(jnp.int32, v.shape, 1)
    shift = 1
    while shift < CHUNK:
        v = v + jnp.where(lane >= shift, pltpu.roll(v, shift, 1), 0.0)
        shift *= 2
    return v


def _proj_kernel(x_ref, xp_ref, xn_ref, nw_ref, sh_ref, sc_ref, w_ref, cw_ref, cb_ref, alog_ref,
                 bias_ref, *rest, tm, tiles_per_seq, full):
    if full:
        (u0, u1, u2, u3, zp0, zp1, zp2, zp3, zs_ref, xs_t_ref, b_ref, c_t_ref, dt_ref, cum_ref,
         pe_ref, xc_ref, mn_ref) = rest
        u_refs, zp_refs = (u0, u1, u2, u3), (zp0, zp1, zp2, zp3)
    else:
        xs_t_ref, b_ref, dt_ref, cum_ref, pe_ref, xc_ref, mn_ref = rest
    i = pl.program_id(0)
    pos = i % tiles_per_seq
    has_prev = pos > 0
    has_next = pos < tiles_per_seq - 1
    n_chunks = tm // CHUNK

    def modulated(v):
        ms = jnp.mean(v * v, axis=-1, keepdims=True)
        y = v * jax.lax.rsqrt(ms + EPS) * nw_ref[...]
        return y * (1.0 + sc_ref[0]) + sh_ref[0]

    m_tok = modulated(x_ref[...])
    hm = _bf(m_tok)
    for t in range(D_MODEL // LANES):
        mn_ref[t] = m_tok[:, t * LANES:(t + 1) * LANES]

    if full:
        for g in range(N_POOL_GROUPS):
            cols = slice(g * POOL_GROUP_W, (g + 1) * POOL_GROUP_W)
            u_refs[g][...] = _dot(hm, w_ref[:, cols])

    p_dt = _dot(hm, w_ref[:, OFF_DT:OFF_DT + DT_PAD])
    a_col = -jnp.exp(alog_ref[...])
    for q in range(n_chunks):
        dt = _softplus(p_dt[q * CHUNK:(q + 1) * CHUNK].T[:2 * HEADS] + bias_ref[...])
        dt_ref[q] = dt
        cum_ref[q] = _lane_cumsum(dt * a_col)

    rows = [jnp.concatenate([mn_ref[t, pl.ds(q * CHUNK + b, SUBLANES, stride=IL_GROUPS), :]
                             for t in range(D_MODEL // LANES)], axis=1)
            for q in range(n_chunks) for b in range(IL_GROUPS)]
    halo = [jnp.where(has_prev, modulated(xp_ref[...]), 0.0),
            jnp.where(has_next, modulated(xn_ref[...]), 0.0)]
    hm_il = _bf(jnp.concatenate(halo + rows, axis=0))
    seg = pe_ref.shape[1]
    sub = jax.lax.broadcasted_iota(jnp.int32, (SUBLANES, seg), 0)

    def group(q, b):
        lo = 2 * SUBLANES + q * CHUNK + b * SUBLANES
        return pe_ref[lo:lo + SUBLANES]

    def shifted(q, b, delta):
        bb = b + delta
        if 0 <= bb < IL_GROUPS:
            return group(q, bb)
        if bb < 0:
            bb += IL_GROUPS
            if q == 0:
                first = pe_ref[bb - SUBLANES:bb - SUBLANES + 1]
            else:
                first = pe_ref[2 * SUBLANES + (q - 1) * CHUNK + bb * SUBLANES + SUBLANES - 1:
                               2 * SUBLANES + (q - 1) * CHUNK + (bb + 1) * SUBLANES]
            return jnp.where(sub == 0, first, pltpu.roll(group(q, bb), 1, 0))
        bb -= IL_GROUPS
        if q == n_chunks - 1:
            last = pe_ref[SUBLANES + bb:SUBLANES + bb + 1]
        else:
            nxt = 2 * SUBLANES + (q + 1) * CHUNK + bb * SUBLANES
            last = pe_ref[nxt:nxt + 1]
        return jnp.where(sub == SUBLANES - 1, last, pltpu.roll(group(q, bb), SUBLANES - 1, 0))

    for j in range(0, CONV_DIM, seg):
        is_x = j < W_SSD
        is_b = W_SSD <= j < W_SSD + GN
        if not full and not (is_x or is_b):
            continue
        pe_ref[...] = _dot(hm_il, w_ref[:, OFF_XBC + j:OFF_XBC + j + seg])
        taps = [cw_ref[k:k + 1, j:j + seg] for k in range(D_CONV)]
        bias = cb_ref[:, j:j + seg]
        for q in range(n_chunks):
            for b in range(IL_GROUPS):
                acc = bias
                for k in range(D_CONV):
                    acc = acc + shifted(q, b, k - CONV_LEFT) * taps[k]
                lo = q * CHUNK + b * SUBLANES
                act = _silu(acc)
                for t in range(seg // LANES):
                    xc_ref[t, lo:lo + SUBLANES] = act[:, t * LANES:(t + 1) * LANES]
        for q in range(n_chunks):
            xc = jnp.concatenate(
                [jnp.concatenate(
                    [xc_ref[t, pl.ds(q * CHUNK + (m % 2) * (CHUNK // 2) + m // 2, SUBLANES,
                                     stride=SUBLANES), :] for t in range(seg // LANES)], axis=1)
                 for m in range(IL_GROUPS)], axis=0)
            if is_b:
                b_ref[q * CHUNK:(q + 1) * CHUNK, j - W_SSD:j - W_SSD + seg] = _bf(xc)
            else:
                dst, off = (xs_t_ref, j) if is_x else (c_t_ref, j - W_SSD - GN)
                dst[q, off:off + seg] = _bf(xc.T)

    if full:
        for g in range(N_POOL_GROUPS):
            zcols = slice(OFF_POOL_Z + g * POOL_GROUP_W, OFF_POOL_Z + (g + 1) * POOL_GROUP_W)
            zp_refs[g][...] = _bf(_silu(_dot(hm, w_ref[:, zcols])))
        for j in range(0, W_SSD, CONV_SEG):
            zcols = slice(OFF_SSD_Z + j, OFF_SSD_Z + j + CONV_SEG)
            zs_ref[:, j:j + CONV_SEG] = _bf(_silu(_dot(hm, w_ref[:, zcols])))


def _projection(x2d, norm_w, shift, scale, w_bf, conv_w, conv_b, alog_col, bias_col, seq_len, tm,
                full):
    n_tok = x2d.shape[0]
    tiles_per_seq = seq_len // tm
    n_mod = shift.shape[0]
    nct = n_tok // CHUNK
    per = tm // SUBLANES
    last_halo = n_tok // SUBLANES - 1
    kern = functools.partial(_proj_kernel, tm=tm, tiles_per_seq=tiles_per_seq, full=full)
    mod_map = (lambda i: (i // tiles_per_seq, 0, 0)) if n_mod > 1 else (lambda i: (0, 0, 0))
    mod_spec = pl.BlockSpec((1, 1, D_MODEL), mod_map)
    const = lambda i: (0, 0)
    tok = lambda i: (i, 0)
    chunk3 = lambda i: (i, 0, 0)
    q = tm // CHUNK
    xs_t = (jax.ShapeDtypeStruct((nct, W_SSD, CHUNK), jnp.bfloat16),
            pl.BlockSpec((q, W_SSD, CHUNK), chunk3))
    b_tok = (jax.ShapeDtypeStruct((n_tok, GN), jnp.bfloat16), pl.BlockSpec((tm, GN), tok))
    c_t = (jax.ShapeDtypeStruct((nct, GN, CHUNK), jnp.bfloat16), pl.BlockSpec((q, GN, CHUNK), chunk3))
    dt = (jax.ShapeDtypeStruct((nct, 2 * HEADS, CHUNK), jnp.float32),
          pl.BlockSpec((q, 2 * HEADS, CHUNK), chunk3))
    if full:
        u = (jax.ShapeDtypeStruct((n_tok, POOL_GROUP_W), jnp.float32),
             pl.BlockSpec((tm, POOL_GROUP_W), tok))
        zp = (jax.ShapeDtypeStruct((n_tok, POOL_GROUP_W), jnp.bfloat16),
              pl.BlockSpec((tm, POOL_GROUP_W), tok))
        zs = (jax.ShapeDtypeStruct((n_tok, W_SSD), jnp.bfloat16), pl.BlockSpec((tm, W_SSD), tok))
        outs = [u] * N_POOL_GROUPS + [zp] * N_POOL_GROUPS + [zs, xs_t, b_tok, c_t, dt, dt]
    else:
        outs = [xs_t, b_tok, dt, dt]
    return pl.pallas_call(
        kern,
        grid=(n_tok // tm,),
        in_specs=[pl.BlockSpec((tm, D_MODEL), tok),
                  pl.BlockSpec((SUBLANES, D_MODEL), lambda i: (jnp.maximum(i * per - 1, 0), 0)),
                  pl.BlockSpec((SUBLANES, D_MODEL), lambda i: (jnp.minimum((i + 1) * per, last_halo), 0)),
                  pl.BlockSpec((1, D_MODEL), const),
                  mod_spec, mod_spec,
                  pl.BlockSpec(w_bf.shape, const),
                  pl.BlockSpec((D_CONV, CONV_DIM), const),
                  pl.BlockSpec((1, CONV_DIM), const),
                  pl.BlockSpec((2 * HEADS, 1), const),
                  pl.BlockSpec((2 * HEADS, 1), const)],
        out_specs=[o[1] for o in outs],
        out_shape=[o[0] for o in outs],
        scratch_shapes=[pltpu.VMEM((tm + 2 * SUBLANES, CONV_SEG), jnp.float32),
                        pltpu.VMEM((CONV_SEG // LANES, tm, LANES), jnp.float32),
                        pltpu.VMEM((D_MODEL // LANES, tm, LANES), jnp.float32)],
        compiler_params=pltpu.CompilerParams(vmem_limit_bytes=VMEM_LIMIT),
        name="proj" if full else "proj_ctx",
    )(x2d, x2d, x2d, norm_w.reshape(1, D_MODEL), shift, scale, w_bf, conv_w, conv_b, alog_col,
      bias_col)


def _sweep_chunks(chunk_body, chunks_per_step, reverse):
    def body(i, carry):
        chunk_body(chunks_per_step - 1 - i if reverse else i)
        return carry
    jax.lax.fori_loop(0, chunks_per_step, body, 0)


def _tok_rows(q):
    return pl.ds(pl.multiple_of(q * CHUNK, CHUNK), CHUNK)


def _state_update(h_ref, xs_t_ref, b_ref, q, scale_in, chunk_decay):
    for g in range(GROUPS):
        bg = b_ref[_tok_rows(q), g * D_STATE:(g + 1) * D_STATE]
        xd = []
        for r in range(HEADS_PER_GROUP):
            h = g * HEADS_PER_GROUP + r
            x_h = xs_t_ref[q, h * HEADDIM:(h + 1) * HEADDIM].astype(jnp.float32)
            xd.append(_bf(x_h * scale_in[h:h + 1]))
        s_new = _dot(jnp.concatenate(xd, axis=0), bg)
        for r in range(HEADS_PER_GROUP):
            h = g * HEADS_PER_GROUP + r
            hr = slice(h * HEADDIM, (h + 1) * HEADDIM)
            h_ref[hr] = h_ref[hr] * chunk_decay[h:h + 1] + s_new[r * HEADDIM:(r + 1) * HEADDIM]


def _ssd_fwd_kernel(xs_t_ref, b_ref, dt_ref, cum_ref, alog_ref, h0_ref, *rest, n_steps, cps, with_y):
    if with_y:
        c_t_ref, dskip_ref, y_ref, hout_ref, h_ref = rest
    else:
        hout_ref, h_ref = rest
    step = pl.program_id(1)

    @pl.when(step == 0)
    def _():
        h_ref[...] = h0_ref[0]

    if with_y:
        a_b = -jnp.exp(alog_ref[HEADS:])
        src = jax.lax.broadcasted_iota(jnp.int32, (CHUNK, CHUNK), 0)
        dst = jax.lax.broadcasted_iota(jnp.int32, (CHUNK, CHUNK), 1)
        causal = src <= dst
        is_diag = src == dst

    def chunk(q):
        dt_f, cum_f = dt_ref[q, :HEADS], cum_ref[q, :HEADS]
        tot_f = cum_f[:, CHUNK - 1:CHUNK]
        scale_in = dt_f * jnp.exp(tot_f - cum_f)
        chunk_decay = jnp.exp(tot_f)

        if with_y:
            dt_b, cum_b = dt_ref[q, HEADS:], cum_ref[q, HEADS:]
            cumx_b = cum_b - dt_b * a_b
            col_terms = jnp.concatenate(
                [jnp.log(dt_f) - cum_f, jnp.log(dt_b) + cumx_b,
                 jnp.zeros((CHUNK - 2 * HEADS, CHUNK), jnp.float32)], axis=0).T
            row_f = cum_f
            row_b = -cumx_b
            decay_out_f = jnp.exp(cum_f)
            y_parts = []
            for g in range(GROUPS):
                bg = b_ref[_tok_rows(q), g * D_STATE:(g + 1) * D_STATE]
                cg_t = c_t_ref[q, g * D_STATE:(g + 1) * D_STATE]
                rows = slice(g * HEADS_PER_GROUP * HEADDIM, (g + 1) * HEADS_PER_GROUP * HEADDIM)
                g_t = _dot(bg, cg_t)
                g_diag = jnp.sum(jnp.where(is_diag, g_t, 0.0), axis=0, keepdims=True)
                y_off = _dot(_bf(h_ref[rows]), cg_t)
                for r in range(HEADS_PER_GROUP):
                    h = g * HEADS_PER_GROUP + r
                    hr = slice(h * HEADDIM, (h + 1) * HEADDIM)
                    x_bf = xs_t_ref[q, hr]
                    col_f = jnp.broadcast_to(col_terms[:, h:h + 1], (CHUNK, CHUNK))
                    col_b = jnp.broadcast_to(col_terms[:, HEADS + h:HEADS + h + 1], (CHUNK, CHUNK))
                    expo = jnp.where(causal, col_f + row_f[h:h + 1], col_b + row_b[h:h + 1])
                    w_t = _bf(g_t * jnp.exp(expo))
                    y_h = _dot(x_bf, w_t)
                    y_h = y_h + y_off[r * HEADDIM:(r + 1) * HEADDIM] * decay_out_f[h:h + 1]
                    skip = dskip_ref[hr] + g_diag * dt_b[h:h + 1]
                    y_parts.append(y_h + skip * x_bf.astype(jnp.float32))
            y_ref[_tok_rows(q), :] = jnp.concatenate(y_parts, axis=0).T

        _state_update(h_ref, xs_t_ref, b_ref, q, scale_in, chunk_decay)

    _sweep_chunks(chunk, cps, reverse=False)

    @pl.when(step == n_steps - 1)
    def _():
        hout_ref[0] = h_ref[...]


def _ssd_bwd_kernel(xs_t_ref, b_ref, dt_ref, cum_ref, alog_ref, h0_ref, *rest, n_steps, cps, with_y):
    if with_y:
        c_t_ref, ypart_ref, y_ref, hout_ref, h_ref = rest
    else:
        hout_ref, h_ref = rest
    step = pl.program_id(1)

    @pl.when(step == 0)
    def _():
        h_ref[...] = h0_ref[0]

    a_b = -jnp.exp(alog_ref[HEADS:])

    def chunk(q):
        dt_b, cum_b = dt_ref[q, HEADS:], cum_ref[q, HEADS:]
        tot_b = cum_b[:, CHUNK - 1:CHUNK]
        cumx_b = cum_b - dt_b * a_b
        scale_in = dt_b * jnp.exp(cumx_b)
        chunk_decay = jnp.exp(tot_b)

        if with_y:
            decay_out = jnp.exp(tot_b - cumx_b)
            y_parts = []
            for g in range(GROUPS):
                cg_t = c_t_ref[q, g * D_STATE:(g + 1) * D_STATE]
                rows = slice(g * HEADS_PER_GROUP * HEADDIM, (g + 1) * HEADS_PER_GROUP * HEADDIM)
                y_off = _dot(_bf(h_ref[rows]), cg_t)
                for r in range(HEADS_PER_GROUP):
                    h = g * HEADS_PER_GROUP + r
                    y_parts.append(y_off[r * HEADDIM:(r + 1) * HEADDIM] * decay_out[h:h + 1])
            y_ref[_tok_rows(q), :] = (ypart_ref[_tok_rows(q), :]
                                      + jnp.concatenate(y_parts, axis=0).T)

        _state_update(h_ref, xs_t_ref, b_ref, q, scale_in, chunk_decay)

    _sweep_chunks(chunk, cps, reverse=True)

    @pl.when(step == n_steps - 1)
    def _():
        hout_ref[0] = h_ref[...]


def _ssd_sweep(reverse, xs_t, b_tok, dt, cum, alog_col, h0, bsz, n_chunks, c_t=None, dskip_b=None,
               y_part=None):
    with_y = c_t is not None
    n_tok = b_tok.shape[0]
    cps = min(n_chunks, SSD_CHUNKS_PER_STEP)
    n_steps = n_chunks // cps

    def block_of(b, s):
        return b * n_steps + (n_steps - 1 - s if reverse else s)

    tok = lambda b, s: (block_of(b, s), 0)
    chunk3 = lambda b, s: (block_of(b, s), 0, 0)
    const2 = lambda b, s: (0, 0)
    state3 = lambda b, s: (b, 0, 0)
    h_spec = pl.BlockSpec((1, W_SSD, D_STATE), state3)
    h_shape = jax.ShapeDtypeStruct((bsz, W_SSD, D_STATE), jnp.float32)
    head_spec = pl.BlockSpec((cps, 2 * HEADS, CHUNK), chunk3)
    in_specs = [pl.BlockSpec((cps, W_SSD, CHUNK), chunk3),
                pl.BlockSpec((cps * CHUNK, GN), tok),
                head_spec, head_spec,
                pl.BlockSpec((2 * HEADS, 1), const2),
                h_spec]
    args = [xs_t, b_tok, dt, cum, alog_col, h0]
    y_spec = pl.BlockSpec((cps * CHUNK, W_SSD), tok)
    if with_y:
        in_specs.append(pl.BlockSpec((cps, GN, CHUNK), chunk3))
        args.append(c_t)
        if reverse:
            in_specs.append(y_spec)
            args.append(y_part)
        else:
            in_specs.append(pl.BlockSpec((W_SSD, CHUNK), const2))
            args.append(dskip_b)
        out_shape = [jax.ShapeDtypeStruct((n_tok, W_SSD), jnp.float32), h_shape]
        out_specs = [y_spec, h_spec]
    else:
        out_shape = [h_shape]
        out_specs = [h_spec]
    body = _ssd_bwd_kernel if reverse else _ssd_fwd_kernel
    name = ("ssd_bwd" if reverse else "ssd_fwd") + ("" if with_y else "_state")
    return pl.pallas_call(
        functools.partial(body, n_steps=n_steps, cps=cps, with_y=with_y),
        grid=(bsz, n_steps),
        in_specs=in_specs,
        out_specs=out_specs,
        out_shape=out_shape,
        scratch_shapes=[pltpu.VMEM((W_SSD, D_STATE), jnp.float32)],
        compiler_params=pltpu.CompilerParams(
            dimension_semantics=("arbitrary", "arbitrary"), vmem_limit_bytes=VMEM_LIMIT),
        name=name,
    )(*args)


POOL_TILE_ROWS = 4
POOL_TILE = POOL_TILE_ROWS * GRID_W
POOL_PAD_ROWS = max(POOL_WINDOWS) // 2


def _pool_constants(window, n_rows):
    lo_off, hi_off = -(window // 2), window - window // 2
    col = np.arange(GRID_W)
    lo = np.clip(col + lo_off, 0, GRID_W)
    hi = np.clip(col + hi_off, 0, GRID_W)
    band = ((col[None, :] >= lo[:, None]) & (col[None, :] < hi[:, None])).astype(np.float32)
    band_tile = np.kron(np.eye(POOL_TILE_ROWS, dtype=np.float32), band)
    row = np.arange(n_rows)
    cnt_r = np.clip(row + hi_off, 0, n_rows) - np.clip(row + lo_off, 0, n_rows)
    inv = 1.0 / (cnt_r[:, None] * (hi - lo)[None, :]).astype(np.float64)
    inv = np.broadcast_to(inv.reshape(-1, 1), (n_rows * GRID_W, 128)).astype(np.float32)
    return jnp.asarray(band_tile, jnp.bfloat16), jnp.asarray(inv)


def _pool_kernel(u_ref, z_ref, band_ref, inv_ref, w_ref, scale_ref, o_ref, pad_ref, *, window,
                 n_rows):
    pad_tok = POOL_PAD_ROWS * GRID_W
    n_tok = n_rows * GRID_W
    zeros = jnp.zeros((pad_tok, POOL_GROUP_W), jnp.float32)
    pad_ref[0:pad_tok] = zeros
    pad_ref[pad_tok + n_tok:] = zeros
    pad_ref[pad_tok:pad_tok + n_tok] = u_ref[...]
    first = (POOL_PAD_ROWS - window // 2) * GRID_W

    def tile(t, carry):
        base = pl.multiple_of(t * POOL_TILE, POOL_TILE)
        rsum = pad_ref[pl.ds(base + first, POOL_TILE)]
        for k in range(1, window):
            rsum = rsum + pad_ref[pl.ds(base + first + k * GRID_W, POOL_TILE)]
        hi = _bf(rsum)
        lo = _bf(rsum - hi.astype(jnp.float32))
        band = band_ref[...]
        box = _dot(band, hi) + _dot(band, lo)
        inv = inv_ref[pl.ds(base, POOL_TILE)]
        mean = box * jnp.concatenate([inv, inv], axis=1)
        d = mean - u_ref[pl.ds(base, POOL_TILE)]
        y = _dot(_bf(d), w_ref[0]) * scale_ref[...]
        gate = z_ref[pl.ds(base, POOL_TILE)].astype(jnp.float32)
        o_ref[pl.ds(base, POOL_TILE)] = _bf(y * gate)
        return carry

    jax.lax.fori_loop(0, n_tok // POOL_TILE, tile, 0)


def _pool_group(u, gate, pool_w_bf, pool_scale, g, bsz, n_img_tok):
    window = POOL_WINDOWS[g]
    n_rows = n_img_tok // GRID_W
    band, inv = _pool_constants(window, n_rows)
    kern = functools.partial(_pool_kernel, window=window, n_rows=n_rows)
    img = pl.BlockSpec((n_img_tok, POOL_GROUP_W), lambda b: (b, 0))
    return pl.pallas_call(
        kern,
        grid=(bsz,),
        in_specs=[img, img,
                  pl.BlockSpec((POOL_TILE, POOL_TILE), lambda b: (0, 0)),
                  pl.BlockSpec((n_img_tok, 128), lambda b: (0, 0)),
                  pl.BlockSpec((1, POOL_GROUP_W, POOL_GROUP_W), lambda b: (g, 0, 0)),
                  pl.BlockSpec((1, POOL_GROUP_W), lambda b: (0, g))],
        out_specs=img,
        out_shape=jax.ShapeDtypeStruct((bsz * n_img_tok, POOL_GROUP_W), jnp.bfloat16),
        scratch_shapes=[pltpu.VMEM((n_img_tok + 2 * POOL_PAD_ROWS * GRID_W, POOL_GROUP_W),
                                   jnp.float32)],
        compiler_params=pltpu.CompilerParams(vmem_limit_bytes=VMEM_LIMIT),
        name=f"pool{window}",
    )(u, gate, band, inv, pool_w_bf, pool_scale)


def _out_kernel(yp0_ref, yp1_ref, yp2_ref, yp3_ref, ys_ref, zs_ref, x_ref, gate_ref, snw_ref,
                wout_ref, fnw_ref, o_ref):
    acc = None
    for g, yp_ref in enumerate((yp0_ref, yp1_ref, yp2_ref, yp3_ref)):
        part = _dot(yp_ref[...], wout_ref[g * POOL_GROUP_W:(g + 1) * POOL_GROUP_W])
        acc = part if acc is None else acc + part
    gw = W_SSD // GROUPS
    for g in range(GROUPS):
        cols = slice(g * gw, (g + 1) * gw)
        gated = ys_ref[:, cols] * zs_ref[:, cols].astype(jnp.float32)
        ms = jnp.mean(gated * gated, axis=-1, keepdims=True)
        yn = gated * jax.lax.rsqrt(ms + EPS) * snw_ref[:, cols]
        acc = acc + _dot(_bf(yn), wout_ref[W_POOL + g * gw:W_POOL + (g + 1) * gw])
    h = x_ref[...] + gate_ref[0] * acc
    ms = jnp.mean(h * h, axis=-1, keepdims=True)
    o_ref[...] = h * jax.lax.rsqrt(ms + EPS) * fnw_ref[...]


def _output(y_pool, y_ssd, z_ssd, x2d, gate, ssd_norm_w, w_out_bf, final_norm_w, rows_per_mod, tm):
    n_tok = x2d.shape[0]
    tiles_per_mod = rows_per_mod // tm
    tok = lambda i: (i, 0)
    const = lambda i: (0, 0)
    return pl.pallas_call(
        _out_kernel,
        grid=(n_tok // tm,),
        in_specs=[pl.BlockSpec((tm, POOL_GROUP_W), tok)] * 4 + [
            pl.BlockSpec((tm, W_SSD), tok),
            pl.BlockSpec((tm, W_SSD), tok),
            pl.BlockSpec((tm, D_MODEL), tok),
            pl.BlockSpec((1, 1, D_MODEL), lambda i: (i // tiles_per_mod, 0, 0)),
            pl.BlockSpec((1, W_SSD), const),
            pl.BlockSpec((W_POOL + W_SSD, D_MODEL), const),
            pl.BlockSpec((1, D_MODEL), const)],
        out_specs=pl.BlockSpec((tm, D_MODEL), tok),
        out_shape=jax.ShapeDtypeStruct((n_tok, D_MODEL), jnp.float32),
        compiler_params=pltpu.CompilerParams(vmem_limit_bytes=VMEM_LIMIT),
        name="out",
    )(*y_pool, y_ssd, z_ssd, x2d, gate, ssd_norm_w.reshape(1, W_SSD), w_out_bf,
      final_norm_w.reshape(1, D_MODEL))


def kernel(x, c, ctx, c_ctx, norm_w, w_ada, b_ada, w_in, conv_w, conv_b, a_log, dt_bias, d_skip,
           ssd_norm_w, pool_w, pool_scale, w_out, final_norm_w):
    bsz, seq, _ = x.shape
    ctx_len = ctx.shape[1]
    depth = norm_w.shape[0]
    assert depth == 1, "single-layer block: the context stream update is never consumed"
    assert seq % 512 == 0 and ctx_len % CHUNK == 0 and seq % GRID_W == 0

    mod_rows = -(-(bsz + 1) // SUBLANES) * SUBLANES
    cond = jnp.concatenate([c, c_ctx[None], jnp.zeros((mod_rows - bsz - 1, D_MODEL), c.dtype)])
    mod = _modulation(cond, w_ada[0], b_ada[0])
    shift = mod[:, :D_MODEL].reshape(mod_rows, 1, D_MODEL)
    scale = mod[:, D_MODEL:2 * D_MODEL].reshape(mod_rows, 1, D_MODEL)
    gate = mod[:, 2 * D_MODEL:].reshape(mod_rows, 1, D_MODEL)

    w_in_bf = _bf(jnp.pad(w_in[0], ((0, 0), (0, DT_PAD - 2 * HEADS))))
    alog_col = a_log[0].reshape(2 * HEADS, 1)
    bias_col = dt_bias[0].reshape(2 * HEADS, 1)
    dskip_b = jnp.broadcast_to(jnp.repeat(d_skip[0], HEADDIM)[:, None], (W_SSD, CHUNK))
    conv_b2 = conv_b[0].reshape(1, CONV_DIM)
    zero_state = jnp.zeros((bsz, W_SSD, D_STATE), jnp.float32)

    ctx2d = ctx.reshape(bsz * ctx_len, D_MODEL)
    xs_t_c, b_c, dt_c, cum_c = _projection(
        ctx2d, norm_w[0], shift[bsz:bsz + 1], scale[bsz:bsz + 1], w_in_bf, conv_w[0], conv_b2,
        alog_col, bias_col, ctx_len, ctx_len, full=False)
    nc_ctx = ctx_len // CHUNK
    (h_fwd,) = _ssd_sweep(False, xs_t_c, b_c, dt_c, cum_c, alog_col, zero_state, bsz, nc_ctx)
    (h_bwd,) = _ssd_sweep(True, xs_t_c, b_c, dt_c, cum_c, alog_col, zero_state, bsz, nc_ctx)

    x2d = x.reshape(bsz * seq, D_MODEL)
    outs = _projection(x2d, norm_w[0], shift, scale, w_in_bf, conv_w[0], conv_b2, alog_col,
                       bias_col, seq, 512, full=True)
    u_pool, gate_pool = outs[:N_POOL_GROUPS], outs[N_POOL_GROUPS:2 * N_POOL_GROUPS]
    gate_ssd, xs_t, b_tok, c_t, dt, cum = outs[2 * N_POOL_GROUPS:]
    nc = seq // CHUNK
    y_part, _ = _ssd_sweep(False, xs_t, b_tok, dt, cum, alog_col, h_fwd, bsz, nc, c_t=c_t,
                           dskip_b=dskip_b)
    y_ssd, _ = _ssd_sweep(True, xs_t, b_tok, dt, cum, alog_col, h_bwd, bsz, nc, c_t=c_t,
                          y_part=y_part)

    pool_w_bf = _bf(pool_w[0])
    y_pool = [_pool_group(u_pool[g], gate_pool[g], pool_w_bf, pool_scale, g, bsz, seq)
              for g in range(N_POOL_GROUPS)]
    out = _output(y_pool, y_ssd, gate_ssd, x2d, gate, ssd_norm_w[0], _bf(w_out[0]), final_norm_w,
                  seq, 512)
    return out.reshape(bsz, seq, D_MODEL)
```

```python
import functools

import numpy as np
import jax
import jax.numpy as jnp
from jax.experimental import pallas as pl
from jax.experimental.pallas import tpu as pltpu

D_MODEL = 1024
GRID_W = 64
W_POOL = 1024
W_SSD = 1024
POOL_WINDOWS = (2, 4, 8, 16)
N_POOL_GROUPS = len(POOL_WINDOWS)
POOL_GROUP_W = 256
HEADDIM = 64
HEADS = 16
GROUPS = 4
HEADS_PER_GROUP = 4
D_STATE = 128
D_CONV = 4
CONV_LEFT = 2
CHUNK = 128
GN = GROUPS * D_STATE
CONV_DIM = W_SSD + 2 * GN
OFF_POOL_Z = W_POOL
OFF_SSD_Z = 2 * W_POOL
OFF_XBC = 2 * W_POOL + W_SSD
OFF_DT = OFF_XBC + CONV_DIM
DT_PAD = 128
EPS = 1e-6
SUBLANES = 8
LANES = 128
IL_GROUPS = CHUNK // SUBLANES
CONV_SEG = 512
SSD_CHUNKS_PER_STEP = 4
VMEM_LIMIT = 56 * 1024 * 1024


def _silu(v):
    h = 0.5 * v
    return h + h * jnp.tanh(h)


def _softplus(v):
    return jnp.maximum(v, 0.0) + jnp.log1p(jnp.exp(-jnp.abs(v)))


def _bf(v):
    return v.astype(jnp.bfloat16)


def _dot(a, b):
    return jnp.dot(a, b, preferred_element_type=jnp.float32)


def _mod_kernel(c_ref, w_ref, b_ref, o_ref):
    s = _silu(c_ref[...])
    o_ref[...] = jnp.dot(s, w_ref[...], preferred_element_type=jnp.float32,
                         precision=jax.lax.Precision.HIGHEST) + b_ref[...]


def _modulation(cond_rows, w_ada, b_ada):
    rows = cond_rows.shape[0]
    n_out = w_ada.shape[1]
    tn = 1024
    return pl.pallas_call(
        _mod_kernel,
        grid=(n_out // tn,),
        in_specs=[pl.BlockSpec((rows, D_MODEL), lambda j: (0, 0)),
                  pl.BlockSpec((D_MODEL, tn), lambda j: (0, j)),
                  pl.BlockSpec((1, tn), lambda j: (0, j))],
        out_specs=pl.BlockSpec((rows, tn), lambda j: (0, j)),
        out_shape=jax.ShapeDtypeStruct((rows, n_out), jnp.float32),
        compiler_params=pltpu.CompilerParams(vmem_limit_bytes=VMEM_LIMIT),
        name="mod",
    )(cond_rows, w_ada, b_ada.reshape(1, n_out))


def _lane_cumsum(v):
    lane = jax.lax.broadcasted_iota(jnp.int32, v.shape, 1)
    shift = 1
    while shift < CHUNK:
        v = v + jnp.where(lane >= shift, pltpu.roll(v, shift, 1), 0.0)
        shift *= 2
    return v


def _proj_kernel(x_ref, xp_ref, xn_ref, nw_ref, sh_ref, sc_ref, w_ref, wdt_ref, cw_ref, cb_ref,
                 alog_ref, bias_ref, *rest, tm, tiles_per_seq, full):
    if full:
        (u0, u1, u2, u3, zp0, zp1, zp2, zp3, zs_ref, xs_t_ref, b_ref, c_t_ref, dt_ref, cum_ref,
         pe_ref, xc_ref, mn_ref) = rest
        u_refs, zp_refs = (u0, u1, u2, u3), (zp0, zp1, zp2, zp3)
    else:
        xs_t_ref, b_ref, dt_ref, cum_ref, pe_ref, xc_ref, mn_ref = rest
    i = pl.program_id(0)
    pos = i % tiles_per_seq
    has_prev = pos > 0
    has_next = pos < tiles_per_seq - 1
    n_chunks = tm // CHUNK

    def modulated(v):
        ms = jnp.mean(v * v, axis=-1, keepdims=True)
        y = v * jax.lax.rsqrt(ms + EPS) * nw_ref[...]
        return y * (1.0 + sc_ref[0]) + sh_ref[0]

    m_tok = modulated(x_ref[...])
    hm = _bf(m_tok)
    for t in range(D_MODEL // LANES):
        mn_ref[t] = m_tok[:, t * LANES:(t + 1) * LANES]

    if full:
        for g in range(N_POOL_GROUPS):
            cols = slice(g * POOL_GROUP_W, (g + 1) * POOL_GROUP_W)
            u_refs[g][...] = _dot(hm, w_ref[:, cols])

    p_dt = _dot(hm, wdt_ref[...])
    a_col = -jnp.exp(alog_ref[...])
    for q in range(n_chunks):
        dt = _softplus(p_dt[q * CHUNK:(q + 1) * CHUNK].T[:2 * HEADS] + bias_ref[...])
        dt_ref[q] = dt
        cum_ref[q] = _lane_cumsum(dt * a_col)

    rows = [jnp.concatenate([mn_ref[t, pl.ds(q * CHUNK + b, SUBLANES, stride=IL_GROUPS), :]
                             for t in range(D_MODEL // LANES)], axis=1)
            for q in range(n_chunks) for b in range(IL_GROUPS)]
    halo = [jnp.where(has_prev, modulated(xp_ref[...]), 0.0),
            jnp.where(has_next, modulated(xn_ref[...]), 0.0)]
    hm_il = _bf(jnp.concatenate(halo + rows, axis=0))
    seg = pe_ref.shape[1]
    sub = jax.lax.broadcasted_iota(jnp.int32, (SUBLANES, seg), 0)

    def group(q, b):
        lo = 2 * SUBLANES + q * CHUNK + b * SUBLANES
        return pe_ref[lo:lo + SUBLANES]

    def shifted(q, b, delta):
        bb = b + delta
        if 0 <= bb < IL_GROUPS:
            return group(q, bb)
        if bb < 0:
            bb += IL_GROUPS
            if q == 0:
                first = pe_ref[bb - SUBLANES:bb - SUBLANES + 1]
            else:
                first = pe_ref[2 * SUBLANES + (q - 1) * CHUNK + bb * SUBLANES + SUBLANES - 1:
                               2 * SUBLANES + (q - 1) * CHUNK + (bb + 1) * SUBLANES]
            return jnp.where(sub == 0, first, pltpu.roll(group(q, bb), 1, 0))
        bb -= IL_GROUPS
        if q == n_chunks - 1:
            last = pe_ref[SUBLANES + bb:SUBLANES + bb + 1]
        else:
            nxt = 2 * SUBLANES + (q + 1) * CHUNK + bb * SUBLANES
            last = pe_ref[nxt:nxt + 1]
        return jnp.where(sub == SUBLANES - 1, last, pltpu.roll(group(q, bb), SUBLANES - 1, 0))

    for j in range(0, CONV_DIM, seg):
        is_x = j < W_SSD
        is_b = W_SSD <= j < W_SSD + GN
        if not full and not (is_x or is_b):
            continue
        pe_ref[...] = _dot(hm_il, w_ref[:, OFF_XBC + j:OFF_XBC + j + seg])
        taps = [cw_ref[k:k + 1, j:j + seg] for k in range(D_CONV)]
        bias = cb_ref[:, j:j + seg]
        for q in range(n_chunks):
            for b in range(IL_GROUPS):
                acc = bias
                for k in range(D_CONV):
                    acc = acc + shifted(q, b, k - CONV_LEFT) * taps[k]
                lo = q * CHUNK + b * SUBLANES
                act = _silu(acc)
                for t in range(seg // LANES):
                    xc_ref[t, lo:lo + SUBLANES] = act[:, t * LANES:(t + 1) * LANES]
        for q in range(n_chunks):
            xc = jnp.concatenate(
                [jnp.concatenate(
                    [xc_ref[t, pl.ds(q * CHUNK + (m % 2) * (CHUNK // 2) + m // 2, SUBLANES,
                                     stride=SUBLANES), :] for t in range(seg // LANES)], axis=1)
                 for m in range(IL_GROUPS)], axis=0)
            if is_b:
                b_ref[q * CHUNK:(q + 1) * CHUNK, j - W_SSD:j - W_SSD + seg] = _bf(xc)
            else:
                dst, off = (xs_t_ref, j) if is_x else (c_t_ref, j - W_SSD - GN)
                dst[q, off:off + seg] = _bf(xc.T)

    if full:
        for g in range(N_POOL_GROUPS):
            zcols = slice(OFF_POOL_Z + g * POOL_GROUP_W, OFF_POOL_Z + (g + 1) * POOL_GROUP_W)
            zp_refs[g][...] = _bf(_silu(_dot(hm, w_ref[:, zcols])))
        for j in range(0, W_SSD, CONV_SEG):
            zcols = slice(OFF_SSD_Z + j, OFF_SSD_Z + j + CONV_SEG)
            zs_ref[:, j:j + CONV_SEG] = _bf(_silu(_dot(hm, w_ref[:, zcols])))


def _projection(x2d, norm_w, shift, scale, w_bf, wdt_bf, conv_w, conv_b, alog_col, bias_col,
                seq_len, tm, full):
    n_tok = x2d.shape[0]
    tiles_per_seq = seq_len // tm
    n_mod = shift.shape[0]
    nct = n_tok // CHUNK
    per = tm // SUBLANES
    last_halo = n_tok // SUBLANES - 1
    kern = functools.partial(_proj_kernel, tm=tm, tiles_per_seq=tiles_per_seq, full=full)
    mod_map = (lambda i: (i // tiles_per_seq, 0, 0)) if n_mod > 1 else (lambda i: (0, 0, 0))
    mod_spec = pl.BlockSpec((1, 1, D_MODEL), mod_map)
    const = lambda i: (0, 0)
    tok = lambda i: (i, 0)
    chunk3 = lambda i: (i, 0, 0)
    q = tm // CHUNK
    xs_t = (jax.ShapeDtypeStruct((nct, W_SSD, CHUNK), jnp.bfloat16),
            pl.BlockSpec((q, W_SSD, CHUNK), chunk3))
    b_tok = (jax.ShapeDtypeStruct((n_tok, GN), jnp.bfloat16), pl.BlockSpec((tm, GN), tok))
    c_t = (jax.ShapeDtypeStruct((nct, GN, CHUNK), jnp.bfloat16), pl.BlockSpec((q, GN, CHUNK), chunk3))
    dt = (jax.ShapeDtypeStruct((nct, 2 * HEADS, CHUNK), jnp.float32),
          pl.BlockSpec((q, 2 * HEADS, CHUNK), chunk3))
    if full:
        u = (jax.ShapeDtypeStruct((n_tok, POOL_GROUP_W), jnp.float32),
             pl.BlockSpec((tm, POOL_GROUP_W), tok))
        zp = (jax.ShapeDtypeStruct((n_tok, POOL_GROUP_W), jnp.bfloat16),
              pl.BlockSpec((tm, POOL_GROUP_W), tok))
        zs = (jax.ShapeDtypeStruct((n_tok, W_SSD), jnp.bfloat16), pl.BlockSpec((tm, W_SSD), tok))
        outs = [u] * N_POOL_GROUPS + [zp] * N_POOL_GROUPS + [zs, xs_t, b_tok, c_t, dt, dt]
    else:
        outs = [xs_t, b_tok, dt, dt]
    return pl.pallas_call(
        kern,
        grid=(n_tok // tm,),
        in_specs=[pl.BlockSpec((tm, D_MODEL), tok),
                  pl.BlockSpec((SUBLANES, D_MODEL), lambda i: (jnp.maximum(i * per - 1, 0), 0)),
                  pl.BlockSpec((SUBLANES, D_MODEL), lambda i: (jnp.minimum((i + 1) * per, last_halo), 0)),
                  pl.BlockSpec((1, D_MODEL), const),
                  mod_spec, mod_spec,
                  pl.BlockSpec(w_bf.shape, const),
                  pl.BlockSpec((D_MODEL, DT_PAD), const),
                  pl.BlockSpec((D_CONV, CONV_DIM), const),
                  pl.BlockSpec((1, CONV_DIM), const),
                  pl.BlockSpec((2 * HEADS, 1), const),
                  pl.BlockSpec((2 * HEADS, 1), const)],
        out_specs=[o[1] for o in outs],
        out_shape=[o[0] for o in outs],
        scratch_shapes=[pltpu.VMEM((tm + 2 * SUBLANES, CONV_SEG), jnp.float32),
                        pltpu.VMEM((CONV_SEG // LANES, tm, LANES), jnp.float32),
                        pltpu.VMEM((D_MODEL // LANES, tm, LANES), jnp.float32)],
        compiler_params=pltpu.CompilerParams(vmem_limit_bytes=VMEM_LIMIT),
        name="proj" if full else "proj_ctx",
    )(x2d, x2d, x2d, norm_w.reshape(1, D_MODEL), shift, scale, w_bf, wdt_bf, conv_w, conv_b,
      alog_col, bias_col)


def _sweep_chunks(chunk_body, chunks_per_step, reverse):
    def body(i, carry):
        chunk_body(chunks_per_step - 1 - i if reverse else i)
        return carry
    jax.lax.fori_loop(0, chunks_per_step, body, 0)


def _tok_rows(q):
    return pl.ds(pl.multiple_of(q * CHUNK, CHUNK), CHUNK)


def _state_update(h_ref, xs_t_ref, b_ref, q, scale_in, chunk_decay):
    for g in range(GROUPS):
        bg = b_ref[_tok_rows(q), g * D_STATE:(g + 1) * D_STATE]
        xd = []
        for r in range(HEADS_PER_GROUP):
            h = g * HEADS_PER_GROUP + r
            x_h = xs_t_ref[q, h * HEADDIM:(h + 1) * HEADDIM].astype(jnp.float32)
            xd.append(_bf(x_h * scale_in[h:h + 1]))
        s_new = _dot(jnp.concatenate(xd, axis=0), bg)
        for r in range(HEADS_PER_GROUP):
            h = g * HEADS_PER_GROUP + r
            hr = slice(h * HEADDIM, (h + 1) * HEADDIM)
            h_ref[hr] = h_ref[hr] * chunk_decay[h:h + 1] + s_new[r * HEADDIM:(r + 1) * HEADDIM]


def _ssd_fwd_kernel(xs_t_ref, b_ref, dt_ref, cum_ref, alog_ref, h0_ref, *rest, n_steps, cps, with_y):
    if with_y:
        c_t_ref, dskip_ref, y_ref, hout_ref, h_ref = rest
    else:
        hout_ref, h_ref = rest
    step = pl.program_id(1)

    @pl.when(step == 0)
    def _():
        h_ref[...] = h0_ref[0]

    if with_y:
        a_b = -jnp.exp(alog_ref[HEADS:])
        src = jax.lax.broadcasted_iota(jnp.int32, (CHUNK, CHUNK), 0)
        dst = jax.lax.broadcasted_iota(jnp.int32, (CHUNK, CHUNK), 1)
        causal = src <= dst
        is_diag = src == dst

    def chunk(q):
        dt_f, cum_f = dt_ref[q, :HEADS], cum_ref[q, :HEADS]
        tot_f = cum_f[:, CHUNK - 1:CHUNK]
        scale_in = dt_f * jnp.exp(tot_f - cum_f)
        chunk_decay = jnp.exp(tot_f)

        if with_y:
            dt_b, cum_b = dt_ref[q, HEADS:], cum_ref[q, HEADS:]
            cumx_b = cum_b - dt_b * a_b
            col_terms = jnp.concatenate(
                [jnp.log(dt_f) - cum_f, jnp.log(dt_b) + cumx_b,
                 jnp.zeros((CHUNK - 2 * HEADS, CHUNK), jnp.float32)], axis=0).T
            row_f = cum_f
            row_b = -cumx_b
            decay_out_f = jnp.exp(cum_f)
            y_parts = []
            for g in range(GROUPS):
                bg = b_ref[_tok_rows(q), g * D_STATE:(g + 1) * D_STATE]
                cg_t = c_t_ref[q, g * D_STATE:(g + 1) * D_STATE]
                rows = slice(g * HEADS_PER_GROUP * HEADDIM, (g + 1) * HEADS_PER_GROUP * HEADDIM)
                g_t = _dot(bg, cg_t)
                g_diag = jnp.sum(jnp.where(is_diag, g_t, 0.0), axis=0, keepdims=True)
                y_off = _dot(_bf(h_ref[rows]), cg_t)
                for r in range(HEADS_PER_GROUP):
                    h = g * HEADS_PER_GROUP + r
                    hr = slice(h * HEADDIM, (h + 1) * HEADDIM)
                    x_bf = xs_t_ref[q, hr]
                    col_f = jnp.broadcast_to(col_terms[:, h:h + 1], (CHUNK, CHUNK))
                    col_b = jnp.broadcast_to(col_terms[:, HEADS + h:HEADS + h + 1], (CHUNK, CHUNK))
                    expo = jnp.where(causal, col_f + row_f[h:h + 1], col_b + row_b[h:h + 1])
                    w_t = _bf(g_t * jnp.exp(expo))
                    y_h = _dot(x_bf, w_t)
                    y_h = y_h + y_off[r * HEADDIM:(r + 1) * HEADDIM] * decay_out_f[h:h + 1]
                    skip = dskip_ref[hr] + g_diag * dt_b[h:h + 1]
                    y_parts.append(y_h + skip * x_bf.astype(jnp.float32))
            y_ref[_tok_rows(q), :] = jnp.concatenate(y_parts, axis=0).T

        _state_update(h_ref, xs_t_ref, b_ref, q, scale_in, chunk_decay)

    _sweep_chunks(chunk, cps, reverse=False)

    @pl.when(step == n_steps - 1)
    def _():
        hout_ref[0] = h_ref[...]


def _ssd_bwd_kernel(xs_t_ref, b_ref, dt_ref, cum_ref, alog_ref, h0_ref, *rest, n_steps, cps, with_y):
    if with_y:
        c_t_ref, ypart_ref, y_ref, hout_ref, h_ref = rest
    else:
        hout_ref, h_ref = rest
    step = pl.program_id(1)

    @pl.when(step == 0)
    def _():
        h_ref[...] = h0_ref[0]

    a_b = -jnp.exp(alog_ref[HEADS:])

    def chunk(q):
        dt_b, cum_b = dt_ref[q, HEADS:], cum_ref[q, HEADS:]
        tot_b = cum_b[:, CHUNK - 1:CHUNK]
        cumx_b = cum_b - dt_b * a_b
        scale_in = dt_b * jnp.exp(cumx_b)
        chunk_decay = jnp.exp(tot_b)

        if with_y:
            decay_out = jnp.exp(tot_b - cumx_b)
            y_parts = []
            for g in range(GROUPS):
                cg_t = c_t_ref[q, g * D_STATE:(g + 1) * D_STATE]
                rows = slice(g * HEADS_PER_GROUP * HEADDIM, (g + 1) * HEADS_PER_GROUP * HEADDIM)
                y_off = _dot(_bf(h_ref[rows]), cg_t)
                for r in range(HEADS_PER_GROUP):
                    h = g * HEADS_PER_GROUP + r
                    y_parts.append(y_off[r * HEADDIM:(r + 1) * HEADDIM] * decay_out[h:h + 1])
            y_ref[_tok_rows(q), :] = (ypart_ref[_tok_rows(q), :]
                                      + jnp.concatenate(y_parts, axis=0).T)

        _state_update(h_ref, xs_t_ref, b_ref, q, scale_in, chunk_decay)

    _sweep_chunks(chunk, cps, reverse=True)

    @pl.when(step == n_steps - 1)
    def _():
        hout_ref[0] = h_ref[...]


def _ssd_sweep(reverse, xs_t, b_tok, dt, cum, alog_col, h0, bsz, n_chunks, c_t=None, dskip_b=None,
               y_part=None):
    with_y = c_t is not None
    n_tok = b_tok.shape[0]
    cps = min(n_chunks, SSD_CHUNKS_PER_STEP)
    n_steps = n_chunks // cps

    def block_of(b, s):
        return b * n_steps + (n_steps - 1 - s if reverse else s)

    tok = lambda b, s: (block_of(b, s), 0)
    chunk3 = lambda b, s: (block_of(b, s), 0, 0)
    const2 = lambda b, s: (0, 0)
    state3 = lambda b, s: (b, 0, 0)
    h_spec = pl.BlockSpec((1, W_SSD, D_STATE), state3)
    h_shape = jax.ShapeDtypeStruct((bsz, W_SSD, D_STATE), jnp.float32)
    head_spec = pl.BlockSpec((cps, 2 * HEADS, CHUNK), chunk3)
    in_specs = [pl.BlockSpec((cps, W_SSD, CHUNK), chunk3),
                pl.BlockSpec((cps * CHUNK, GN), tok),
                head_spec, head_spec,
                pl.BlockSpec((2 * HEADS, 1), const2),
                h_spec]
    args = [xs_t, b_tok, dt, cum, alog_col, h0]
    y_spec = pl.BlockSpec((cps * CHUNK, W_SSD), tok)
    if with_y:
        in_specs.append(pl.BlockSpec((cps, GN, CHUNK), chunk3))
        args.append(c_t)
        if reverse:
            in_specs.append(y_spec)
            args.append(y_part)
        else:
            in_specs.append(pl.BlockSpec((W_SSD, CHUNK), const2))
            args.append(dskip_b)
        out_shape = [jax.ShapeDtypeStruct((n_tok, W_SSD), jnp.float32), h_shape]
        out_specs = [y_spec, h_spec]
    else:
        out_shape = [h_shape]
        out_specs = [h_spec]
    body = _ssd_bwd_kernel if reverse else _ssd_fwd_kernel
    name = ("ssd_bwd" if reverse else "ssd_fwd") + ("" if with_y else "_state")
    return pl.pallas_call(
        functools.partial(body, n_steps=n_steps, cps=cps, with_y=with_y),
        grid=(bsz, n_steps),
        in_specs=in_specs,
        out_specs=out_specs,
        out_shape=out_shape,
        scratch_shapes=[pltpu.VMEM((W_SSD, D_STATE), jnp.float32)],
        compiler_params=pltpu.CompilerParams(
            dimension_semantics=("arbitrary", "arbitrary"), vmem_limit_bytes=VMEM_LIMIT),
        name=name,
    )(*args)


POOL_TILE_ROWS = 4
POOL_TILE = POOL_TILE_ROWS * GRID_W


def _pool_constants(window, n_rows):
    lo_off, hi_off = -(window // 2), window - window // 2
    col = np.arange(GRID_W)
    lo = np.clip(col + lo_off, 0, GRID_W)
    hi = np.clip(col + hi_off, 0, GRID_W)
    band = ((col[None, :] >= lo[:, None]) & (col[None, :] < hi[:, None])).astype(np.float32)
    band_tile = np.kron(np.eye(POOL_TILE_ROWS, dtype=np.float32), band)
    row = np.arange(n_rows)
    cnt_r = np.clip(row + hi_off, 0, n_rows) - np.clip(row + lo_off, 0, n_rows)
    inv = 1.0 / (cnt_r[:, None] * (hi - lo)[None, :]).astype(np.float64)
    inv = np.broadcast_to(inv.reshape(-1, 1), (n_rows * GRID_W, 128)).astype(np.float32)
    return jnp.asarray(band_tile, jnp.bfloat16), jnp.asarray(inv)


def _pool_kernel(u_ref, z_ref, band_ref, inv_ref, w_ref, scale_ref, o_ref, *, window, n_rows):
    def grid_row(r):
        return u_ref[r * GRID_W:(r + 1) * GRID_W]

    def bounds(r):
        return max(r - window // 2, 0), min(r + window - window // 2, n_rows)

    band = band_ref[...]
    rsum, tile_rows = None, []
    for r in range(n_rows):
        lo, hi = bounds(r)
        if r == 0 or window <= 2:
            rsum = grid_row(lo)
            for k in range(lo + 1, hi):
                rsum = rsum + grid_row(k)
        else:
            prev_lo, prev_hi = bounds(r - 1)
            if hi > prev_hi:
                rsum = rsum + grid_row(hi - 1)
            if lo > prev_lo:
                rsum = rsum - grid_row(prev_lo)
        tile_rows.append(rsum)
        if len(tile_rows) < POOL_TILE_ROWS:
            continue
        base = (r + 1 - POOL_TILE_ROWS) * GRID_W
        rows = slice(base, base + POOL_TILE)
        rs = jnp.concatenate(tile_rows, axis=0)
        tile_rows = []
        hi_part = _bf(rs)
        lo_part = _bf(rs - hi_part.astype(jnp.float32))
        box = _dot(band, hi_part) + _dot(band, lo_part)
        inv = inv_ref[rows]
        mean = box * jnp.concatenate([inv, inv], axis=1)
        d = mean - u_ref[rows]
        y = _dot(_bf(d), w_ref[0]) * scale_ref[...]
        o_ref[rows] = _bf(y * z_ref[rows].astype(jnp.float32))


def _pool_group(u, gate, pool_w_bf, pool_scale, g, bsz, n_img_tok):
    window = POOL_WINDOWS[g]
    n_rows = n_img_tok // GRID_W
    band, inv = _pool_constants(window, n_rows)
    kern = functools.partial(_pool_kernel, window=window, n_rows=n_rows)
    img = pl.BlockSpec((n_img_tok, POOL_GROUP_W), lambda b: (b, 0))
    return pl.pallas_call(
        kern,
        grid=(bsz,),
        in_specs=[img, img,
                  pl.BlockSpec((POOL_TILE, POOL_TILE), lambda b: (0, 0)),
                  pl.BlockSpec((n_img_tok, 128), lambda b: (0, 0)),
                  pl.BlockSpec((1, POOL_GROUP_W, POOL_GROUP_W), lambda b: (g, 0, 0)),
                  pl.BlockSpec((1, POOL_GROUP_W), lambda b: (0, g))],
        out_specs=img,
        out_shape=jax.ShapeDtypeStruct((bsz * n_img_tok, POOL_GROUP_W), jnp.bfloat16),
        compiler_params=pltpu.CompilerParams(vmem_limit_bytes=VMEM_LIMIT),
        name=f"pool{window}",
    )(u, gate, band, inv, pool_w_bf, pool_scale)


def _out_kernel(yp0_ref, yp1_ref, yp2_ref, yp3_ref, ys_ref, zs_ref, x_ref, gate_ref, snw_ref,
                wout_ref, fnw_ref, o_ref):
    acc = None
    for g, yp_ref in enumerate((yp0_ref, yp1_ref, yp2_ref, yp3_ref)):
        part = _dot(yp_ref[...], wout_ref[g * POOL_GROUP_W:(g + 1) * POOL_GROUP_W])
        acc = part if acc is None else acc + part
    gw = W_SSD // GROUPS
    for g in range(GROUPS):
        cols = slice(g * gw, (g + 1) * gw)
        gated = ys_ref[:, cols] * zs_ref[:, cols].astype(jnp.float32)
        ms = jnp.mean(gated * gated, axis=-1, keepdims=True)
        yn = gated * jax.lax.rsqrt(ms + EPS) * snw_ref[:, cols]
        acc = acc + _dot(_bf(yn), wout_ref[W_POOL + g * gw:W_POOL + (g + 1) * gw])
    h = x_ref[...] + gate_ref[0] * acc
    ms = jnp.mean(h * h, axis=-1, keepdims=True)
    o_ref[...] = h * jax.lax.rsqrt(ms + EPS) * fnw_ref[...]


def _output(y_pool, y_ssd, z_ssd, x2d, gate, ssd_norm_w, w_out_bf, final_norm_w, rows_per_mod, tm):
    n_tok = x2d.shape[0]
    tiles_per_mod = rows_per_mod // tm
    tok = lambda i: (i, 0)
    const = lambda i: (0, 0)
    return pl.pallas_call(
        _out_kernel,
        grid=(n_tok // tm,),
        in_specs=[pl.BlockSpec((tm, POOL_GROUP_W), tok)] * 4 + [
            pl.BlockSpec((tm, W_SSD), tok),
            pl.BlockSpec((tm, W_SSD), tok),
            pl.BlockSpec((tm, D_MODEL), tok),
            pl.BlockSpec((1, 1, D_MODEL), lambda i: (i // tiles_per_mod, 0, 0)),
            pl.BlockSpec((1, W_SSD), const),
            pl.BlockSpec((W_POOL + W_SSD, D_MODEL), const),
            pl.BlockSpec((1, D_MODEL), const)],
        out_specs=pl.BlockSpec((tm, D_MODEL), tok),
        out_shape=jax.ShapeDtypeStruct((n_tok, D_MODEL), jnp.float32),
        compiler_params=pltpu.CompilerParams(vmem_limit_bytes=VMEM_LIMIT),
        name="out",
    )(*y_pool, y_ssd, z_ssd, x2d, gate, ssd_norm_w.reshape(1, W_SSD), w_out_bf,
      final_norm_w.reshape(1, D_MODEL))


def kernel(x, c, ctx, c_ctx, norm_w, w_ada, b_ada, w_in, conv_w, conv_b, a_log, dt_bias, d_skip,
           ssd_norm_w, pool_w, pool_scale, w_out, final_norm_w):
    bsz, seq, _ = x.shape
    ctx_len = ctx.shape[1]
    depth = norm_w.shape[0]
    assert depth == 1, "single-layer block: the context stream update is never consumed"
    assert seq % 512 == 0 and ctx_len % CHUNK == 0 and seq % GRID_W == 0

    mod_rows = -(-(bsz + 1) // SUBLANES) * SUBLANES
    cond = jnp.concatenate([c, c_ctx[None], jnp.zeros((mod_rows - bsz - 1, D_MODEL), c.dtype)])
    mod = _modulation(cond, w_ada[0], b_ada[0])
    shift = mod[:, :D_MODEL].reshape(mod_rows, 1, D_MODEL)
    scale = mod[:, D_MODEL:2 * D_MODEL].reshape(mod_rows, 1, D_MODEL)
    gate = mod[:, 2 * D_MODEL:].reshape(mod_rows, 1, D_MODEL)

    w_in_bf = _bf(w_in[0, :, :OFF_DT])
    w_dt_bf = jnp.pad(_bf(w_in[0, :, OFF_DT:]), ((0, 0), (0, DT_PAD - 2 * HEADS)))
    alog_col = a_log[0].reshape(2 * HEADS, 1)
    bias_col = dt_bias[0].reshape(2 * HEADS, 1)
    dskip_b = jnp.broadcast_to(jnp.repeat(d_skip[0], HEADDIM)[:, None], (W_SSD, CHUNK))
    conv_b2 = conv_b[0].reshape(1, CONV_DIM)
    zero_state = jnp.zeros((bsz, W_SSD, D_STATE), jnp.float32)

    ctx2d = ctx.reshape(bsz * ctx_len, D_MODEL)
    xs_t_c, b_c, dt_c, cum_c = _projection(
        ctx2d, norm_w[0], shift[bsz:bsz + 1], scale[bsz:bsz + 1], w_in_bf, w_dt_bf, conv_w[0],
        conv_b2, alog_col, bias_col, ctx_len, ctx_len, full=False)
    nc_ctx = ctx_len // CHUNK
    (h_fwd,) = _ssd_sweep(False, xs_t_c, b_c, dt_c, cum_c, alog_col, zero_state, bsz, nc_ctx)
    (h_bwd,) = _ssd_sweep(True, xs_t_c, b_c, dt_c, cum_c, alog_col, zero_state, bsz, nc_ctx)

    x2d = x.reshape(bsz * seq, D_MODEL)
    outs = _projection(x2d, norm_w[0], shift, scale, w_in_bf, w_dt_bf, conv_w[0], conv_b2,
                       alog_col, bias_col, seq, 512, full=True)
    u_pool, gate_pool = outs[:N_POOL_GROUPS], outs[N_POOL_GROUPS:2 * N_POOL_GROUPS]
    gate_ssd, xs_t, b_tok, c_t, dt, cum = outs[2 * N_POOL_GROUPS:]
    nc = seq // CHUNK
    y_part, _ = _ssd_sweep(False, xs_t, b_tok, dt, cum, alog_col, h_fwd, bsz, nc, c_t=c_t,
                           dskip_b=dskip_b)
    y_ssd, _ = _ssd_sweep(True, xs_t, b_tok, dt, cum, alog_col, h_bwd, bsz, nc, c_t=c_t,
                          y_part=y_part)

    pool_w_bf = _bf(pool_w[0])
    y_pool = [_pool_group(u_pool[g], gate_pool[g], pool_w_bf, pool_scale, g, bsz, seq)
              for g in range(N_POOL_GROUPS)]
    out = _output(y_pool, y_ssd, gate_ssd, x2d, gate, ssd_norm_w[0], _bf(w_out[0]), final_norm_w,
                  seq, 512)
    return out.reshape(bsz, seq, D_MODEL)
```

```python
import functools

import numpy as np
import jax
import jax.numpy as jnp
from jax.experimental import pallas as pl
from jax.experimental.pallas import tpu as pltpu

D_MODEL = 1024
GRID_W = 64
W_POOL = 1024
W_SSD = 1024
POOL_WINDOWS = (2, 4, 8, 16)
N_POOL_GROUPS = len(POOL_WINDOWS)
POOL_GROUP_W = 256
HEADDIM = 64
HEADS = 16
GROUPS = 4
HEADS_PER_GROUP = 4
D_STATE = 128
D_CONV = 4
CONV_LEFT = 2
CHUNK = 128
GN = GROUPS * D_STATE
CONV_DIM = W_SSD + 2 * GN
OFF_POOL_Z = W_POOL
OFF_SSD_Z = 2 * W_POOL
OFF_XBC = 2 * W_POOL + W_SSD
OFF_DT = OFF_XBC + CONV_DIM
DT_PAD = 128
EPS = 1e-6
SUBLANES = 8
LANES = 128
IL_GROUPS = CHUNK // SUBLANES
CONV_SEG = 512
SSD_CHUNKS_PER_STEP = 4
VMEM_LIMIT = 56 * 1024 * 1024


def _silu(v):
    h = 0.5 * v
    return h + h * jnp.tanh(h)


def _softplus(v):
    return jnp.maximum(v, 0.0) + jnp.log1p(jnp.exp(-jnp.abs(v)))


def _bf(v):
    return v.astype(jnp.bfloat16)


def _dot(a, b):
    return jnp.dot(a, b, preferred_element_type=jnp.float32)


def _mod_kernel(c_ref, w_ref, b_ref, o_ref):
    s = _silu(c_ref[...])
    o_ref[...] = jnp.dot(s, w_ref[...], preferred_element_type=jnp.float32,
                         precision=jax.lax.Precision.HIGHEST) + b_ref[...]


def _modulation(cond_rows, w_ada, b_ada):
    rows = cond_rows.shape[0]
    n_out = w_ada.shape[1]
    tn = 1024
    return pl.pallas_call(
        _mod_kernel,
        grid=(n_out // tn,),
        in_specs=[pl.BlockSpec((rows, D_MODEL), lambda j: (0, 0)),
                  pl.BlockSpec((D_MODEL, tn), lambda j: (0, j)),
                  pl.BlockSpec((1, tn), lambda j: (0, j))],
        out_specs=pl.BlockSpec((rows, tn), lambda j: (0, j)),
        out_shape=jax.ShapeDtypeStruct((rows, n_out), jnp.float32),
        compiler_params=pltpu.CompilerParams(vmem_limit_bytes=VMEM_LIMIT),
        name="mod",
    )(cond_rows, w_ada, b_ada.reshape(1, n_out))


def _lane_cumsum(v):
    lane = jax.lax.broadcasted_iota(jnp.int32, v.shape, 1)
    shift = 1
    while shift < CHUNK:
        v = v + jnp.where(lane >= shift, pltpu.roll(v, shift, 1), 0.0)
        shift *= 2
    return v


def _proj_kernel(x_ref, xp_ref, xn_ref, nw_ref, sh_ref, sc_ref, w_ref, wdt_ref, cw_ref, cb_ref,
                 alog_ref, bias_ref, *rest, tm, tiles_per_seq, full):
    if full:
        (u0, u1, u2, u3, zp0, zp1, zp2, zp3, zs_ref, xs_t_ref, b_ref, c_t_ref, dt_ref, cum_ref,
         pe_ref, xc_ref, mn_ref) = rest
        u_refs, zp_refs = (u0, u1, u2, u3), (zp0, zp1, zp2, zp3)
    else:
        xs_t_ref, b_ref, dt_ref, cum_ref, pe_ref, xc_ref, mn_ref = rest
    i = pl.program_id(0)
    pos = i % tiles_per_seq
    has_prev = pos > 0
    has_next = pos < tiles_per_seq - 1
    n_chunks = tm // CHUNK

    def modulated(v):
        ms = jnp.mean(v * v, axis=-1, keepdims=True)
        y = v * jax.lax.rsqrt(ms + EPS) * nw_ref[...]
        return y * (1.0 + sc_ref[0]) + sh_ref[0]

    m_tok = modulated(x_ref[...])
    hm = _bf(m_tok)
    for t in range(D_MODEL // LANES):
        mn_ref[t] = m_tok[:, t * LANES:(t + 1) * LANES]

    if full:
        for g in range(N_POOL_GROUPS):
            cols = slice(g * POOL_GROUP_W, (g + 1) * POOL_GROUP_W)
            u_refs[g][...] = _dot(hm, w_ref[:, cols])

    p_dt = _dot(hm, wdt_ref[...])
    a_col = -jnp.exp(alog_ref[...])
    for q in range(n_chunks):
        dt = _softplus(p_dt[q * CHUNK:(q + 1) * CHUNK].T[:2 * HEADS] + bias_ref[...])
        dt_ref[q] = dt
        cum_ref[q] = _lane_cumsum(dt * a_col)

    rows = [jnp.concatenate([mn_ref[t, pl.ds(q * CHUNK + b, SUBLANES, stride=IL_GROUPS), :]
                             for t in range(D_MODEL // LANES)], axis=1)
            for q in range(n_chunks) for b in range(IL_GROUPS)]
    halo = [jnp.where(has_prev, modulated(xp_ref[...]), 0.0),
            jnp.where(has_next, modulated(xn_ref[...]), 0.0)]
    hm_il = _bf(jnp.concatenate(halo + rows, axis=0))
    seg = pe_ref.shape[1]
    sub = jax.lax.broadcasted_iota(jnp.int32, (SUBLANES, seg), 0)

    def group(q, b):
        lo = 2 * SUBLANES + q * CHUNK + b * SUBLANES
        return pe_ref[lo:lo + SUBLANES]

    def shifted(q, b, delta):
        bb = b + delta
        if 0 <= bb < IL_GROUPS:
            return group(q, bb)
        if bb < 0:
            bb += IL_GROUPS
            if q == 0:
                first = pe_ref[bb - SUBLANES:bb - SUBLANES + 1]
            else:
                first = pe_ref[2 * SUBLANES + (q - 1) * CHUNK + bb * SUBLANES + SUBLANES - 1:
                               2 * SUBLANES + (q - 1) * CHUNK + (bb + 1) * SUBLANES]
            return jnp.where(sub == 0, first, pltpu.roll(group(q, bb), 1, 0))
        bb -= IL_GROUPS
        if q == n_chunks - 1:
            last = pe_ref[SUBLANES + bb:SUBLANES + bb + 1]
        else:
            nxt = 2 * SUBLANES + (q + 1) * CHUNK + bb * SUBLANES
            last = pe_ref[nxt:nxt + 1]
        return jnp.where(sub == SUBLANES - 1, last, pltpu.roll(group(q, bb), SUBLANES - 1, 0))

    for j in range(0, CONV_DIM, seg):
        is_x = j < W_SSD
        is_b = W_SSD <= j < W_SSD + GN
        if not full and not (is_x or is_b):
            continue
        pe_ref[...] = _dot(hm_il, w_ref[:, OFF_XBC + j:OFF_XBC + j + seg])
        taps = [cw_ref[k:k + 1, j:j + seg] for k in range(D_CONV)]
        bias = cb_ref[:, j:j + seg]
        for q in range(n_chunks):
            for b in range(IL_GROUPS):
                acc = bias
                for k in range(D_CONV):
                    acc = acc + shifted(q, b, k - CONV_LEFT) * taps[k]
                lo = q * CHUNK + b * SUBLANES
                act = _silu(acc)
                for t in range(seg // LANES):
                    xc_ref[t, lo:lo + SUBLANES] = act[:, t * LANES:(t + 1) * LANES]
        for q in range(n_chunks):
            xc = jnp.concatenate(
                [jnp.concatenate(
                    [xc_ref[t, pl.ds(q * CHUNK + (m % 2) * (CHUNK // 2) + m // 2, SUBLANES,
                                     stride=SUBLANES), :] for t in range(seg // LANES)], axis=1)
                 for m in range(IL_GROUPS)], axis=0)
            if is_b:
                b_ref[q * CHUNK:(q + 1) * CHUNK, j - W_SSD:j - W_SSD + seg] = _bf(xc)
            else:
                dst, off = (xs_t_ref, j) if is_x else (c_t_ref, j - W_SSD - GN)
                dst[q, off:off + seg] = _bf(xc.T)

    if full:
        for g in range(N_POOL_GROUPS):
            zcols = slice(OFF_POOL_Z + g * POOL_GROUP_W, OFF_POOL_Z + (g + 1) * POOL_GROUP_W)
            zp_refs[g][...] = _bf(_silu(_dot(hm, w_ref[:, zcols])))
        for j in range(0, W_SSD, CONV_SEG):
            zcols = slice(OFF_SSD_Z + j, OFF_SSD_Z + j + CONV_SEG)
            zs_ref[:, j:j + CONV_SEG] = _bf(_silu(_dot(hm, w_ref[:, zcols])))


def _projection(x2d, norm_w, shift, scale, w_bf, wdt_bf, conv_w, conv_b, alog_col, bias_col,
                seq_len, tm, full):
    n_tok = x2d.shape[0]
    tiles_per_seq = seq_len // tm
    n_mod = shift.shape[0]
    nct = n_tok // CHUNK
    per = tm // SUBLANES
    last_halo = n_tok // SUBLANES - 1
    kern = functools.partial(_proj_kernel, tm=tm, tiles_per_seq=tiles_per_seq, full=full)
    mod_map = (lambda i: (i // tiles_per_seq, 0, 0)) if n_mod > 1 else (lambda i: (0, 0, 0))
    mod_spec = pl.BlockSpec((1, 1, D_MODEL), mod_map)
    const = lambda i: (0, 0)
    tok = lambda i: (i, 0)
    chunk3 = lambda i: (i, 0, 0)
    q = tm // CHUNK
    xs_t = (jax.ShapeDtypeStruct((nct, W_SSD, CHUNK), jnp.bfloat16),
            pl.BlockSpec((q, W_SSD, CHUNK), chunk3))
    b_tok = (jax.ShapeDtypeStruct((n_tok, GN), jnp.bfloat16), pl.BlockSpec((tm, GN), tok))
    c_t = (jax.ShapeDtypeStruct((nct, GN, CHUNK), jnp.bfloat16), pl.BlockSpec((q, GN, CHUNK), chunk3))
    dt = (jax.ShapeDtypeStruct((nct, 2 * HEADS, CHUNK), jnp.float32),
          pl.BlockSpec((q, 2 * HEADS, CHUNK), chunk3))
    if full:
        u = (jax.ShapeDtypeStruct((n_tok, POOL_GROUP_W), jnp.float32),
             pl.BlockSpec((tm, POOL_GROUP_W), tok))
        zp = (jax.ShapeDtypeStruct((n_tok, POOL_GROUP_W), jnp.bfloat16),
              pl.BlockSpec((tm, POOL_GROUP_W), tok))
        zs = (jax.ShapeDtypeStruct((n_tok, W_SSD), jnp.bfloat16), pl.BlockSpec((tm, W_SSD), tok))
        outs = [u] * N_POOL_GROUPS + [zp] * N_POOL_GROUPS + [zs, xs_t, b_tok, c_t, dt, dt]
    else:
        outs = [xs_t, b_tok, dt, dt]
    return pl.pallas_call(
        kern,
        grid=(n_tok // tm,),
        in_specs=[pl.BlockSpec((tm, D_MODEL), tok),
                  pl.BlockSpec((SUBLANES, D_MODEL), lambda i: (jnp.maximum(i * per - 1, 0), 0)),
                  pl.BlockSpec((SUBLANES, D_MODEL), lambda i: (jnp.minimum((i + 1) * per, last_halo), 0)),
                  pl.BlockSpec((1, D_MODEL), const),
                  mod_spec, mod_spec,
                  pl.BlockSpec(w_bf.shape, const),
                  pl.BlockSpec((D_MODEL, DT_PAD), const),
                  pl.BlockSpec((D_CONV, CONV_DIM), const),
                  pl.BlockSpec((1, CONV_DIM), const),
                  pl.BlockSpec((2 * HEADS, 1), const),
                  pl.BlockSpec((2 * HEADS, 1), const)],
        out_specs=[o[1] for o in outs],
        out_shape=[o[0] for o in outs],
        scratch_shapes=[pltpu.VMEM((tm + 2 * SUBLANES, CONV_SEG), jnp.float32),
                        pltpu.VMEM((CONV_SEG // LANES, tm, LANES), jnp.float32),
                        pltpu.VMEM((D_MODEL // LANES, tm, LANES), jnp.float32)],
        compiler_params=pltpu.CompilerParams(vmem_limit_bytes=VMEM_LIMIT),
        name="proj" if full else "proj_ctx",
    )(x2d, x2d, x2d, norm_w.reshape(1, D_MODEL), shift, scale, w_bf, wdt_bf, conv_w, conv_b,
      alog_col, bias_col)


def _sweep_chunks(chunk_body, chunks_per_step, reverse):
    def body(i, carry):
        chunk_body(chunks_per_step - 1 - i if reverse else i)
        return carry
    jax.lax.fori_loop(0, chunks_per_step, body, 0, unroll=True)


def _tok_rows(q):
    return pl.ds(pl.multiple_of(q * CHUNK, CHUNK), CHUNK)


def _state_update(h_ref, xs_t_ref, b_ref, q, scale_in, chunk_decay):
    for g in range(GROUPS):
        bg = b_ref[_tok_rows(q), g * D_STATE:(g + 1) * D_STATE]
        xd = []
        for r in range(HEADS_PER_GROUP):
            h = g * HEADS_PER_GROUP + r
            x_h = xs_t_ref[q, h * HEADDIM:(h + 1) * HEADDIM].astype(jnp.float32)
            xd.append(_bf(x_h * scale_in[h:h + 1]))
        s_new = _dot(jnp.concatenate(xd, axis=0), bg)
        for r in range(HEADS_PER_GROUP):
            h = g * HEADS_PER_GROUP + r
            hr = slice(h * HEADDIM, (h + 1) * HEADDIM)
            h_ref[hr] = h_ref[hr] * chunk_decay[h:h + 1] + s_new[r * HEADDIM:(r + 1) * HEADDIM]


def _ssd_fwd_kernel(xs_t_ref, b_ref, dt_ref, cum_ref, alog_ref, h0_ref, *rest, n_steps, cps, with_y):
    if with_y:
        c_t_ref, dskip_ref, y_ref, hout_ref, h_ref = rest
    else:
        hout_ref, h_ref = rest
    step = pl.program_id(1)

    @pl.when(step == 0)
    def _():
        h_ref[...] = h0_ref[0]

    if with_y:
        a_b = -jnp.exp(alog_ref[HEADS:])
        src = jax.lax.broadcasted_iota(jnp.int32, (CHUNK, CHUNK), 0)
        dst = jax.lax.broadcasted_iota(jnp.int32, (CHUNK, CHUNK), 1)
        causal = src <= dst
        is_diag = src == dst

    def chunk(q):
        dt_f, cum_f = dt_ref[q, :HEADS], cum_ref[q, :HEADS]
        tot_f = cum_f[:, CHUNK - 1:CHUNK]
        scale_in = dt_f * jnp.exp(tot_f - cum_f)
        chunk_decay = jnp.exp(tot_f)

        if with_y:
            dt_b, cum_b = dt_ref[q, HEADS:], cum_ref[q, HEADS:]
            cumx_b = cum_b - dt_b * a_b
            col_terms = jnp.concatenate(
                [jnp.log(dt_f) - cum_f, jnp.log(dt_b) + cumx_b,
                 jnp.zeros((CHUNK - 2 * HEADS, CHUNK), jnp.float32)], axis=0).T
            row_f = cum_f
            row_b = -cumx_b
            decay_out_f = jnp.exp(cum_f)
            for g in range(GROUPS):
                bg = b_ref[_tok_rows(q), g * D_STATE:(g + 1) * D_STATE]
                cg_t = c_t_ref[q, g * D_STATE:(g + 1) * D_STATE]
                rows = slice(g * HEADS_PER_GROUP * HEADDIM, (g + 1) * HEADS_PER_GROUP * HEADDIM)
                g_t = _dot(bg, cg_t)
                g_diag = jnp.sum(jnp.where(is_diag, g_t, 0.0), axis=0, keepdims=True)
                y_off = _dot(_bf(h_ref[rows]), cg_t)
                for r in range(HEADS_PER_GROUP):
                    h = g * HEADS_PER_GROUP + r
                    hr = slice(h * HEADDIM, (h + 1) * HEADDIM)
                    x_bf = xs_t_ref[q, hr]
                    col_f = jnp.broadcast_to(col_terms[:, h:h + 1], (CHUNK, CHUNK))
                    col_b = jnp.broadcast_to(col_terms[:, HEADS + h:HEADS + h + 1], (CHUNK, CHUNK))
                    expo = jnp.where(causal, col_f + row_f[h:h + 1], col_b + row_b[h:h + 1])
                    w_t = _bf(g_t * jnp.exp(expo))
                    y_h = _dot(x_bf, w_t)
                    y_h = y_h + y_off[r * HEADDIM:(r + 1) * HEADDIM] * decay_out_f[h:h + 1]
                    skip = dskip_ref[hr] + g_diag * dt_b[h:h + 1]
                    y_ref[q, hr] = y_h + skip * x_bf.astype(jnp.float32)

        _state_update(h_ref, xs_t_ref, b_ref, q, scale_in, chunk_decay)

    _sweep_chunks(chunk, cps, reverse=False)

    @pl.when(step == n_steps - 1)
    def _():
        hout_ref[0] = h_ref[...]


def _ssd_bwd_kernel(xs_t_ref, b_ref, dt_ref, cum_ref, alog_ref, h0_ref, *rest, n_steps, cps, with_y):
    if with_y:
        c_t_ref, ypart_ref, y_ref, hout_ref, h_ref = rest
    else:
        hout_ref, h_ref = rest
    step = pl.program_id(1)

    @pl.when(step == 0)
    def _():
        h_ref[...] = h0_ref[0]

    a_b = -jnp.exp(alog_ref[HEADS:])

    def chunk(q):
        dt_b, cum_b = dt_ref[q, HEADS:], cum_ref[q, HEADS:]
        tot_b = cum_b[:, CHUNK - 1:CHUNK]
        cumx_b = cum_b - dt_b * a_b
        scale_in = dt_b * jnp.exp(cumx_b)
        chunk_decay = jnp.exp(tot_b)

        if with_y:
            decay_out = jnp.exp(tot_b - cumx_b)
            y_parts = []
            for g in range(GROUPS):
                cg_t = c_t_ref[q, g * D_STATE:(g + 1) * D_STATE]
                rows = slice(g * HEADS_PER_GROUP * HEADDIM, (g + 1) * HEADS_PER_GROUP * HEADDIM)
                y_off = _dot(_bf(h_ref[rows]), cg_t)
                for r in range(HEADS_PER_GROUP):
                    h = g * HEADS_PER_GROUP + r
                    hr = slice(h * HEADDIM, (h + 1) * HEADDIM)
                    y_parts.append(ypart_ref[q, hr]
                                   + y_off[r * HEADDIM:(r + 1) * HEADDIM] * decay_out[h:h + 1])
            y_ref[_tok_rows(q), :] = jnp.concatenate(y_parts, axis=0).T

        _state_update(h_ref, xs_t_ref, b_ref, q, scale_in, chunk_decay)

    _sweep_chunks(chunk, cps, reverse=True)

    @pl.when(step == n_steps - 1)
    def _():
        hout_ref[0] = h_ref[...]


def _ssd_sweep(reverse, xs_t, b_tok, dt, cum, alog_col, h0, bsz, n_chunks, c_t=None, dskip_b=None,
               y_part=None):
    with_y = c_t is not None
    n_tok = b_tok.shape[0]
    cps = min(n_chunks, SSD_CHUNKS_PER_STEP)
    n_steps = n_chunks // cps

    def block_of(b, s):
        return b * n_steps + (n_steps - 1 - s if reverse else s)

    tok = lambda b, s: (block_of(b, s), 0)
    chunk3 = lambda b, s: (block_of(b, s), 0, 0)
    const2 = lambda b, s: (0, 0)
    state3 = lambda b, s: (b, 0, 0)
    h_spec = pl.BlockSpec((1, W_SSD, D_STATE), state3)
    h_shape = jax.ShapeDtypeStruct((bsz, W_SSD, D_STATE), jnp.float32)
    head_spec = pl.BlockSpec((cps, 2 * HEADS, CHUNK), chunk3)
    in_specs = [pl.BlockSpec((cps, W_SSD, CHUNK), chunk3),
                pl.BlockSpec((cps * CHUNK, GN), tok),
                head_spec, head_spec,
                pl.BlockSpec((2 * HEADS, 1), const2),
                h_spec]
    args = [xs_t, b_tok, dt, cum, alog_col, h0]
    y_t_spec = pl.BlockSpec((cps, W_SSD, CHUNK), chunk3)
    if with_y:
        in_specs.append(pl.BlockSpec((cps, GN, CHUNK), chunk3))
        args.append(c_t)
        if reverse:
            in_specs.append(y_t_spec)
            args.append(y_part)
            out_shape = [jax.ShapeDtypeStruct((n_tok, W_SSD), jnp.float32), h_shape]
            out_specs = [pl.BlockSpec((cps * CHUNK, W_SSD), tok), h_spec]
        else:
            in_specs.append(pl.BlockSpec((W_SSD, CHUNK), const2))
            args.append(dskip_b)
            out_shape = [jax.ShapeDtypeStruct(xs_t.shape, jnp.float32), h_shape]
            out_specs = [y_t_spec, h_spec]
    else:
        out_shape = [h_shape]
        out_specs = [h_spec]
    body = _ssd_bwd_kernel if reverse else _ssd_fwd_kernel
    name = ("ssd_bwd" if reverse else "ssd_fwd") + ("" if with_y else "_state")
    return pl.pallas_call(
        functools.partial(body, n_steps=n_steps, cps=cps, with_y=with_y),
        grid=(bsz, n_steps),
        in_specs=in_specs,
        out_specs=out_specs,
        out_shape=out_shape,
        scratch_shapes=[pltpu.VMEM((W_SSD, D_STATE), jnp.float32)],
        compiler_params=pltpu.CompilerParams(
            dimension_semantics=("arbitrary", "arbitrary"), vmem_limit_bytes=VMEM_LIMIT),
        name=name,
    )(*args)


POOL_TILE_ROWS = 4
POOL_TILE = POOL_TILE_ROWS * GRID_W


def _pool_constants(window, n_rows):
    lo_off, hi_off = -(window // 2), window - window // 2
    col = np.arange(GRID_W)
    lo = np.clip(col + lo_off, 0, GRID_W)
    hi = np.clip(col + hi_off, 0, GRID_W)
    band = ((col[None, :] >= lo[:, None]) & (col[None, :] < hi[:, None])).astype(np.float32)
    band_tile = np.kron(np.eye(POOL_TILE_ROWS, dtype=np.float32), band)
    row = np.arange(n_rows)
    cnt_r = np.clip(row + hi_off, 0, n_rows) - np.clip(row + lo_off, 0, n_rows)
    inv = 1.0 / (cnt_r[:, None] * (hi - lo)[None, :]).astype(np.float64)
    inv = np.broadcast_to(inv.reshape(-1, 1), (n_rows * GRID_W, 128)).astype(np.float32)
    return jnp.asarray(band_tile, jnp.bfloat16), jnp.asarray(inv)


def _pool_kernel(u_ref, z_ref, band_ref, inv_ref, w_ref, scale_ref, o_ref, *, window, n_rows):
    def grid_row(r):
        return u_ref[r * GRID_W:(r + 1) * GRID_W]

    def bounds(r):
        return max(r - window // 2, 0), min(r + window - window // 2, n_rows)

    band = band_ref[...]
    rsum, tile_rows = None, []
    for r in range(n_rows):
        lo, hi = bounds(r)
        if r == 0 or window <= 2:
            rsum = grid_row(lo)
            for k in range(lo + 1, hi):
                rsum = rsum + grid_row(k)
        else:
            prev_lo, prev_hi = bounds(r - 1)
            if hi > prev_hi:
                rsum = rsum + grid_row(hi - 1)
            if lo > prev_lo:
                rsum = rsum - grid_row(prev_lo)
        tile_rows.append(rsum)
        if len(tile_rows) < POOL_TILE_ROWS:
            continue
        base = (r + 1 - POOL_TILE_ROWS) * GRID_W
        rows = slice(base, base + POOL_TILE)
        rs = jnp.concatenate(tile_rows, axis=0)
        tile_rows = []
        box = _dot(band, _bf(rs))
        inv = inv_ref[rows]
        mean = box * jnp.concatenate([inv, inv], axis=1)
        d = mean - u_ref[rows]
        y = _dot(_bf(d), w_ref[0]) * scale_ref[...]
        o_ref[rows] = _bf(y * z_ref[rows].astype(jnp.float32))


def _pool_group(u, gate, pool_w_bf, pool_scale, g, bsz, n_img_tok):
    window = POOL_WINDOWS[g]
    n_rows = n_img_tok // GRID_W
    band, inv = _pool_constants(window, n_rows)
    kern = functools.partial(_pool_kernel, window=window, n_rows=n_rows)
    img = pl.BlockSpec((n_img_tok, POOL_GROUP_W), lambda b: (b, 0))
    return pl.pallas_call(
        kern,
        grid=(bsz,),
        in_specs=[img, img,
                  pl.BlockSpec((POOL_TILE, POOL_TILE), lambda b: (0, 0)),
                  pl.BlockSpec((n_img_tok, 128), lambda b: (0, 0)),
                  pl.BlockSpec((1, POOL_GROUP_W, POOL_GROUP_W), lambda b: (g, 0, 0)),
                  pl.BlockSpec((1, POOL_GROUP_W), lambda b: (0, g))],
        out_specs=img,
        out_shape=jax.ShapeDtypeStruct((bsz * n_img_tok, POOL_GROUP_W), jnp.bfloat16),
        compiler_params=pltpu.CompilerParams(vmem_limit_bytes=VMEM_LIMIT),
        name=f"pool{window}",
    )(u, gate, band, inv, pool_w_bf, pool_scale)


def _out_kernel(yp0_ref, yp1_ref, yp2_ref, yp3_ref, ys_ref, zs_ref, x_ref, gate_ref, snw_ref,
                wout_ref, fnw_ref, o_ref):
    acc = None
    for g, yp_ref in enumerate((yp0_ref, yp1_ref, yp2_ref, yp3_ref)):
        part = _dot(yp_ref[...], wout_ref[g * POOL_GROUP_W:(g + 1) * POOL_GROUP_W])
        acc = part if acc is None else acc + part
    gw = W_SSD // GROUPS
    for g in range(GROUPS):
        cols = slice(g * gw, (g + 1) * gw)
        gated = ys_ref[:, cols] * zs_ref[:, cols].astype(jnp.float32)
        ms = jnp.mean(gated * gated, axis=-1, keepdims=True)
        yn = gated * jax.lax.rsqrt(ms + EPS) * snw_ref[:, cols]
        acc = acc + _dot(_bf(yn), wout_ref[W_POOL + g * gw:W_POOL + (g + 1) * gw])
    h = x_ref[...] + gate_ref[0] * acc
    ms = jnp.mean(h * h, axis=-1, keepdims=True)
    o_ref[...] = h * jax.lax.rsqrt(ms + EPS) * fnw_ref[...]


def _output(y_pool, y_ssd, z_ssd, x2d, gate, ssd_norm_w, w_out_bf, final_norm_w, rows_per_mod, tm):
    n_tok = x2d.shape[0]
    tiles_per_mod = rows_per_mod // tm
    tok = lambda i: (i, 0)
    const = lambda i: (0, 0)
    return pl.pallas_call(
        _out_kernel,
        grid=(n_tok // tm,),
        in_specs=[pl.BlockSpec((tm, POOL_GROUP_W), tok)] * 4 + [
            pl.BlockSpec((tm, W_SSD), tok),
            pl.BlockSpec((tm, W_SSD), tok),
            pl.BlockSpec((tm, D_MODEL), tok),
            pl.BlockSpec((1, 1, D_MODEL), lambda i: (i // tiles_per_mod, 0, 0)),
            pl.BlockSpec((1, W_SSD), const),
            pl.BlockSpec((W_POOL + W_SSD, D_MODEL), const),
            pl.BlockSpec((1, D_MODEL), const)],
        out_specs=pl.BlockSpec((tm, D_MODEL), tok),
        out_shape=jax.ShapeDtypeStruct((n_tok, D_MODEL), jnp.float32),
        compiler_params=pltpu.CompilerParams(vmem_limit_bytes=VMEM_LIMIT),
        name="out",
    )(*y_pool, y_ssd, z_ssd, x2d, gate, ssd_norm_w.reshape(1, W_SSD), w_out_bf,
      final_norm_w.reshape(1, D_MODEL))


def kernel(x, c, ctx, c_ctx, norm_w, w_ada, b_ada, w_in, conv_w, conv_b, a_log, dt_bias, d_skip,
           ssd_norm_w, pool_w, pool_scale, w_out, final_norm_w):
    bsz, seq, _ = x.shape
    ctx_len = ctx.shape[1]
    depth = norm_w.shape[0]
    assert depth == 1, "single-layer block: the context stream update is never consumed"
    assert seq % 512 == 0 and ctx_len % CHUNK == 0 and seq % GRID_W == 0

    mod_rows = -(-(bsz + 1) // SUBLANES) * SUBLANES
    cond = jnp.concatenate([c, c_ctx[None], jnp.zeros((mod_rows - bsz - 1, D_MODEL), c.dtype)])
    mod = _modulation(cond, w_ada[0], b_ada[0])
    shift = mod[:, :D_MODEL].reshape(mod_rows, 1, D_MODEL)
    scale = mod[:, D_MODEL:2 * D_MODEL].reshape(mod_rows, 1, D_MODEL)
    gate = mod[:, 2 * D_MODEL:].reshape(mod_rows, 1, D_MODEL)

    w_in_bf = _bf(w_in[0])
    w_dt_bf = jnp.pad(_bf(w_in[0, :, OFF_DT:]), ((0, 0), (0, DT_PAD - 2 * HEADS)))
    alog_col = a_log[0].reshape(2 * HEADS, 1)
    bias_col = dt_bias[0].reshape(2 * HEADS, 1)
    dskip_b = jnp.broadcast_to(jnp.repeat(d_skip[0], HEADDIM)[:, None], (W_SSD, CHUNK))
    conv_b2 = conv_b[0].reshape(1, CONV_DIM)
    zero_state = jnp.zeros((bsz, W_SSD, D_STATE), jnp.float32)

    ctx2d = ctx.reshape(bsz * ctx_len, D_MODEL)
    xs_t_c, b_c, dt_c, cum_c = _projection(
        ctx2d, norm_w[0], shift[bsz:bsz + 1], scale[bsz:bsz + 1], w_in_bf, w_dt_bf, conv_w[0],
        conv_b2, alog_col, bias_col, ctx_len, ctx_len, full=False)
    nc_ctx = ctx_len // CHUNK
    (h_fwd,) = _ssd_sweep(False, xs_t_c, b_c, dt_c, cum_c, alog_col, zero_state, bsz, nc_ctx)
    (h_bwd,) = _ssd_sweep(True, xs_t_c, b_c, dt_c, cum_c, alog_col, zero_state, bsz, nc_ctx)

    x2d = x.reshape(bsz * seq, D_MODEL)
    outs = _projection(x2d, norm_w[0], shift, scale, w_in_bf, w_dt_bf, conv_w[0], conv_b2,
                       alog_col, bias_col, seq, 512, full=True)
    u_pool, gate_pool = outs[:N_POOL_GROUPS], outs[N_POOL_GROUPS:2 * N_POOL_GROUPS]
    gate_ssd, xs_t, b_tok, c_t, dt, cum = outs[2 * N_POOL_GROUPS:]
    nc = seq // CHUNK
    y_part, _ = _ssd_sweep(False, xs_t, b_tok, dt, cum, alog_col, h_fwd, bsz, nc, c_t=c_t,
                           dskip_b=dskip_b)
    y_ssd, _ = _ssd_sweep(True, xs_t, b_tok, dt, cum, alog_col, h_bwd, bsz, nc, c_t=c_t,
                          y_part=y_part)

    pool_w_bf = _bf(pool_w[0])
    y_pool = [_pool_group(u_pool[g], gate_pool[g], pool_w_bf, pool_scale, g, bsz, seq)
              for g in range(N_POOL_GROUPS)]
    out = _output(y_pool, y_ssd, gate_ssd, x2d, gate, ssd_norm_w[0], _bf(w_out[0]), final_norm_w,
                  seq, 512)
    return out.reshape(bsz, seq, D_MODEL)
```

```python
import functools

import numpy as np
import jax
import jax.numpy as jnp
from jax.experimental import pallas as pl
from jax.experimental.pallas import tpu as pltpu

D_MODEL = 1024
GRID_W = 64
W_POOL = 1024
W_SSD = 1024
POOL_WINDOWS = (2, 4, 8, 16)
N_POOL_GROUPS = len(POOL_WINDOWS)
POOL_GROUP_W = 256
HEADDIM = 64
HEADS = 16
GROUPS = 4
HEADS_PER_GROUP = 4
D_STATE = 128
D_CONV = 4
CONV_LEFT = 2
CHUNK = 128
GN = GROUPS * D_STATE
CONV_DIM = W_SSD + 2 * GN
OFF_POOL_Z = W_POOL
OFF_SSD_Z = 2 * W_POOL
OFF_XBC = 2 * W_POOL + W_SSD
OFF_DT = OFF_XBC + CONV_DIM
DT_PAD = 128
EPS = 1e-6
SUBLANES = 8
LANES = 128
IL_GROUPS = CHUNK // SUBLANES
CONV_SEG = 512
SSD_CHUNKS_PER_STEP = 4
VMEM_LIMIT = 56 * 1024 * 1024


def _silu(v):
    h = 0.5 * v
    return h + h * jnp.tanh(h)


def _softplus(v):
    return jnp.maximum(v, 0.0) + jnp.log1p(jnp.exp(-jnp.abs(v)))


def _bf(v):
    return v.astype(jnp.bfloat16)


def _dot(a, b):
    return jnp.dot(a, b, preferred_element_type=jnp.float32)


def _mod_kernel(c_ref, w_ref, b_ref, o_ref):
    s = _silu(c_ref[...])
    o_ref[...] = jnp.dot(s, w_ref[...], preferred_element_type=jnp.float32,
                         precision=jax.lax.Precision.HIGHEST) + b_ref[...]


def _modulation(cond_rows, w_ada, b_ada):
    rows = cond_rows.shape[0]
    n_out = w_ada.shape[1]
    tn = 1024
    return pl.pallas_call(
        _mod_kernel,
        grid=(n_out // tn,),
        in_specs=[pl.BlockSpec((rows, D_MODEL), lambda j: (0, 0)),
                  pl.BlockSpec((D_MODEL, tn), lambda j: (0, j)),
                  pl.BlockSpec((1, tn), lambda j: (0, j))],
        out_specs=pl.BlockSpec((rows, tn), lambda j: (0, j)),
        out_shape=jax.ShapeDtypeStruct((rows, n_out), jnp.float32),
        compiler_params=pltpu.CompilerParams(vmem_limit_bytes=VMEM_LIMIT),
        name="mod",
    )(cond_rows, w_ada, b_ada.reshape(1, n_out))


def _lane_cumsum(v):
    lane = jax.lax.broadcasted_iota(jnp.int32, v.shape, 1)
    shift = 1
    while shift < CHUNK:
        v = v + jnp.where(lane >= shift, pltpu.roll(v, shift, 1), 0.0)
        shift *= 2
    return v


def _proj_kernel(x_ref, xp_ref, xn_ref, nw_ref, sh_ref, sc_ref, w_ref, wdt_ref, cw_ref, cb_ref,
                 alog_ref, bias_ref, *rest, tm, tiles_per_seq, full):
    if full:
        (u0, u1, u2, u3, zp0, zp1, zp2, zp3, zs_ref, xs_t_ref, b_ref, c_t_ref, dt_ref, cum_ref,
         pe_ref, xc_ref, mn_ref) = rest
        u_refs, zp_refs = (u0, u1, u2, u3), (zp0, zp1, zp2, zp3)
    else:
        xs_t_ref, b_ref, dt_ref, cum_ref, pe_ref, xc_ref, mn_ref = rest
    i = pl.program_id(0)
    pos = i % tiles_per_seq
    has_prev = pos > 0
    has_next = pos < tiles_per_seq - 1
    n_chunks = tm // CHUNK

    def modulated(v):
        ms = jnp.mean(v * v, axis=-1, keepdims=True)
        y = v * jax.lax.rsqrt(ms + EPS) * nw_ref[...]
        return y * (1.0 + sc_ref[0]) + sh_ref[0]

    m_tok = modulated(x_ref[...])
    hm = _bf(m_tok)
    for t in range(D_MODEL // LANES):
        mn_ref[t] = m_tok[:, t * LANES:(t + 1) * LANES]

    if full:
        for g in range(N_POOL_GROUPS):
            cols = slice(g * POOL_GROUP_W, (g + 1) * POOL_GROUP_W)
            u_refs[g][...] = _dot(hm, w_ref[:, cols])

    p_dt = _dot(hm, wdt_ref[...])
    a_col = -jnp.exp(alog_ref[...])
    for q in range(n_chunks):
        dt = _softplus(p_dt[q * CHUNK:(q + 1) * CHUNK].T[:2 * HEADS] + bias_ref[...])
        dt_ref[q] = dt
        cum_ref[q] = _lane_cumsum(dt * a_col)

    rows = [jnp.concatenate([mn_ref[t, pl.ds(q * CHUNK + b, SUBLANES, stride=IL_GROUPS), :]
                             for t in range(D_MODEL // LANES)], axis=1)
            for q in range(n_chunks) for b in range(IL_GROUPS)]
    halo = [jnp.where(has_prev, modulated(xp_ref[...]), 0.0),
            jnp.where(has_next, modulated(xn_ref[...]), 0.0)]
    hm_il = _bf(jnp.concatenate(halo + rows, axis=0))
    seg = pe_ref.shape[1]
    sub = jax.lax.broadcasted_iota(jnp.int32, (SUBLANES, seg), 0)

    def group(q, b):
        lo = 2 * SUBLANES + q * CHUNK + b * SUBLANES
        return pe_ref[lo:lo + SUBLANES]

    def shifted(q, b, delta):
        bb = b + delta
        if 0 <= bb < IL_GROUPS:
            return group(q, bb)
        if bb < 0:
            bb += IL_GROUPS
            if q == 0:
                first = pe_ref[bb - SUBLANES:bb - SUBLANES + 1]
            else:
                first = pe_ref[2 * SUBLANES + (q - 1) * CHUNK + bb * SUBLANES + SUBLANES - 1:
                               2 * SUBLANES + (q - 1) * CHUNK + (bb + 1) * SUBLANES]
            return jnp.where(sub == 0, first, pltpu.roll(group(q, bb), 1, 0))
        bb -= IL_GROUPS
        if q == n_chunks - 1:
            last = pe_ref[SUBLANES + bb:SUBLANES + bb + 1]
        else:
            nxt = 2 * SUBLANES + (q + 1) * CHUNK + bb * SUBLANES
            last = pe_ref[nxt:nxt + 1]
        return jnp.where(sub == SUBLANES - 1, last, pltpu.roll(group(q, bb), SUBLANES - 1, 0))

    for j in range(0, CONV_DIM, seg):
        is_x = j < W_SSD
        is_b = W_SSD <= j < W_SSD + GN
        if not full and not (is_x or is_b):
            continue
        pe_ref[...] = _dot(hm_il, w_ref[:, OFF_XBC + j:OFF_XBC + j + seg])
        taps = [cw_ref[k:k + 1, j:j + seg] for k in range(D_CONV)]
        bias = cb_ref[:, j:j + seg]
        for q in range(n_chunks):
            for b in range(IL_GROUPS):
                acc = bias
                for k in range(D_CONV):
                    acc = acc + shifted(q, b, k - CONV_LEFT) * taps[k]
                lo = q * CHUNK + b * SUBLANES
                act = _silu(acc)
                for t in range(seg // LANES):
                    xc_ref[t, lo:lo + SUBLANES] = act[:, t * LANES:(t + 1) * LANES]
        for q in range(n_chunks):
            xc = jnp.concatenate(
                [jnp.concatenate(
                    [xc_ref[t, pl.ds(q * CHUNK + (m % 2) * (CHUNK // 2) + m // 2, SUBLANES,
                                     stride=SUBLANES), :] for t in range(seg // LANES)], axis=1)
                 for m in range(IL_GROUPS)], axis=0)
            if is_b:
                b_ref[q * CHUNK:(q + 1) * CHUNK, j - W_SSD:j - W_SSD + seg] = _bf(xc)
            else:
                dst, off = (xs_t_ref, j) if is_x else (c_t_ref, j - W_SSD - GN)
                dst[q, off:off + seg] = _bf(xc.T)

    if full:
        for g in range(N_POOL_GROUPS):
            zcols = slice(OFF_POOL_Z + g * POOL_GROUP_W, OFF_POOL_Z + (g + 1) * POOL_GROUP_W)
            zp_refs[g][...] = _bf(_silu(_dot(hm, w_ref[:, zcols])))
        for j in range(0, W_SSD, CONV_SEG):
            zcols = slice(OFF_SSD_Z + j, OFF_SSD_Z + j + CONV_SEG)
            zs_ref[:, j:j + CONV_SEG] = _bf(_silu(_dot(hm, w_ref[:, zcols])))


def _projection(x2d, norm_w, shift, scale, w_bf, wdt_bf, conv_w, conv_b, alog_col, bias_col,
                seq_len, tm, full):
    n_tok = x2d.shape[0]
    tiles_per_seq = seq_len // tm
    n_mod = shift.shape[0]
    nct = n_tok // CHUNK
    per = tm // SUBLANES
    last_halo = n_tok // SUBLANES - 1
    kern = functools.partial(_proj_kernel, tm=tm, tiles_per_seq=tiles_per_seq, full=full)
    mod_map = (lambda i: (i // tiles_per_seq, 0, 0)) if n_mod > 1 else (lambda i: (0, 0, 0))
    mod_spec = pl.BlockSpec((1, 1, D_MODEL), mod_map)
    const = lambda i: (0, 0)
    tok = lambda i: (i, 0)
    chunk3 = lambda i: (i, 0, 0)
    q = tm // CHUNK
    xs_t = (jax.ShapeDtypeStruct((nct, W_SSD, CHUNK), jnp.bfloat16),
            pl.BlockSpec((q, W_SSD, CHUNK), chunk3))
    b_tok = (jax.ShapeDtypeStruct((n_tok, GN), jnp.bfloat16), pl.BlockSpec((tm, GN), tok))
    c_t = (jax.ShapeDtypeStruct((nct, GN, CHUNK), jnp.bfloat16), pl.BlockSpec((q, GN, CHUNK), chunk3))
    dt = (jax.ShapeDtypeStruct((nct, 2 * HEADS, CHUNK), jnp.float32),
          pl.BlockSpec((q, 2 * HEADS, CHUNK), chunk3))
    if full:
        u = (jax.ShapeDtypeStruct((n_tok, POOL_GROUP_W), jnp.float32),
             pl.BlockSpec((tm, POOL_GROUP_W), tok))
        zp = (jax.ShapeDtypeStruct((n_tok, POOL_GROUP_W), jnp.bfloat16),
              pl.BlockSpec((tm, POOL_GROUP_W), tok))
        zs = (jax.ShapeDtypeStruct((n_tok, W_SSD), jnp.bfloat16), pl.BlockSpec((tm, W_SSD), tok))
        outs = [u] * N_POOL_GROUPS + [zp] * N_POOL_GROUPS + [zs, xs_t, b_tok, c_t, dt, dt]
    else:
        outs = [xs_t, b_tok, dt, dt]
    return pl.pallas_call(
        kern,
        grid=(n_tok // tm,),
        in_specs=[pl.BlockSpec((tm, D_MODEL), tok),
                  pl.BlockSpec((SUBLANES, D_MODEL), lambda i: (jnp.maximum(i * per - 1, 0), 0)),
                  pl.BlockSpec((SUBLANES, D_MODEL), lambda i: (jnp.minimum((i + 1) * per, last_halo), 0)),
                  pl.BlockSpec((1, D_MODEL), const),
                  mod_spec, mod_spec,
                  pl.BlockSpec(w_bf.shape, const),
                  pl.BlockSpec((D_MODEL, DT_PAD), const),
                  pl.BlockSpec((D_CONV, CONV_DIM), const),
                  pl.BlockSpec((1, CONV_DIM), const),
                  pl.BlockSpec((2 * HEADS, 1), const),
                  pl.BlockSpec((2 * HEADS, 1), const)],
        out_specs=[o[1] for o in outs],
        out_shape=[o[0] for o in outs],
        scratch_shapes=[pltpu.VMEM((tm + 2 * SUBLANES, CONV_SEG), jnp.float32),
                        pltpu.VMEM((CONV_SEG // LANES, tm, LANES), jnp.float32),
                        pltpu.VMEM((D_MODEL // LANES, tm, LANES), jnp.float32)],
        compiler_params=pltpu.CompilerParams(vmem_limit_bytes=VMEM_LIMIT),
        name="proj" if full else "proj_ctx",
    )(x2d, x2d, x2d, norm_w.reshape(1, D_MODEL), shift, scale, w_bf, wdt_bf, conv_w, conv_b,
      alog_col, bias_col)


def _sweep_chunks(chunk_body, chunks_per_step, reverse):
    def body(i, carry):
        chunk_body(chunks_per_step - 1 - i if reverse else i)
        return carry
    jax.lax.fori_loop(0, chunks_per_step, body, 0, unroll=True)


def _tok_rows(q):
    return pl.ds(pl.multiple_of(q * CHUNK, CHUNK), CHUNK)


def _state_update(h_ref, xs_t_ref, b_ref, q, scale_in, chunk_decay):
    for g in range(GROUPS):
        bg = b_ref[_tok_rows(q), g * D_STATE:(g + 1) * D_STATE]
        xd = []
        for r in range(HEADS_PER_GROUP):
            h = g * HEADS_PER_GROUP + r
            x_h = xs_t_ref[q, h * HEADDIM:(h + 1) * HEADDIM].astype(jnp.float32)
            xd.append(_bf(x_h * scale_in[h:h + 1]))
        s_new = _dot(jnp.concatenate(xd, axis=0), bg)
        for r in range(HEADS_PER_GROUP):
            h = g * HEADS_PER_GROUP + r
            hr = slice(h * HEADDIM, (h + 1) * HEADDIM)
            h_ref[hr] = h_ref[hr] * chunk_decay[h:h + 1] + s_new[r * HEADDIM:(r + 1) * HEADDIM]


def _ssd_fwd_kernel(xs_t_ref, b_ref, dt_ref, cum_ref, alog_ref, h0_ref, *rest, n_steps, cps, with_y):
    if with_y:
        c_t_ref, dskip_ref, y_ref, hout_ref, h_ref = rest
    else:
        hout_ref, h_ref = rest
    step = pl.program_id(1)

    @pl.when(step == 0)
    def _():
        h_ref[...] = h0_ref[0]

    if with_y:
        a_b = -jnp.exp(alog_ref[HEADS:])
        src = jax.lax.broadcasted_iota(jnp.int32, (CHUNK, CHUNK), 0)
        dst = jax.lax.broadcasted_iota(jnp.int32, (CHUNK, CHUNK), 1)
        causal = src <= dst
        is_diag = src == dst

    def chunk(q):
        dt_f, cum_f = dt_ref[q, :HEADS], cum_ref[q, :HEADS]
        tot_f = cum_f[:, CHUNK - 1:CHUNK]
        scale_in = dt_f * jnp.exp(tot_f - cum_f)
        chunk_decay = jnp.exp(tot_f)

        if with_y:
            dt_b, cum_b = dt_ref[q, HEADS:], cum_ref[q, HEADS:]
            cumx_b = cum_b - dt_b * a_b
            col_terms = jnp.concatenate(
                [jnp.log(dt_f) - cum_f, jnp.log(dt_b) + cumx_b,
                 jnp.zeros((CHUNK - 2 * HEADS, CHUNK), jnp.float32)], axis=0).T
            row_f = cum_f
            row_b = -cumx_b
            decay_out_f = jnp.exp(cum_f)
            for g in range(GROUPS):
                bg = b_ref[_tok_rows(q), g * D_STATE:(g + 1) * D_STATE]
                cg_t = c_t_ref[q, g * D_STATE:(g + 1) * D_STATE]
                rows = slice(g * HEADS_PER_GROUP * HEADDIM, (g + 1) * HEADS_PER_GROUP * HEADDIM)
                g_t = _dot(bg, cg_t)
                g_diag = jnp.sum(jnp.where(is_diag, g_t, 0.0), axis=0, keepdims=True)
                y_off = _dot(_bf(h_ref[rows]), cg_t)
                for r in range(HEADS_PER_GROUP):
                    h = g * HEADS_PER_GROUP + r
                    hr = slice(h * HEADDIM, (h + 1) * HEADDIM)
                    x_bf = xs_t_ref[q, hr]
                    col_f = jnp.broadcast_to(col_terms[:, h:h + 1], (CHUNK, CHUNK))
                    col_b = jnp.broadcast_to(col_terms[:, HEADS + h:HEADS + h + 1], (CHUNK, CHUNK))
                    expo = jnp.where(causal, col_f + row_f[h:h + 1], col_b + row_b[h:h + 1])
                    w_t = _bf(g_t * jnp.exp(expo))
                    y_h = _dot(x_bf, w_t)
                    y_h = y_h + y_off[r * HEADDIM:(r + 1) * HEADDIM] * decay_out_f[h:h + 1]
                    skip = dskip_ref[hr] + g_diag * dt_b[h:h + 1]
                    y_ref[q, hr] = _bf(y_h + skip * x_bf.astype(jnp.float32))

        _state_update(h_ref, xs_t_ref, b_ref, q, scale_in, chunk_decay)

    _sweep_chunks(chunk, cps, reverse=False)

    @pl.when(step == n_steps - 1)
    def _():
        hout_ref[0] = h_ref[...]


def _ssd_bwd_kernel(xs_t_ref, b_ref, dt_ref, cum_ref, alog_ref, h0_ref, *rest, n_steps, cps, with_y):
    if with_y:
        c_t_ref, ypart_ref, y_ref, hout_ref, h_ref = rest
    else:
        hout_ref, h_ref = rest
    step = pl.program_id(1)

    @pl.when(step == 0)
    def _():
        h_ref[...] = h0_ref[0]

    a_b = -jnp.exp(alog_ref[HEADS:])

    def chunk(q):
        dt_b, cum_b = dt_ref[q, HEADS:], cum_ref[q, HEADS:]
        tot_b = cum_b[:, CHUNK - 1:CHUNK]
        cumx_b = cum_b - dt_b * a_b
        scale_in = dt_b * jnp.exp(cumx_b)
        chunk_decay = jnp.exp(tot_b)

        if with_y:
            decay_out = jnp.exp(tot_b - cumx_b)
            y_parts = []
            for g in range(GROUPS):
                cg_t = c_t_ref[q, g * D_STATE:(g + 1) * D_STATE]
                rows = slice(g * HEADS_PER_GROUP * HEADDIM, (g + 1) * HEADS_PER_GROUP * HEADDIM)
                y_off = _dot(_bf(h_ref[rows]), cg_t)
                for r in range(HEADS_PER_GROUP):
                    h = g * HEADS_PER_GROUP + r
                    hr = slice(h * HEADDIM, (h + 1) * HEADDIM)
                    y_parts.append(ypart_ref[q, hr].astype(jnp.float32)
                                   + y_off[r * HEADDIM:(r + 1) * HEADDIM] * decay_out[h:h + 1])
            y_ref[_tok_rows(q), :] = _bf(jnp.concatenate(y_parts, axis=0).T)

        _state_update(h_ref, xs_t_ref, b_ref, q, scale_in, chunk_decay)

    _sweep_chunks(chunk, cps, reverse=True)

    @pl.when(step == n_steps - 1)
    def _():
        hout_ref[0] = h_ref[...]


def _ssd_sweep(reverse, xs_t, b_tok, dt, cum, alog_col, h0, bsz, n_chunks, c_t=None, dskip_b=None,
               y_part=None):
    with_y = c_t is not None
    n_tok = b_tok.shape[0]
    cps = min(n_chunks, SSD_CHUNKS_PER_STEP)
    n_steps = n_chunks // cps

    def block_of(b, s):
        return b * n_steps + (n_steps - 1 - s if reverse else s)

    tok = lambda b, s: (block_of(b, s), 0)
    chunk3 = lambda b, s: (block_of(b, s), 0, 0)
    const2 = lambda b, s: (0, 0)
    state3 = lambda b, s: (b, 0, 0)
    h_spec = pl.BlockSpec((1, W_SSD, D_STATE), state3)
    h_shape = jax.ShapeDtypeStruct((bsz, W_SSD, D_STATE), jnp.float32)
    head_spec = pl.BlockSpec((cps, 2 * HEADS, CHUNK), chunk3)
    in_specs = [pl.BlockSpec((cps, W_SSD, CHUNK), chunk3),
                pl.BlockSpec((cps * CHUNK, GN), tok),
                head_spec, head_spec,
                pl.BlockSpec((2 * HEADS, 1), const2),
                h_spec]
    args = [xs_t, b_tok, dt, cum, alog_col, h0]
    y_t_spec = pl.BlockSpec((cps, W_SSD, CHUNK), chunk3)
    if with_y:
        in_specs.append(pl.BlockSpec((cps, GN, CHUNK), chunk3))
        args.append(c_t)
        if reverse:
            in_specs.append(y_t_spec)
            args.append(y_part)
            out_shape = [jax.ShapeDtypeStruct((n_tok, W_SSD), jnp.bfloat16), h_shape]
            out_specs = [pl.BlockSpec((cps * CHUNK, W_SSD), tok), h_spec]
        else:
            in_specs.append(pl.BlockSpec((W_SSD, CHUNK), const2))
            args.append(dskip_b)
            out_shape = [jax.ShapeDtypeStruct(xs_t.shape, jnp.bfloat16), h_shape]
            out_specs = [y_t_spec, h_spec]
    else:
        out_shape = [h_shape]
        out_specs = [h_spec]
    body = _ssd_bwd_kernel if reverse else _ssd_fwd_kernel
    name = ("ssd_bwd" if reverse else "ssd_fwd") + ("" if with_y else "_state")
    return pl.pallas_call(
        functools.partial(body, n_steps=n_steps, cps=cps, with_y=with_y),
        grid=(bsz, n_steps),
        in_specs=in_specs,
        out_specs=out_specs,
        out_shape=out_shape,
        scratch_shapes=[pltpu.VMEM((W_SSD, D_STATE), jnp.float32)],
        compiler_params=pltpu.CompilerParams(
            dimension_semantics=("arbitrary", "arbitrary"), vmem_limit_bytes=VMEM_LIMIT),
        name=name,
    )(*args)


POOL_TILE_ROWS = 4
POOL_TILE = POOL_TILE_ROWS * GRID_W


def _pool_constants(window, n_rows):
    lo_off, hi_off = -(window // 2), window - window // 2
    col = np.arange(GRID_W)
    lo = np.clip(col + lo_off, 0, GRID_W)
    hi = np.clip(col + hi_off, 0, GRID_W)
    band = ((col[None, :] >= lo[:, None]) & (col[None, :] < hi[:, None])).astype(np.float32)
    band_tile = np.kron(np.eye(POOL_TILE_ROWS, dtype=np.float32), band)
    row = np.arange(n_rows)
    cnt_r = np.clip(row + hi_off, 0, n_rows) - np.clip(row + lo_off, 0, n_rows)
    inv = 1.0 / (cnt_r[:, None] * (hi - lo)[None, :]).astype(np.float64)
    inv = np.broadcast_to(inv.reshape(-1, 1), (n_rows * GRID_W, 128)).astype(np.float32)
    return jnp.asarray(band_tile, jnp.bfloat16), jnp.asarray(inv)


def _pool_kernel(u_ref, z_ref, band_ref, inv_ref, w_ref, scale_ref, o_ref, *, window, n_rows):
    def grid_row(r):
        return u_ref[r * GRID_W:(r + 1) * GRID_W]

    def bounds(r):
        return max(r - window // 2, 0), min(r + window - window // 2, n_rows)

    band = band_ref[...]
    rsum, tile_rows = None, []
    for r in range(n_rows):
        lo, hi = bounds(r)
        if r == 0 or window <= 2:
            rsum = grid_row(lo)
            for k in range(lo + 1, hi):
                rsum = rsum + grid_row(k)
        else:
            prev_lo, prev_hi = bounds(r - 1)
            if hi > prev_hi:
                rsum = rsum + grid_row(hi - 1)
            if lo > prev_lo:
                rsum = rsum - grid_row(prev_lo)
        tile_rows.append(rsum)
        if len(tile_rows) < POOL_TILE_ROWS:
            continue
        base = (r + 1 - POOL_TILE_ROWS) * GRID_W
        rows = slice(base, base + POOL_TILE)
        rs = jnp.concatenate(tile_rows, axis=0)
        tile_rows = []
        box = _dot(band, _bf(rs))
        inv = inv_ref[rows]
        mean = box * jnp.concatenate([inv, inv], axis=1)
        d = mean - u_ref[rows]
        y = _dot(_bf(d), w_ref[0]) * scale_ref[...]
        o_ref[rows] = _bf(y * z_ref[rows].astype(jnp.float32))


def _pool_group(u, gate, pool_w_bf, pool_scale, g, bsz, n_img_tok):
    window = POOL_WINDOWS[g]
    n_rows = n_img_tok // GRID_W
    band, inv = _pool_constants(window, n_rows)
    kern = functools.partial(_pool_kernel, window=window, n_rows=n_rows)
    img = pl.BlockSpec((n_img_tok, POOL_GROUP_W), lambda b: (b, 0))
    return pl.pallas_call(
        kern,
        grid=(bsz,),
        in_specs=[img, img,
                  pl.BlockSpec((POOL_TILE, POOL_TILE), lambda b: (0, 0)),
                  pl.BlockSpec((n_img_tok, 128), lambda b: (0, 0)),
                  pl.BlockSpec((1, POOL_GROUP_W, POOL_GROUP_W), lambda b: (g, 0, 0)),
                  pl.BlockSpec((1, POOL_GROUP_W), lambda b: (0, g))],
        out_specs=img,
        out_shape=jax.ShapeDtypeStruct((bsz * n_img_tok, POOL_GROUP_W), jnp.bfloat16),
        compiler_params=pltpu.CompilerParams(vmem_limit_bytes=VMEM_LIMIT),
        name=f"pool{window}",
    )(u, gate, band, inv, pool_w_bf, pool_scale)


def _out_kernel(yp0_ref, yp1_ref, yp2_ref, yp3_ref, ys_ref, zs_ref, x_ref, gate_ref, snw_ref,
                wout_ref, fnw_ref, o_ref):
    acc = None
    for g, yp_ref in enumerate((yp0_ref, yp1_ref, yp2_ref, yp3_ref)):
        part = _dot(yp_ref[...], wout_ref[g * POOL_GROUP_W:(g + 1) * POOL_GROUP_W])
        acc = part if acc is None else acc + part
    gw = W_SSD // GROUPS
    for g in range(GROUPS):
        cols = slice(g * gw, (g + 1) * gw)
        gated = ys_ref[:, cols].astype(jnp.float32) * zs_ref[:, cols].astype(jnp.float32)
        ms = jnp.mean(gated * gated, axis=-1, keepdims=True)
        yn = gated * jax.lax.rsqrt(ms + EPS) * snw_ref[:, cols]
        acc = acc + _dot(_bf(yn), wout_ref[W_POOL + g * gw:W_POOL + (g + 1) * gw])
    h = x_ref[...] + gate_ref[0] * acc
    ms = jnp.mean(h * h, axis=-1, keepdims=True)
    o_ref[...] = h * jax.lax.rsqrt(ms + EPS) * fnw_ref[...]


def _output(y_pool, y_ssd, z_ssd, x2d, gate, ssd_norm_w, w_out_bf, final_norm_w, rows_per_mod, tm):
    n_tok = x2d.shape[0]
    tiles_per_mod = rows_per_mod // tm
    tok = lambda i: (i, 0)
    const = lambda i: (0, 0)
    return pl.pallas_call(
        _out_kernel,
        grid=(n_tok // tm,),
        in_specs=[pl.BlockSpec((tm, POOL_GROUP_W), tok)] * 4 + [
            pl.BlockSpec((tm, W_SSD), tok),
            pl.BlockSpec((tm, W_SSD), tok),
            pl.BlockSpec((tm, D_MODEL), tok),
            pl.BlockSpec((1, 1, D_MODEL), lambda i: (i // tiles_per_mod, 0, 0)),
            pl.BlockSpec((1, W_SSD), const),
            pl.BlockSpec((W_POOL + W_SSD, D_MODEL), const),
            pl.BlockSpec((1, D_MODEL), const)],
        out_specs=pl.BlockSpec((tm, D_MODEL), tok),
        out_shape=jax.ShapeDtypeStruct((n_tok, D_MODEL), jnp.float32),
        compiler_params=pltpu.CompilerParams(vmem_limit_bytes=VMEM_LIMIT),
        name="out",
    )(*y_pool, y_ssd, z_ssd, x2d, gate, ssd_norm_w.reshape(1, W_SSD), w_out_bf,
      final_norm_w.reshape(1, D_MODEL))


def kernel(x, c, ctx, c_ctx, norm_w, w_ada, b_ada, w_in, conv_w, conv_b, a_log, dt_bias, d_skip,
           ssd_norm_w, pool_w, pool_scale, w_out, final_norm_w):
    bsz, seq, _ = x.shape
    ctx_len = ctx.shape[1]
    depth = norm_w.shape[0]
    assert depth == 1, "single-layer block: the context stream update is never consumed"
    assert seq % 512 == 0 and ctx_len % CHUNK == 0 and seq % GRID_W == 0

    mod_rows = -(-(bsz + 1) // SUBLANES) * SUBLANES
    cond = jnp.concatenate([c, c_ctx[None], jnp.zeros((mod_rows - bsz - 1, D_MODEL), c.dtype)])
    mod = _modulation(cond, w_ada[0], b_ada[0])
    shift = mod[:, :D_MODEL].reshape(mod_rows, 1, D_MODEL)
    scale = mod[:, D_MODEL:2 * D_MODEL].reshape(mod_rows, 1, D_MODEL)
    gate = mod[:, 2 * D_MODEL:].reshape(mod_rows, 1, D_MODEL)

    w_in_bf = _bf(w_in[0])
    w_dt_bf = jnp.pad(_bf(w_in[0, :, OFF_DT:]), ((0, 0), (0, DT_PAD - 2 * HEADS)))
    alog_col = a_log[0].reshape(2 * HEADS, 1)
    bias_col = dt_bias[0].reshape(2 * HEADS, 1)
    dskip_b = jnp.broadcast_to(jnp.repeat(d_skip[0], HEADDIM)[:, None], (W_SSD, CHUNK))
    conv_b2 = conv_b[0].reshape(1, CONV_DIM)
    zero_state = jnp.zeros((bsz, W_SSD, D_STATE), jnp.float32)

    ctx2d = ctx.reshape(bsz * ctx_len, D_MODEL)
    xs_t_c, b_c, dt_c, cum_c = _projection(
        ctx2d, norm_w[0], shift[bsz:bsz + 1], scale[bsz:bsz + 1], w_in_bf, w_dt_bf, conv_w[0],
        conv_b2, alog_col, bias_col, ctx_len, ctx_len, full=False)
    nc_ctx = ctx_len // CHUNK
    (h_fwd,) = _ssd_sweep(False, xs_t_c, b_c, dt_c, cum_c, alog_col, zero_state, bsz, nc_ctx)
    (h_bwd,) = _ssd_sweep(True, xs_t_c, b_c, dt_c, cum_c, alog_col, zero_state, bsz, nc_ctx)

    x2d = x.reshape(bsz * seq, D_MODEL)
    outs = _projection(x2d, norm_w[0], shift, scale, w_in_bf, w_dt_bf, conv_w[0], conv_b2,
                       alog_col, bias_col, seq, 512, full=True)
    u_pool, gate_pool = outs[:N_POOL_GROUPS], outs[N_POOL_GROUPS:2 * N_POOL_GROUPS]
    gate_ssd, xs_t, b_tok, c_t, dt, cum = outs[2 * N_POOL_GROUPS:]
    nc = seq // CHUNK
    y_part, _ = _ssd_sweep(False, xs_t, b_tok, dt, cum, alog_col, h_fwd, bsz, nc, c_t=c_t,
                           dskip_b=dskip_b)
    y_ssd, _ = _ssd_sweep(True, xs_t, b_tok, dt, cum, alog_col, h_bwd, bsz, nc, c_t=c_t,
                          y_part=y_part)

    pool_w_bf = _bf(pool_w[0])
    y_pool = [_pool_group(u_pool[g], gate_pool[g], pool_w_bf, pool_scale, g, bsz, seq)
              for g in range(N_POOL_GROUPS)]
    out = _output(y_pool, y_ssd, gate_ssd, x2d, gate, ssd_norm_w[0], _bf(w_out[0]), final_norm_w,
                  seq, 512)
    return out.reshape(bsz, seq, D_MODEL)
```

```python
import functools

import numpy as np
import jax
import jax.numpy as jnp
from jax.experimental import pallas as pl
from jax.experimental.pallas import tpu as pltpu

D_MODEL = 1024
GRID_W = 64
W_POOL = 1024
W_SSD = 1024
POOL_WINDOWS = (2, 4, 8, 16)
N_POOL_GROUPS = len(POOL_WINDOWS)
POOL_GROUP_W = 256
HEADDIM = 64
HEADS = 16
GROUPS = 4
HEADS_PER_GROUP = 4
D_STATE = 128
D_CONV = 4
CONV_LEFT = 2
CHUNK = 128
GN = GROUPS * D_STATE
CONV_DIM = W_SSD + 2 * GN
OFF_POOL_Z = W_POOL
OFF_SSD_Z = 2 * W_POOL
OFF_XBC = 2 * W_POOL + W_SSD
OFF_DT = OFF_XBC + CONV_DIM
DT_PAD = 128
EPS = 1e-6
SUBLANES = 8
LANES = 128
IL_GROUPS = CHUNK // SUBLANES
CONV_SEG = 512
SSD_CHUNKS_PER_STEP = 4
VMEM_LIMIT = 56 * 1024 * 1024


def _silu(v):
    h = 0.5 * v
    return h + h * jnp.tanh(h)


def _softplus(v):
    return jnp.maximum(v, 0.0) + jnp.log1p(jnp.exp(-jnp.abs(v)))


def _bf(v):
    return v.astype(jnp.bfloat16)


def _dot(a, b):
    return jnp.dot(a, b, preferred_element_type=jnp.float32)


def _mod_kernel(c_ref, w_ref, b_ref, o_ref):
    s = _silu(c_ref[...])
    o_ref[...] = jnp.dot(s, w_ref[...], preferred_element_type=jnp.float32,
                         precision=jax.lax.Precision.HIGHEST) + b_ref[...]


def _modulation(cond_rows, w_ada, b_ada):
    rows = cond_rows.shape[0]
    n_out = w_ada.shape[1]
    tn = 1024
    return pl.pallas_call(
        _mod_kernel,
        grid=(n_out // tn,),
        in_specs=[pl.BlockSpec((rows, D_MODEL), lambda j: (0, 0)),
                  pl.BlockSpec((D_MODEL, tn), lambda j: (0, j)),
                  pl.BlockSpec((1, tn), lambda j: (0, j))],
        out_specs=pl.BlockSpec((rows, tn), lambda j: (0, j)),
        out_shape=jax.ShapeDtypeStruct((rows, n_out), jnp.float32),
        compiler_params=pltpu.CompilerParams(vmem_limit_bytes=VMEM_LIMIT),
        name="mod",
    )(cond_rows, w_ada, b_ada.reshape(1, n_out))


def _lane_cumsum(v):
    lane = jax.lax.broadcasted_iota(jnp.int32, v.shape, 1)
    shift = 1
    while shift < CHUNK:
        v = v + jnp.where(lane >= shift, pltpu.roll(v, shift, 1), 0.0)
        shift *= 2
    return v


def _proj_kernel(x_ref, xp_ref, xn_ref, nw_ref, sh_ref, sc_ref, w_ref, wdt_ref, cw_ref, cb_ref,
                 alog_ref, bias_ref, *rest, tm, tiles_per_seq, full):
    if full:
        (u0, u1, u2, u3, zp0, zp1, zp2, zp3, zs_ref, xs_t_ref, b_ref, c_t_ref, dt_ref, cum_ref,
         pe_ref, xc_ref, mn_ref) = rest
        u_refs, zp_refs = (u0, u1, u2, u3), (zp0, zp1, zp2, zp3)
    else:
        xs_t_ref, b_ref, dt_ref, cum_ref, pe_ref, xc_ref, mn_ref = rest
    i = pl.program_id(0)
    pos = i % tiles_per_seq
    has_prev = pos > 0
    has_next = pos < tiles_per_seq - 1
    n_chunks = tm // CHUNK
    seg = CONV_SEG

    def modulated(v):
        ms = jnp.mean(v * v, axis=-1, keepdims=True)
        y = v * jax.lax.rsqrt(ms + EPS) * nw_ref[...]
        return y * (1.0 + sc_ref[0]) + sh_ref[0]

    m_tok = modulated(x_ref[...])
    hm = _bf(m_tok)
    for t in range(D_MODEL // LANES):
        mn_ref[t] = m_tok[:, t * LANES:(t + 1) * LANES]

    rows = [jnp.concatenate([mn_ref[t, pl.ds(q * CHUNK + b, SUBLANES, stride=IL_GROUPS), :]
                             for t in range(D_MODEL // LANES)], axis=1)
            for q in range(n_chunks) for b in range(IL_GROUPS)]
    halo = [jnp.where(has_prev, modulated(xp_ref[...]), 0.0),
            jnp.where(has_next, modulated(xn_ref[...]), 0.0)]
    hm_il = _bf(jnp.concatenate(halo + rows, axis=0))
    sub = jax.lax.broadcasted_iota(jnp.int32, (SUBLANES, seg), 0)

    def conv_stage(j, slot):
        is_x = j < W_SSD
        is_b = W_SSD <= j < W_SSD + GN

        def matmul():
            pe_ref[slot] = _dot(hm_il, w_ref[:, OFF_XBC + j:OFF_XBC + j + seg])

        def group(q, b):
            lo = 2 * SUBLANES + q * CHUNK + b * SUBLANES
            return pe_ref[slot, lo:lo + SUBLANES]

        def shifted(q, b, delta):
            bb = b + delta
            if 0 <= bb < IL_GROUPS:
                return group(q, bb)
            if bb < 0:
                bb += IL_GROUPS
                if q == 0:
                    first = pe_ref[slot, bb - SUBLANES:bb - SUBLANES + 1]
                else:
                    row = 2 * SUBLANES + (q - 1) * CHUNK + bb * SUBLANES + SUBLANES - 1
                    first = pe_ref[slot, row:row + 1]
                return jnp.where(sub == 0, first, pltpu.roll(group(q, bb), 1, 0))
            bb -= IL_GROUPS
            if q == n_chunks - 1:
                last = pe_ref[slot, SUBLANES + bb:SUBLANES + bb + 1]
            else:
                nxt = 2 * SUBLANES + (q + 1) * CHUNK + bb * SUBLANES
                last = pe_ref[slot, nxt:nxt + 1]
            return jnp.where(sub == SUBLANES - 1, last, pltpu.roll(group(q, bb), SUBLANES - 1, 0))

        def epilogue():
            taps = [cw_ref[k:k + 1, j:j + seg] for k in range(D_CONV)]
            bias = cb_ref[:, j:j + seg]
            for q in range(n_chunks):
                for b in range(IL_GROUPS):
                    acc = bias
                    for k in range(D_CONV):
                        acc = acc + shifted(q, b, k - CONV_LEFT) * taps[k]
                    lo = q * CHUNK + b * SUBLANES
                    act = _silu(acc)
                    for t in range(seg // LANES):
                        xc_ref[slot, t, lo:lo + SUBLANES] = act[:, t * LANES:(t + 1) * LANES]
            for q in range(n_chunks):
                xc = jnp.concatenate(
                    [jnp.concatenate(
                        [xc_ref[slot, t, pl.ds(q * CHUNK + (m % 2) * (CHUNK // 2) + m // 2,
                                               SUBLANES, stride=SUBLANES), :]
                         for t in range(seg // LANES)], axis=1)
                     for m in range(IL_GROUPS)], axis=0)
                if is_b:
                    b_ref[q * CHUNK:(q + 1) * CHUNK, j - W_SSD:j - W_SSD + seg] = _bf(xc)
                else:
                    dst, off = (xs_t_ref, j) if is_x else (c_t_ref, j - W_SSD - GN)
                    dst[q, off:off + seg] = _bf(xc.T)

        return matmul, epilogue

    def plain_stage(cols, finish):
        box = []
        return (lambda: box.append(_dot(hm, w_ref[:, cols]))), (lambda: finish(box.pop()))

    def dt_stage():
        box = []

        def epilogue():
            p_dt = box.pop()
            a_col = -jnp.exp(alog_ref[...])
            for q in range(n_chunks):
                dt = _softplus(p_dt[q * CHUNK:(q + 1) * CHUNK].T[:2 * HEADS] + bias_ref[...])
                dt_ref[q] = dt
                cum_ref[q] = _lane_cumsum(dt * a_col)

        return (lambda: box.append(_dot(hm, wdt_ref[...]))), epilogue

    def store_to(ref, cols=None, act=None):
        def finish(v):
            v = v if act is None else _bf(act(v))
            if cols is None:
                ref[...] = v
            else:
                ref[:, cols] = v
        return finish

    def store_pair(refs, act=None):
        def finish(v):
            for n, ref in enumerate(refs):
                part = v[:, n * POOL_GROUP_W:(n + 1) * POOL_GROUP_W]
                ref[...] = part if act is None else _bf(act(part))
        return finish

    stages = []
    if full:
        for g in range(0, N_POOL_GROUPS, 2):
            stages.append(plain_stage(slice(g * POOL_GROUP_W, (g + 2) * POOL_GROUP_W),
                                      store_pair(u_refs[g:g + 2])))
    stages.append(dt_stage())
    n_conv = 0
    for j in range(0, CONV_DIM, seg):
        if full or j < W_SSD + GN:
            stages.append(conv_stage(j, n_conv % 2))
            n_conv += 1
    if full:
        for g in range(0, N_POOL_GROUPS, 2):
            zcols = slice(OFF_POOL_Z + g * POOL_GROUP_W, OFF_POOL_Z + (g + 2) * POOL_GROUP_W)
            stages.append(plain_stage(zcols, store_pair(zp_refs[g:g + 2], act=_silu)))
        for j in range(0, W_SSD, seg):
            stages.append(plain_stage(slice(OFF_SSD_Z + j, OFF_SSD_Z + j + seg),
                                      store_to(zs_ref, cols=slice(j, j + seg), act=_silu)))

    stages[0][0]()
    for k, (_, epilogue) in enumerate(stages):
        if k + 1 < len(stages):
            stages[k + 1][0]()
        epilogue()


def _projection(x2d, norm_w, shift, scale, w_bf, wdt_bf, conv_w, conv_b, alog_col, bias_col,
                seq_len, tm, full):
    n_tok = x2d.shape[0]
    tiles_per_seq = seq_len // tm
    n_mod = shift.shape[0]
    nct = n_tok // CHUNK
    per = tm // SUBLANES
    last_halo = n_tok // SUBLANES - 1
    kern = functools.partial(_proj_kernel, tm=tm, tiles_per_seq=tiles_per_seq, full=full)
    mod_map = (lambda i: (i // tiles_per_seq, 0, 0)) if n_mod > 1 else (lambda i: (0, 0, 0))
    mod_spec = pl.BlockSpec((1, 1, D_MODEL), mod_map)
    const = lambda i: (0, 0)
    tok = lambda i: (i, 0)
    chunk3 = lambda i: (i, 0, 0)
    q = tm // CHUNK
    xs_t = (jax.ShapeDtypeStruct((nct, W_SSD, CHUNK), jnp.bfloat16),
            pl.BlockSpec((q, W_SSD, CHUNK), chunk3))
    b_tok = (jax.ShapeDtypeStruct((n_tok, GN), jnp.bfloat16), pl.BlockSpec((tm, GN), tok))
    c_t = (jax.ShapeDtypeStruct((nct, GN, CHUNK), jnp.bfloat16), pl.BlockSpec((q, GN, CHUNK), chunk3))
    dt = (jax.ShapeDtypeStruct((nct, 2 * HEADS, CHUNK), jnp.float32),
          pl.BlockSpec((q, 2 * HEADS, CHUNK), chunk3))
    if full:
        u = (jax.ShapeDtypeStruct((n_tok, POOL_GROUP_W), jnp.float32),
             pl.BlockSpec((tm, POOL_GROUP_W), tok))
        zp = (jax.ShapeDtypeStruct((n_tok, POOL_GROUP_W), jnp.bfloat16),
              pl.BlockSpec((tm, POOL_GROUP_W), tok))
        zs = (jax.ShapeDtypeStruct((n_tok, W_SSD), jnp.bfloat16), pl.BlockSpec((tm, W_SSD), tok))
        outs = [u] * N_POOL_GROUPS + [zp] * N_POOL_GROUPS + [zs, xs_t, b_tok, c_t, dt, dt]
    else:
        outs = [xs_t, b_tok, dt, dt]
    return pl.pallas_call(
        kern,
        grid=(n_tok // tm,),
        in_specs=[pl.BlockSpec((tm, D_MODEL), tok),
                  pl.BlockSpec((SUBLANES, D_MODEL), lambda i: (jnp.maximum(i * per - 1, 0), 0)),
                  pl.BlockSpec((SUBLANES, D_MODEL), lambda i: (jnp.minimum((i + 1) * per, last_halo), 0)),
                  pl.BlockSpec((1, D_MODEL), const),
                  mod_spec, mod_spec,
                  pl.BlockSpec(w_bf.shape, const),
                  pl.BlockSpec((D_MODEL, DT_PAD), const),
                  pl.BlockSpec((D_CONV, CONV_DIM), const),
                  pl.BlockSpec((1, CONV_DIM), const),
                  pl.BlockSpec((2 * HEADS, 1), const),
                  pl.BlockSpec((2 * HEADS, 1), const)],
        out_specs=[o[1] for o in outs],
        out_shape=[o[0] for o in outs],
        scratch_shapes=[pltpu.VMEM((2, tm + 2 * SUBLANES, CONV_SEG), jnp.float32),
                        pltpu.VMEM((2, CONV_SEG // LANES, tm, LANES), jnp.float32),
                        pltpu.VMEM((D_MODEL // LANES, tm, LANES), jnp.float32)],
        compiler_params=pltpu.CompilerParams(vmem_limit_bytes=VMEM_LIMIT),
        name="proj" if full else "proj_ctx",
    )(x2d, x2d, x2d, norm_w.reshape(1, D_MODEL), shift, scale, w_bf, wdt_bf, conv_w, conv_b,
      alog_col, bias_col)


def _sweep_chunks(chunk_body, chunks_per_step, reverse):
    def body(i, carry):
        chunk_body(chunks_per_step - 1 - i if reverse else i)
        return carry
    jax.lax.fori_loop(0, chunks_per_step, body, 0, unroll=True)


def _tok_rows(q):
    return pl.ds(pl.multiple_of(q * CHUNK, CHUNK), CHUNK)


def _state_update(h_ref, xs_t_ref, b_ref, q, scale_in, chunk_decay):
    for g in range(GROUPS):
        bg = b_ref[_tok_rows(q), g * D_STATE:(g + 1) * D_STATE]
        xd = []
        for r in range(HEADS_PER_GROUP):
            h = g * HEADS_PER_GROUP + r
            x_h = xs_t_ref[q, h * HEADDIM:(h + 1) * HEADDIM].astype(jnp.float32)
            xd.append(_bf(x_h * scale_in[h:h + 1]))
        s_new = _dot(jnp.concatenate(xd, axis=0), bg)
        for r in range(HEADS_PER_GROUP):
            h = g * HEADS_PER_GROUP + r
            hr = slice(h * HEADDIM, (h + 1) * HEADDIM)
            h_ref[hr] = h_ref[hr] * chunk_decay[h:h + 1] + s_new[r * HEADDIM:(r + 1) * HEADDIM]


def _ssd_fwd_kernel(xs_t_ref, b_ref, dt_ref, cum_ref, alog_ref, h0_ref, *rest, n_steps, cps, with_y):
    if with_y:
        c_t_ref, dskip_ref, y_ref, hout_ref, h_ref = rest
    else:
        hout_ref, h_ref = rest
    step = pl.program_id(1)

    @pl.when(step == 0)
    def _():
        h_ref[...] = h0_ref[0]

    if with_y:
        a_b = -jnp.exp(alog_ref[HEADS:])
        src = jax.lax.broadcasted_iota(jnp.int32, (CHUNK, CHUNK), 0)
        dst = jax.lax.broadcasted_iota(jnp.int32, (CHUNK, CHUNK), 1)
        causal = src <= dst
        is_diag = src == dst

    def chunk(q):
        dt_f, cum_f = dt_ref[q, :HEADS], cum_ref[q, :HEADS]
        tot_f = cum_f[:, CHUNK - 1:CHUNK]
        scale_in = dt_f * jnp.exp(tot_f - cum_f)
        chunk_decay = jnp.exp(tot_f)

        if with_y:
            dt_b, cum_b = dt_ref[q, HEADS:], cum_ref[q, HEADS:]
            cumx_b = cum_b - dt_b * a_b
            col_terms = jnp.concatenate(
                [jnp.log(dt_f) - cum_f, jnp.log(dt_b) + cumx_b,
                 jnp.zeros((CHUNK - 2 * HEADS, CHUNK), jnp.float32)], axis=0).T
            row_f = cum_f
            row_b = -cumx_b
            decay_out_f = jnp.exp(cum_f)
            for g in range(GROUPS):
                bg = b_ref[_tok_rows(q), g * D_STATE:(g + 1) * D_STATE]
                cg_t = c_t_ref[q, g * D_STATE:(g + 1) * D_STATE]
                rows = slice(g * HEADS_PER_GROUP * HEADDIM, (g + 1) * HEADS_PER_GROUP * HEADDIM)
                g_t = _dot(bg, cg_t)
                g_diag = jnp.sum(jnp.where(is_diag, g_t, 0.0), axis=0, keepdims=True)
                y_off = _dot(_bf(h_ref[rows]), cg_t)
                for r in range(HEADS_PER_GROUP):
                    h = g * HEADS_PER_GROUP + r
                    hr = slice(h * HEADDIM, (h + 1) * HEADDIM)
                    x_bf = xs_t_ref[q, hr]
                    col_f = jnp.broadcast_to(col_terms[:, h:h + 1], (CHUNK, CHUNK))
                    col_b = jnp.broadcast_to(col_terms[:, HEADS + h:HEADS + h + 1], (CHUNK, CHUNK))
                    expo = jnp.where(causal, col_f + row_f[h:h + 1], col_b + row_b[h:h + 1])
                    w_t = _bf(g_t * jnp.exp(expo))
                    y_h = _dot(x_bf, w_t)
                    y_h = y_h + y_off[r * HEADDIM:(r + 1) * HEADDIM] * decay_out_f[h:h + 1]
                    skip = dskip_ref[hr] + g_diag * dt_b[h:h + 1]
                    y_ref[q, hr] = _bf(y_h + skip * x_bf.astype(jnp.float32))

        _state_update(h_ref, xs_t_ref, b_ref, q, scale_in, chunk_decay)

    _sweep_chunks(chunk, cps, reverse=False)

    @pl.when(step == n_steps - 1)
    def _():
        hout_ref[0] = h_ref[...]


def _ssd_bwd_kernel(xs_t_ref, b_ref, dt_ref, cum_ref, alog_ref, h0_ref, *rest, n_steps, cps, with_y):
    if with_y:
        c_t_ref, ypart_ref, y_ref, hout_ref, h_ref = rest
    else:
        hout_ref, h_ref = rest
    step = pl.program_id(1)

    @pl.when(step == 0)
    def _():
        h_ref[...] = h0_ref[0]

    a_b = -jnp.exp(alog_ref[HEADS:])

    def chunk(q):
        dt_b, cum_b = dt_ref[q, HEADS:], cum_ref[q, HEADS:]
        tot_b = cum_b[:, CHUNK - 1:CHUNK]
        cumx_b = cum_b - dt_b * a_b
        scale_in = dt_b * jnp.exp(cumx_b)
        chunk_decay = jnp.exp(tot_b)

        if with_y:
            decay_out = jnp.exp(tot_b - cumx_b)
            y_parts = []
            for g in range(GROUPS):
                cg_t = c_t_ref[q, g * D_STATE:(g + 1) * D_STATE]
                rows = slice(g * HEADS_PER_GROUP * HEADDIM, (g + 1) * HEADS_PER_GROUP * HEADDIM)
                y_off = _dot(_bf(h_ref[rows]), cg_t)
                for r in range(HEADS_PER_GROUP):
                    h = g * HEADS_PER_GROUP + r
                    hr = slice(h * HEADDIM, (h + 1) * HEADDIM)
                    y_parts.append(ypart_ref[q, hr].astype(jnp.float32)
                                   + y_off[r * HEADDIM:(r + 1) * HEADDIM] * decay_out[h:h + 1])
            y_ref[_tok_rows(q), :] = _bf(jnp.concatenate(y_parts, axis=0).T)

        _state_update(h_ref, xs_t_ref, b_ref, q, scale_in, chunk_decay)

    _sweep_chunks(chunk, cps, reverse=True)

    @pl.when(step == n_steps - 1)
    def _():
        hout_ref[0] = h_ref[...]


def _ssd_sweep(reverse, xs_t, b_tok, dt, cum, alog_col, h0, bsz, n_chunks, c_t=None, dskip_b=None,
               y_part=None):
    with_y = c_t is not None
    n_tok = b_tok.shape[0]
    cps = min(n_chunks, SSD_CHUNKS_PER_STEP)
    n_steps = n_chunks // cps

    def block_of(b, s):
        return b * n_steps + (n_steps - 1 - s if reverse else s)

    tok = lambda b, s: (block_of(b, s), 0)
    chunk3 = lambda b, s: (block_of(b, s), 0, 0)
    const2 = lambda b, s: (0, 0)
    state3 = lambda b, s: (b, 0, 0)
    h_spec = pl.BlockSpec((1, W_SSD, D_STATE), state3)
    h_shape = jax.ShapeDtypeStruct((bsz, W_SSD, D_STATE), jnp.float32)
    head_spec = pl.BlockSpec((cps, 2 * HEADS, CHUNK), chunk3)
    in_specs = [pl.BlockSpec((cps, W_SSD, CHUNK), chunk3),
                pl.BlockSpec((cps * CHUNK, GN), tok),
                head_spec, head_spec,
                pl.BlockSpec((2 * HEADS, 1), const2),
                h_spec]
    args = [xs_t, b_tok, dt, cum, alog_col, h0]
    y_t_spec = pl.BlockSpec((cps, W_SSD, CHUNK), chunk3)
    if with_y:
        in_specs.append(pl.BlockSpec((cps, GN, CHUNK), chunk3))
        args.append(c_t)
        if reverse:
            in_specs.append(y_t_spec)
            args.append(y_part)
            out_shape = [jax.ShapeDtypeStruct((n_tok, W_SSD), jnp.bfloat16), h_shape]
            out_specs = [pl.BlockSpec((cps * CHUNK, W_SSD), tok), h_spec]
        else:
            in_specs.append(pl.BlockSpec((W_SSD, CHUNK), const2))
            args.append(dskip_b)
            out_shape = [jax.ShapeDtypeStruct(xs_t.shape, jnp.bfloat16), h_shape]
            out_specs = [y_t_spec, h_spec]
    else:
        out_shape = [h_shape]
        out_specs = [h_spec]
    body = _ssd_bwd_kernel if reverse else _ssd_fwd_kernel
    name = ("ssd_bwd" if reverse else "ssd_fwd") + ("" if with_y else "_state")
    return pl.pallas_call(
        functools.partial(body, n_steps=n_steps, cps=cps, with_y=with_y),
        grid=(bsz, n_steps),
        in_specs=in_specs,
        out_specs=out_specs,
        out_shape=out_shape,
        scratch_shapes=[pltpu.VMEM((W_SSD, D_STATE), jnp.float32)],
        compiler_params=pltpu.CompilerParams(
            dimension_semantics=("arbitrary", "arbitrary"), vmem_limit_bytes=VMEM_LIMIT),
        name=name,
    )(*args)


POOL_TILE_ROWS = 4
POOL_TILE = POOL_TILE_ROWS * GRID_W


def _pool_constants(window, n_rows):
    lo_off, hi_off = -(window // 2), window - window // 2
    col = np.arange(GRID_W)
    lo = np.clip(col + lo_off, 0, GRID_W)
    hi = np.clip(col + hi_off, 0, GRID_W)
    band = ((col[None, :] >= lo[:, None]) & (col[None, :] < hi[:, None])).astype(np.float32)
    band_tile = np.kron(np.eye(POOL_TILE_ROWS, dtype=np.float32), band)
    row = np.arange(n_rows)
    cnt_r = np.clip(row + hi_off, 0, n_rows) - np.clip(row + lo_off, 0, n_rows)
    inv = 1.0 / (cnt_r[:, None] * (hi - lo)[None, :]).astype(np.float64)
    inv = np.broadcast_to(inv.reshape(-1, 1), (n_rows * GRID_W, 128)).astype(np.float32)
    return jnp.asarray(band_tile, jnp.bfloat16), jnp.asarray(inv)


def _pool_kernel(u_ref, z_ref, band_ref, inv_ref, w_ref, scale_ref, o_ref, *, window, n_rows):
    def grid_row(r):
        return u_ref[r * GRID_W:(r + 1) * GRID_W]

    def bounds(r):
        return max(r - window // 2, 0), min(r + window - window // 2, n_rows)

    band = band_ref[...]
    rsum, tile_rows = None, []
    for r in range(n_rows):
        lo, hi = bounds(r)
        if r == 0 or window <= 2:
            rsum = grid_row(lo)
            for k in range(lo + 1, hi):
                rsum = rsum + grid_row(k)
        else:
            prev_lo, prev_hi = bounds(r - 1)
            if hi > prev_hi:
                rsum = rsum + grid_row(hi - 1)
            if lo > prev_lo:
                rsum = rsum - grid_row(prev_lo)
        tile_rows.append(rsum)
        if len(tile_rows) < POOL_TILE_ROWS:
            continue
        base = (r + 1 - POOL_TILE_ROWS) * GRID_W
        rows = slice(base, base + POOL_TILE)
        rs = jnp.concatenate(tile_rows, axis=0)
        tile_rows = []
        box = _dot(band, _bf(rs))
        inv = inv_ref[rows]
        mean = box * jnp.concatenate([inv, inv], axis=1)
        d = mean - u_ref[rows]
        y = _dot(_bf(d), w_ref[0]) * scale_ref[...]
        o_ref[rows] = _bf(y * z_ref[rows].astype(jnp.float32))


def _pool_group(u, gate, pool_w_bf, pool_scale, g, bsz, n_img_tok):
    window = POOL_WINDOWS[g]
    n_rows = n_img_tok // GRID_W
    band, inv = _pool_constants(window, n_rows)
    kern = functools.partial(_pool_kernel, window=window, n_rows=n_rows)
    img = pl.BlockSpec((n_img_tok, POOL_GROUP_W), lambda b: (b, 0))
    return pl.pallas_call(
        kern,
        grid=(bsz,),
        in_specs=[img, img,
                  pl.BlockSpec((POOL_TILE, POOL_TILE), lambda b: (0, 0)),
                  pl.BlockSpec((n_img_tok, 128), lambda b: (0, 0)),
                  pl.BlockSpec((1, POOL_GROUP_W, POOL_GROUP_W), lambda b: (g, 0, 0)),
                  pl.BlockSpec((1, POOL_GROUP_W), lambda b: (0, g))],
        out_specs=img,
        out_shape=jax.ShapeDtypeStruct((bsz * n_img_tok, POOL_GROUP_W), jnp.bfloat16),
        compiler_params=pltpu.CompilerParams(vmem_limit_bytes=VMEM_LIMIT),
        name=f"pool{window}",
    )(u, gate, band, inv, pool_w_bf, pool_scale)


def _out_kernel(yp0_ref, yp1_ref, yp2_ref, yp3_ref, ys_ref, zs_ref, x_ref, gate_ref, snw_ref,
                wout_ref, fnw_ref, o_ref):
    acc = None
    for g, yp_ref in enumerate((yp0_ref, yp1_ref, yp2_ref, yp3_ref)):
        part = _dot(yp_ref[...], wout_ref[g * POOL_GROUP_W:(g + 1) * POOL_GROUP_W])
        acc = part if acc is None else acc + part
    gw = W_SSD // GROUPS
    for g in range(GROUPS):
        cols = slice(g * gw, (g + 1) * gw)
        gated = ys_ref[:, cols].astype(jnp.float32) * zs_ref[:, cols].astype(jnp.float32)
        ms = jnp.mean(gated * gated, axis=-1, keepdims=True)
        yn = gated * jax.lax.rsqrt(ms + EPS) * snw_ref[:, cols]
        acc = acc + _dot(_bf(yn), wout_ref[W_POOL + g * gw:W_POOL + (g + 1) * gw])
    h = x_ref[...] + gate_ref[0] * acc
    ms = jnp.mean(h * h, axis=-1, keepdims=True)
    o_ref[...] = h * jax.lax.rsqrt(ms + EPS) * fnw_ref[...]


def _output(y_pool, y_ssd, z_ssd, x2d, gate, ssd_norm_w, w_out_bf, final_norm_w, rows_per_mod, tm):
    n_tok = x2d.shape[0]
    tiles_per_mod = rows_per_mod // tm
    tok = lambda i: (i, 0)
    const = lambda i: (0, 0)
    return pl.pallas_call(
        _out_kernel,
        grid=(n_tok // tm,),
        in_specs=[pl.BlockSpec((tm, POOL_GROUP_W), tok)] * 4 + [
            pl.BlockSpec((tm, W_SSD), tok),
            pl.BlockSpec((tm, W_SSD), tok),
            pl.BlockSpec((tm, D_MODEL), tok),
            pl.BlockSpec((1, 1, D_MODEL), lambda i: (i // tiles_per_mod, 0, 0)),
            pl.BlockSpec((1, W_SSD), const),
            pl.BlockSpec((W_POOL + W_SSD, D_MODEL), const),
            pl.BlockSpec((1, D_MODEL), const)],
        out_specs=pl.BlockSpec((tm, D_MODEL), tok),
        out_shape=jax.ShapeDtypeStruct((n_tok, D_MODEL), jnp.float32),
        compiler_params=pltpu.CompilerParams(vmem_limit_bytes=VMEM_LIMIT),
        name="out",
    )(*y_pool, y_ssd, z_ssd, x2d, gate, ssd_norm_w.reshape(1, W_SSD), w_out_bf,
      final_norm_w.reshape(1, D_MODEL))


def kernel(x, c, ctx, c_ctx, norm_w, w_ada, b_ada, w_in, conv_w, conv_b, a_log, dt_bias, d_skip,
           ssd_norm_w, pool_w, pool_scale, w_out, final_norm_w):
    bsz, seq, _ = x.shape
    ctx_len = ctx.shape[1]
    depth = norm_w.shape[0]
    assert depth == 1, "single-layer block: the context stream update is never consumed"
    assert seq % 512 == 0 and ctx_len % CHUNK == 0 and seq % GRID_W == 0

    mod_rows = -(-(bsz + 1) // SUBLANES) * SUBLANES
    cond = jnp.concatenate([c, c_ctx[None], jnp.zeros((mod_rows - bsz - 1, D_MODEL), c.dtype)])
    mod = _modulation(cond, w_ada[0], b_ada[0])
    shift = mod[:, :D_MODEL].reshape(mod_rows, 1, D_MODEL)
    scale = mod[:, D_MODEL:2 * D_MODEL].reshape(mod_rows, 1, D_MODEL)
    gate = mod[:, 2 * D_MODEL:].reshape(mod_rows, 1, D_MODEL)

    w_in_bf = _bf(w_in[0])
    w_dt_bf = jnp.pad(_bf(w_in[0, :, OFF_DT:]), ((0, 0), (0, DT_PAD - 2 * HEADS)))
    alog_col = a_log[0].reshape(2 * HEADS, 1)
    bias_col = dt_bias[0].reshape(2 * HEADS, 1)
    dskip_b = jnp.broadcast_to(jnp.repeat(d_skip[0], HEADDIM)[:, None], (W_SSD, CHUNK))
    conv_b2 = conv_b[0].reshape(1, CONV_DIM)
    zero_state = jnp.zeros((bsz, W_SSD, D_STATE), jnp.float32)

    ctx2d = ctx.reshape(bsz * ctx_len, D_MODEL)
    xs_t_c, b_c, dt_c, cum_c = _projection(
        ctx2d, norm_w[0], shift[bsz:bsz + 1], scale[bsz:bsz + 1], w_in_bf, w_dt_bf, conv_w[0],
        conv_b2, alog_col, bias_col, ctx_len, ctx_len, full=False)
    nc_ctx = ctx_len // CHUNK
    (h_fwd,) = _ssd_sweep(False, xs_t_c, b_c, dt_c, cum_c, alog_col, zero_state, bsz, nc_ctx)
    (h_bwd,) = _ssd_sweep(True, xs_t_c, b_c, dt_c, cum_c, alog_col, zero_state, bsz, nc_ctx)

    x2d = x.reshape(bsz * seq, D_MODEL)
    outs = _projection(x2d, norm_w[0], shift, scale, w_in_bf, w_dt_bf, conv_w[0], conv_b2,
                       alog_col, bias_col, seq, 512, full=True)
    u_pool, gate_pool = outs[:N_POOL_GROUPS], outs[N_POOL_GROUPS:2 * N_POOL_GROUPS]
    gate_ssd, xs_t, b_tok, c_t, dt, cum = outs[2 * N_POOL_GROUPS:]
    nc = seq // CHUNK
    y_part, _ = _ssd_sweep(False, xs_t, b_tok, dt, cum, alog_col, h_fwd, bsz, nc, c_t=c_t,
                           dskip_b=dskip_b)
    y_ssd, _ = _ssd_sweep(True, xs_t, b_tok, dt, cum, alog_col, h_bwd, bsz, nc, c_t=c_t,
                          y_part=y_part)

    pool_w_bf = _bf(pool_w[0])
    y_pool = [_pool_group(u_pool[g], gate_pool[g], pool_w_bf, pool_scale, g, bsz, seq)
              for g in range(N_POOL_GROUPS)]
    out = _output(y_pool, y_ssd, gate_ssd, x2d, gate, ssd_norm_w[0], _bf(w_out[0]), final_norm_w,
                  seq, 512)
    return out.reshape(bsz, seq, D_MODEL)
```

```python
import functools

import numpy as np
import jax
import jax.numpy as jnp
from jax.experimental import pallas as pl
from jax.experimental.pallas import tpu as pltpu

D_MODEL = 1024
GRID_W = 64
W_POOL = 1024
W_SSD = 1024
POOL_WINDOWS = (2, 4, 8, 16)
N_POOL_GROUPS = len(POOL_WINDOWS)
POOL_GROUP_W = 256
HEADDIM = 64
HEADS = 16
GROUPS = 4
HEADS_PER_GROUP = 4
D_STATE = 128
D_CONV = 4
CONV_LEFT = 2
CHUNK = 128
GN = GROUPS * D_STATE
CONV_DIM = W_SSD + 2 * GN
OFF_POOL_Z = W_POOL
OFF_SSD_Z = 2 * W_POOL
OFF_XBC = 2 * W_POOL + W_SSD
OFF_DT = OFF_XBC + CONV_DIM
DT_PAD = 128
EPS = 1e-6
SUBLANES = 8
LANES = 128
IL_GROUPS = CHUNK // SUBLANES
CONV_SEG = 512
SSD_CHUNKS_PER_STEP = 4
VMEM_LIMIT = 56 * 1024 * 1024


def _silu(v):
    h = 0.5 * v
    return h + h * jnp.tanh(h)


def _softplus(v):
    return jnp.maximum(v, 0.0) + jnp.log1p(jnp.exp(-jnp.abs(v)))


def _bf(v):
    return v.astype(jnp.bfloat16)


def _dot(a, b):
    return jnp.dot(a, b, preferred_element_type=jnp.float32)


def _mod_kernel(c_ref, w_ref, b_ref, o_ref):
    s = _silu(c_ref[...])
    o_ref[...] = jnp.dot(s, w_ref[...], preferred_element_type=jnp.float32,
                         precision=jax.lax.Precision.HIGHEST) + b_ref[...]


def _modulation(cond_rows, w_ada, b_ada):
    rows = cond_rows.shape[0]
    n_out = w_ada.shape[1]
    tn = 1024
    return pl.pallas_call(
        _mod_kernel,
        grid=(n_out // tn,),
        in_specs=[pl.BlockSpec((rows, D_MODEL), lambda j: (0, 0)),
                  pl.BlockSpec((D_MODEL, tn), lambda j: (0, j)),
                  pl.BlockSpec((1, tn), lambda j: (0, j))],
        out_specs=pl.BlockSpec((rows, tn), lambda j: (0, j)),
        out_shape=jax.ShapeDtypeStruct((rows, n_out), jnp.float32),
        compiler_params=pltpu.CompilerParams(vmem_limit_bytes=VMEM_LIMIT),
        name="mod",
    )(cond_rows, w_ada, b_ada.reshape(1, n_out))


def _lane_cumsum(v):
    lane = jax.lax.broadcasted_iota(jnp.int32, v.shape, 1)
    shift = 1
    while shift < CHUNK:
        v = v + jnp.where(lane >= shift, pltpu.roll(v, shift, 1), 0.0)
        shift *= 2
    return v


def _proj_kernel(x_ref, xp_ref, xn_ref, nw_ref, sh_ref, sc_ref, w_ref, wdt_ref, cw_ref, cb_ref,
                 alog_ref, bias_ref, *rest, tm, tiles_per_seq, full):
    if full:
        (u0, u1, u2, u3, zp0, zp1, zp2, zp3, zs_ref, xs_t_ref, b_ref, c_t_ref, dt_ref, cum_ref,
         pe_ref, xc_ref, mn_ref) = rest
        u_refs, zp_refs = (u0, u1, u2, u3), (zp0, zp1, zp2, zp3)
    else:
        xs_t_ref, b_ref, dt_ref, cum_ref, pe_ref, xc_ref, mn_ref = rest
    i = pl.program_id(0)
    pos = i % tiles_per_seq
    has_prev = pos > 0
    has_next = pos < tiles_per_seq - 1
    n_chunks = tm // CHUNK
    seg = CONV_SEG

    def modulated(v):
        ms = jnp.mean(v * v, axis=-1, keepdims=True)
        y = v * jax.lax.rsqrt(ms + EPS) * nw_ref[...]
        return y * (1.0 + sc_ref[0]) + sh_ref[0]

    m_tok = modulated(x_ref[...])
    hm = _bf(m_tok)
    for t in range(D_MODEL // LANES):
        mn_ref[t] = m_tok[:, t * LANES:(t + 1) * LANES]

    rows = [jnp.concatenate([mn_ref[t, pl.ds(q * CHUNK + b, SUBLANES, stride=IL_GROUPS), :]
                             for t in range(D_MODEL // LANES)], axis=1)
            for q in range(n_chunks) for b in range(IL_GROUPS)]
    halo = [jnp.where(has_prev, modulated(xp_ref[...]), 0.0),
            jnp.where(has_next, modulated(xn_ref[...]), 0.0)]
    hm_il = _bf(jnp.concatenate(halo + rows, axis=0))
    sub = jax.lax.broadcasted_iota(jnp.int32, (SUBLANES, seg), 0)

    def conv_stage(j, slot):
        is_x = j < W_SSD
        is_b = W_SSD <= j < W_SSD + GN

        def matmul():
            pe_ref[slot] = _dot(hm_il, w_ref[:, OFF_XBC + j:OFF_XBC + j + seg])

        def group(q, b):
            lo = 2 * SUBLANES + q * CHUNK + b * SUBLANES
            return pe_ref[slot, lo:lo + SUBLANES]

        def shifted(q, b, delta):
            bb = b + delta
            if 0 <= bb < IL_GROUPS:
                return group(q, bb)
            if bb < 0:
                bb += IL_GROUPS
                if q == 0:
                    first = pe_ref[slot, bb - SUBLANES:bb - SUBLANES + 1]
                else:
                    row = 2 * SUBLANES + (q - 1) * CHUNK + bb * SUBLANES + SUBLANES - 1
                    first = pe_ref[slot, row:row + 1]
                return jnp.where(sub == 0, first, pltpu.roll(group(q, bb), 1, 0))
            bb -= IL_GROUPS
            if q == n_chunks - 1:
                last = pe_ref[slot, SUBLANES + bb:SUBLANES + bb + 1]
            else:
                nxt = 2 * SUBLANES + (q + 1) * CHUNK + bb * SUBLANES
                last = pe_ref[slot, nxt:nxt + 1]
            return jnp.where(sub == SUBLANES - 1, last, pltpu.roll(group(q, bb), SUBLANES - 1, 0))

        def epilogue():
            taps = [cw_ref[k:k + 1, j:j + seg] for k in range(D_CONV)]
            bias = cb_ref[:, j:j + seg]
            for q in range(n_chunks):
                for b in range(IL_GROUPS):
                    acc = bias
                    for k in range(D_CONV):
                        acc = acc + shifted(q, b, k - CONV_LEFT) * taps[k]
                    lo = q * CHUNK + b * SUBLANES
                    act = _silu(acc)
                    for t in range(seg // LANES):
                        xc_ref[slot, t, lo:lo + SUBLANES] = act[:, t * LANES:(t + 1) * LANES]
            for q in range(n_chunks):
                xc = jnp.concatenate(
                    [jnp.concatenate(
                        [xc_ref[slot, t, pl.ds(q * CHUNK + (m % 2) * (CHUNK // 2) + m // 2,
                                               SUBLANES, stride=SUBLANES), :]
                         for t in range(seg // LANES)], axis=1)
                     for m in range(IL_GROUPS)], axis=0)
                if is_b:
                    b_ref[q * CHUNK:(q + 1) * CHUNK, j - W_SSD:j - W_SSD + seg] = _bf(xc)
                else:
                    dst, off = (xs_t_ref, j) if is_x else (c_t_ref, j - W_SSD - GN)
                    dst[q, off:off + seg] = _bf(xc.T)

        return matmul, epilogue

    def plain_stage(cols, finish):
        box = []
        return (lambda: box.append(_dot(hm, w_ref[:, cols]))), (lambda: finish(box.pop()))

    def dt_stage():
        box = []

        def epilogue():
            p_dt = box.pop()
            a_col = -jnp.exp(alog_ref[...])
            for q in range(n_chunks):
                dt = _softplus(p_dt[q * CHUNK:(q + 1) * CHUNK].T[:2 * HEADS] + bias_ref[...])
                dt_ref[q] = dt
                cum_ref[q] = _lane_cumsum(dt * a_col)

        return (lambda: box.append(_dot(hm, wdt_ref[...]))), epilogue

    def store_to(ref, cols=None, act=None):
        def finish(v):
            v = v if act is None else _bf(act(v))
            if cols is None:
                ref[...] = v
            else:
                ref[:, cols] = v
        return finish

    def store_pair(refs, act=None):
        def finish(v):
            for n, ref in enumerate(refs):
                part = v[:, n * POOL_GROUP_W:(n + 1) * POOL_GROUP_W]
                ref[...] = part if act is None else _bf(act(part))
        return finish

    stages = []
    if full:
        for g in range(0, N_POOL_GROUPS, 2):
            stages.append(plain_stage(slice(g * POOL_GROUP_W, (g + 2) * POOL_GROUP_W),
                                      store_pair(u_refs[g:g + 2])))
    stages.append(dt_stage())
    n_conv = 0
    for j in range(0, CONV_DIM, seg):
        if full or j < W_SSD + GN:
            stages.append(conv_stage(j, n_conv % 2))
            n_conv += 1
    if full:
        for g in range(0, N_POOL_GROUPS, 2):
            zcols = slice(OFF_POOL_Z + g * POOL_GROUP_W, OFF_POOL_Z + (g + 2) * POOL_GROUP_W)
            stages.append(plain_stage(zcols, store_pair(zp_refs[g:g + 2], act=_silu)))
        for j in range(0, W_SSD, seg):
            stages.append(plain_stage(slice(OFF_SSD_Z + j, OFF_SSD_Z + j + seg),
                                      store_to(zs_ref, cols=slice(j, j + seg), act=_silu)))

    stages[0][0]()
    for k, (_, epilogue) in enumerate(stages):
        if k + 1 < len(stages):
            stages[k + 1][0]()
        epilogue()


def _projection(x2d, norm_w, shift, scale, w_bf, wdt_bf, conv_w, conv_b, alog_col, bias_col,
                seq_len, tm, full):
    n_tok = x2d.shape[0]
    tiles_per_seq = seq_len // tm
    n_mod = shift.shape[0]
    nct = n_tok // CHUNK
    per = tm // SUBLANES
    last_halo = n_tok // SUBLANES - 1
    kern = functools.partial(_proj_kernel, tm=tm, tiles_per_seq=tiles_per_seq, full=full)
    mod_map = (lambda i: (i // tiles_per_seq, 0, 0)) if n_mod > 1 else (lambda i: (0, 0, 0))
    mod_spec = pl.BlockSpec((1, 1, D_MODEL), mod_map)
    const = lambda i: (0, 0)
    tok = lambda i: (i, 0)
    chunk3 = lambda i: (i, 0, 0)
    q = tm // CHUNK
    xs_t = (jax.ShapeDtypeStruct((nct, W_SSD, CHUNK), jnp.bfloat16),
            pl.BlockSpec((q, W_SSD, CHUNK), chunk3))
    b_tok = (jax.ShapeDtypeStruct((n_tok, GN), jnp.bfloat16), pl.BlockSpec((tm, GN), tok))
    c_t = (jax.ShapeDtypeStruct((nct, GN, CHUNK), jnp.bfloat16), pl.BlockSpec((q, GN, CHUNK), chunk3))
    dt = (jax.ShapeDtypeStruct((nct, 2 * HEADS, CHUNK), jnp.float32),
          pl.BlockSpec((q, 2 * HEADS, CHUNK), chunk3))
    if full:
        u = (jax.ShapeDtypeStruct((n_tok, POOL_GROUP_W), jnp.float32),
             pl.BlockSpec((tm, POOL_GROUP_W), tok))
        zp = (jax.ShapeDtypeStruct((n_tok, POOL_GROUP_W), jnp.bfloat16),
              pl.BlockSpec((tm, POOL_GROUP_W), tok))
        zs = (jax.ShapeDtypeStruct((n_tok, W_SSD), jnp.bfloat16), pl.BlockSpec((tm, W_SSD), tok))
        outs = [u] * N_POOL_GROUPS + [zp] * N_POOL_GROUPS + [zs, xs_t, b_tok, c_t, dt, dt]
    else:
        outs = [xs_t, b_tok, dt, dt]
    return pl.pallas_call(
        kern,
        grid=(n_tok // tm,),
        in_specs=[pl.BlockSpec((tm, D_MODEL), tok),
                  pl.BlockSpec((SUBLANES, D_MODEL), lambda i: (jnp.maximum(i * per - 1, 0), 0)),
                  pl.BlockSpec((SUBLANES, D_MODEL), lambda i: (jnp.minimum((i + 1) * per, last_halo), 0)),
                  pl.BlockSpec((1, D_MODEL), const),
                  mod_spec, mod_spec,
                  pl.BlockSpec(w_bf.shape, const),
                  pl.BlockSpec((D_MODEL, DT_PAD), const),
                  pl.BlockSpec((D_CONV, CONV_DIM), const),
                  pl.BlockSpec((1, CONV_DIM), const),
                  pl.BlockSpec((2 * HEADS, 1), const),
                  pl.BlockSpec((2 * HEADS, 1), const)],
        out_specs=[o[1] for o in outs],
        out_shape=[o[0] for o in outs],
        scratch_shapes=[pltpu.VMEM((2, tm + 2 * SUBLANES, CONV_SEG), jnp.float32),
                        pltpu.VMEM((2, CONV_SEG // LANES, tm, LANES), jnp.float32),
                        pltpu.VMEM((D_MODEL // LANES, tm, LANES), jnp.float32)],
        compiler_params=pltpu.CompilerParams(vmem_limit_bytes=VMEM_LIMIT),
        name="proj" if full else "proj_ctx",
    )(x2d, x2d, x2d, norm_w.reshape(1, D_MODEL), shift, scale, w_bf, wdt_bf, conv_w, conv_b,
      alog_col, bias_col)


def _sweep_chunks(chunk_body, chunks_per_step, reverse):
    def body(i, carry):
        chunk_body(chunks_per_step - 1 - i if reverse else i)
        return carry
    jax.lax.fori_loop(0, chunks_per_step, body, 0, unroll=True)


def _tok_rows(q):
    return pl.ds(pl.multiple_of(q * CHUNK, CHUNK), CHUNK)


def _state_update(h_ref, xs_t_ref, b_ref, q, scale_in, chunk_decay):
    for g in range(GROUPS):
        bg = b_ref[_tok_rows(q), g * D_STATE:(g + 1) * D_STATE]
        xd = []
        for r in range(HEADS_PER_GROUP):
            h = g * HEADS_PER_GROUP + r
            x_h = xs_t_ref[q, h * HEADDIM:(h + 1) * HEADDIM].astype(jnp.float32)
            xd.append(_bf(x_h * scale_in[h:h + 1]))
        s_new = _dot(jnp.concatenate(xd, axis=0), bg)
        for r in range(HEADS_PER_GROUP):
            h = g * HEADS_PER_GROUP + r
            hr = slice(h * HEADDIM, (h + 1) * HEADDIM)
            h_ref[hr] = h_ref[hr] * chunk_decay[h:h + 1] + s_new[r * HEADDIM:(r + 1) * HEADDIM]


def _ssd_fwd_kernel(xs_t_ref, b_ref, dt_ref, cum_ref, alog_ref, h0_ref, *rest, n_steps, cps, with_y):
    if with_y:
        c_t_ref, dskip_ref, y_ref, hout_ref, h_ref = rest
    else:
        hout_ref, h_ref = rest
    step = pl.program_id(1)

    @pl.when(step == 0)
    def _():
        h_ref[...] = h0_ref[0]

    if with_y:
        a_b = -jnp.exp(alog_ref[HEADS:])
        src = jax.lax.broadcasted_iota(jnp.int32, (CHUNK, CHUNK), 0)
        dst = jax.lax.broadcasted_iota(jnp.int32, (CHUNK, CHUNK), 1)
        causal = src <= dst
        is_diag = src == dst

    def chunk(q):
        dt_f, cum_f = dt_ref[q, :HEADS], cum_ref[q, :HEADS]
        tot_f = cum_f[:, CHUNK - 1:CHUNK]
        scale_in = dt_f * jnp.exp(tot_f - cum_f)
        chunk_decay = jnp.exp(tot_f)

        if with_y:
            dt_b, cum_b = dt_ref[q, HEADS:], cum_ref[q, HEADS:]
            cumx_b = cum_b - dt_b * a_b
            col_terms = jnp.concatenate(
                [jnp.log(dt_f) - cum_f, jnp.log(dt_b) + cumx_b,
                 jnp.zeros((CHUNK - 2 * HEADS, CHUNK), jnp.float32)], axis=0).T
            row_f = cum_f
            row_b = -cumx_b
            decay_out_f = jnp.exp(cum_f)
            for g in range(GROUPS):
                bg = b_ref[_tok_rows(q), g * D_STATE:(g + 1) * D_STATE]
                cg_t = c_t_ref[q, g * D_STATE:(g + 1) * D_STATE]
                rows = slice(g * HEADS_PER_GROUP * HEADDIM, (g + 1) * HEADS_PER_GROUP * HEADDIM)
                g_t = _dot(bg, cg_t)
                g_diag = jnp.sum(jnp.where(is_diag, g_t, 0.0), axis=0, keepdims=True)
                y_off = _dot(_bf(h_ref[rows]), cg_t)
                for r in range(HEADS_PER_GROUP):
                    h = g * HEADS_PER_GROUP + r
                    hr = slice(h * HEADDIM, (h + 1) * HEADDIM)
                    x_bf = xs_t_ref[q, hr]
                    col_f = jnp.broadcast_to(col_terms[:, h:h + 1], (CHUNK, CHUNK))
                    col_b = jnp.broadcast_to(col_terms[:, HEADS + h:HEADS + h + 1], (CHUNK, CHUNK))
                    expo = jnp.where(causal, col_f + row_f[h:h + 1], col_b + row_b[h:h + 1])
                    w_t = _bf(g_t * jnp.exp(expo))
                    y_h = _dot(x_bf, w_t)
                    y_h = y_h + y_off[r * HEADDIM:(r + 1) * HEADDIM] * decay_out_f[h:h + 1]
                    skip = dskip_ref[hr] + g_diag * dt_b[h:h + 1]
                    y_ref[q, hr] = _bf(y_h + skip * x_bf.astype(jnp.float32))

        _state_update(h_ref, xs_t_ref, b_ref, q, scale_in, chunk_decay)

    _sweep_chunks(chunk, cps, reverse=False)

    @pl.when(step == n_steps - 1)
    def _():
        hout_ref[0] = h_ref[...]


def _ssd_bwd_kernel(xs_t_ref, b_ref, dt_ref, cum_ref, alog_ref, h0_ref, *rest, n_steps, cps, with_y):
    if with_y:
        (c_t_ref, ypart_ref, yp0_ref, yp1_ref, yp2_ref, yp3_ref, zs_ref, x_ref, gate_ref, snw_ref,
         wout_ref, fnw_ref, o_ref, h_ref, y_ref) = rest
    else:
        hout_ref, h_ref = rest
    step = pl.program_id(1)

    @pl.when(step == 0)
    def _():
        h_ref[...] = h0_ref[0]

    a_b = -jnp.exp(alog_ref[HEADS:])
    if with_y:
        acc = None
        for g, yp_ref in enumerate((yp0_ref, yp1_ref, yp2_ref, yp3_ref)):
            part = _dot(yp_ref[...], wout_ref[g * POOL_GROUP_W:(g + 1) * POOL_GROUP_W])
            acc = part if acc is None else acc + part

    def chunk(q):
        dt_b, cum_b = dt_ref[q, HEADS:], cum_ref[q, HEADS:]
        tot_b = cum_b[:, CHUNK - 1:CHUNK]
        cumx_b = cum_b - dt_b * a_b
        scale_in = dt_b * jnp.exp(cumx_b)
        chunk_decay = jnp.exp(tot_b)

        if with_y:
            decay_out = jnp.exp(tot_b - cumx_b)
            y_parts = []
            for g in range(GROUPS):
                cg_t = c_t_ref[q, g * D_STATE:(g + 1) * D_STATE]
                rows = slice(g * HEADS_PER_GROUP * HEADDIM, (g + 1) * HEADS_PER_GROUP * HEADDIM)
                y_off = _dot(_bf(h_ref[rows]), cg_t)
                for r in range(HEADS_PER_GROUP):
                    h = g * HEADS_PER_GROUP + r
                    hr = slice(h * HEADDIM, (h + 1) * HEADDIM)
                    y_parts.append(ypart_ref[q, hr].astype(jnp.float32)
                                   + y_off[r * HEADDIM:(r + 1) * HEADDIM] * decay_out[h:h + 1])
            y_ref[_tok_rows(q), :] = jnp.concatenate(y_parts, axis=0).T

        _state_update(h_ref, xs_t_ref, b_ref, q, scale_in, chunk_decay)

    _sweep_chunks(chunk, cps, reverse=True)

    if with_y:
        gw = W_SSD // GROUPS
        for g in range(GROUPS):
            cols = slice(g * gw, (g + 1) * gw)
            gated = y_ref[:, cols] * zs_ref[:, cols].astype(jnp.float32)
            ms = jnp.mean(gated * gated, axis=-1, keepdims=True)
            yn = gated * jax.lax.rsqrt(ms + EPS) * snw_ref[:, cols]
            acc = acc + _dot(_bf(yn), wout_ref[W_POOL + g * gw:W_POOL + (g + 1) * gw])
        hres = x_ref[...] + gate_ref[0] * acc
        ms = jnp.mean(hres * hres, axis=-1, keepdims=True)
        o_ref[...] = hres * jax.lax.rsqrt(ms + EPS) * fnw_ref[...]
    else:
        @pl.when(step == n_steps - 1)
        def _():
            hout_ref[0] = h_ref[...]


def _ssd_sweep(reverse, xs_t, b_tok, dt, cum, alog_col, h0, bsz, n_chunks, c_t=None, dskip_b=None,
               y_part=None, out_args=None):
    with_y = c_t is not None
    n_tok = b_tok.shape[0]
    cps = min(n_chunks, SSD_CHUNKS_PER_STEP)
    n_steps = n_chunks // cps

    def block_of(b, s):
        return b * n_steps + (n_steps - 1 - s if reverse else s)

    tok = lambda b, s: (block_of(b, s), 0)
    chunk3 = lambda b, s: (block_of(b, s), 0, 0)
    const2 = lambda b, s: (0, 0)
    state3 = lambda b, s: (b, 0, 0)
    h_spec = pl.BlockSpec((1, W_SSD, D_STATE), state3)
    h_shape = jax.ShapeDtypeStruct((bsz, W_SSD, D_STATE), jnp.float32)
    head_spec = pl.BlockSpec((cps, 2 * HEADS, CHUNK), chunk3)
    in_specs = [pl.BlockSpec((cps, W_SSD, CHUNK), chunk3),
                pl.BlockSpec((cps * CHUNK, GN), tok),
                head_spec, head_spec,
                pl.BlockSpec((2 * HEADS, 1), const2),
                h_spec]
    args = [xs_t, b_tok, dt, cum, alog_col, h0]
    y_t_spec = pl.BlockSpec((cps, W_SSD, CHUNK), chunk3)
    scratch = [pltpu.VMEM((W_SSD, D_STATE), jnp.float32)]
    if with_y:
        in_specs.append(pl.BlockSpec((cps, GN, CHUNK), chunk3))
        args.append(c_t)
        if reverse:
            y_pool, gate_ssd, x2d, gate, ssd_norm_w, w_out_bf, final_norm_w = out_args
            tm = cps * CHUNK
            in_specs += [y_t_spec] + [pl.BlockSpec((tm, POOL_GROUP_W), tok)] * N_POOL_GROUPS + [
                pl.BlockSpec((tm, W_SSD), tok),
                pl.BlockSpec((tm, D_MODEL), tok),
                pl.BlockSpec((1, 1, D_MODEL), lambda b, s: (b, 0, 0)),
                pl.BlockSpec((1, W_SSD), const2),
                pl.BlockSpec((W_POOL + W_SSD, D_MODEL), const2),
                pl.BlockSpec((1, D_MODEL), const2)]
            args += [y_part, *y_pool, gate_ssd, x2d, gate, ssd_norm_w.reshape(1, W_SSD), w_out_bf,
                     final_norm_w.reshape(1, D_MODEL)]
            out_shape = [jax.ShapeDtypeStruct((n_tok, D_MODEL), jnp.float32)]
            out_specs = [pl.BlockSpec((tm, D_MODEL), tok)]
            scratch.append(pltpu.VMEM((tm, W_SSD), jnp.float32))
        else:
            in_specs.append(pl.BlockSpec((W_SSD, CHUNK), const2))
            args.append(dskip_b)
            out_shape = [jax.ShapeDtypeStruct(xs_t.shape, jnp.bfloat16), h_shape]
            out_specs = [y_t_spec, h_spec]
    else:
        out_shape = [h_shape]
        out_specs = [h_spec]
    body = _ssd_bwd_kernel if reverse else _ssd_fwd_kernel
    name = ("ssd_bwd" if reverse else "ssd_fwd") + ("" if with_y else "_state")
    return pl.pallas_call(
        functools.partial(body, n_steps=n_steps, cps=cps, with_y=with_y),
        grid=(bsz, n_steps),
        in_specs=in_specs,
        out_specs=out_specs,
        out_shape=out_shape,
        scratch_shapes=scratch,
        compiler_params=pltpu.CompilerParams(
            dimension_semantics=("arbitrary", "arbitrary"), vmem_limit_bytes=VMEM_LIMIT),
        name=name,
    )(*args)


POOL_TILE_ROWS = 4
POOL_TILE = POOL_TILE_ROWS * GRID_W


def _pool_constants(window, n_rows):
    lo_off, hi_off = -(window // 2), window - window // 2
    col = np.arange(GRID_W)
    lo = np.clip(col + lo_off, 0, GRID_W)
    hi = np.clip(col + hi_off, 0, GRID_W)
    band = ((col[None, :] >= lo[:, None]) & (col[None, :] < hi[:, None])).astype(np.float32)
    band_tile = np.kron(np.eye(POOL_TILE_ROWS, dtype=np.float32), band)
    row = np.arange(n_rows)
    cnt_r = np.clip(row + hi_off, 0, n_rows) - np.clip(row + lo_off, 0, n_rows)
    inv = 1.0 / (cnt_r[:, None] * (hi - lo)[None, :]).astype(np.float64)
    inv = np.broadcast_to(inv.reshape(-1, 1), (n_rows * GRID_W, 128)).astype(np.float32)
    return jnp.asarray(band_tile, jnp.bfloat16), jnp.asarray(inv)


def _pool_kernel(u_ref, z_ref, band_ref, inv_ref, w_ref, scale_ref, o_ref, *, window, n_rows):
    def grid_row(r):
        return u_ref[r * GRID_W:(r + 1) * GRID_W]

    def bounds(r):
        return max(r - window // 2, 0), min(r + window - window // 2, n_rows)

    band = band_ref[...]
    rsum, tile_rows = None, []
    for r in range(n_rows):
        lo, hi = bounds(r)
        if r == 0 or window <= 2:
            rsum = grid_row(lo)
            for k in range(lo + 1, hi):
                rsum = rsum + grid_row(k)
        else:
            prev_lo, prev_hi = bounds(r - 1)
            if hi > prev_hi:
                rsum = rsum + grid_row(hi - 1)
            if lo > prev_lo:
                rsum = rsum - grid_row(prev_lo)
        tile_rows.append(rsum)
        if len(tile_rows) < POOL_TILE_ROWS:
            continue
        base = (r + 1 - POOL_TILE_ROWS) * GRID_W
        rows = slice(base, base + POOL_TILE)
        rs = jnp.concatenate(tile_rows, axis=0)
        tile_rows = []
        box = _dot(band, _bf(rs))
        inv = inv_ref[rows]
        mean = box * jnp.concatenate([inv, inv], axis=1)
        d = mean - u_ref[rows]
        y = _dot(_bf(d), w_ref[0]) * scale_ref[...]
        o_ref[rows] = _bf(y * z_ref[rows].astype(jnp.float32))


def _pool_group(u, gate, pool_w_bf, pool_scale, g, bsz, n_img_tok):
    window = POOL_WINDOWS[g]
    n_rows = n_img_tok // GRID_W
    band, inv = _pool_constants(window, n_rows)
    kern = functools.partial(_pool_kernel, window=window, n_rows=n_rows)
    img = pl.BlockSpec((n_img_tok, POOL_GROUP_W), lambda b: (b, 0))
    return pl.pallas_call(
        kern,
        grid=(bsz,),
        in_specs=[img, img,
                  pl.BlockSpec((POOL_TILE, POOL_TILE), lambda b: (0, 0)),
                  pl.BlockSpec((n_img_tok, 128), lambda b: (0, 0)),
                  pl.BlockSpec((1, POOL_GROUP_W, POOL_GROUP_W), lambda b: (g, 0, 0)),
                  pl.BlockSpec((1, POOL_GROUP_W), lambda b: (0, g))],
        out_specs=img,
        out_shape=jax.ShapeDtypeStruct((bsz * n_img_tok, POOL_GROUP_W), jnp.bfloat16),
        compiler_params=pltpu.CompilerParams(vmem_limit_bytes=VMEM_LIMIT),
        name=f"pool{window}",
    )(u, gate, band, inv, pool_w_bf, pool_scale)


def kernel(x, c, ctx, c_ctx, norm_w, w_ada, b_ada, w_in, conv_w, conv_b, a_log, dt_bias, d_skip,
           ssd_norm_w, pool_w, pool_scale, w_out, final_norm_w):
    bsz, seq, _ = x.shape
    ctx_len = ctx.shape[1]
    depth = norm_w.shape[0]
    assert depth == 1, "single-layer block: the context stream update is never consumed"
    assert seq % 512 == 0 and ctx_len % CHUNK == 0 and seq % GRID_W == 0

    mod_rows = -(-(bsz + 1) // SUBLANES) * SUBLANES
    cond = jnp.concatenate([c, c_ctx[None], jnp.zeros((mod_rows - bsz - 1, D_MODEL), c.dtype)])
    mod = _modulation(cond, w_ada[0], b_ada[0])
    shift = mod[:, :D_MODEL].reshape(mod_rows, 1, D_MODEL)
    scale = mod[:, D_MODEL:2 * D_MODEL].reshape(mod_rows, 1, D_MODEL)
    gate = mod[:, 2 * D_MODEL:].reshape(mod_rows, 1, D_MODEL)

    w_in_bf = _bf(w_in[0])
    w_dt_bf = jnp.pad(_bf(w_in[0, :, OFF_DT:]), ((0, 0), (0, DT_PAD - 2 * HEADS)))
    alog_col = a_log[0].reshape(2 * HEADS, 1)
    bias_col = dt_bias[0].reshape(2 * HEADS, 1)
    dskip_b = jnp.broadcast_to(jnp.repeat(d_skip[0], HEADDIM)[:, None], (W_SSD, CHUNK))
    conv_b2 = conv_b[0].reshape(1, CONV_DIM)
    zero_state = jnp.zeros((bsz, W_SSD, D_STATE), jnp.float32)

    ctx2d = ctx.reshape(bsz * ctx_len, D_MODEL)
    xs_t_c, b_c, dt_c, cum_c = _projection(
        ctx2d, norm_w[0], shift[bsz:bsz + 1], scale[bsz:bsz + 1], w_in_bf, w_dt_bf, conv_w[0],
        conv_b2, alog_col, bias_col, ctx_len, ctx_len, full=False)
    nc_ctx = ctx_len // CHUNK
    (h_fwd,) = _ssd_sweep(False, xs_t_c, b_c, dt_c, cum_c, alog_col, zero_state, bsz, nc_ctx)
    (h_bwd,) = _ssd_sweep(True, xs_t_c, b_c, dt_c, cum_c, alog_col, zero_state, bsz, nc_ctx)

    x2d = x.reshape(bsz * seq, D_MODEL)
    outs = _projection(x2d, norm_w[0], shift, scale, w_in_bf, w_dt_bf, conv_w[0], conv_b2,
                       alog_col, bias_col, seq, 512, full=True)
    u_pool, gate_pool = outs[:N_POOL_GROUPS], outs[N_POOL_GROUPS:2 * N_POOL_GROUPS]
    gate_ssd, xs_t, b_tok, c_t, dt, cum = outs[2 * N_POOL_GROUPS:]
    nc = seq // CHUNK
    y_part, _ = _ssd_sweep(False, xs_t, b_tok, dt, cum, alog_col, h_fwd, bsz, nc, c_t=c_t,
                           dskip_b=dskip_b)
    pool_w_bf = _bf(pool_w[0])
    y_pool = [_pool_group(u_pool[g], gate_pool[g], pool_w_bf, pool_scale, g, bsz, seq)
              for g in range(N_POOL_GROUPS)]
    (out,) = _ssd_sweep(True, xs_t, b_tok, dt, cum, alog_col, h_bwd, bsz, nc, c_t=c_t,
                        y_part=y_part,
                        out_args=(y_pool, gate_ssd, x2d, gate, ssd_norm_w[0], _bf(w_out[0]),
                                  final_norm_w))
    return out.reshape(bsz, seq, D_MODEL)
```

```python
import functools

import numpy as np
import jax
import jax.numpy as jnp
from jax.experimental import pallas as pl
from jax.experimental.pallas import tpu as pltpu

D_MODEL = 1024
GRID_W = 64
W_POOL = 1024
W_SSD = 1024
POOL_WINDOWS = (2, 4, 8, 16)
N_POOL_GROUPS = len(POOL_WINDOWS)
POOL_GROUP_W = 256
HEADDIM = 64
HEADS = 16
GROUPS = 4
HEADS_PER_GROUP = 4
D_STATE = 128
D_CONV = 4
CONV_LEFT = 2
CHUNK = 128
GN = GROUPS * D_STATE
CONV_DIM = W_SSD + 2 * GN
OFF_POOL_Z = W_POOL
OFF_SSD_Z = 2 * W_POOL
OFF_XBC = 2 * W_POOL + W_SSD
OFF_DT = OFF_XBC + CONV_DIM
DT_PAD = 128
EPS = 1e-6
SUBLANES = 8
LANES = 128
IL_GROUPS = CHUNK // SUBLANES
CONV_SEG = 512
SSD_CHUNKS_PER_STEP = 4
VMEM_LIMIT = 56 * 1024 * 1024


def _silu(v):
    h = 0.5 * v
    return h + h * jnp.tanh(h)


def _softplus(v):
    return jnp.maximum(v, 0.0) + jnp.log1p(jnp.exp(-jnp.abs(v)))


def _bf(v):
    return v.astype(jnp.bfloat16)


def _dot(a, b):
    return jnp.dot(a, b, preferred_element_type=jnp.float32)


def _mod_kernel(c_ref, w_ref, b_ref, o_ref):
    s = _silu(c_ref[...])
    o_ref[...] = jnp.dot(s, w_ref[...], preferred_element_type=jnp.float32,
                         precision=jax.lax.Precision.HIGHEST) + b_ref[...]


def _modulation(cond_rows, w_ada, b_ada):
    rows = cond_rows.shape[0]
    n_out = w_ada.shape[1]
    tn = 1024
    return pl.pallas_call(
        _mod_kernel,
        grid=(n_out // tn,),
        in_specs=[pl.BlockSpec((rows, D_MODEL), lambda j: (0, 0)),
                  pl.BlockSpec((D_MODEL, tn), lambda j: (0, j)),
                  pl.BlockSpec((1, tn), lambda j: (0, j))],
        out_specs=pl.BlockSpec((rows, tn), lambda j: (0, j)),
        out_shape=jax.ShapeDtypeStruct((rows, n_out), jnp.float32),
        compiler_params=pltpu.CompilerParams(vmem_limit_bytes=VMEM_LIMIT),
        name="mod",
    )(cond_rows, w_ada, b_ada.reshape(1, n_out))


def _lane_cumsum(v):
    lane = jax.lax.broadcasted_iota(jnp.int32, v.shape, 1)
    shift = 1
    while shift < CHUNK:
        v = v + jnp.where(lane >= shift, pltpu.roll(v, shift, 1), 0.0)
        shift *= 2
    return v


def _proj_kernel(x_ref, xp_ref, xn_ref, nw_ref, sh_ref, sc_ref, w_ref, wdt_ref, cw_ref, cb_ref,
                 alog_ref, bias_ref, *rest, tm, tiles_per_seq, full):
    if full:
        (u0, u1, u2, u3, zp0, zp1, zp2, zp3, zs_ref, xs_t_ref, b_ref, c_t_ref, dt_ref, cum_ref,
         pe_ref, xc_ref, mn_ref) = rest
        u_refs, zp_refs = (u0, u1, u2, u3), (zp0, zp1, zp2, zp3)
    else:
        xs_t_ref, b_ref, dt_ref, cum_ref, pe_ref, xc_ref, mn_ref = rest
    i = pl.program_id(0)
    pos = i % tiles_per_seq
    has_prev = pos > 0
    has_next = pos < tiles_per_seq - 1
    n_chunks = tm // CHUNK
    seg = CONV_SEG

    def modulated(v):
        ms = jnp.mean(v * v, axis=-1, keepdims=True)
        y = v * jax.lax.rsqrt(ms + EPS) * nw_ref[...]
        return y * (1.0 + sc_ref[0]) + sh_ref[0]

    m_tok = modulated(x_ref[...])
    hm = _bf(m_tok)
    for t in range(D_MODEL // LANES):
        mn_ref[t] = m_tok[:, t * LANES:(t + 1) * LANES]

    rows = [jnp.concatenate([mn_ref[t, pl.ds(q * CHUNK + b, SUBLANES, stride=IL_GROUPS), :]
                             for t in range(D_MODEL // LANES)], axis=1)
            for q in range(n_chunks) for b in range(IL_GROUPS)]
    halo = [jnp.where(has_prev, modulated(xp_ref[...]), 0.0),
            jnp.where(has_next, modulated(xn_ref[...]), 0.0)]
    hm_il = _bf(jnp.concatenate(halo + rows, axis=0))
    sub = jax.lax.broadcasted_iota(jnp.int32, (SUBLANES, seg), 0)

    def conv_stage(j, slot):
        is_x = j < W_SSD
        is_b = W_SSD <= j < W_SSD + GN

        def matmul():
            pe_ref[slot] = _dot(hm_il, w_ref[:, OFF_XBC + j:OFF_XBC + j + seg])

        def group(q, b):
            lo = 2 * SUBLANES + q * CHUNK + b * SUBLANES
            return pe_ref[slot, lo:lo + SUBLANES]

        def shifted(q, b, delta):
            bb = b + delta
            if 0 <= bb < IL_GROUPS:
                return group(q, bb)
            if bb < 0:
                bb += IL_GROUPS
                if q == 0:
                    first = pe_ref[slot, bb - SUBLANES:bb - SUBLANES + 1]
                else:
                    row = 2 * SUBLANES + (q - 1) * CHUNK + bb * SUBLANES + SUBLANES - 1
                    first = pe_ref[slot, row:row + 1]
                return jnp.where(sub == 0, first, pltpu.roll(group(q, bb), 1, 0))
            bb -= IL_GROUPS
            if q == n_chunks - 1:
                last = pe_ref[slot, SUBLANES + bb:SUBLANES + bb + 1]
            else:
                nxt = 2 * SUBLANES + (q + 1) * CHUNK + bb * SUBLANES
                last = pe_ref[slot, nxt:nxt + 1]
            return jnp.where(sub == SUBLANES - 1, last, pltpu.roll(group(q, bb), SUBLANES - 1, 0))

        def epilogue():
            taps = [cw_ref[k:k + 1, j:j + seg] for k in range(D_CONV)]
            bias = cb_ref[:, j:j + seg]
            for q in range(n_chunks):
                for b in range(IL_GROUPS):
                    acc = bias
                    for k in range(D_CONV):
                        acc = acc + shifted(q, b, k - CONV_LEFT) * taps[k]
                    lo = q * CHUNK + b * SUBLANES
                    act = _silu(acc)
                    for t in range(seg // LANES):
                        xc_ref[slot, t, lo:lo + SUBLANES] = act[:, t * LANES:(t + 1) * LANES]
            for q in range(n_chunks):
                xc = jnp.concatenate(
                    [jnp.concatenate(
                        [xc_ref[slot, t, pl.ds(q * CHUNK + (m % 2) * (CHUNK // 2) + m // 2,
                                               SUBLANES, stride=SUBLANES), :]
                         for t in range(seg // LANES)], axis=1)
                     for m in range(IL_GROUPS)], axis=0)
                if is_b:
                    b_ref[q * CHUNK:(q + 1) * CHUNK, j - W_SSD:j - W_SSD + seg] = _bf(xc)
                else:
                    dst, off = (xs_t_ref, j) if is_x else (c_t_ref, j - W_SSD - GN)
                    dst[q, off:off + seg] = _bf(xc.T)

        return matmul, epilogue

    def plain_stage(cols, finish):
        box = []
        return (lambda: box.append(_dot(hm, w_ref[:, cols]))), (lambda: finish(box.pop()))

    def dt_stage():
        box = []

        def epilogue():
            p_dt = box.pop()
            a_col = -jnp.exp(alog_ref[...])
            for q in range(n_chunks):
                dt = _softplus(p_dt[q * CHUNK:(q + 1) * CHUNK].T[:2 * HEADS] + bias_ref[...])
                dt_ref[q] = dt
                cum_ref[q] = _lane_cumsum(dt * a_col)

        return (lambda: box.append(_dot(hm, wdt_ref[...]))), epilogue

    def store_to(ref, cols=None, act=None):
        def finish(v):
            v = v if act is None else _bf(act(v))
            if cols is None:
                ref[...] = v
            else:
                ref[:, cols] = v
        return finish

    def store_pair(refs, act=None):
        def finish(v):
            for n, ref in enumerate(refs):
                part = v[:, n * POOL_GROUP_W:(n + 1) * POOL_GROUP_W]
                ref[...] = part if act is None else _bf(act(part))
        return finish

    stages = []
    if full:
        for g in range(0, N_POOL_GROUPS, 2):
            stages.append(plain_stage(slice(g * POOL_GROUP_W, (g + 2) * POOL_GROUP_W),
                                      store_pair(u_refs[g:g + 2])))
    stages.append(dt_stage())
    n_conv = 0
    for j in range(0, CONV_DIM, seg):
        if full or j < W_SSD + GN:
            stages.append(conv_stage(j, n_conv % 2))
            n_conv += 1
    if full:
        for g in range(0, N_POOL_GROUPS, 2):
            zcols = slice(OFF_POOL_Z + g * POOL_GROUP_W, OFF_POOL_Z + (g + 2) * POOL_GROUP_W)
            stages.append(plain_stage(zcols, store_pair(zp_refs[g:g + 2], act=_silu)))
        for j in range(0, W_SSD, seg):
            stages.append(plain_stage(slice(OFF_SSD_Z + j, OFF_SSD_Z + j + seg),
                                      store_to(zs_ref, cols=slice(j, j + seg), act=_silu)))

    stages[0][0]()
    for k, (_, epilogue) in enumerate(stages):
        if k + 1 < len(stages):
            stages[k + 1][0]()
        epilogue()


def _projection(x2d, norm_w, shift, scale, w_bf, wdt_bf, conv_w, conv_b, alog_col, bias_col,
                seq_len, tm, full):
    n_tok = x2d.shape[0]
    tiles_per_seq = seq_len // tm
    n_mod = shift.shape[0]
    nct = n_tok // CHUNK
    per = tm // SUBLANES
    last_halo = n_tok // SUBLANES - 1
    kern = functools.partial(_proj_kernel, tm=tm, tiles_per_seq=tiles_per_seq, full=full)
    mod_map = (lambda i: (i // tiles_per_seq, 0, 0)) if n_mod > 1 else (lambda i: (0, 0, 0))
    mod_spec = pl.BlockSpec((1, 1, D_MODEL), mod_map)
    const = lambda i: (0, 0)
    tok = lambda i: (i, 0)
    chunk3 = lambda i: (i, 0, 0)
    q = tm // CHUNK
    xs_t = (jax.ShapeDtypeStruct((nct, W_SSD, CHUNK), jnp.bfloat16),
            pl.BlockSpec((q, W_SSD, CHUNK), chunk3))
    b_tok = (jax.ShapeDtypeStruct((n_tok, GN), jnp.bfloat16), pl.BlockSpec((tm, GN), tok))
    c_t = (jax.ShapeDtypeStruct((nct, GN, CHUNK), jnp.bfloat16), pl.BlockSpec((q, GN, CHUNK), chunk3))
    dt = (jax.ShapeDtypeStruct((nct, 2 * HEADS, CHUNK), jnp.float32),
          pl.BlockSpec((q, 2 * HEADS, CHUNK), chunk3))
    if full:
        u = (jax.ShapeDtypeStruct((n_tok, POOL_GROUP_W), jnp.float32),
             pl.BlockSpec((tm, POOL_GROUP_W), tok))
        zp = (jax.ShapeDtypeStruct((n_tok, POOL_GROUP_W), jnp.bfloat16),
              pl.BlockSpec((tm, POOL_GROUP_W), tok))
        zs = (jax.ShapeDtypeStruct((n_tok, W_SSD), jnp.bfloat16), pl.BlockSpec((tm, W_SSD), tok))
        outs = [u] * N_POOL_GROUPS + [zp] * N_POOL_GROUPS + [zs, xs_t, b_tok, c_t, dt, dt]
    else:
        outs = [xs_t, b_tok, dt, dt]
    return pl.pallas_call(
        kern,
        grid=(n_tok // tm,),
        in_specs=[pl.BlockSpec((tm, D_MODEL), tok),
                  pl.BlockSpec((SUBLANES, D_MODEL), lambda i: (jnp.maximum(i * per - 1, 0), 0)),
                  pl.BlockSpec((SUBLANES, D_MODEL), lambda i: (jnp.minimum((i + 1) * per, last_halo), 0)),
                  pl.BlockSpec((1, D_MODEL), const),
                  mod_spec, mod_spec,
                  pl.BlockSpec(w_bf.shape, const),
                  pl.BlockSpec((D_MODEL, DT_PAD), const),
                  pl.BlockSpec((D_CONV, CONV_DIM), const),
                  pl.BlockSpec((1, CONV_DIM), const),
                  pl.BlockSpec((2 * HEADS, 1), const),
                  pl.BlockSpec((2 * HEADS, 1), const)],
        out_specs=[o[1] for o in outs],
        out_shape=[o[0] for o in outs],
        scratch_shapes=[pltpu.VMEM((2, tm + 2 * SUBLANES, CONV_SEG), jnp.float32),
                        pltpu.VMEM((2, CONV_SEG // LANES, tm, LANES), jnp.float32),
                        pltpu.VMEM((D_MODEL // LANES, tm, LANES), jnp.float32)],
        compiler_params=pltpu.CompilerParams(vmem_limit_bytes=VMEM_LIMIT),
        name="proj" if full else "proj_ctx",
    )(x2d, x2d, x2d, norm_w.reshape(1, D_MODEL), shift, scale, w_bf, wdt_bf, conv_w, conv_b,
      alog_col, bias_col)


def _sweep_chunks(chunk_body, chunks_per_step, reverse):
    def body(i, carry):
        chunk_body(chunks_per_step - 1 - i if reverse else i)
        return carry
    jax.lax.fori_loop(0, chunks_per_step, body, 0, unroll=True)


def _tok_rows(q):
    if isinstance(q, int):
        return pl.ds(q * CHUNK, CHUNK)
    return pl.ds(pl.multiple_of(q * CHUNK, CHUNK), CHUNK)


def _state_update(h_ref, xs_t_ref, b_ref, q, scale_in, chunk_decay):
    for g in range(GROUPS):
        bg = b_ref[_tok_rows(q), g * D_STATE:(g + 1) * D_STATE]
        xd = []
        for r in range(HEADS_PER_GROUP):
            h = g * HEADS_PER_GROUP + r
            x_h = xs_t_ref[q, h * HEADDIM:(h + 1) * HEADDIM].astype(jnp.float32)
            xd.append(_bf(x_h * scale_in[h:h + 1]))
        s_new = _dot(jnp.concatenate(xd, axis=0), bg)
        for r in range(HEADS_PER_GROUP):
            h = g * HEADS_PER_GROUP + r
            hr = slice(h * HEADDIM, (h + 1) * HEADDIM)
            h_ref[hr] = h_ref[hr] * chunk_decay[h:h + 1] + s_new[r * HEADDIM:(r + 1) * HEADDIM]


def _ssd_fwd_kernel(xs_t_ref, b_ref, dt_ref, cum_ref, alog_ref, h0_ref, *rest, n_steps, cps, with_y):
    if with_y:
        c_t_ref, dskip_ref, y_ref, hout_ref, h_ref = rest
    else:
        hout_ref, h_ref = rest
    step = pl.program_id(1)

    @pl.when(step == 0)
    def _():
        h_ref[...] = h0_ref[0]

    if with_y:
        a_b = -jnp.exp(alog_ref[HEADS:])
        src = jax.lax.broadcasted_iota(jnp.int32, (CHUNK, CHUNK), 0)
        dst = jax.lax.broadcasted_iota(jnp.int32, (CHUNK, CHUNK), 1)
        causal = src <= dst
        is_diag = src == dst

    def chunk(q):
        dt_f, cum_f = dt_ref[q, :HEADS], cum_ref[q, :HEADS]
        tot_f = cum_f[:, CHUNK - 1:CHUNK]
        scale_in = dt_f * jnp.exp(tot_f - cum_f)
        chunk_decay = jnp.exp(tot_f)

        if with_y:
            dt_b, cum_b = dt_ref[q, HEADS:], cum_ref[q, HEADS:]
            cumx_b = cum_b - dt_b * a_b
            col_terms = jnp.concatenate(
                [jnp.log(dt_f) - cum_f, jnp.log(dt_b) + cumx_b,
                 jnp.zeros((CHUNK - 2 * HEADS, CHUNK), jnp.float32)], axis=0).T
            row_f = cum_f
            row_b = -cumx_b
            decay_out_f = jnp.exp(cum_f)
            for g in range(GROUPS):
                bg = b_ref[_tok_rows(q), g * D_STATE:(g + 1) * D_STATE]
                cg_t = c_t_ref[q, g * D_STATE:(g + 1) * D_STATE]
                rows = slice(g * HEADS_PER_GROUP * HEADDIM, (g + 1) * HEADS_PER_GROUP * HEADDIM)
                g_t = _dot(bg, cg_t)
                g_diag = jnp.sum(jnp.where(is_diag, g_t, 0.0), axis=0, keepdims=True)
                y_off = _dot(_bf(h_ref[rows]), cg_t)
                for r in range(HEADS_PER_GROUP):
                    h = g * HEADS_PER_GROUP + r
                    hr = slice(h * HEADDIM, (h + 1) * HEADDIM)
                    x_bf = xs_t_ref[q, hr]
                    col_f = jnp.broadcast_to(col_terms[:, h:h + 1], (CHUNK, CHUNK))
                    col_b = jnp.broadcast_to(col_terms[:, HEADS + h:HEADS + h + 1], (CHUNK, CHUNK))
                    expo = jnp.where(causal, col_f + row_f[h:h + 1], col_b + row_b[h:h + 1])
                    w_t = _bf(g_t * jnp.exp(expo))
                    y_h = _dot(x_bf, w_t)
                    y_h = y_h + y_off[r * HEADDIM:(r + 1) * HEADDIM] * decay_out_f[h:h + 1]
                    skip = dskip_ref[hr] + g_diag * dt_b[h:h + 1]
                    y_ref[q, hr] = _bf(y_h + skip * x_bf.astype(jnp.float32))

        _state_update(h_ref, xs_t_ref, b_ref, q, scale_in, chunk_decay)

    _sweep_chunks(chunk, cps, reverse=False)

    @pl.when(step == n_steps - 1)
    def _():
        hout_ref[0] = h_ref[...]


def _ssd_bwd_kernel(xs_t_ref, b_ref, dt_ref, cum_ref, alog_ref, h0_ref, *rest, n_steps, cps, with_y):
    if with_y:
        (c_t_ref, ypart_ref, yp0_ref, yp1_ref, yp2_ref, yp3_ref, zs_ref, x_ref, gate_ref, snw_ref,
         wout_ref, fnw_ref, o_ref, h_ref, y_ref) = rest
    else:
        hout_ref, h_ref = rest
    step = pl.program_id(1)

    @pl.when(step == 0)
    def _():
        h_ref[...] = h0_ref[0]

    a_b = -jnp.exp(alog_ref[HEADS:])
    def chunk(q):
        dt_b, cum_b = dt_ref[q, HEADS:], cum_ref[q, HEADS:]
        tot_b = cum_b[:, CHUNK - 1:CHUNK]
        cumx_b = cum_b - dt_b * a_b
        scale_in = dt_b * jnp.exp(cumx_b)
        chunk_decay = jnp.exp(tot_b)

        if with_y:
            decay_out = jnp.exp(tot_b - cumx_b)
            y_parts = []
            for g in range(GROUPS):
                cg_t = c_t_ref[q, g * D_STATE:(g + 1) * D_STATE]
                rows = slice(g * HEADS_PER_GROUP * HEADDIM, (g + 1) * HEADS_PER_GROUP * HEADDIM)
                y_off = _dot(_bf(h_ref[rows]), cg_t)
                for r in range(HEADS_PER_GROUP):
                    h = g * HEADS_PER_GROUP + r
                    hr = slice(h * HEADDIM, (h + 1) * HEADDIM)
                    y_parts.append(ypart_ref[q, hr].astype(jnp.float32)
                                   + y_off[r * HEADDIM:(r + 1) * HEADDIM] * decay_out[h:h + 1])
            y_ref[_tok_rows(q), :] = jnp.concatenate(y_parts, axis=0).T

        _state_update(h_ref, xs_t_ref, b_ref, q, scale_in, chunk_decay)

    if with_y:
        yp_refs = [yp0_ref, yp1_ref, yp2_ref, yp3_ref]
        acc = jnp.zeros((cps * CHUNK, D_MODEL), jnp.float32)
        for i in range(max(cps, N_POOL_GROUPS)):
            if i < cps:
                chunk(cps - 1 - i)
            if i < N_POOL_GROUPS:
                acc = acc + _dot(yp_refs[i][...],
                                 wout_ref[i * POOL_GROUP_W:(i + 1) * POOL_GROUP_W])
    else:
        _sweep_chunks(chunk, cps, reverse=True)

    if with_y:
        gw = W_SSD // GROUPS
        for g in range(GROUPS):
            cols = slice(g * gw, (g + 1) * gw)
            gated = y_ref[:, cols] * zs_ref[:, cols].astype(jnp.float32)
            ms = jnp.mean(gated * gated, axis=-1, keepdims=True)
            yn = gated * jax.lax.rsqrt(ms + EPS) * snw_ref[:, cols]
            acc = acc + _dot(_bf(yn), wout_ref[W_POOL + g * gw:W_POOL + (g + 1) * gw])
        hres = x_ref[...] + gate_ref[0] * acc
        ms = jnp.mean(hres * hres, axis=-1, keepdims=True)
        o_ref[...] = hres * jax.lax.rsqrt(ms + EPS) * fnw_ref[...]
    else:
        @pl.when(step == n_steps - 1)
        def _():
            hout_ref[0] = h_ref[...]


def _ssd_sweep(reverse, xs_t, b_tok, dt, cum, alog_col, h0, bsz, n_chunks, c_t=None, dskip_b=None,
               y_part=None, out_args=None):
    with_y = c_t is not None
    n_tok = b_tok.shape[0]
    cps = min(n_chunks, SSD_CHUNKS_PER_STEP)
    n_steps = n_chunks // cps

    def block_of(b, s):
        return b * n_steps + (n_steps - 1 - s if reverse else s)

    tok = lambda b, s: (block_of(b, s), 0)
    chunk3 = lambda b, s: (block_of(b, s), 0, 0)
    const2 = lambda b, s: (0, 0)
    state3 = lambda b, s: (b, 0, 0)
    h_spec = pl.BlockSpec((1, W_SSD, D_STATE), state3)
    h_shape = jax.ShapeDtypeStruct((bsz, W_SSD, D_STATE), jnp.float32)
    head_spec = pl.BlockSpec((cps, 2 * HEADS, CHUNK), chunk3)
    in_specs = [pl.BlockSpec((cps, W_SSD, CHUNK), chunk3),
                pl.BlockSpec((cps * CHUNK, GN), tok),
                head_spec, head_spec,
                pl.BlockSpec((2 * HEADS, 1), const2),
                h_spec]
    args = [xs_t, b_tok, dt, cum, alog_col, h0]
    y_t_spec = pl.BlockSpec((cps, W_SSD, CHUNK), chunk3)
    scratch = [pltpu.VMEM((W_SSD, D_STATE), jnp.float32)]
    if with_y:
        in_specs.append(pl.BlockSpec((cps, GN, CHUNK), chunk3))
        args.append(c_t)
        if reverse:
            y_pool, gate_ssd, x2d, gate, ssd_norm_w, w_out_bf, final_norm_w = out_args
            tm = cps * CHUNK
            in_specs += [y_t_spec] + [pl.BlockSpec((tm, POOL_GROUP_W), tok)] * N_POOL_GROUPS + [
                pl.BlockSpec((tm, W_SSD), tok),
                pl.BlockSpec((tm, D_MODEL), tok),
                pl.BlockSpec((1, 1, D_MODEL), lambda b, s: (b, 0, 0)),
                pl.BlockSpec((1, W_SSD), const2),
                pl.BlockSpec((W_POOL + W_SSD, D_MODEL), const2),
                pl.BlockSpec((1, D_MODEL), const2)]
            args += [y_part, *y_pool, gate_ssd, x2d, gate, ssd_norm_w.reshape(1, W_SSD), w_out_bf,
                     final_norm_w.reshape(1, D_MODEL)]
            out_shape = [jax.ShapeDtypeStruct((n_tok, D_MODEL), jnp.float32)]
            out_specs = [pl.BlockSpec((tm, D_MODEL), tok)]
            scratch.append(pltpu.VMEM((tm, W_SSD), jnp.float32))
        else:
            in_specs.append(pl.BlockSpec((W_SSD, CHUNK), const2))
            args.append(dskip_b)
            out_shape = [jax.ShapeDtypeStruct(xs_t.shape, jnp.bfloat16), h_shape]
            out_specs = [y_t_spec, h_spec]
    else:
        out_shape = [h_shape]
        out_specs = [h_spec]
    body = _ssd_bwd_kernel if reverse else _ssd_fwd_kernel
    name = ("ssd_bwd" if reverse else "ssd_fwd") + ("" if with_y else "_state")
    return pl.pallas_call(
        functools.partial(body, n_steps=n_steps, cps=cps, with_y=with_y),
        grid=(bsz, n_steps),
        in_specs=in_specs,
        out_specs=out_specs,
        out_shape=out_shape,
        scratch_shapes=scratch,
        compiler_params=pltpu.CompilerParams(
            dimension_semantics=("arbitrary", "arbitrary"), vmem_limit_bytes=VMEM_LIMIT),
        name=name,
    )(*args)


POOL_TILE_ROWS = 4
POOL_TILE = POOL_TILE_ROWS * GRID_W


def _pool_constants(window, n_rows):
    lo_off, hi_off = -(window // 2), window - window // 2
    col = np.arange(GRID_W)
    lo = np.clip(col + lo_off, 0, GRID_W)
    hi = np.clip(col + hi_off, 0, GRID_W)
    band = ((col[None, :] >= lo[:, None]) & (col[None, :] < hi[:, None])).astype(np.float32)
    band_tile = np.kron(np.eye(POOL_TILE_ROWS, dtype=np.float32), band)
    row = np.arange(n_rows)
    cnt_r = np.clip(row + hi_off, 0, n_rows) - np.clip(row + lo_off, 0, n_rows)
    inv = 1.0 / (cnt_r[:, None] * (hi - lo)[None, :]).astype(np.float64)
    inv = np.broadcast_to(inv.reshape(-1, 1), (n_rows * GRID_W, 128)).astype(np.float32)
    return jnp.asarray(band_tile, jnp.bfloat16), jnp.asarray(inv)


def _pool_kernel(u_ref, z_ref, band_ref, inv_ref, w_ref, scale_ref, o_ref, *, window, n_rows):
    def grid_row(r):
        return u_ref[r * GRID_W:(r + 1) * GRID_W]

    def bounds(r):
        return max(r - window // 2, 0), min(r + window - window // 2, n_rows)

    band = band_ref[...]
    rsum, tile_rows = None, []
    for r in range(n_rows):
        lo, hi = bounds(r)
        if r == 0 or window <= 2:
            rsum = grid_row(lo)
            for k in range(lo + 1, hi):
                rsum = rsum + grid_row(k)
        else:
            prev_lo, prev_hi = bounds(r - 1)
            if hi > prev_hi:
                rsum = rsum + grid_row(hi - 1)
            if lo > prev_lo:
                rsum = rsum - grid_row(prev_lo)
        tile_rows.append(rsum)
        if len(tile_rows) < POOL_TILE_ROWS:
            continue
        base = (r + 1 - POOL_TILE_ROWS) * GRID_W
        rows = slice(base, base + POOL_TILE)
        rs = jnp.concatenate(tile_rows, axis=0)
        tile_rows = []
        box = _dot(band, _bf(rs))
        inv = inv_ref[rows]
        mean = box * jnp.concatenate([inv, inv], axis=1)
        d = mean - u_ref[rows]
        y = _dot(_bf(d), w_ref[0]) * scale_ref[...]
        o_ref[rows] = _bf(y * z_ref[rows].astype(jnp.float32))


def _pool_group(u, gate, pool_w_bf, pool_scale, g, bsz, n_img_tok):
    window = POOL_WINDOWS[g]
    n_rows = n_img_tok // GRID_W
    band, inv = _pool_constants(window, n_rows)
    kern = functools.partial(_pool_kernel, window=window, n_rows=n_rows)
    img = pl.BlockSpec((n_img_tok, POOL_GROUP_W), lambda b: (b, 0))
    return pl.pallas_call(
        kern,
        grid=(bsz,),
        in_specs=[img, img,
                  pl.BlockSpec((POOL_TILE, POOL_TILE), lambda b: (0, 0)),
                  pl.BlockSpec((n_img_tok, 128), lambda b: (0, 0)),
                  pl.BlockSpec((1, POOL_GROUP_W, POOL_GROUP_W), lambda b: (g, 0, 0)),
                  pl.BlockSpec((1, POOL_GROUP_W), lambda b: (0, g))],
        out_specs=img,
        out_shape=jax.ShapeDtypeStruct((bsz * n_img_tok, POOL_GROUP_W), jnp.bfloat16),
        compiler_params=pltpu.CompilerParams(vmem_limit_bytes=VMEM_LIMIT),
        name=f"pool{window}",
    )(u, gate, band, inv, pool_w_bf, pool_scale)


def kernel(x, c, ctx, c_ctx, norm_w, w_ada, b_ada, w_in, conv_w, conv_b, a_log, dt_bias, d_skip,
           ssd_norm_w, pool_w, pool_scale, w_out, final_norm_w):
    bsz, seq, _ = x.shape
    ctx_len = ctx.shape[1]
    depth = norm_w.shape[0]
    assert depth == 1, "single-layer block: the context stream update is never consumed"
    assert seq % 512 == 0 and ctx_len % CHUNK == 0 and seq % GRID_W == 0

    mod_rows = -(-(bsz + 1) // SUBLANES) * SUBLANES
    cond = jnp.concatenate([c, c_ctx[None], jnp.zeros((mod_rows - bsz - 1, D_MODEL), c.dtype)])
    mod = _modulation(cond, w_ada[0], b_ada[0])
    shift = mod[:, :D_MODEL].reshape(mod_rows, 1, D_MODEL)
    scale = mod[:, D_MODEL:2 * D_MODEL].reshape(mod_rows, 1, D_MODEL)
    gate = mod[:, 2 * D_MODEL:].reshape(mod_rows, 1, D_MODEL)

    w_in_bf = _bf(w_in[0])
    w_dt_bf = jnp.pad(_bf(w_in[0, :, OFF_DT:]), ((0, 0), (0, DT_PAD - 2 * HEADS)))
    alog_col = a_log[0].reshape(2 * HEADS, 1)
    bias_col = dt_bias[0].reshape(2 * HEADS, 1)
    dskip_b = jnp.broadcast_to(jnp.repeat(d_skip[0], HEADDIM)[:, None], (W_SSD, CHUNK))
    conv_b2 = conv_b[0].reshape(1, CONV_DIM)
    zero_state = jnp.zeros((bsz, W_SSD, D_STATE), jnp.float32)

    ctx2d = ctx.reshape(bsz * ctx_len, D_MODEL)
    xs_t_c, b_c, dt_c, cum_c = _projection(
        ctx2d, norm_w[0], shift[bsz:bsz + 1], scale[bsz:bsz + 1], w_in_bf, w_dt_bf, conv_w[0],
        conv_b2, alog_col, bias_col, ctx_len, ctx_len, full=False)
    nc_ctx = ctx_len // CHUNK
    (h_fwd,) = _ssd_sweep(False, xs_t_c, b_c, dt_c, cum_c, alog_col, zero_state, bsz, nc_ctx)
    (h_bwd,) = _ssd_sweep(True, xs_t_c, b_c, dt_c, cum_c, alog_col, zero_state, bsz, nc_ctx)

    x2d = x.reshape(bsz * seq, D_MODEL)
    outs = _projection(x2d, norm_w[0], shift, scale, w_in_bf, w_dt_bf, conv_w[0], conv_b2,
                       alog_col, bias_col, seq, 512, full=True)
    u_pool, gate_pool = outs[:N_POOL_GROUPS], outs[N_POOL_GROUPS:2 * N_POOL_GROUPS]
    gate_ssd, xs_t, b_tok, c_t, dt, cum = outs[2 * N_POOL_GROUPS:]
    nc = seq // CHUNK
    y_part, _ = _ssd_sweep(False, xs_t, b_tok, dt, cum, alog_col, h_fwd, bsz, nc, c_t=c_t,
                           dskip_b=dskip_b)
    pool_w_bf = _bf(pool_w[0])
    y_pool = [_pool_group(u_pool[g], gate_pool[g], pool_w_bf, pool_scale, g, bsz, seq)
              for g in range(N_POOL_GROUPS)]
    (out,) = _ssd_sweep(True, xs_t, b_tok, dt, cum, alog_col, h_bwd, bsz, nc, c_t=c_t,
                        y_part=y_part,
                        out_args=(y_pool, gate_ssd, x2d, gate, ssd_norm_w[0], _bf(w_out[0]),
                                  final_norm_w))
    return out.reshape(bsz, seq, D_MODEL)
```

```python
import functools

import numpy as np
import jax
import jax.numpy as jnp
from jax.experimental import pallas as pl
from jax.experimental.pallas import tpu as pltpu

D_MODEL = 1024
GRID_W = 64
W_POOL = 1024
W_SSD = 1024
POOL_WINDOWS = (2, 4, 8, 16)
N_POOL_GROUPS = len(POOL_WINDOWS)
POOL_GROUP_W = 256
HEADDIM = 64
HEADS = 16
GROUPS = 4
HEADS_PER_GROUP = 4
D_STATE = 128
D_CONV = 4
CONV_LEFT = 2
CHUNK = 128
GN = GROUPS * D_STATE
CONV_DIM = W_SSD + 2 * GN
OFF_POOL_Z = W_POOL
OFF_SSD_Z = 2 * W_POOL
OFF_XBC = 2 * W_POOL + W_SSD
OFF_DT = OFF_XBC + CONV_DIM
DT_PAD = 128
EPS = 1e-6
SUBLANES = 8
LANES = 128
IL_GROUPS = CHUNK // SUBLANES
CONV_SEG = 256
SSD_CHUNKS_PER_STEP = 4
VMEM_LIMIT = 56 * 1024 * 1024


def _silu(v):
    h = 0.5 * v
    return h + h * jnp.tanh(h)


def _softplus(v):
    return jnp.maximum(v, 0.0) + jnp.log1p(jnp.exp(-jnp.abs(v)))


def _bf(v):
    return v.astype(jnp.bfloat16)


def _dot(a, b):
    return jnp.dot(a, b, preferred_element_type=jnp.float32)


def _mod_kernel(c_ref, w_ref, b_ref, o_ref):
    s = _silu(c_ref[...])
    o_ref[...] = jnp.dot(s, w_ref[...], preferred_element_type=jnp.float32,
                         precision=jax.lax.Precision.HIGHEST) + b_ref[...]


def _modulation(cond_rows, w_ada, b_ada):
    rows = cond_rows.shape[0]
    n_out = w_ada.shape[1]
    tn = 1024
    return pl.pallas_call(
        _mod_kernel,
        grid=(n_out // tn,),
        in_specs=[pl.BlockSpec((rows, D_MODEL), lambda j: (0, 0)),
                  pl.BlockSpec((D_MODEL, tn), lambda j: (0, j)),
                  pl.BlockSpec((1, tn), lambda j: (0, j))],
        out_specs=pl.BlockSpec((rows, tn), lambda j: (0, j)),
        out_shape=jax.ShapeDtypeStruct((rows, n_out), jnp.float32),
        compiler_params=pltpu.CompilerParams(vmem_limit_bytes=VMEM_LIMIT),
        name="mod",
    )(cond_rows, w_ada, b_ada.reshape(1, n_out))


def _lane_cumsum(v):
    lane = jax.lax.broadcasted_iota(jnp.int32, v.shape, 1)
    shift = 1
    while shift < CHUNK:
        v = v + jnp.where(lane >= shift, pltpu.roll(v, shift, 1), 0.0)
        shift *= 2
    return v


def _proj_kernel(x_ref, xp_ref, xn_ref, nw_ref, sh_ref, sc_ref, w_ref, wdt_ref, cw_ref, cb_ref,
                 alog_ref, bias_ref, *rest, tm, tiles_per_seq, n_tiles, full):
    if full:
        (dskip_ref, h0_ref, u0, u1, u2, u3, zp0, zp1, zp2, zp3, zs_ref, xs_t_ref, b_ref, c_t_ref,
         dt_ref, cum_ref, ypart_ref, pe_ref, xc_ref, mn_ref, kxs_ref, kb_ref, kc_ref, kdt_ref,
         kcum_ref, h_ref) = rest
        u_refs, zp_refs = (u0, u1, u2, u3), (zp0, zp1, zp2, zp3)
    else:
        xs_t_ref, b_ref, dt_ref, cum_ref, pe_ref, xc_ref, mn_ref = rest
    i = pl.program_id(0)
    pos = jnp.minimum(i, n_tiles - 1) % tiles_per_seq
    has_prev = pos > 0
    has_next = pos < tiles_per_seq - 1
    n_chunks = tm // CHUNK
    seg = CONV_SEG

    if full:
        @pl.when(i == 0)
        def _():
            for ref in (kxs_ref, kb_ref, kc_ref, kdt_ref, kcum_ref, h_ref):
                ref[...] = jnp.zeros(ref.shape, ref.dtype)

    def modulated(v):
        ms = jnp.mean(v * v, axis=-1, keepdims=True)
        y = v * jax.lax.rsqrt(ms + EPS) * nw_ref[...]
        return y * (1.0 + sc_ref[0]) + sh_ref[0]

    m_tok = modulated(x_ref[...])
    hm = _bf(m_tok)
    for t in range(D_MODEL // LANES):
        mn_ref[t] = m_tok[:, t * LANES:(t + 1) * LANES]

    rows = [jnp.concatenate([mn_ref[t, pl.ds(q * CHUNK + b, SUBLANES, stride=IL_GROUPS), :]
                             for t in range(D_MODEL // LANES)], axis=1)
            for q in range(n_chunks) for b in range(IL_GROUPS)]
    halo = [jnp.where(has_prev, modulated(xp_ref[...]), 0.0),
            jnp.where(has_next, modulated(xn_ref[...]), 0.0)]
    hm_il = _bf(jnp.concatenate(halo + rows, axis=0))
    sub = jax.lax.broadcasted_iota(jnp.int32, (SUBLANES, seg), 0)

    def conv_stage(j, slot):
        is_x = j < W_SSD
        is_b = W_SSD <= j < W_SSD + GN

        def matmul():
            pe_ref[slot] = _dot(hm_il, w_ref[:, OFF_XBC + j:OFF_XBC + j + seg])

        def group(q, b):
            lo = 2 * SUBLANES + q * CHUNK + b * SUBLANES
            return pe_ref[slot, lo:lo + SUBLANES]

        def shifted(q, b, delta):
            bb = b + delta
            if 0 <= bb < IL_GROUPS:
                return group(q, bb)
            if bb < 0:
                bb += IL_GROUPS
                if q == 0:
                    first = pe_ref[slot, bb - SUBLANES:bb - SUBLANES + 1]
                else:
                    row = 2 * SUBLANES + (q - 1) * CHUNK + bb * SUBLANES + SUBLANES - 1
                    first = pe_ref[slot, row:row + 1]
                return jnp.where(sub == 0, first, pltpu.roll(group(q, bb), 1, 0))
            bb -= IL_GROUPS
            if q == n_chunks - 1:
                last = pe_ref[slot, SUBLANES + bb:SUBLANES + bb + 1]
            else:
                nxt = 2 * SUBLANES + (q + 1) * CHUNK + bb * SUBLANES
                last = pe_ref[slot, nxt:nxt + 1]
            return jnp.where(sub == SUBLANES - 1, last, pltpu.roll(group(q, bb), SUBLANES - 1, 0))

        def epilogue():
            taps = [cw_ref[k:k + 1, j:j + seg] for k in range(D_CONV)]
            bias = cb_ref[:, j:j + seg]
            for q in range(n_chunks):
                for b in range(IL_GROUPS):
                    acc = bias
                    for k in range(D_CONV):
                        acc = acc + shifted(q, b, k - CONV_LEFT) * taps[k]
                    lo = q * CHUNK + b * SUBLANES
                    act = _silu(acc)
                    for t in range(seg // LANES):
                        xc_ref[slot, t, lo:lo + SUBLANES] = act[:, t * LANES:(t + 1) * LANES]
            for q in range(n_chunks):
                xc = jnp.concatenate(
                    [jnp.concatenate(
                        [xc_ref[slot, t, pl.ds(q * CHUNK + (m % 2) * (CHUNK // 2) + m // 2,
                                               SUBLANES, stride=SUBLANES), :]
                         for t in range(seg // LANES)], axis=1)
                     for m in range(IL_GROUPS)], axis=0)
                if is_b:
                    b_ref[q * CHUNK:(q + 1) * CHUNK, j - W_SSD:j - W_SSD + seg] = _bf(xc)
                else:
                    dst, off = (xs_t_ref, j) if is_x else (c_t_ref, j - W_SSD - GN)
                    dst[q, off:off + seg] = _bf(xc.T)

        return matmul, epilogue

    def plain_stage(cols, finish):
        box = []
        return (lambda: box.append(_dot(hm, w_ref[:, cols]))), (lambda: finish(box.pop()))

    def dt_stage():
        box = []

        def epilogue():
            p_dt = box.pop()
            a_col = -jnp.exp(alog_ref[...])
            for q in range(n_chunks):
                dt = _softplus(p_dt[q * CHUNK:(q + 1) * CHUNK].T[:2 * HEADS] + bias_ref[...])
                dt_ref[q] = dt
                cum_ref[q] = _lane_cumsum(dt * a_col)

        return (lambda: box.append(_dot(hm, wdt_ref[...]))), epilogue

    def store_to(ref, cols=None, act=None):
        def finish(v):
            v = v if act is None else _bf(act(v))
            if cols is None:
                ref[...] = v
            else:
                ref[:, cols] = v
        return finish

    def store_pair(refs, act=None):
        def finish(v):
            for n, ref in enumerate(refs):
                part = v[:, n * POOL_GROUP_W:(n + 1) * POOL_GROUP_W]
                ref[...] = part if act is None else _bf(act(part))
        return finish

    stages = []
    if full:
        for g in range(0, N_POOL_GROUPS, 2):
            stages.append(plain_stage(slice(g * POOL_GROUP_W, (g + 2) * POOL_GROUP_W),
                                      store_pair(u_refs[g:g + 2])))
    stages.append(dt_stage())
    n_conv = 0
    for j in range(0, CONV_DIM, seg):
        if full or j < W_SSD + GN:
            stages.append(conv_stage(j, n_conv % 2))
            n_conv += 1
    if full:
        for g in range(0, N_POOL_GROUPS, 2):
            zcols = slice(OFF_POOL_Z + g * POOL_GROUP_W, OFF_POOL_Z + (g + 2) * POOL_GROUP_W)
            stages.append(plain_stage(zcols, store_pair(zp_refs[g:g + 2], act=_silu)))
        for j in range(0, W_SSD, seg):
            stages.append(plain_stage(slice(OFF_SSD_Z + j, OFF_SSD_Z + j + seg),
                                      store_to(zs_ref, cols=slice(j, j + seg), act=_silu)))

    sweep = []
    if full:
        swept = jnp.maximum(i - 1, 0)
        h_ref[...] = jnp.where(swept % tiles_per_seq == 0, h0_ref[0], h_ref[...])
        src = jax.lax.broadcasted_iota(jnp.int32, (CHUNK, CHUNK), 0)
        dst = jax.lax.broadcasted_iota(jnp.int32, (CHUNK, CHUNK), 1)
        a_b = -jnp.exp(alog_ref[HEADS:])
        pairs = []
        for q in range(n_chunks):
            pairs += _fwd_chunk_slices(q, kxs_ref, kb_ref, kc_ref, kdt_ref, kcum_ref, dskip_ref,
                                       ypart_ref, h_ref, a_b, src <= dst, src == dst)
        sweep = [pairs[0][0]]
        for p in range(len(pairs)):
            nxt = pairs[p + 1][0] if p + 1 < len(pairs) else (lambda: None)
            sweep.append(functools.partial(lambda a, w: (a(), w()), pairs[p][1], nxt))

    stages[0][0]()
    done = 0
    for k, (_, epilogue) in enumerate(stages):
        if k + 1 < len(stages):
            stages[k + 1][0]()
        epilogue()
        upto = -(-len(sweep) * (k + 1) // len(stages))
        for piece in sweep[done:upto]:
            piece()
        done = upto

    if full:
        for kept, ref in ((kxs_ref, xs_t_ref), (kb_ref, b_ref), (kc_ref, c_t_ref),
                          (kdt_ref, dt_ref), (kcum_ref, cum_ref)):
            kept[...] = ref[...]


def _projection(x2d, norm_w, shift, scale, w_bf, wdt_bf, conv_w, conv_b, alog_col, bias_col,
                seq_len, tm, full, dskip_b=None, h0=None):
    n_tok = x2d.shape[0]
    tiles_per_seq = seq_len // tm
    n_tiles = n_tok // tm
    n_mod = shift.shape[0]
    nct = n_tok // CHUNK
    per = tm // SUBLANES
    last_halo = n_tok // SUBLANES - 1
    kern = functools.partial(_proj_kernel, tm=tm, tiles_per_seq=tiles_per_seq, n_tiles=n_tiles,
                             full=full)
    tile = lambda i: jnp.minimum(i, n_tiles - 1)
    mod_map = (lambda i: (tile(i) // tiles_per_seq, 0, 0)) if n_mod > 1 else (lambda i: (0, 0, 0))
    mod_spec = pl.BlockSpec((1, 1, D_MODEL), mod_map)
    const = lambda i: (0, 0)
    tok = lambda i: (tile(i), 0)
    chunk3 = lambda i: (tile(i), 0, 0)
    q = tm // CHUNK
    xs_t = (jax.ShapeDtypeStruct((nct, W_SSD, CHUNK), jnp.bfloat16),
            pl.BlockSpec((q, W_SSD, CHUNK), chunk3))
    b_tok = (jax.ShapeDtypeStruct((n_tok, GN), jnp.bfloat16), pl.BlockSpec((tm, GN), tok))
    c_t = (jax.ShapeDtypeStruct((nct, GN, CHUNK), jnp.bfloat16), pl.BlockSpec((q, GN, CHUNK), chunk3))
    dt = (jax.ShapeDtypeStruct((nct, 2 * HEADS, CHUNK), jnp.float32),
          pl.BlockSpec((q, 2 * HEADS, CHUNK), chunk3))
    if full:
        u = (jax.ShapeDtypeStruct((n_tok, POOL_GROUP_W), jnp.float32),
             pl.BlockSpec((tm, POOL_GROUP_W), tok))
        zp = (jax.ShapeDtypeStruct((n_tok, POOL_GROUP_W), jnp.bfloat16),
              pl.BlockSpec((tm, POOL_GROUP_W), tok))
        zs = (jax.ShapeDtypeStruct((n_tok, W_SSD), jnp.bfloat16), pl.BlockSpec((tm, W_SSD), tok))
        swept = lambda i: jnp.maximum(i - 1, 0)
        y_part = (jax.ShapeDtypeStruct((nct, W_SSD, CHUNK), jnp.bfloat16),
                  pl.BlockSpec((q, W_SSD, CHUNK), lambda i: (swept(i), 0, 0)))
        outs = [u] * N_POOL_GROUPS + [zp] * N_POOL_GROUPS + [zs, xs_t, b_tok, c_t, dt, dt, y_part]
        extra_in = [pl.BlockSpec((W_SSD, CHUNK), const),
                    pl.BlockSpec((1, W_SSD, D_STATE), lambda i: (swept(i) // tiles_per_seq, 0, 0))]
        extra_args = [dskip_b, h0]
        extra_scratch = [pltpu.VMEM((q, W_SSD, CHUNK), jnp.bfloat16),
                         pltpu.VMEM((tm, GN), jnp.bfloat16),
                         pltpu.VMEM((q, GN, CHUNK), jnp.bfloat16),
                         pltpu.VMEM((q, 2 * HEADS, CHUNK), jnp.float32),
                         pltpu.VMEM((q, 2 * HEADS, CHUNK), jnp.float32),
                         pltpu.VMEM((W_SSD, D_STATE), jnp.float32)]
    else:
        outs = [xs_t, b_tok, dt, dt]
        extra_in, extra_args, extra_scratch = [], [], []
    return pl.pallas_call(
        kern,
        grid=(n_tiles + 1 if full else n_tiles,),
        in_specs=[pl.BlockSpec((tm, D_MODEL), tok),
                  pl.BlockSpec((SUBLANES, D_MODEL), lambda i: (jnp.maximum(tile(i) * per - 1, 0), 0)),
                  pl.BlockSpec((SUBLANES, D_MODEL),
                               lambda i: (jnp.minimum((tile(i) + 1) * per, last_halo), 0)),
                  pl.BlockSpec((1, D_MODEL), const),
                  mod_spec, mod_spec,
                  pl.BlockSpec(w_bf.shape, const),
                  pl.BlockSpec((D_MODEL, DT_PAD), const),
                  pl.BlockSpec((D_CONV, CONV_DIM), const),
                  pl.BlockSpec((1, CONV_DIM), const),
                  pl.BlockSpec((2 * HEADS, 1), const),
                  pl.BlockSpec((2 * HEADS, 1), const)] + extra_in,
        out_specs=[o[1] for o in outs],
        out_shape=[o[0] for o in outs],
        scratch_shapes=[pltpu.VMEM((2, tm + 2 * SUBLANES, CONV_SEG), jnp.float32),
                        pltpu.VMEM((2, CONV_SEG // LANES, tm, LANES), jnp.float32),
                        pltpu.VMEM((D_MODEL // LANES, tm, LANES), jnp.float32)] + extra_scratch,
        compiler_params=pltpu.CompilerParams(
            dimension_semantics=("arbitrary",), vmem_limit_bytes=VMEM_LIMIT),
        name="proj" if full else "proj_ctx",
    )(x2d, x2d, x2d, norm_w.reshape(1, D_MODEL), shift, scale, w_bf, wdt_bf, conv_w, conv_b,
      alog_col, bias_col, *extra_args)


def _sweep_chunks(chunk_body, chunks_per_step, reverse):
    def body(i, carry):
        chunk_body(chunks_per_step - 1 - i if reverse else i)
        return carry
    jax.lax.fori_loop(0, chunks_per_step, body, 0, unroll=True)


def _tok_rows(q):
    if isinstance(q, int):
        return pl.ds(q * CHUNK, CHUNK)
    return pl.ds(pl.multiple_of(q * CHUNK, CHUNK), CHUNK)


def _state_update_group(h_ref, xs_t_ref, b_ref, q, g, scale_in, chunk_decay):
    bg = b_ref[_tok_rows(q), g * D_STATE:(g + 1) * D_STATE]
    xd = []
    for r in range(HEADS_PER_GROUP):
        h = g * HEADS_PER_GROUP + r
        x_h = xs_t_ref[q, h * HEADDIM:(h + 1) * HEADDIM].astype(jnp.float32)
        xd.append(_bf(x_h * scale_in[h:h + 1]))
    s_new = _dot(jnp.concatenate(xd, axis=0), bg)
    for r in range(HEADS_PER_GROUP):
        h = g * HEADS_PER_GROUP + r
        hr = slice(h * HEADDIM, (h + 1) * HEADDIM)
        h_ref[hr] = h_ref[hr] * chunk_decay[h:h + 1] + s_new[r * HEADDIM:(r + 1) * HEADDIM]


def _state_update(h_ref, xs_t_ref, b_ref, q, scale_in, chunk_decay):
    for g in range(GROUPS):
        _state_update_group(h_ref, xs_t_ref, b_ref, q, g, scale_in, chunk_decay)


def _fwd_chunk_slices(q, xs_t_ref, b_ref, c_t_ref, dt_ref, cum_ref, dskip_ref, y_ref, h_ref, a_b,
                      causal, is_diag):
    ctx = {}

    def setup():
        dt_f, cum_f = dt_ref[q, :HEADS], cum_ref[q, :HEADS]
        tot_f = cum_f[:, CHUNK - 1:CHUNK]
        dt_b, cum_b = dt_ref[q, HEADS:], cum_ref[q, HEADS:]
        cumx_b = cum_b - dt_b * a_b
        ctx["scale_in"] = dt_f * jnp.exp(tot_f - cum_f)
        ctx["chunk_decay"] = jnp.exp(tot_f)
        ctx["col_terms"] = jnp.concatenate(
            [jnp.log(dt_f) - cum_f, jnp.log(dt_b) + cumx_b,
             jnp.zeros((CHUNK - 2 * HEADS, CHUNK), jnp.float32)], axis=0).T
        ctx["row_f"], ctx["row_b"] = cum_f, -cumx_b
        ctx["decay_out_f"] = jnp.exp(cum_f)
        ctx["dt_b"] = dt_b

    def weights(g):
        if g == 0:
            setup()
        col_terms, row_f, row_b = ctx["col_terms"], ctx["row_f"], ctx["row_b"]
        bg = b_ref[_tok_rows(q), g * D_STATE:(g + 1) * D_STATE]
        cg_t = c_t_ref[q, g * D_STATE:(g + 1) * D_STATE]
        rows = slice(g * HEADS_PER_GROUP * HEADDIM, (g + 1) * HEADS_PER_GROUP * HEADDIM)
        g_t = _dot(bg, cg_t)
        ctx["g_diag", g] = jnp.sum(jnp.where(is_diag, g_t, 0.0), axis=0, keepdims=True)
        ctx["y_off", g] = _dot(_bf(h_ref[rows]), cg_t)
        for r in range(HEADS_PER_GROUP):
            h = g * HEADS_PER_GROUP + r
            col_f = jnp.broadcast_to(col_terms[:, h:h + 1], (CHUNK, CHUNK))
            col_b = jnp.broadcast_to(col_terms[:, HEADS + h:HEADS + h + 1], (CHUNK, CHUNK))
            expo = jnp.where(causal, col_f + row_f[h:h + 1], col_b + row_b[h:h + 1])
            ctx["w_t", h] = _bf(g_t * jnp.exp(expo))

    def apply(g):
        y_off, g_diag = ctx.pop(("y_off", g)), ctx.pop(("g_diag", g))
        for r in range(HEADS_PER_GROUP):
            h = g * HEADS_PER_GROUP + r
            hr = slice(h * HEADDIM, (h + 1) * HEADDIM)
            x_bf = xs_t_ref[q, hr]
            y_h = _dot(x_bf, ctx.pop(("w_t", h)))
            y_h = y_h + y_off[r * HEADDIM:(r + 1) * HEADDIM] * ctx["decay_out_f"][h:h + 1]
            skip = dskip_ref[hr] + g_diag * ctx["dt_b"][h:h + 1]
            y_ref[q, hr] = _bf(y_h + skip * x_bf.astype(jnp.float32))
        _state_update_group(h_ref, xs_t_ref, b_ref, q, g, ctx["scale_in"], ctx["chunk_decay"])

    return [(functools.partial(weights, g), functools.partial(apply, g)) for g in range(GROUPS)]


def _ssd_fwd_kernel(xs_t_ref, b_ref, dt_ref, cum_ref, alog_ref, h0_ref, *rest, n_steps, cps, with_y):
    if with_y:
        c_t_ref, dskip_ref, y_ref, hout_ref, h_ref = rest
    else:
        hout_ref, h_ref = rest
    step = pl.program_id(1)

    @pl.when(step == 0)
    def _():
        h_ref[...] = h0_ref[0]

    if with_y:
        a_b = -jnp.exp(alog_ref[HEADS:])
        src = jax.lax.broadcasted_iota(jnp.int32, (CHUNK, CHUNK), 0)
        dst = jax.lax.broadcasted_iota(jnp.int32, (CHUNK, CHUNK), 1)
        causal = src <= dst
        is_diag = src == dst

    def chunk(q):
        if with_y:
            for weights, apply in _fwd_chunk_slices(q, xs_t_ref, b_ref, c_t_ref, dt_ref, cum_ref,
                                                    dskip_ref, y_ref, h_ref, a_b, causal, is_diag):
                weights()
                apply()
        else:
            dt_f, cum_f = dt_ref[q, :HEADS], cum_ref[q, :HEADS]
            tot_f = cum_f[:, CHUNK - 1:CHUNK]
            _state_update(h_ref, xs_t_ref, b_ref, q, dt_f * jnp.exp(tot_f - cum_f), jnp.exp(tot_f))

    _sweep_chunks(chunk, cps, reverse=False)

    @pl.when(step == n_steps - 1)
    def _():
        hout_ref[0] = h_ref[...]


def _ssd_bwd_kernel(xs_t_ref, b_ref, dt_ref, cum_ref, alog_ref, h0_ref, *rest, n_steps, cps, with_y):
    if with_y:
        (c_t_ref, ypart_ref, yp0_ref, yp1_ref, yp2_ref, yp3_ref, zs_ref, x_ref, gate_ref, snw_ref,
         wout_ref, fnw_ref, o_ref, h_ref, y_ref) = rest
    else:
        hout_ref, h_ref = rest
    step = pl.program_id(1)

    @pl.when(step == 0)
    def _():
        h_ref[...] = h0_ref[0]

    a_b = -jnp.exp(alog_ref[HEADS:])
    def chunk(q):
        dt_b, cum_b = dt_ref[q, HEADS:], cum_ref[q, HEADS:]
        tot_b = cum_b[:, CHUNK - 1:CHUNK]
        cumx_b = cum_b - dt_b * a_b
        scale_in = dt_b * jnp.exp(cumx_b)
        chunk_decay = jnp.exp(tot_b)

        if with_y:
            decay_out = jnp.exp(tot_b - cumx_b)
            y_parts = []
            for g in range(GROUPS):
                cg_t = c_t_ref[q, g * D_STATE:(g + 1) * D_STATE]
                rows = slice(g * HEADS_PER_GROUP * HEADDIM, (g + 1) * HEADS_PER_GROUP * HEADDIM)
                y_off = _dot(_bf(h_ref[rows]), cg_t)
                for r in range(HEADS_PER_GROUP):
                    h = g * HEADS_PER_GROUP + r
                    hr = slice(h * HEADDIM, (h + 1) * HEADDIM)
                    y_parts.append(ypart_ref[q, hr].astype(jnp.float32)
                                   + y_off[r * HEADDIM:(r + 1) * HEADDIM] * decay_out[h:h + 1])
            y_ref[_tok_rows(q), :] = jnp.concatenate(y_parts, axis=0).T

        _state_update(h_ref, xs_t_ref, b_ref, q, scale_in, chunk_decay)

    if with_y:
        yp_refs = [yp0_ref, yp1_ref, yp2_ref, yp3_ref]
        acc = jnp.zeros((cps * CHUNK, D_MODEL), jnp.float32)
        for i in range(max(cps, N_POOL_GROUPS)):
            if i < cps:
                chunk(cps - 1 - i)
            if i < N_POOL_GROUPS:
                acc = acc + _dot(yp_refs[i][...],
                                 wout_ref[i * POOL_GROUP_W:(i + 1) * POOL_GROUP_W])
    else:
        _sweep_chunks(chunk, cps, reverse=True)

    if with_y:
        gw = W_SSD // GROUPS
        for g in range(GROUPS):
            cols = slice(g * gw, (g + 1) * gw)
            gated = y_ref[:, cols] * zs_ref[:, cols].astype(jnp.float32)
            ms = jnp.mean(gated * gated, axis=-1, keepdims=True)
            yn = gated * jax.lax.rsqrt(ms + EPS) * snw_ref[:, cols]
            acc = acc + _dot(_bf(yn), wout_ref[W_POOL + g * gw:W_POOL + (g + 1) * gw])
        hres = x_ref[...] + gate_ref[0] * acc
        ms = jnp.mean(hres * hres, axis=-1, keepdims=True)
        o_ref[...] = hres * jax.lax.rsqrt(ms + EPS) * fnw_ref[...]
    else:
        @pl.when(step == n_steps - 1)
        def _():
            hout_ref[0] = h_ref[...]


def _ssd_sweep(reverse, xs_t, b_tok, dt, cum, alog_col, h0, bsz, n_chunks, c_t=None, dskip_b=None,
               y_part=None, out_args=None):
    with_y = c_t is not None
    n_tok = b_tok.shape[0]
    cps = min(n_chunks, SSD_CHUNKS_PER_STEP)
    n_steps = n_chunks // cps

    def block_of(b, s):
        return b * n_steps + (n_steps - 1 - s if reverse else s)

    tok = lambda b, s: (block_of(b, s), 0)
    chunk3 = lambda b, s: (block_of(b, s), 0, 0)
    const2 = lambda b, s: (0, 0)
    state3 = lambda b, s: (b, 0, 0)
    h_spec = pl.BlockSpec((1, W_SSD, D_STATE), state3)
    h_shape = jax.ShapeDtypeStruct((bsz, W_SSD, D_STATE), jnp.float32)
    head_spec = pl.BlockSpec((cps, 2 * HEADS, CHUNK), chunk3)
    in_specs = [pl.BlockSpec((cps, W_SSD, CHUNK), chunk3),
                pl.BlockSpec((cps * CHUNK, GN), tok),
                head_spec, head_spec,
                pl.BlockSpec((2 * HEADS, 1), const2),
                h_spec]
    args = [xs_t, b_tok, dt, cum, alog_col, h0]
    y_t_spec = pl.BlockSpec((cps, W_SSD, CHUNK), chunk3)
    scratch = [pltpu.VMEM((W_SSD, D_STATE), jnp.float32)]
    if with_y:
        in_specs.append(pl.BlockSpec((cps, GN, CHUNK), chunk3))
        args.append(c_t)
        if reverse:
            y_pool, gate_ssd, x2d, gate, ssd_norm_w, w_out_bf, final_norm_w = out_args
            tm = cps * CHUNK
            in_specs += [y_t_spec] + [pl.BlockSpec((tm, POOL_GROUP_W), tok)] * N_POOL_GROUPS + [
                pl.BlockSpec((tm, W_SSD), tok),
                pl.BlockSpec((tm, D_MODEL), tok),
                pl.BlockSpec((1, 1, D_MODEL), lambda b, s: (b, 0, 0)),
                pl.BlockSpec((1, W_SSD), const2),
                pl.BlockSpec((W_POOL + W_SSD, D_MODEL), const2),
                pl.BlockSpec((1, D_MODEL), const2)]
            args += [y_part, *y_pool, gate_ssd, x2d, gate, ssd_norm_w.reshape(1, W_SSD), w_out_bf,
                     final_norm_w.reshape(1, D_MODEL)]
            out_shape = [jax.ShapeDtypeStruct((n_tok, D_MODEL), jnp.float32)]
            out_specs = [pl.BlockSpec((tm, D_MODEL), tok)]
            scratch.append(pltpu.VMEM((tm, W_SSD), jnp.float32))
        else:
            in_specs.append(pl.BlockSpec((W_SSD, CHUNK), const2))
            args.append(dskip_b)
            out_shape = [jax.ShapeDtypeStruct(xs_t.shape, jnp.bfloat16), h_shape]
            out_specs = [y_t_spec, h_spec]
    else:
        out_shape = [h_shape]
        out_specs = [h_spec]
    body = _ssd_bwd_kernel if reverse else _ssd_fwd_kernel
    name = ("ssd_bwd" if reverse else "ssd_fwd") + ("" if with_y else "_state")
    return pl.pallas_call(
        functools.partial(body, n_steps=n_steps, cps=cps, with_y=with_y),
        grid=(bsz, n_steps),
        in_specs=in_specs,
        out_specs=out_specs,
        out_shape=out_shape,
        scratch_shapes=scratch,
        compiler_params=pltpu.CompilerParams(
            dimension_semantics=("arbitrary", "arbitrary"), vmem_limit_bytes=VMEM_LIMIT),
        name=name,
    )(*args)


POOL_TILE_ROWS = 4
POOL_TILE = POOL_TILE_ROWS * GRID_W


def _pool_constants(window, n_rows):
    lo_off, hi_off = -(window // 2), window - window // 2
    col = np.arange(GRID_W)
    lo = np.clip(col + lo_off, 0, GRID_W)
    hi = np.clip(col + hi_off, 0, GRID_W)
    band = ((col[None, :] >= lo[:, None]) & (col[None, :] < hi[:, None])).astype(np.float32)
    band_tile = np.kron(np.eye(POOL_TILE_ROWS, dtype=np.float32), band)
    row = np.arange(n_rows)
    cnt_r = np.clip(row + hi_off, 0, n_rows) - np.clip(row + lo_off, 0, n_rows)
    inv = 1.0 / (cnt_r[:, None] * (hi - lo)[None, :]).astype(np.float64)
    inv = np.broadcast_to(inv.reshape(-1, 1), (n_rows * GRID_W, 128)).astype(np.float32)
    return jnp.asarray(band_tile, jnp.bfloat16), jnp.asarray(inv)


def _pool_kernel(u_ref, z_ref, band_ref, inv_ref, w_ref, scale_ref, o_ref, *, window, n_rows):
    def grid_row(r):
        return u_ref[r * GRID_W:(r + 1) * GRID_W]

    def bounds(r):
        return max(r - window // 2, 0), min(r + window - window // 2, n_rows)

    band = band_ref[...]
    rsum, tile_rows = None, []
    for r in range(n_rows):
        lo, hi = bounds(r)
        if r == 0 or window <= 2:
            rsum = grid_row(lo)
            for k in range(lo + 1, hi):
                rsum = rsum + grid_row(k)
        else:
            prev_lo, prev_hi = bounds(r - 1)
            if hi > prev_hi:
                rsum = rsum + grid_row(hi - 1)
            if lo > prev_lo:
                rsum = rsum - grid_row(prev_lo)
        tile_rows.append(rsum)
        if len(tile_rows) < POOL_TILE_ROWS:
            continue
        base = (r + 1 - POOL_TILE_ROWS) * GRID_W
        rows = slice(base, base + POOL_TILE)
        rs = jnp.concatenate(tile_rows, axis=0)
        tile_rows = []
        box = _dot(band, _bf(rs))
        inv = inv_ref[rows]
        mean = box * jnp.concatenate([inv, inv], axis=1)
        d = mean - u_ref[rows]
        y = _dot(_bf(d), w_ref[0]) * scale_ref[...]
        o_ref[rows] = _bf(y * z_ref[rows].astype(jnp.float32))


def _pool_group(u, gate, pool_w_bf, pool_scale, g, bsz, n_img_tok):
    window = POOL_WINDOWS[g]
    n_rows = n_img_tok // GRID_W
    band, inv = _pool_constants(window, n_rows)
    kern = functools.partial(_pool_kernel, window=window, n_rows=n_rows)
    img = pl.BlockSpec((n_img_tok, POOL_GROUP_W), lambda b: (b, 0))
    return pl.pallas_call(
        kern,
        grid=(bsz,),
        in_specs=[img, img,
                  pl.BlockSpec((POOL_TILE, POOL_TILE), lambda b: (0, 0)),
                  pl.BlockSpec((n_img_tok, 128), lambda b: (0, 0)),
                  pl.BlockSpec((1, POOL_GROUP_W, POOL_GROUP_W), lambda b: (g, 0, 0)),
                  pl.BlockSpec((1, POOL_GROUP_W), lambda b: (0, g))],
        out_specs=img,
        out_shape=jax.ShapeDtypeStruct((bsz * n_img_tok, POOL_GROUP_W), jnp.bfloat16),
        compiler_params=pltpu.CompilerParams(vmem_limit_bytes=VMEM_LIMIT),
        name=f"pool{window}",
    )(u, gate, band, inv, pool_w_bf, pool_scale)


def kernel(x, c, ctx, c_ctx, norm_w, w_ada, b_ada, w_in, conv_w, conv_b, a_log, dt_bias, d_skip,
           ssd_norm_w, pool_w, pool_scale, w_out, final_norm_w):
    bsz, seq, _ = x.shape
    ctx_len = ctx.shape[1]
    depth = norm_w.shape[0]
    assert depth == 1, "single-layer block: the context stream update is never consumed"
    assert seq % 512 == 0 and ctx_len % CHUNK == 0 and seq % GRID_W == 0

    mod_rows = -(-(bsz + 1) // SUBLANES) * SUBLANES
    cond = jnp.concatenate([c, c_ctx[None], jnp.zeros((mod_rows - bsz - 1, D_MODEL), c.dtype)])
    mod = _modulation(cond, w_ada[0], b_ada[0])
    shift = mod[:, :D_MODEL].reshape(mod_rows, 1, D_MODEL)
    scale = mod[:, D_MODEL:2 * D_MODEL].reshape(mod_rows, 1, D_MODEL)
    gate = mod[:, 2 * D_MODEL:].reshape(mod_rows, 1, D_MODEL)

    w_in_bf = _bf(w_in[0])
    w_dt_bf = jnp.pad(_bf(w_in[0, :, OFF_DT:]), ((0, 0), (0, DT_PAD - 2 * HEADS)))
    alog_col = a_log[0].reshape(2 * HEADS, 1)
    bias_col = dt_bias[0].reshape(2 * HEADS, 1)
    dskip_b = jnp.broadcast_to(jnp.repeat(d_skip[0], HEADDIM)[:, None], (W_SSD, CHUNK))
    conv_b2 = conv_b[0].reshape(1, CONV_DIM)
    zero_state = jnp.zeros((bsz, W_SSD, D_STATE), jnp.float32)

    ctx2d = ctx.reshape(bsz * ctx_len, D_MODEL)
    xs_t_c, b_c, dt_c, cum_c = _projection(
        ctx2d, norm_w[0], shift[bsz:bsz + 1], scale[bsz:bsz + 1], w_in_bf, w_dt_bf, conv_w[0],
        conv_b2, alog_col, bias_col, ctx_len, ctx_len, full=False)
    nc_ctx = ctx_len // CHUNK
    (h_fwd,) = _ssd_sweep(False, xs_t_c, b_c, dt_c, cum_c, alog_col, zero_state, bsz, nc_ctx)
    (h_bwd,) = _ssd_sweep(True, xs_t_c, b_c, dt_c, cum_c, alog_col, zero_state, bsz, nc_ctx)

    x2d = x.reshape(bsz * seq, D_MODEL)
    outs = _projection(x2d, norm_w[0], shift, scale, w_in_bf, w_dt_bf, conv_w[0], conv_b2,
                       alog_col, bias_col, seq, 512, full=True, dskip_b=dskip_b, h0=h_fwd)
    u_pool, gate_pool = outs[:N_POOL_GROUPS], outs[N_POOL_GROUPS:2 * N_POOL_GROUPS]
    gate_ssd, xs_t, b_tok, c_t, dt, cum, y_part = outs[2 * N_POOL_GROUPS:]
    nc = seq // CHUNK
    pool_w_bf = _bf(pool_w[0])
    y_pool = [_pool_group(u_pool[g], gate_pool[g], pool_w_bf, pool_scale, g, bsz, seq)
              for g in range(N_POOL_GROUPS)]
    (out,) = _ssd_sweep(True, xs_t, b_tok, dt, cum, alog_col, h_bwd, bsz, nc, c_t=c_t,
                        y_part=y_part,
                        out_args=(y_pool, gate_ssd, x2d, gate, ssd_norm_w[0], _bf(w_out[0]),
                                  final_norm_w))
    return out.reshape(bsz, seq, D_MODEL)
```

```python
import functools

import numpy as np
import jax
import jax.numpy as jnp
from jax.experimental import pallas as pl
from jax.experimental.pallas import tpu as pltpu

D_MODEL = 1024
GRID_W = 64
W_POOL = 1024
W_SSD = 1024
POOL_WINDOWS = (2, 4, 8, 16)
N_POOL_GROUPS = len(POOL_WINDOWS)
POOL_GROUP_W = 256
HEADDIM = 64
HEADS = 16
GROUPS = 4
HEADS_PER_GROUP = 4
D_STATE = 128
D_CONV = 4
CONV_LEFT = 2
CHUNK = 128
GN = GROUPS * D_STATE
CONV_DIM = W_SSD + 2 * GN
OFF_POOL_Z = W_POOL
OFF_SSD_Z = 2 * W_POOL
OFF_XBC = 2 * W_POOL + W_SSD
OFF_DT = OFF_XBC + CONV_DIM
DT_PAD = 128
EPS = 1e-6
SUBLANES = 8
LANES = 128
IL_GROUPS = CHUNK // SUBLANES
CONV_SEG = 256
SSD_CHUNKS_PER_STEP = 4
VMEM_LIMIT = 56 * 1024 * 1024


def _silu(v):
    h = 0.5 * v
    return h + h * jnp.tanh(h)


def _softplus(v):
    return jnp.maximum(v, 0.0) + jnp.log1p(jnp.exp(-jnp.abs(v)))


def _bf(v):
    return v.astype(jnp.bfloat16)


def _dot(a, b):
    return jnp.dot(a, b, preferred_element_type=jnp.float32)


def _mod_kernel(c_ref, w_ref, b_ref, o_ref):
    s = _silu(c_ref[...])
    o_ref[...] = jnp.dot(s, w_ref[...], preferred_element_type=jnp.float32,
                         precision=jax.lax.Precision.HIGHEST) + b_ref[...]


def _modulation(cond_rows, w_ada, b_ada):
    rows = cond_rows.shape[0]
    n_out = w_ada.shape[1]
    tn = 1024
    return pl.pallas_call(
        _mod_kernel,
        grid=(n_out // tn,),
        in_specs=[pl.BlockSpec((rows, D_MODEL), lambda j: (0, 0)),
                  pl.BlockSpec((D_MODEL, tn), lambda j: (0, j)),
                  pl.BlockSpec((1, tn), lambda j: (0, j))],
        out_specs=pl.BlockSpec((rows, tn), lambda j: (0, j)),
        out_shape=jax.ShapeDtypeStruct((rows, n_out), jnp.float32),
        compiler_params=pltpu.CompilerParams(vmem_limit_bytes=VMEM_LIMIT),
        name="mod",
    )(cond_rows, w_ada, b_ada.reshape(1, n_out))


def _lane_cumsum(v):
    lane = jax.lax.broadcasted_iota(jnp.int32, v.shape, 1)
    shift = 1
    while shift < CHUNK:
        v = v + jnp.where(lane >= shift, pltpu.roll(v, shift, 1), 0.0)
        shift *= 2
    return v


def _proj_kernel(x_ref, xp_ref, xn_ref, nw_ref, sh_ref, sc_ref, w_ref, wdt_ref, cw_ref, cb_ref,
                 alog_ref, bias_ref, *rest, tm, tiles_per_seq, n_tiles, full):
    if full:
        (dskip_ref, h0_ref, u0, u1, u2, u3, zp0, zp1, zp2, zp3, zs_ref, xs_t_ref, b_ref, c_t_ref,
         dt_ref, cum_ref, ypart_ref, pe_ref, xc_ref, mn_ref, kxs_ref, kb_ref, kc_ref, kdt_ref,
         kcum_ref, h_ref) = rest
        u_refs, zp_refs = (u0, u1, u2, u3), (zp0, zp1, zp2, zp3)
    else:
        hf_ref, hb_ref, pe_ref, xc_ref, mn_ref, xs_t_ref, b_ref, dt_ref, cum_ref, h_ref = rest
    i = pl.program_id(0)
    pos = jnp.minimum(i, n_tiles - 1) % tiles_per_seq
    has_prev = pos > 0
    has_next = pos < tiles_per_seq - 1
    n_chunks = tm // CHUNK
    seg = CONV_SEG

    if full:
        @pl.when(i == 0)
        def _():
            for ref in (kxs_ref, kb_ref, kc_ref, kdt_ref, kcum_ref, h_ref):
                ref[...] = jnp.zeros(ref.shape, ref.dtype)

    def modulated(v):
        ms = jnp.mean(v * v, axis=-1, keepdims=True)
        y = v * jax.lax.rsqrt(ms + EPS) * nw_ref[...]
        return y * (1.0 + sc_ref[0]) + sh_ref[0]

    m_tok = modulated(x_ref[...])
    hm = _bf(m_tok)
    for t in range(D_MODEL // LANES):
        mn_ref[t] = m_tok[:, t * LANES:(t + 1) * LANES]

    rows = [jnp.concatenate([mn_ref[t, pl.ds(q * CHUNK + b, SUBLANES, stride=IL_GROUPS), :]
                             for t in range(D_MODEL // LANES)], axis=1)
            for q in range(n_chunks) for b in range(IL_GROUPS)]
    halo = [jnp.where(has_prev, modulated(xp_ref[...]), 0.0),
            jnp.where(has_next, modulated(xn_ref[...]), 0.0)]
    hm_il = _bf(jnp.concatenate(halo + rows, axis=0))
    sub = jax.lax.broadcasted_iota(jnp.int32, (SUBLANES, seg), 0)

    def conv_stage(j, slot):
        is_x = j < W_SSD
        is_b = W_SSD <= j < W_SSD + GN

        def matmul():
            pe_ref[slot] = _dot(hm_il, w_ref[:, OFF_XBC + j:OFF_XBC + j + seg])

        def group(q, b):
            lo = 2 * SUBLANES + q * CHUNK + b * SUBLANES
            return pe_ref[slot, lo:lo + SUBLANES]

        def shifted(q, b, delta):
            bb = b + delta
            if 0 <= bb < IL_GROUPS:
                return group(q, bb)
            if bb < 0:
                bb += IL_GROUPS
                if q == 0:
                    first = pe_ref[slot, bb - SUBLANES:bb - SUBLANES + 1]
                else:
                    row = 2 * SUBLANES + (q - 1) * CHUNK + bb * SUBLANES + SUBLANES - 1
                    first = pe_ref[slot, row:row + 1]
                return jnp.where(sub == 0, first, pltpu.roll(group(q, bb), 1, 0))
            bb -= IL_GROUPS
            if q == n_chunks - 1:
                last = pe_ref[slot, SUBLANES + bb:SUBLANES + bb + 1]
            else:
                nxt = 2 * SUBLANES + (q + 1) * CHUNK + bb * SUBLANES
                last = pe_ref[slot, nxt:nxt + 1]
            return jnp.where(sub == SUBLANES - 1, last, pltpu.roll(group(q, bb), SUBLANES - 1, 0))

        def epilogue():
            taps = [cw_ref[k:k + 1, j:j + seg] for k in range(D_CONV)]
            bias = cb_ref[:, j:j + seg]
            for q in range(n_chunks):
                for b in range(IL_GROUPS):
                    acc = bias
                    for k in range(D_CONV):
                        acc = acc + shifted(q, b, k - CONV_LEFT) * taps[k]
                    lo = q * CHUNK + b * SUBLANES
                    act = _silu(acc)
                    for t in range(seg // LANES):
                        xc_ref[slot, t, lo:lo + SUBLANES] = act[:, t * LANES:(t + 1) * LANES]
            for q in range(n_chunks):
                xc = jnp.concatenate(
                    [jnp.concatenate(
                        [xc_ref[slot, t, pl.ds(q * CHUNK + (m % 2) * (CHUNK // 2) + m // 2,
                                               SUBLANES, stride=SUBLANES), :]
                         for t in range(seg // LANES)], axis=1)
                     for m in range(IL_GROUPS)], axis=0)
                if is_b:
                    b_ref[q * CHUNK:(q + 1) * CHUNK, j - W_SSD:j - W_SSD + seg] = _bf(xc)
                else:
                    dst, off = (xs_t_ref, j) if is_x else (c_t_ref, j - W_SSD - GN)
                    dst[q, off:off + seg] = _bf(xc.T)

        return matmul, epilogue

    def plain_stage(cols, finish):
        box = []
        return (lambda: box.append(_dot(hm, w_ref[:, cols]))), (lambda: finish(box.pop()))

    def dt_stage():
        box = []

        def epilogue():
            p_dt = box.pop()
            a_col = -jnp.exp(alog_ref[...])
            for q in range(n_chunks):
                dt = _softplus(p_dt[q * CHUNK:(q + 1) * CHUNK].T[:2 * HEADS] + bias_ref[...])
                dt_ref[q] = dt
                cum_ref[q] = _lane_cumsum(dt * a_col)

        return (lambda: box.append(_dot(hm, wdt_ref[...]))), epilogue

    def store_to(ref, cols=None, act=None):
        def finish(v):
            v = v if act is None else _bf(act(v))
            if cols is None:
                ref[...] = v
            else:
                ref[:, cols] = v
        return finish

    def store_pair(refs, act=None):
        def finish(v):
            for n, ref in enumerate(refs):
                part = v[:, n * POOL_GROUP_W:(n + 1) * POOL_GROUP_W]
                ref[...] = part if act is None else _bf(act(part))
        return finish

    stages = []
    if full:
        for g in range(0, N_POOL_GROUPS, 2):
            stages.append(plain_stage(slice(g * POOL_GROUP_W, (g + 2) * POOL_GROUP_W),
                                      store_pair(u_refs[g:g + 2])))
    stages.append(dt_stage())
    n_conv = 0
    for j in range(0, CONV_DIM, seg):
        if full or j < W_SSD + GN:
            stages.append(conv_stage(j, n_conv % 2))
            n_conv += 1
    if full:
        for g in range(0, N_POOL_GROUPS, 2):
            zcols = slice(OFF_POOL_Z + g * POOL_GROUP_W, OFF_POOL_Z + (g + 2) * POOL_GROUP_W)
            stages.append(plain_stage(zcols, store_pair(zp_refs[g:g + 2], act=_silu)))
        for j in range(0, W_SSD, seg):
            stages.append(plain_stage(slice(OFF_SSD_Z + j, OFF_SSD_Z + j + seg),
                                      store_to(zs_ref, cols=slice(j, j + seg), act=_silu)))

    sweep = []
    if full:
        swept = jnp.maximum(i - 1, 0)
        h_ref[...] = jnp.where(swept % tiles_per_seq == 0, h0_ref[0], h_ref[...])
        src = jax.lax.broadcasted_iota(jnp.int32, (CHUNK, CHUNK), 0)
        dst = jax.lax.broadcasted_iota(jnp.int32, (CHUNK, CHUNK), 1)
        a_b = -jnp.exp(alog_ref[HEADS:])
        pairs = []
        for q in range(n_chunks):
            pairs += _fwd_chunk_slices(q, kxs_ref, kb_ref, kc_ref, kdt_ref, kcum_ref, dskip_ref,
                                       ypart_ref, h_ref, a_b, src <= dst, src == dst)
        sweep = [pairs[0][0]]
        for p in range(len(pairs)):
            nxt = pairs[p + 1][0] if p + 1 < len(pairs) else (lambda: None)
            sweep.append(functools.partial(lambda a, w: (a(), w()), pairs[p][1], nxt))

    stages[0][0]()
    done = 0
    for k, (_, epilogue) in enumerate(stages):
        if k + 1 < len(stages):
            stages[k + 1][0]()
        epilogue()
        upto = -(-len(sweep) * (k + 1) // len(stages))
        for piece in sweep[done:upto]:
            piece()
        done = upto

    if full:
        for kept, ref in ((kxs_ref, xs_t_ref), (kb_ref, b_ref), (kc_ref, c_t_ref),
                          (kdt_ref, dt_ref), (kcum_ref, cum_ref)):
            kept[...] = ref[...]
    else:
        a_b = -jnp.exp(alog_ref[HEADS:])
        for reverse, out_ref in ((False, hf_ref), (True, hb_ref)):
            h_ref[...] = jnp.zeros(h_ref.shape, h_ref.dtype)
            for q in (range(n_chunks - 1, -1, -1) if reverse else range(n_chunks)):
                if reverse:
                    dt_b, cum_b = dt_ref[q, HEADS:], cum_ref[q, HEADS:]
                    scale_in = dt_b * jnp.exp(cum_b - dt_b * a_b)
                    decay = jnp.exp(cum_b[:, CHUNK - 1:CHUNK])
                else:
                    dt_f, cum_f = dt_ref[q, :HEADS], cum_ref[q, :HEADS]
                    tot_f = cum_f[:, CHUNK - 1:CHUNK]
                    scale_in, decay = dt_f * jnp.exp(tot_f - cum_f), jnp.exp(tot_f)
                _state_update(h_ref, xs_t_ref, b_ref, q, scale_in, decay)
            out_ref[0] = h_ref[...]


def _projection(x2d, norm_w, shift, scale, w_bf, wdt_bf, conv_w, conv_b, alog_col, bias_col,
                seq_len, tm, full, dskip_b=None, h0=None):
    n_tok = x2d.shape[0]
    tiles_per_seq = seq_len // tm
    n_tiles = n_tok // tm
    n_mod = shift.shape[0]
    nct = n_tok // CHUNK
    per = tm // SUBLANES
    last_halo = n_tok // SUBLANES - 1
    kern = functools.partial(_proj_kernel, tm=tm, tiles_per_seq=tiles_per_seq, n_tiles=n_tiles,
                             full=full)
    tile = lambda i: jnp.minimum(i, n_tiles - 1)
    mod_map = (lambda i: (tile(i) // tiles_per_seq, 0, 0)) if n_mod > 1 else (lambda i: (0, 0, 0))
    mod_spec = pl.BlockSpec((1, 1, D_MODEL), mod_map)
    const = lambda i: (0, 0)
    tok = lambda i: (tile(i), 0)
    chunk3 = lambda i: (tile(i), 0, 0)
    q = tm // CHUNK
    xs_t = (jax.ShapeDtypeStruct((nct, W_SSD, CHUNK), jnp.bfloat16),
            pl.BlockSpec((q, W_SSD, CHUNK), chunk3))
    b_tok = (jax.ShapeDtypeStruct((n_tok, GN), jnp.bfloat16), pl.BlockSpec((tm, GN), tok))
    c_t = (jax.ShapeDtypeStruct((nct, GN, CHUNK), jnp.bfloat16), pl.BlockSpec((q, GN, CHUNK), chunk3))
    dt = (jax.ShapeDtypeStruct((nct, 2 * HEADS, CHUNK), jnp.float32),
          pl.BlockSpec((q, 2 * HEADS, CHUNK), chunk3))
    if full:
        u = (jax.ShapeDtypeStruct((n_tok, POOL_GROUP_W), jnp.float32),
             pl.BlockSpec((tm, POOL_GROUP_W), tok))
        zp = (jax.ShapeDtypeStruct((n_tok, POOL_GROUP_W), jnp.bfloat16),
              pl.BlockSpec((tm, POOL_GROUP_W), tok))
        zs = (jax.ShapeDtypeStruct((n_tok, W_SSD), jnp.bfloat16), pl.BlockSpec((tm, W_SSD), tok))
        swept = lambda i: jnp.maximum(i - 1, 0)
        y_part = (jax.ShapeDtypeStruct((nct, W_SSD, CHUNK), jnp.bfloat16),
                  pl.BlockSpec((q, W_SSD, CHUNK), lambda i: (swept(i), 0, 0)))
        outs = [u] * N_POOL_GROUPS + [zp] * N_POOL_GROUPS + [zs, xs_t, b_tok, c_t, dt, dt, y_part]
        extra_in = [pl.BlockSpec((W_SSD, CHUNK), const),
                    pl.BlockSpec((1, W_SSD, D_STATE), lambda i: (swept(i) // tiles_per_seq, 0, 0))]
        extra_args = [dskip_b, h0]
        extra_scratch = [pltpu.VMEM((q, W_SSD, CHUNK), jnp.bfloat16),
                         pltpu.VMEM((tm, GN), jnp.bfloat16),
                         pltpu.VMEM((q, GN, CHUNK), jnp.bfloat16),
                         pltpu.VMEM((q, 2 * HEADS, CHUNK), jnp.float32),
                         pltpu.VMEM((q, 2 * HEADS, CHUNK), jnp.float32),
                         pltpu.VMEM((W_SSD, D_STATE), jnp.float32)]
    else:
        assert tiles_per_seq == 1, "prefix states are computed from one whole sequence per step"
        state = (jax.ShapeDtypeStruct((n_tiles, W_SSD, D_STATE), jnp.float32),
                 pl.BlockSpec((1, W_SSD, D_STATE), chunk3))
        outs = [state, state]
        extra_in, extra_args = [], []
        extra_scratch = [pltpu.VMEM((q, W_SSD, CHUNK), jnp.bfloat16),
                         pltpu.VMEM((tm, GN), jnp.bfloat16),
                         pltpu.VMEM((q, 2 * HEADS, CHUNK), jnp.float32),
                         pltpu.VMEM((q, 2 * HEADS, CHUNK), jnp.float32),
                         pltpu.VMEM((W_SSD, D_STATE), jnp.float32)]
    return pl.pallas_call(
        kern,
        grid=(n_tiles + 1 if full else n_tiles,),
        in_specs=[pl.BlockSpec((tm, D_MODEL), tok),
                  pl.BlockSpec((SUBLANES, D_MODEL), lambda i: (jnp.maximum(tile(i) * per - 1, 0), 0)),
                  pl.BlockSpec((SUBLANES, D_MODEL),
                               lambda i: (jnp.minimum((tile(i) + 1) * per, last_halo), 0)),
                  pl.BlockSpec((1, D_MODEL), const),
                  mod_spec, mod_spec,
                  pl.BlockSpec(w_bf.shape, const),
                  pl.BlockSpec((D_MODEL, DT_PAD), const),
                  pl.BlockSpec((D_CONV, CONV_DIM), const),
                  pl.BlockSpec((1, CONV_DIM), const),
                  pl.BlockSpec((2 * HEADS, 1), const),
                  pl.BlockSpec((2 * HEADS, 1), const)] + extra_in,
        out_specs=[o[1] for o in outs],
        out_shape=[o[0] for o in outs],
        scratch_shapes=[pltpu.VMEM((2, tm + 2 * SUBLANES, CONV_SEG), jnp.float32),
                        pltpu.VMEM((2, CONV_SEG // LANES, tm, LANES), jnp.float32),
                        pltpu.VMEM((D_MODEL // LANES, tm, LANES), jnp.float32)] + extra_scratch,
        compiler_params=pltpu.CompilerParams(
            dimension_semantics=("arbitrary",), vmem_limit_bytes=VMEM_LIMIT),
        name="proj" if full else "proj_ctx",
    )(x2d, x2d, x2d, norm_w.reshape(1, D_MODEL), shift, scale, w_bf, wdt_bf, conv_w, conv_b,
      alog_col, bias_col, *extra_args)


def _tok_rows(q):
    return pl.ds(q * CHUNK, CHUNK)


def _state_update_group(h_ref, xs_t_ref, b_ref, q, g, scale_in, chunk_decay):
    bg = b_ref[_tok_rows(q), g * D_STATE:(g + 1) * D_STATE]
    xd = []
    for r in range(HEADS_PER_GROUP):
        h = g * HEADS_PER_GROUP + r
        x_h = xs_t_ref[q, h * HEADDIM:(h + 1) * HEADDIM].astype(jnp.float32)
        xd.append(_bf(x_h * scale_in[h:h + 1]))
    s_new = _dot(jnp.concatenate(xd, axis=0), bg)
    for r in range(HEADS_PER_GROUP):
        h = g * HEADS_PER_GROUP + r
        hr = slice(h * HEADDIM, (h + 1) * HEADDIM)
        h_ref[hr] = h_ref[hr] * chunk_decay[h:h + 1] + s_new[r * HEADDIM:(r + 1) * HEADDIM]


def _state_update(h_ref, xs_t_ref, b_ref, q, scale_in, chunk_decay):
    for g in range(GROUPS):
        _state_update_group(h_ref, xs_t_ref, b_ref, q, g, scale_in, chunk_decay)


def _fwd_chunk_slices(q, xs_t_ref, b_ref, c_t_ref, dt_ref, cum_ref, dskip_ref, y_ref, h_ref, a_b,
                      causal, is_diag):
    ctx = {}

    def setup():
        dt_f, cum_f = dt_ref[q, :HEADS], cum_ref[q, :HEADS]
        tot_f = cum_f[:, CHUNK - 1:CHUNK]
        dt_b, cum_b = dt_ref[q, HEADS:], cum_ref[q, HEADS:]
        cumx_b = cum_b - dt_b * a_b
        ctx["scale_in"] = dt_f * jnp.exp(tot_f - cum_f)
        ctx["chunk_decay"] = jnp.exp(tot_f)
        ctx["col_terms"] = jnp.concatenate(
            [jnp.log(dt_f) - cum_f, jnp.log(dt_b) + cumx_b,
             jnp.zeros((CHUNK - 2 * HEADS, CHUNK), jnp.float32)], axis=0).T
        ctx["row_f"], ctx["row_b"] = cum_f, -cumx_b
        ctx["decay_out_f"] = jnp.exp(cum_f)
        ctx["dt_b"] = dt_b

    def weights(g):
        if g == 0:
            setup()
        col_terms, row_f, row_b = ctx["col_terms"], ctx["row_f"], ctx["row_b"]
        bg = b_ref[_tok_rows(q), g * D_STATE:(g + 1) * D_STATE]
        cg_t = c_t_ref[q, g * D_STATE:(g + 1) * D_STATE]
        rows = slice(g * HEADS_PER_GROUP * HEADDIM, (g + 1) * HEADS_PER_GROUP * HEADDIM)
        g_t = _dot(bg, cg_t)
        ctx["g_diag", g] = jnp.sum(jnp.where(is_diag, g_t, 0.0), axis=0, keepdims=True)
        ctx["y_off", g] = _dot(_bf(h_ref[rows]), cg_t)
        for r in range(HEADS_PER_GROUP):
            h = g * HEADS_PER_GROUP + r
            col_f = jnp.broadcast_to(col_terms[:, h:h + 1], (CHUNK, CHUNK))
            col_b = jnp.broadcast_to(col_terms[:, HEADS + h:HEADS + h + 1], (CHUNK, CHUNK))
            expo = jnp.where(causal, col_f + row_f[h:h + 1], col_b + row_b[h:h + 1])
            ctx["w_t", h] = _bf(g_t * jnp.exp(expo))

    def apply(g):
        y_off, g_diag = ctx.pop(("y_off", g)), ctx.pop(("g_diag", g))
        for r in range(HEADS_PER_GROUP):
            h = g * HEADS_PER_GROUP + r
            hr = slice(h * HEADDIM, (h + 1) * HEADDIM)
            x_bf = xs_t_ref[q, hr]
            y_h = _dot(x_bf, ctx.pop(("w_t", h)))
            y_h = y_h + y_off[r * HEADDIM:(r + 1) * HEADDIM] * ctx["decay_out_f"][h:h + 1]
            skip = dskip_ref[hr] + g_diag * ctx["dt_b"][h:h + 1]
            y_ref[q, hr] = _bf(y_h + skip * x_bf.astype(jnp.float32))
        _state_update_group(h_ref, xs_t_ref, b_ref, q, g, ctx["scale_in"], ctx["chunk_decay"])

    return [(functools.partial(weights, g), functools.partial(apply, g)) for g in range(GROUPS)]


def _bwd_out_kernel(xs_t_ref, b_ref, dt_ref, cum_ref, alog_ref, h0_ref, c_t_ref, ypart_ref,
                    yp0_ref, yp1_ref, yp2_ref, yp3_ref, zs_ref, x_ref, gate_ref, snw_ref, wout_ref,
                    fnw_ref, o_ref, h_ref, y_ref, *, cps):
    @pl.when(pl.program_id(1) == 0)
    def _():
        h_ref[...] = h0_ref[0]

    a_b = -jnp.exp(alog_ref[HEADS:])

    def chunk(q):
        dt_b, cum_b = dt_ref[q, HEADS:], cum_ref[q, HEADS:]
        tot_b = cum_b[:, CHUNK - 1:CHUNK]
        cumx_b = cum_b - dt_b * a_b
        decay_out = jnp.exp(tot_b - cumx_b)
        y_parts = []
        for g in range(GROUPS):
            cg_t = c_t_ref[q, g * D_STATE:(g + 1) * D_STATE]
            rows = slice(g * HEADS_PER_GROUP * HEADDIM, (g + 1) * HEADS_PER_GROUP * HEADDIM)
            y_off = _dot(_bf(h_ref[rows]), cg_t)
            for r in range(HEADS_PER_GROUP):
                h = g * HEADS_PER_GROUP + r
                hr = slice(h * HEADDIM, (h + 1) * HEADDIM)
                y_parts.append(ypart_ref[q, hr].astype(jnp.float32)
                               + y_off[r * HEADDIM:(r + 1) * HEADDIM] * decay_out[h:h + 1])
        y_ref[_tok_rows(q), :] = jnp.concatenate(y_parts, axis=0).T
        _state_update(h_ref, xs_t_ref, b_ref, q, dt_b * jnp.exp(cumx_b), jnp.exp(tot_b))

    yp_refs = [yp0_ref, yp1_ref, yp2_ref, yp3_ref]
    acc = jnp.zeros((cps * CHUNK, D_MODEL), jnp.float32)
    for i in range(max(cps, N_POOL_GROUPS)):
        if i < cps:
            chunk(cps - 1 - i)
        if i < N_POOL_GROUPS:
            acc = acc + _dot(yp_refs[i][...], wout_ref[i * POOL_GROUP_W:(i + 1) * POOL_GROUP_W])

    gw = W_SSD // GROUPS
    for g in range(GROUPS):
        cols = slice(g * gw, (g + 1) * gw)
        gated = y_ref[:, cols] * zs_ref[:, cols].astype(jnp.float32)
        ms = jnp.mean(gated * gated, axis=-1, keepdims=True)
        yn = gated * jax.lax.rsqrt(ms + EPS) * snw_ref[:, cols]
        acc = acc + _dot(_bf(yn), wout_ref[W_POOL + g * gw:W_POOL + (g + 1) * gw])
    hres = x_ref[...] + gate_ref[0] * acc
    ms = jnp.mean(hres * hres, axis=-1, keepdims=True)
    o_ref[...] = hres * jax.lax.rsqrt(ms + EPS) * fnw_ref[...]


def _backward_output(xs_t, b_tok, dt, cum, c_t, y_part, alog_col, h0, y_pool, gate_ssd, x2d, gate,
                     ssd_norm_w, w_out_bf, final_norm_w, bsz, n_chunks):
    n_tok = b_tok.shape[0]
    cps = min(n_chunks, SSD_CHUNKS_PER_STEP)
    n_steps = n_chunks // cps
    tm = cps * CHUNK
    block_of = lambda b, s: b * n_steps + (n_steps - 1 - s)
    tok = lambda b, s: (block_of(b, s), 0)
    chunk3 = lambda b, s: (block_of(b, s), 0, 0)
    const2 = lambda b, s: (0, 0)
    per_seq = lambda b, s: (b, 0, 0)
    head_spec = pl.BlockSpec((cps, 2 * HEADS, CHUNK), chunk3)
    return pl.pallas_call(
        functools.partial(_bwd_out_kernel, cps=cps),
        grid=(bsz, n_steps),
        in_specs=[pl.BlockSpec((cps, W_SSD, CHUNK), chunk3),
                  pl.BlockSpec((tm, GN), tok),
                  head_spec, head_spec,
                  pl.BlockSpec((2 * HEADS, 1), const2),
                  pl.BlockSpec((1, W_SSD, D_STATE), per_seq),
                  pl.BlockSpec((cps, GN, CHUNK), chunk3),
                  pl.BlockSpec((cps, W_SSD, CHUNK), chunk3)] + [
                  pl.BlockSpec((tm, POOL_GROUP_W), tok)] * N_POOL_GROUPS + [
                  pl.BlockSpec((tm, W_SSD), tok),
                  pl.BlockSpec((tm, D_MODEL), tok),
                  pl.BlockSpec((1, 1, D_MODEL), per_seq),
                  pl.BlockSpec((1, W_SSD), const2),
                  pl.BlockSpec((W_POOL + W_SSD, D_MODEL), const2),
                  pl.BlockSpec((1, D_MODEL), const2)],
        out_specs=pl.BlockSpec((tm, D_MODEL), tok),
        out_shape=jax.ShapeDtypeStruct((n_tok, D_MODEL), jnp.float32),
        scratch_shapes=[pltpu.VMEM((W_SSD, D_STATE), jnp.float32),
                        pltpu.VMEM((tm, W_SSD), jnp.float32)],
        compiler_params=pltpu.CompilerParams(
            dimension_semantics=("arbitrary", "arbitrary"), vmem_limit_bytes=VMEM_LIMIT),
        name="bwd_out",
    )(xs_t, b_tok, dt, cum, alog_col, h0, c_t, y_part, *y_pool, gate_ssd, x2d, gate,
      ssd_norm_w.reshape(1, W_SSD), w_out_bf, final_norm_w.reshape(1, D_MODEL))


POOL_TILE_ROWS = 4
POOL_TILE = POOL_TILE_ROWS * GRID_W


def _pool_constants(window, n_rows):
    lo_off, hi_off = -(window // 2), window - window // 2
    col = np.arange(GRID_W)
    lo = np.clip(col + lo_off, 0, GRID_W)
    hi = np.clip(col + hi_off, 0, GRID_W)
    band = ((col[None, :] >= lo[:, None]) & (col[None, :] < hi[:, None])).astype(np.float32)
    band_tile = np.kron(np.eye(POOL_TILE_ROWS, dtype=np.float32), band)
    row = np.arange(n_rows)
    cnt_r = np.clip(row + hi_off, 0, n_rows) - np.clip(row + lo_off, 0, n_rows)
    inv = 1.0 / (cnt_r[:, None] * (hi - lo)[None, :]).astype(np.float64)
    inv = np.broadcast_to(inv.reshape(-1, 1), (n_rows * GRID_W, 128)).astype(np.float32)
    return jnp.asarray(band_tile, jnp.bfloat16), jnp.asarray(inv)


def _pool_kernel(u_ref, z_ref, band_ref, inv_ref, w_ref, scale_ref, o_ref, *, window, n_rows):
    def grid_row(r):
        return u_ref[r * GRID_W:(r + 1) * GRID_W]

    def bounds(r):
        return max(r - window // 2, 0), min(r + window - window // 2, n_rows)

    band = band_ref[...]
    rsum, tile_rows = None, []
    for r in range(n_rows):
        lo, hi = bounds(r)
        if r == 0 or window <= 2:
            rsum = grid_row(lo)
            for k in range(lo + 1, hi):
                rsum = rsum + grid_row(k)
        else:
            prev_lo, prev_hi = bounds(r - 1)
            if hi > prev_hi:
                rsum = rsum + grid_row(hi - 1)
            if lo > prev_lo:
                rsum = rsum - grid_row(prev_lo)
        tile_rows.append(rsum)
        if len(tile_rows) < POOL_TILE_ROWS:
            continue
        base = (r + 1 - POOL_TILE_ROWS) * GRID_W
        rows = slice(base, base + POOL_TILE)
        rs = jnp.concatenate(tile_rows, axis=0)
        tile_rows = []
        box = _dot(band, _bf(rs))
        inv = inv_ref[rows]
        mean = box * jnp.concatenate([inv, inv], axis=1)
        d = mean - u_ref[rows]
        y = _dot(_bf(d), w_ref[0]) * scale_ref[...]
        o_ref[rows] = _bf(y * z_ref[rows].astype(jnp.float32))


def _pool_group(u, gate, pool_w_bf, pool_scale, g, bsz, n_img_tok):
    window = POOL_WINDOWS[g]
    n_rows = n_img_tok // GRID_W
    band, inv = _pool_constants(window, n_rows)
    kern = functools.partial(_pool_kernel, window=window, n_rows=n_rows)
    img = pl.BlockSpec((n_img_tok, POOL_GROUP_W), lambda b: (b, 0))
    return pl.pallas_call(
        kern,
        grid=(bsz,),
        in_specs=[img, img,
                  pl.BlockSpec((POOL_TILE, POOL_TILE), lambda b: (0, 0)),
                  pl.BlockSpec((n_img_tok, 128), lambda b: (0, 0)),
                  pl.BlockSpec((1, POOL_GROUP_W, POOL_GROUP_W), lambda b: (g, 0, 0)),
                  pl.BlockSpec((1, POOL_GROUP_W), lambda b: (0, g))],
        out_specs=img,
        out_shape=jax.ShapeDtypeStruct((bsz * n_img_tok, POOL_GROUP_W), jnp.bfloat16),
        compiler_params=pltpu.CompilerParams(vmem_limit_bytes=VMEM_LIMIT),
        name=f"pool{window}",
    )(u, gate, band, inv, pool_w_bf, pool_scale)


def kernel(x, c, ctx, c_ctx, norm_w, w_ada, b_ada, w_in, conv_w, conv_b, a_log, dt_bias, d_skip,
           ssd_norm_w, pool_w, pool_scale, w_out, final_norm_w):
    bsz, seq, _ = x.shape
    ctx_len = ctx.shape[1]
    depth = norm_w.shape[0]
    assert depth == 1, "single-layer block: the context stream update is never consumed"
    assert seq % 512 == 0 and ctx_len % CHUNK == 0 and seq % GRID_W == 0

    mod_rows = -(-(bsz + 1) // SUBLANES) * SUBLANES
    cond = jnp.concatenate([c, c_ctx[None], jnp.zeros((mod_rows - bsz - 1, D_MODEL), c.dtype)])
    mod = _modulation(cond, w_ada[0], b_ada[0])
    shift = mod[:, :D_MODEL].reshape(mod_rows, 1, D_MODEL)
    scale = mod[:, D_MODEL:2 * D_MODEL].reshape(mod_rows, 1, D_MODEL)
    gate = mod[:, 2 * D_MODEL:].reshape(mod_rows, 1, D_MODEL)

    w_in_bf = _bf(w_in[0])
    w_dt_bf = jnp.pad(_bf(w_in[0, :, OFF_DT:]), ((0, 0), (0, DT_PAD - 2 * HEADS)))
    alog_col = a_log[0].reshape(2 * HEADS, 1)
    bias_col = dt_bias[0].reshape(2 * HEADS, 1)
    dskip_b = jnp.broadcast_to(jnp.repeat(d_skip[0], HEADDIM)[:, None], (W_SSD, CHUNK))
    conv_b2 = conv_b[0].reshape(1, CONV_DIM)

    ctx2d = ctx.reshape(bsz * ctx_len, D_MODEL)
    h_fwd, h_bwd = _projection(
        ctx2d, norm_w[0], shift[bsz:bsz + 1], scale[bsz:bsz + 1], w_in_bf, w_dt_bf, conv_w[0],
        conv_b2, alog_col, bias_col, ctx_len, ctx_len, full=False)

    x2d = x.reshape(bsz * seq, D_MODEL)
    outs = _projection(x2d, norm_w[0], shift, scale, w_in_bf, w_dt_bf, conv_w[0], conv_b2,
                       alog_col, bias_col, seq, 512, full=True, dskip_b=dskip_b, h0=h_fwd)
    u_pool, gate_pool = outs[:N_POOL_GROUPS], outs[N_POOL_GROUPS:2 * N_POOL_GROUPS]
    gate_ssd, xs_t, b_tok, c_t, dt, cum, y_part = outs[2 * N_POOL_GROUPS:]
    nc = seq // CHUNK
    pool_w_bf = _bf(pool_w[0])
    y_pool = [_pool_group(u_pool[g], gate_pool[g], pool_w_bf, pool_scale, g, bsz, seq)
              for g in range(N_POOL_GROUPS)]
    out = _backward_output(xs_t, b_tok, dt, cum, c_t, y_part, alog_col, h_bwd, y_pool, gate_ssd,
                           x2d, gate, ssd_norm_w[0], _bf(w_out[0]), final_norm_w, bsz, nc)
    return out.reshape(bsz, seq, D_MODEL)
```

```python
import functools

import numpy as np
import jax
import jax.numpy as jnp
from jax.experimental import pallas as pl
from jax.experimental.pallas import tpu as pltpu

D_MODEL = 1024
GRID_W = 64
W_POOL = 1024
W_SSD = 1024
POOL_WINDOWS = (2, 4, 8, 16)
N_POOL_GROUPS = len(POOL_WINDOWS)
POOL_GROUP_W = 256
HEADDIM = 64
HEADS = 16
GROUPS = 4
HEADS_PER_GROUP = 4
D_STATE = 128
D_CONV = 4
CONV_LEFT = 2
CHUNK = 128
GN = GROUPS * D_STATE
CONV_DIM = W_SSD + 2 * GN
OFF_POOL_Z = W_POOL
OFF_SSD_Z = 2 * W_POOL
OFF_XBC = 2 * W_POOL + W_SSD
OFF_DT = OFF_XBC + CONV_DIM
DT_PAD = 128
EPS = 1e-6
SUBLANES = 8
LANES = 128
IL_GROUPS = CHUNK // SUBLANES
CONV_SEG = 256
SSD_CHUNKS_PER_STEP = 4
VMEM_LIMIT = 56 * 1024 * 1024


def _silu(v):
    h = 0.5 * v
    return h + h * jnp.tanh(h)


def _softplus(v):
    return jnp.maximum(v, 0.0) + jnp.log1p(jnp.exp(-jnp.abs(v)))


def _bf(v):
    return v.astype(jnp.bfloat16)


def _dot(a, b):
    return jnp.dot(a, b, preferred_element_type=jnp.float32)


def _mod_kernel(c_ref, w_ref, b_ref, o_ref):
    s = _silu(c_ref[...])
    o_ref[...] = jnp.dot(s, w_ref[...], preferred_element_type=jnp.float32,
                         precision=jax.lax.Precision.HIGHEST) + b_ref[...]


def _modulation(cond_rows, w_ada, b_ada):
    rows = cond_rows.shape[0]
    n_out = w_ada.shape[1]
    tn = 1024
    return pl.pallas_call(
        _mod_kernel,
        grid=(n_out // tn,),
        in_specs=[pl.BlockSpec((rows, D_MODEL), lambda j: (0, 0)),
                  pl.BlockSpec((D_MODEL, tn), lambda j: (0, j)),
                  pl.BlockSpec((1, tn), lambda j: (0, j))],
        out_specs=pl.BlockSpec((rows, tn), lambda j: (0, j)),
        out_shape=jax.ShapeDtypeStruct((rows, n_out), jnp.float32),
        compiler_params=pltpu.CompilerParams(vmem_limit_bytes=VMEM_LIMIT),
        name="mod",
    )(cond_rows, w_ada, b_ada.reshape(1, n_out))


def _cast_kernel(src_ref, dst_ref):
    dst_ref[...] = _bf(src_ref[...])


def _to_bf16(w):
    rows, cols = w.shape
    tr = 128
    spec = pl.BlockSpec((tr, cols), lambda i: (i, 0))
    return pl.pallas_call(
        _cast_kernel,
        grid=(rows // tr,),
        in_specs=[spec],
        out_specs=spec,
        out_shape=jax.ShapeDtypeStruct((rows, cols), jnp.bfloat16),
        compiler_params=pltpu.CompilerParams(vmem_limit_bytes=VMEM_LIMIT),
        name="cast",
    )(w)


def _lane_cumsum(v):
    lane = jax.lax.broadcasted_iota(jnp.int32, v.shape, 1)
    shift = 1
    while shift < CHUNK:
        v = v + jnp.where(lane >= shift, pltpu.roll(v, shift, 1), 0.0)
        shift *= 2
    return v


def _proj_kernel(x_ref, xp_ref, xn_ref, nw_ref, sh_ref, sc_ref, w_ref, wdt_ref, cw_ref, cb_ref,
                 alog_ref, bias_ref, *rest, tm, tiles_per_seq, n_tiles, full):
    if full:
        (dskip_ref, h0_ref, u0, u1, u2, u3, zp0, zp1, zp2, zp3, zs_ref, xs_t_ref, b_ref, c_t_ref,
         dt_ref, cum_ref, ypart_ref, pe_ref, xc_ref, mn_ref, kxs_ref, kb_ref, kc_ref, kdt_ref,
         kcum_ref, h_ref) = rest
        u_refs, zp_refs = (u0, u1, u2, u3), (zp0, zp1, zp2, zp3)
    else:
        hf_ref, hb_ref, pe_ref, xc_ref, mn_ref, xs_t_ref, b_ref, dt_ref, cum_ref, h_ref = rest
    i = pl.program_id(0)
    pos = jnp.minimum(i, n_tiles - 1) % tiles_per_seq
    has_prev = pos > 0
    has_next = pos < tiles_per_seq - 1
    n_chunks = tm // CHUNK
    seg = CONV_SEG

    if full:
        @pl.when(i == 0)
        def _():
            for ref in (kxs_ref, kb_ref, kc_ref, kdt_ref, kcum_ref, h_ref):
                ref[...] = jnp.zeros(ref.shape, ref.dtype)

    def modulated(v):
        ms = jnp.mean(v * v, axis=-1, keepdims=True)
        y = v * jax.lax.rsqrt(ms + EPS) * nw_ref[...]
        return y * (1.0 + sc_ref[0]) + sh_ref[0]

    m_tok = modulated(x_ref[...])
    hm = _bf(m_tok)
    for t in range(D_MODEL // LANES):
        mn_ref[t] = m_tok[:, t * LANES:(t + 1) * LANES]

    rows = [jnp.concatenate([mn_ref[t, pl.ds(q * CHUNK + b, SUBLANES, stride=IL_GROUPS), :]
                             for t in range(D_MODEL // LANES)], axis=1)
            for q in range(n_chunks) for b in range(IL_GROUPS)]
    halo = [jnp.where(has_prev, modulated(xp_ref[...]), 0.0),
            jnp.where(has_next, modulated(xn_ref[...]), 0.0)]
    hm_il = _bf(jnp.concatenate(halo + rows, axis=0))
    sub = jax.lax.broadcasted_iota(jnp.int32, (SUBLANES, seg), 0)

    def conv_stage(j, slot):
        is_x = j < W_SSD
        is_b = W_SSD <= j < W_SSD + GN

        def matmul():
            pe_ref[slot] = _dot(hm_il, w_ref[:, OFF_XBC + j:OFF_XBC + j + seg])

        def group(q, b):
            lo = 2 * SUBLANES + q * CHUNK + b * SUBLANES
            return pe_ref[slot, lo:lo + SUBLANES]

        def shifted(q, b, delta):
            bb = b + delta
            if 0 <= bb < IL_GROUPS:
                return group(q, bb)
            if bb < 0:
                bb += IL_GROUPS
                if q == 0:
                    first = pe_ref[slot, bb - SUBLANES:bb - SUBLANES + 1]
                else:
                    row = 2 * SUBLANES + (q - 1) * CHUNK + bb * SUBLANES + SUBLANES - 1
                    first = pe_ref[slot, row:row + 1]
                return jnp.where(sub == 0, first, pltpu.roll(group(q, bb), 1, 0))
            bb -= IL_GROUPS
            if q == n_chunks - 1:
                last = pe_ref[slot, SUBLANES + bb:SUBLANES + bb + 1]
            else:
                nxt = 2 * SUBLANES + (q + 1) * CHUNK + bb * SUBLANES
                last = pe_ref[slot, nxt:nxt + 1]
            return jnp.where(sub == SUBLANES - 1, last, pltpu.roll(group(q, bb), SUBLANES - 1, 0))

        def epilogue():
            taps = [cw_ref[k:k + 1, j:j + seg] for k in range(D_CONV)]
            bias = cb_ref[:, j:j + seg]
            for q in range(n_chunks):
                for b in range(IL_GROUPS):
                    acc = bias
                    for k in range(D_CONV):
                        acc = acc + shifted(q, b, k - CONV_LEFT) * taps[k]
                    lo = q * CHUNK + b * SUBLANES
                    act = _silu(acc)
                    for t in range(seg // LANES):
                        xc_ref[slot, t, lo:lo + SUBLANES] = act[:, t * LANES:(t + 1) * LANES]
            for q in range(n_chunks):
                xc = jnp.concatenate(
                    [jnp.concatenate(
                        [xc_ref[slot, t, pl.ds(q * CHUNK + (m % 2) * (CHUNK // 2) + m // 2,
                                               SUBLANES, stride=SUBLANES), :]
                         for t in range(seg // LANES)], axis=1)
                     for m in range(IL_GROUPS)], axis=0)
                if is_b:
                    b_ref[q * CHUNK:(q + 1) * CHUNK, j - W_SSD:j - W_SSD + seg] = _bf(xc)
                else:
                    dst, off = (xs_t_ref, j) if is_x else (c_t_ref, j - W_SSD - GN)
                    dst[q, off:off + seg] = _bf(xc.T)

        return matmul, epilogue

    def plain_stage(cols, finish):
        box = []
        return (lambda: box.append(_dot(hm, w_ref[:, cols]))), (lambda: finish(box.pop()))

    def dt_stage():
        box = []

        def epilogue():
            p_dt = box.pop()
            a_col = -jnp.exp(alog_ref[...])
            for q in range(n_chunks):
                dt = _softplus(p_dt[q * CHUNK:(q + 1) * CHUNK].T[:2 * HEADS] + bias_ref[...])
                dt_ref[q] = dt
                cum_ref[q] = _lane_cumsum(dt * a_col)

        return (lambda: box.append(_dot(hm, wdt_ref[...]))), epilogue

    def store_to(ref, cols=None, act=None):
        def finish(v):
            v = v if act is None else _bf(act(v))
            if cols is None:
                ref[...] = v
            else:
                ref[:, cols] = v
        return finish

    def store_pair(refs, act=None):
        def finish(v):
            for n, ref in enumerate(refs):
                part = v[:, n * POOL_GROUP_W:(n + 1) * POOL_GROUP_W]
                ref[...] = _bf(part if act is None else act(part))
        return finish

    stages = []
    if full:
        for g in range(0, N_POOL_GROUPS, 2):
            stages.append(plain_stage(slice(g * POOL_GROUP_W, (g + 2) * POOL_GROUP_W),
                                      store_pair(u_refs[g:g + 2])))
    stages.append(dt_stage())
    n_conv = 0
    for j in range(0, CONV_DIM, seg):
        if full or j < W_SSD + GN:
            stages.append(conv_stage(j, n_conv % 2))
            n_conv += 1
    if full:
        for g in range(0, N_POOL_GROUPS, 2):
            zcols = slice(OFF_POOL_Z + g * POOL_GROUP_W, OFF_POOL_Z + (g + 2) * POOL_GROUP_W)
            stages.append(plain_stage(zcols, store_pair(zp_refs[g:g + 2], act=_silu)))
        for j in range(0, W_SSD, seg):
            stages.append(plain_stage(slice(OFF_SSD_Z + j, OFF_SSD_Z + j + seg),
                                      store_to(zs_ref, cols=slice(j, j + seg), act=_silu)))

    sweep = []
    if full:
        swept = jnp.maximum(i - 1, 0)
        h_ref[...] = jnp.where(swept % tiles_per_seq == 0, h0_ref[0], h_ref[...])
        src = jax.lax.broadcasted_iota(jnp.int32, (CHUNK, CHUNK), 0)
        dst = jax.lax.broadcasted_iota(jnp.int32, (CHUNK, CHUNK), 1)
        a_b = -jnp.exp(alog_ref[HEADS:])
        pairs = []
        for q in range(n_chunks):
            pairs += _fwd_chunk_slices(q, kxs_ref, kb_ref, kc_ref, kdt_ref, kcum_ref, dskip_ref,
                                       ypart_ref, h_ref, a_b, src <= dst, src == dst)
        sweep = [pairs[0][0]]
        for p in range(len(pairs)):
            nxt = pairs[p + 1][0] if p + 1 < len(pairs) else (lambda: None)
            sweep.append(functools.partial(lambda a, w: (a(), w()), pairs[p][1], nxt))

    stages[0][0]()
    done = 0
    for k, (_, epilogue) in enumerate(stages):
        if k + 1 < len(stages):
            stages[k + 1][0]()
        epilogue()
        upto = -(-len(sweep) * (k + 1) // len(stages))
        for piece in sweep[done:upto]:
            piece()
        done = upto

    if full:
        for kept, ref in ((kxs_ref, xs_t_ref), (kb_ref, b_ref), (kc_ref, c_t_ref),
                          (kdt_ref, dt_ref), (kcum_ref, cum_ref)):
            kept[...] = ref[...]
    else:
        a_b = -jnp.exp(alog_ref[HEADS:])
        for reverse, out_ref in ((False, hf_ref), (True, hb_ref)):
            h_ref[...] = jnp.zeros(h_ref.shape, h_ref.dtype)
            for q in (range(n_chunks - 1, -1, -1) if reverse else range(n_chunks)):
                if reverse:
                    dt_b, cum_b = dt_ref[q, HEADS:], cum_ref[q, HEADS:]
                    scale_in = dt_b * jnp.exp(cum_b - dt_b * a_b)
                    decay = jnp.exp(cum_b[:, CHUNK - 1:CHUNK])
                else:
                    dt_f, cum_f = dt_ref[q, :HEADS], cum_ref[q, :HEADS]
                    tot_f = cum_f[:, CHUNK - 1:CHUNK]
                    scale_in, decay = dt_f * jnp.exp(tot_f - cum_f), jnp.exp(tot_f)
                _state_update(h_ref, xs_t_ref, b_ref, q, scale_in, decay)
            out_ref[0] = h_ref[...]


def _projection(x2d, norm_w, shift, scale, w_bf, wdt_bf, conv_w, conv_b, alog_col, bias_col,
                seq_len, tm, full, dskip_b=None, h0=None):
    n_tok = x2d.shape[0]
    tiles_per_seq = seq_len // tm
    n_tiles = n_tok // tm
    n_mod = shift.shape[0]
    nct = n_tok // CHUNK
    per = tm // SUBLANES
    last_halo = n_tok // SUBLANES - 1
    kern = functools.partial(_proj_kernel, tm=tm, tiles_per_seq=tiles_per_seq, n_tiles=n_tiles,
                             full=full)
    tile = lambda i: jnp.minimum(i, n_tiles - 1)
    mod_map = (lambda i: (tile(i) // tiles_per_seq, 0, 0)) if n_mod > 1 else (lambda i: (0, 0, 0))
    mod_spec = pl.BlockSpec((1, 1, D_MODEL), mod_map)
    const = lambda i: (0, 0)
    tok = lambda i: (tile(i), 0)
    chunk3 = lambda i: (tile(i), 0, 0)
    q = tm // CHUNK
    xs_t = (jax.ShapeDtypeStruct((nct, W_SSD, CHUNK), jnp.bfloat16),
            pl.BlockSpec((q, W_SSD, CHUNK), chunk3))
    b_tok = (jax.ShapeDtypeStruct((n_tok, GN), jnp.bfloat16), pl.BlockSpec((tm, GN), tok))
    c_t = (jax.ShapeDtypeStruct((nct, GN, CHUNK), jnp.bfloat16), pl.BlockSpec((q, GN, CHUNK), chunk3))
    dt = (jax.ShapeDtypeStruct((nct, 2 * HEADS, CHUNK), jnp.float32),
          pl.BlockSpec((q, 2 * HEADS, CHUNK), chunk3))
    if full:
        zp = (jax.ShapeDtypeStruct((n_tok, POOL_GROUP_W), jnp.bfloat16),
              pl.BlockSpec((tm, POOL_GROUP_W), tok))
        u = zp
        zs = (jax.ShapeDtypeStruct((n_tok, W_SSD), jnp.bfloat16), pl.BlockSpec((tm, W_SSD), tok))
        swept = lambda i: jnp.maximum(i - 1, 0)
        y_part = (jax.ShapeDtypeStruct((nct, W_SSD, CHUNK), jnp.bfloat16),
                  pl.BlockSpec((q, W_SSD, CHUNK), lambda i: (swept(i), 0, 0)))
        outs = [u] * N_POOL_GROUPS + [zp] * N_POOL_GROUPS + [zs, xs_t, b_tok, c_t, dt, dt, y_part]
        extra_in = [pl.BlockSpec((W_SSD, CHUNK), const),
                    pl.BlockSpec((1, W_SSD, D_STATE), lambda i: (swept(i) // tiles_per_seq, 0, 0))]
        extra_args = [dskip_b, h0]
        extra_scratch = [pltpu.VMEM((q, W_SSD, CHUNK), jnp.bfloat16),
                         pltpu.VMEM((tm, GN), jnp.bfloat16),
                         pltpu.VMEM((q, GN, CHUNK), jnp.bfloat16),
                         pltpu.VMEM((q, 2 * HEADS, CHUNK), jnp.float32),
                         pltpu.VMEM((q, 2 * HEADS, CHUNK), jnp.float32),
                         pltpu.VMEM((W_SSD, D_STATE), jnp.float32)]
    else:
        assert tiles_per_seq == 1, "prefix states are computed from one whole sequence per step"
        state = (jax.ShapeDtypeStruct((n_tiles, W_SSD, D_STATE), jnp.float32),
                 pl.BlockSpec((1, W_SSD, D_STATE), chunk3))
        outs = [state, state]
        extra_in, extra_args = [], []
        extra_scratch = [pltpu.VMEM((q, W_SSD, CHUNK), jnp.bfloat16),
                         pltpu.VMEM((tm, GN), jnp.bfloat16),
                         pltpu.VMEM((q, 2 * HEADS, CHUNK), jnp.float32),
                         pltpu.VMEM((q, 2 * HEADS, CHUNK), jnp.float32),
                         pltpu.VMEM((W_SSD, D_STATE), jnp.float32)]
    return pl.pallas_call(
        kern,
        grid=(n_tiles + 1 if full else n_tiles,),
        in_specs=[pl.BlockSpec((tm, D_MODEL), tok),
                  pl.BlockSpec((SUBLANES, D_MODEL), lambda i: (jnp.maximum(tile(i) * per - 1, 0), 0)),
                  pl.BlockSpec((SUBLANES, D_MODEL),
                               lambda i: (jnp.minimum((tile(i) + 1) * per, last_halo), 0)),
                  pl.BlockSpec((1, D_MODEL), const),
                  mod_spec, mod_spec,
                  pl.BlockSpec(w_bf.shape, const),
                  pl.BlockSpec((D_MODEL, DT_PAD), const),
                  pl.BlockSpec((D_CONV, CONV_DIM), const),
                  pl.BlockSpec((1, CONV_DIM), const),
                  pl.BlockSpec((2 * HEADS, 1), const),
                  pl.BlockSpec((2 * HEADS, 1), const)] + extra_in,
        out_specs=[o[1] for o in outs],
        out_shape=[o[0] for o in outs],
        scratch_shapes=[pltpu.VMEM((2, tm + 2 * SUBLANES, CONV_SEG), jnp.float32),
                        pltpu.VMEM((2, CONV_SEG // LANES, tm, LANES), jnp.float32),
                        pltpu.VMEM((D_MODEL // LANES, tm, LANES), jnp.float32)] + extra_scratch,
        compiler_params=pltpu.CompilerParams(
            dimension_semantics=("arbitrary",), vmem_limit_bytes=VMEM_LIMIT),
        name="proj" if full else "proj_ctx",
    )(x2d, x2d, x2d, norm_w.reshape(1, D_MODEL), shift, scale, w_bf, wdt_bf, conv_w, conv_b,
      alog_col, bias_col, *extra_args)


def _tok_rows(q):
    return pl.ds(q * CHUNK, CHUNK)


def _state_update_group(h_ref, xs_t_ref, b_ref, q, g, scale_in, chunk_decay):
    bg = b_ref[_tok_rows(q), g * D_STATE:(g + 1) * D_STATE]
    xd = []
    for r in range(HEADS_PER_GROUP):
        h = g * HEADS_PER_GROUP + r
        x_h = xs_t_ref[q, h * HEADDIM:(h + 1) * HEADDIM].astype(jnp.float32)
        xd.append(_bf(x_h * scale_in[h:h + 1]))
    s_new = _dot(jnp.concatenate(xd, axis=0), bg)
    for r in range(HEADS_PER_GROUP):
        h = g * HEADS_PER_GROUP + r
        hr = slice(h * HEADDIM, (h + 1) * HEADDIM)
        h_ref[hr] = h_ref[hr] * chunk_decay[h:h + 1] + s_new[r * HEADDIM:(r + 1) * HEADDIM]


def _state_update(h_ref, xs_t_ref, b_ref, q, scale_in, chunk_decay):
    for g in range(GROUPS):
        _state_update_group(h_ref, xs_t_ref, b_ref, q, g, scale_in, chunk_decay)


def _fwd_chunk_slices(q, xs_t_ref, b_ref, c_t_ref, dt_ref, cum_ref, dskip_ref, y_ref, h_ref, a_b,
                      causal, is_diag):
    ctx = {}

    def setup():
        dt_f, cum_f = dt_ref[q, :HEADS], cum_ref[q, :HEADS]
        tot_f = cum_f[:, CHUNK - 1:CHUNK]
        dt_b, cum_b = dt_ref[q, HEADS:], cum_ref[q, HEADS:]
        cumx_b = cum_b - dt_b * a_b
        ctx["scale_in"] = dt_f * jnp.exp(tot_f - cum_f)
        ctx["chunk_decay"] = jnp.exp(tot_f)
        ctx["col_terms"] = jnp.concatenate(
            [jnp.log(dt_f) - cum_f, jnp.log(dt_b) + cumx_b,
             jnp.zeros((CHUNK - 2 * HEADS, CHUNK), jnp.float32)], axis=0).T
        ctx["row_f"], ctx["row_b"] = cum_f, -cumx_b
        ctx["decay_out_f"] = jnp.exp(cum_f)
        ctx["dt_b"] = dt_b

    def weights(g):
        if g == 0:
            setup()
        col_terms, row_f, row_b = ctx["col_terms"], ctx["row_f"], ctx["row_b"]
        bg = b_ref[_tok_rows(q), g * D_STATE:(g + 1) * D_STATE]
        cg_t = c_t_ref[q, g * D_STATE:(g + 1) * D_STATE]
        rows = slice(g * HEADS_PER_GROUP * HEADDIM, (g + 1) * HEADS_PER_GROUP * HEADDIM)
        g_t = _dot(bg, cg_t)
        ctx["g_diag", g] = jnp.sum(jnp.where(is_diag, g_t, 0.0), axis=0, keepdims=True)
        ctx["y_off", g] = _dot(_bf(h_ref[rows]), cg_t)
        for r in range(HEADS_PER_GROUP):
            h = g * HEADS_PER_GROUP + r
            col_f = jnp.broadcast_to(col_terms[:, h:h + 1], (CHUNK, CHUNK))
            col_b = jnp.broadcast_to(col_terms[:, HEADS + h:HEADS + h + 1], (CHUNK, CHUNK))
            expo = jnp.where(causal, col_f + row_f[h:h + 1], col_b + row_b[h:h + 1])
            ctx["w_t", h] = _bf(g_t * jnp.exp(expo))

    def apply(g):
        y_off, g_diag = ctx.pop(("y_off", g)), ctx.pop(("g_diag", g))
        for r in range(HEADS_PER_GROUP):
            h = g * HEADS_PER_GROUP + r
            hr = slice(h * HEADDIM, (h + 1) * HEADDIM)
            x_bf = xs_t_ref[q, hr]
            y_h = _dot(x_bf, ctx.pop(("w_t", h)))
            y_h = y_h + y_off[r * HEADDIM:(r + 1) * HEADDIM] * ctx["decay_out_f"][h:h + 1]
            skip = dskip_ref[hr] + g_diag * ctx["dt_b"][h:h + 1]
            y_ref[q, hr] = _bf(y_h + skip * x_bf.astype(jnp.float32))
        _state_update_group(h_ref, xs_t_ref, b_ref, q, g, ctx["scale_in"], ctx["chunk_decay"])

    return [(functools.partial(weights, g), functools.partial(apply, g)) for g in range(GROUPS)]


def _bwd_out_kernel(xs_t_ref, b_ref, dt_ref, cum_ref, alog_ref, h0_ref, c_t_ref, ypart_ref,
                    yp0_ref, yp1_ref, yp2_ref, yp3_ref, zs_ref, x_ref, gate_ref, snw_ref, wout_ref,
                    fnw_ref, o_ref, h_ref, y_ref, *, cps):
    @pl.when(pl.program_id(1) == 0)
    def _():
        h_ref[...] = h0_ref[0]

    a_b = -jnp.exp(alog_ref[HEADS:])

    def chunk(q):
        dt_b, cum_b = dt_ref[q, HEADS:], cum_ref[q, HEADS:]
        tot_b = cum_b[:, CHUNK - 1:CHUNK]
        cumx_b = cum_b - dt_b * a_b
        decay_out = jnp.exp(tot_b - cumx_b)
        y_parts = []
        for g in range(GROUPS):
            cg_t = c_t_ref[q, g * D_STATE:(g + 1) * D_STATE]
            rows = slice(g * HEADS_PER_GROUP * HEADDIM, (g + 1) * HEADS_PER_GROUP * HEADDIM)
            y_off = _dot(_bf(h_ref[rows]), cg_t)
            for r in range(HEADS_PER_GROUP):
                h = g * HEADS_PER_GROUP + r
                hr = slice(h * HEADDIM, (h + 1) * HEADDIM)
                y_parts.append(ypart_ref[q, hr].astype(jnp.float32)
                               + y_off[r * HEADDIM:(r + 1) * HEADDIM] * decay_out[h:h + 1])
        y_ref[_tok_rows(q), :] = jnp.concatenate(y_parts, axis=0).T
        _state_update(h_ref, xs_t_ref, b_ref, q, dt_b * jnp.exp(cumx_b), jnp.exp(tot_b))

    yp_refs = [yp0_ref, yp1_ref, yp2_ref, yp3_ref]
    acc = jnp.zeros((cps * CHUNK, D_MODEL), jnp.float32)
    for i in range(max(cps, N_POOL_GROUPS)):
        if i < cps:
            chunk(cps - 1 - i)
        if i < N_POOL_GROUPS:
            acc = acc + _dot(yp_refs[i][...], wout_ref[i * POOL_GROUP_W:(i + 1) * POOL_GROUP_W])

    gw = W_SSD // GROUPS
    for g in range(GROUPS):
        cols = slice(g * gw, (g + 1) * gw)
        gated = y_ref[:, cols] * zs_ref[:, cols].astype(jnp.float32)
        ms = jnp.mean(gated * gated, axis=-1, keepdims=True)
        yn = gated * jax.lax.rsqrt(ms + EPS) * snw_ref[:, cols]
        acc = acc + _dot(_bf(yn), wout_ref[W_POOL + g * gw:W_POOL + (g + 1) * gw])
    hres = x_ref[...] + gate_ref[0] * acc
    ms = jnp.mean(hres * hres, axis=-1, keepdims=True)
    o_ref[...] = hres * jax.lax.rsqrt(ms + EPS) * fnw_ref[...]


def _backward_output(xs_t, b_tok, dt, cum, c_t, y_part, alog_col, h0, y_pool, gate_ssd, x2d, gate,
                     ssd_norm_w, w_out_bf, final_norm_w, bsz, n_chunks):
    n_tok = b_tok.shape[0]
    cps = min(n_chunks, SSD_CHUNKS_PER_STEP)
    n_steps = n_chunks // cps
    tm = cps * CHUNK
    block_of = lambda b, s: b * n_steps + (n_steps - 1 - s)
    tok = lambda b, s: (block_of(b, s), 0)
    chunk3 = lambda b, s: (block_of(b, s), 0, 0)
    const2 = lambda b, s: (0, 0)
    per_seq = lambda b, s: (b, 0, 0)
    head_spec = pl.BlockSpec((cps, 2 * HEADS, CHUNK), chunk3)
    return pl.pallas_call(
        functools.partial(_bwd_out_kernel, cps=cps),
        grid=(bsz, n_steps),
        in_specs=[pl.BlockSpec((cps, W_SSD, CHUNK), chunk3),
                  pl.BlockSpec((tm, GN), tok),
                  head_spec, head_spec,
                  pl.BlockSpec((2 * HEADS, 1), const2),
                  pl.BlockSpec((1, W_SSD, D_STATE), per_seq),
                  pl.BlockSpec((cps, GN, CHUNK), chunk3),
                  pl.BlockSpec((cps, W_SSD, CHUNK), chunk3)] + [
                  pl.BlockSpec((tm, POOL_GROUP_W), tok)] * N_POOL_GROUPS + [
                  pl.BlockSpec((tm, W_SSD), tok),
                  pl.BlockSpec((tm, D_MODEL), tok),
                  pl.BlockSpec((1, 1, D_MODEL), per_seq),
                  pl.BlockSpec((1, W_SSD), const2),
                  pl.BlockSpec((W_POOL + W_SSD, D_MODEL), const2),
                  pl.BlockSpec((1, D_MODEL), const2)],
        out_specs=pl.BlockSpec((tm, D_MODEL), tok),
        out_shape=jax.ShapeDtypeStruct((n_tok, D_MODEL), jnp.float32),
        scratch_shapes=[pltpu.VMEM((W_SSD, D_STATE), jnp.float32),
                        pltpu.VMEM((tm, W_SSD), jnp.float32)],
        compiler_params=pltpu.CompilerParams(
            dimension_semantics=("arbitrary", "arbitrary"), vmem_limit_bytes=VMEM_LIMIT),
        name="bwd_out",
    )(xs_t, b_tok, dt, cum, alog_col, h0, c_t, y_part, *y_pool, gate_ssd, x2d, gate,
      ssd_norm_w.reshape(1, W_SSD), w_out_bf, final_norm_w.reshape(1, D_MODEL))


POOL_TILE_ROWS = 4
POOL_TILE = POOL_TILE_ROWS * GRID_W


def _pool_constants(window, n_rows):
    lo_off, hi_off = -(window // 2), window - window // 2
    col = np.arange(GRID_W)
    lo = np.clip(col + lo_off, 0, GRID_W)
    hi = np.clip(col + hi_off, 0, GRID_W)
    band = ((col[None, :] >= lo[:, None]) & (col[None, :] < hi[:, None])).astype(np.float32)
    band_tile = np.kron(np.eye(POOL_TILE_ROWS, dtype=np.float32), band)
    row = np.arange(n_rows)
    cnt_r = np.clip(row + hi_off, 0, n_rows) - np.clip(row + lo_off, 0, n_rows)
    inv = 1.0 / (cnt_r[:, None] * (hi - lo)[None, :]).astype(np.float64)
    inv = np.broadcast_to(inv.reshape(-1, 1), (n_rows * GRID_W, 128)).astype(np.float32)
    return jnp.asarray(band_tile, jnp.bfloat16), jnp.asarray(inv)


def _pool_kernel(u_ref, z_ref, band_ref, inv_ref, w_ref, scale_ref, o_ref, *, window, n_rows):
    def grid_row(r):
        return u_ref[r * GRID_W:(r + 1) * GRID_W].astype(jnp.float32)

    def bounds(r):
        return max(r - window // 2, 0), min(r + window - window // 2, n_rows)

    band = band_ref[...]
    rsum, tile_rows = None, []
    for r in range(n_rows):
        lo, hi = bounds(r)
        if r == 0 or window <= 2:
            rsum = grid_row(lo)
            for k in range(lo + 1, hi):
                rsum = rsum + grid_row(k)
        else:
            prev_lo, prev_hi = bounds(r - 1)
            if hi > prev_hi:
                rsum = rsum + grid_row(hi - 1)
            if lo > prev_lo:
                rsum = rsum - grid_row(prev_lo)
        tile_rows.append(rsum)
        if len(tile_rows) < POOL_TILE_ROWS:
            continue
        base = (r + 1 - POOL_TILE_ROWS) * GRID_W
        rows = slice(base, base + POOL_TILE)
        rs = jnp.concatenate(tile_rows, axis=0)
        tile_rows = []
        box = _dot(band, _bf(rs))
        inv = inv_ref[rows]
        mean = box * jnp.concatenate([inv, inv], axis=1)
        d = mean - u_ref[rows].astype(jnp.float32)
        y = _dot(_bf(d), w_ref[0]) * scale_ref[...]
        o_ref[rows] = _bf(y * z_ref[rows].astype(jnp.float32))


def _pool_group(u, gate, pool_w_bf, pool_scale, g, bsz, n_img_tok):
    window = POOL_WINDOWS[g]
    n_rows = n_img_tok // GRID_W
    band, inv = _pool_constants(window, n_rows)
    kern = functools.partial(_pool_kernel, window=window, n_rows=n_rows)
    img = pl.BlockSpec((n_img_tok, POOL_GROUP_W), lambda b: (b, 0))
    return pl.pallas_call(
        kern,
        grid=(bsz,),
        in_specs=[img, img,
                  pl.BlockSpec((POOL_TILE, POOL_TILE), lambda b: (0, 0)),
                  pl.BlockSpec((n_img_tok, 128), lambda b: (0, 0)),
                  pl.BlockSpec((1, POOL_GROUP_W, POOL_GROUP_W), lambda b: (g, 0, 0)),
                  pl.BlockSpec((1, POOL_GROUP_W), lambda b: (0, g))],
        out_specs=img,
        out_shape=jax.ShapeDtypeStruct((bsz * n_img_tok, POOL_GROUP_W), jnp.bfloat16),
        compiler_params=pltpu.CompilerParams(vmem_limit_bytes=VMEM_LIMIT),
        name=f"pool{window}",
    )(u, gate, band, inv, pool_w_bf, pool_scale)


def kernel(x, c, ctx, c_ctx, norm_w, w_ada, b_ada, w_in, conv_w, conv_b, a_log, dt_bias, d_skip,
           ssd_norm_w, pool_w, pool_scale, w_out, final_norm_w):
    bsz, seq, _ = x.shape
    ctx_len = ctx.shape[1]
    depth = norm_w.shape[0]
    assert depth == 1, "single-layer block: the context stream update is never consumed"
    assert seq % 512 == 0 and ctx_len % CHUNK == 0 and seq % GRID_W == 0

    mod_rows = -(-(bsz + 1) // SUBLANES) * SUBLANES
    cond = jnp.concatenate([c, c_ctx[None], jnp.zeros((mod_rows - bsz - 1, D_MODEL), c.dtype)])
    mod = _modulation(cond, w_ada[0], b_ada[0])
    shift = mod[:, :D_MODEL].reshape(mod_rows, 1, D_MODEL)
    scale = mod[:, D_MODEL:2 * D_MODEL].reshape(mod_rows, 1, D_MODEL)
    gate = mod[:, 2 * D_MODEL:].reshape(mod_rows, 1, D_MODEL)

    w_in_bf = _to_bf16(w_in[0])
    w_dt_bf = jnp.pad(_bf(w_in[0, :, OFF_DT:]), ((0, 0), (0, DT_PAD - 2 * HEADS)))
    alog_col = a_log[0].reshape(2 * HEADS, 1)
    bias_col = dt_bias[0].reshape(2 * HEADS, 1)
    dskip_b = jnp.broadcast_to(jnp.repeat(d_skip[0], HEADDIM)[:, None], (W_SSD, CHUNK))
    conv_b2 = conv_b[0].reshape(1, CONV_DIM)

    ctx2d = ctx.reshape(bsz * ctx_len, D_MODEL)
    h_fwd, h_bwd = _projection(
        ctx2d, norm_w[0], shift[bsz:bsz + 1], scale[bsz:bsz + 1], w_in_bf, w_dt_bf, conv_w[0],
        conv_b2, alog_col, bias_col, ctx_len, ctx_len, full=False)

    x2d = x.reshape(bsz * seq, D_MODEL)
    outs = _projection(x2d, norm_w[0], shift, scale, w_in_bf, w_dt_bf, conv_w[0], conv_b2,
                       alog_col, bias_col, seq, 512, full=True, dskip_b=dskip_b, h0=h_fwd)
    u_pool, gate_pool = outs[:N_POOL_GROUPS], outs[N_POOL_GROUPS:2 * N_POOL_GROUPS]
    gate_ssd, xs_t, b_tok, c_t, dt, cum, y_part = outs[2 * N_POOL_GROUPS:]
    nc = seq // CHUNK
    pool_w_bf = _bf(pool_w[0])
    y_pool = [_pool_group(u_pool[g], gate_pool[g], pool_w_bf, pool_scale, g, bsz, seq)
              for g in range(N_POOL_GROUPS)]
    out = _backward_output(xs_t, b_tok, dt, cum, c_t, y_part, alog_col, h_bwd, y_pool, gate_ssd,
                           x2d, gate, ssd_norm_w[0], _bf(w_out[0]), final_norm_w, bsz, nc)
    return out.reshape(bsz, seq, D_MODEL)
```

```python
import functools

import numpy as np
import jax
import jax.numpy as jnp
from jax.experimental import pallas as pl
from jax.experimental.pallas import tpu as pltpu

D_MODEL = 1024
GRID_W = 64
W_POOL = 1024
W_SSD = 1024
POOL_WINDOWS = (2, 4, 8, 16)
N_POOL_GROUPS = len(POOL_WINDOWS)
POOL_GROUP_W = 256
HEADDIM = 64
HEADS = 16
GROUPS = 4
HEADS_PER_GROUP = 4
D_STATE = 128
D_CONV = 4
CONV_LEFT = 2
CHUNK = 128
GN = GROUPS * D_STATE
CONV_DIM = W_SSD + 2 * GN
OFF_POOL_Z = W_POOL
OFF_SSD_Z = 2 * W_POOL
OFF_XBC = 2 * W_POOL + W_SSD
OFF_DT = OFF_XBC + CONV_DIM
DT_PAD = 128
EPS = 1e-6
SUBLANES = 8
LANES = 128
IL_GROUPS = CHUNK // SUBLANES
CONV_SEG = 256
SSD_CHUNKS_PER_STEP = 4
VMEM_LIMIT = 56 * 1024 * 1024


def _silu(v):
    h = 0.5 * v
    return h + h * jnp.tanh(h)


def _softplus(v):
    return jnp.maximum(v, 0.0) + jnp.log1p(jnp.exp(-jnp.abs(v)))


def _bf(v):
    return v.astype(jnp.bfloat16)


def _dot(a, b):
    return jnp.dot(a, b, preferred_element_type=jnp.float32)


def _mod_kernel(c_ref, w_ref, b_ref, o_ref):
    s = _silu(c_ref[...])
    o_ref[...] = jnp.dot(s, w_ref[...], preferred_element_type=jnp.float32,
                         precision=jax.lax.Precision.HIGHEST) + b_ref[...]


def _modulation(cond_rows, w_ada, b_ada):
    rows = cond_rows.shape[0]
    n_out = w_ada.shape[1]
    tn = 1024
    return pl.pallas_call(
        _mod_kernel,
        grid=(n_out // tn,),
        in_specs=[pl.BlockSpec((rows, D_MODEL), lambda j: (0, 0)),
                  pl.BlockSpec((D_MODEL, tn), lambda j: (0, j)),
                  pl.BlockSpec((1, tn), lambda j: (0, j))],
        out_specs=pl.BlockSpec((rows, tn), lambda j: (0, j)),
        out_shape=jax.ShapeDtypeStruct((rows, n_out), jnp.float32),
        compiler_params=pltpu.CompilerParams(vmem_limit_bytes=VMEM_LIMIT),
        name="mod",
    )(cond_rows, w_ada, b_ada.reshape(1, n_out))


def _lane_cumsum(v):
    lane = jax.lax.broadcasted_iota(jnp.int32, v.shape, 1)
    shift = 1
    while shift < CHUNK:
        v = v + jnp.where(lane >= shift, pltpu.roll(v, shift, 1), 0.0)
        shift *= 2
    return v


def _proj_kernel(x_ref, xp_ref, xn_ref, nw_ref, sh_ref, sc_ref, w_ref, wdt_ref, cw_ref, cb_ref,
                 alog_ref, bias_ref, *rest, tm, tiles_per_seq, n_tiles, full):
    if full:
        (dskip_ref, h0_ref, u0, u1, u2, u3, zp0, zp1, zp2, zp3, zs_ref, xs_t_ref, b_ref, c_t_ref,
         dt_ref, cum_ref, ypart_ref, pe_ref, xc_ref, mn_ref, kxs_ref, kb_ref, kc_ref, kdt_ref,
         kcum_ref, h_ref) = rest
        u_refs, zp_refs = (u0, u1, u2, u3), (zp0, zp1, zp2, zp3)
    else:
        hf_ref, hb_ref, pe_ref, xc_ref, mn_ref, xs_t_ref, b_ref, dt_ref, cum_ref, h_ref = rest
    i = pl.program_id(0)
    pos = jnp.minimum(i, n_tiles - 1) % tiles_per_seq
    has_prev = pos > 0
    has_next = pos < tiles_per_seq - 1
    n_chunks = tm // CHUNK
    seg = CONV_SEG

    if full:
        @pl.when(i == 0)
        def _():
            for ref in (kxs_ref, kb_ref, kc_ref, kdt_ref, kcum_ref, h_ref):
                ref[...] = jnp.zeros(ref.shape, ref.dtype)

    def modulated(v):
        ms = jnp.mean(v * v, axis=-1, keepdims=True)
        y = v * jax.lax.rsqrt(ms + EPS) * nw_ref[...]
        return y * (1.0 + sc_ref[0]) + sh_ref[0]

    m_tok = modulated(x_ref[...])
    hm = _bf(m_tok)
    for t in range(D_MODEL // LANES):
        mn_ref[t] = m_tok[:, t * LANES:(t + 1) * LANES]

    rows = [jnp.concatenate([mn_ref[t, pl.ds(q * CHUNK + b, SUBLANES, stride=IL_GROUPS), :]
                             for t in range(D_MODEL // LANES)], axis=1)
            for q in range(n_chunks) for b in range(IL_GROUPS)]
    halo = [jnp.where(has_prev, modulated(xp_ref[...]), 0.0),
            jnp.where(has_next, modulated(xn_ref[...]), 0.0)]
    hm_il = _bf(jnp.concatenate(halo + rows, axis=0))
    sub = jax.lax.broadcasted_iota(jnp.int32, (SUBLANES, seg), 0)

    def conv_stage(j, slot):
        is_x = j < W_SSD
        is_b = W_SSD <= j < W_SSD + GN

        def matmul():
            pe_ref[slot] = _dot(hm_il, w_ref[:, OFF_XBC + j:OFF_XBC + j + seg])

        def group(q, b):
            lo = 2 * SUBLANES + q * CHUNK + b * SUBLANES
            return pe_ref[slot, lo:lo + SUBLANES]

        def shifted(q, b, delta):
            bb = b + delta
            if 0 <= bb < IL_GROUPS:
                return group(q, bb)
            if bb < 0:
                bb += IL_GROUPS
                if q == 0:
                    first = pe_ref[slot, bb - SUBLANES:bb - SUBLANES + 1]
                else:
                    row = 2 * SUBLANES + (q - 1) * CHUNK + bb * SUBLANES + SUBLANES - 1
                    first = pe_ref[slot, row:row + 1]
                return jnp.where(sub == 0, first, pltpu.roll(group(q, bb), 1, 0))
            bb -= IL_GROUPS
            if q == n_chunks - 1:
                last = pe_ref[slot, SUBLANES + bb:SUBLANES + bb + 1]
            else:
                nxt = 2 * SUBLANES + (q + 1) * CHUNK + bb * SUBLANES
                last = pe_ref[slot, nxt:nxt + 1]
            return jnp.where(sub == SUBLANES - 1, last, pltpu.roll(group(q, bb), SUBLANES - 1, 0))

        def epilogue():
            taps = [cw_ref[k:k + 1, j:j + seg] for k in range(D_CONV)]
            bias = cb_ref[:, j:j + seg]
            for q in range(n_chunks):
                for b in range(IL_GROUPS):
                    acc = bias
                    for k in range(D_CONV):
                        acc = acc + shifted(q, b, k - CONV_LEFT) * taps[k]
                    lo = q * CHUNK + b * SUBLANES
                    act = _silu(acc)
                    for t in range(seg // LANES):
                        xc_ref[slot, t, lo:lo + SUBLANES] = act[:, t * LANES:(t + 1) * LANES]
            for q in range(n_chunks):
                xc = jnp.concatenate(
                    [jnp.concatenate(
                        [xc_ref[slot, t, pl.ds(q * CHUNK + (m % 2) * (CHUNK // 2) + m // 2,
                                               SUBLANES, stride=SUBLANES), :]
                         for t in range(seg // LANES)], axis=1)
                     for m in range(IL_GROUPS)], axis=0)
                if is_b:
                    b_ref[q * CHUNK:(q + 1) * CHUNK, j - W_SSD:j - W_SSD + seg] = _bf(xc)
                else:
                    dst, off = (xs_t_ref, j) if is_x else (c_t_ref, j - W_SSD - GN)
                    dst[q, off:off + seg] = _bf(xc.T)

        return matmul, epilogue

    def plain_stage(cols, finish):
        box = []
        return (lambda: box.append(_dot(hm, w_ref[:, cols]))), (lambda: finish(box.pop()))

    def dt_stage():
        box = []

        def epilogue():
            p_dt = box.pop()
            a_col = -jnp.exp(alog_ref[...])
            for q in range(n_chunks):
                dt = _softplus(p_dt[q * CHUNK:(q + 1) * CHUNK].T[:2 * HEADS] + bias_ref[...])
                dt_ref[q] = dt
                cum_ref[q] = _lane_cumsum(dt * a_col)

        return (lambda: box.append(_dot(hm, wdt_ref[...]))), epilogue

    def store_to(ref, cols=None, act=None):
        def finish(v):
            v = v if act is None else _bf(act(v))
            if cols is None:
                ref[...] = v
            else:
                ref[:, cols] = v
        return finish

    def store_pair(refs, act=None):
        def finish(v):
            for n, ref in enumerate(refs):
                part = v[:, n * POOL_GROUP_W:(n + 1) * POOL_GROUP_W]
                ref[...] = _bf(part if act is None else act(part))
        return finish

    stages = []
    if full:
        for g in range(0, N_POOL_GROUPS, 2):
            stages.append(plain_stage(slice(g * POOL_GROUP_W, (g + 2) * POOL_GROUP_W),
                                      store_pair(u_refs[g:g + 2])))
    stages.append(dt_stage())
    n_conv = 0
    for j in range(0, CONV_DIM, seg):
        if full or j < W_SSD + GN:
            stages.append(conv_stage(j, n_conv % 2))
            n_conv += 1
    if full:
        for g in range(0, N_POOL_GROUPS, 2):
            zcols = slice(OFF_POOL_Z + g * POOL_GROUP_W, OFF_POOL_Z + (g + 2) * POOL_GROUP_W)
            stages.append(plain_stage(zcols, store_pair(zp_refs[g:g + 2], act=_silu)))
        for j in range(0, W_SSD, seg):
            stages.append(plain_stage(slice(OFF_SSD_Z + j, OFF_SSD_Z + j + seg),
                                      store_to(zs_ref, cols=slice(j, j + seg), act=_silu)))

    sweep = []
    if full:
        swept = jnp.maximum(i - 1, 0)
        h_ref[...] = jnp.where(swept % tiles_per_seq == 0, h0_ref[0], h_ref[...])
        src = jax.lax.broadcasted_iota(jnp.int32, (CHUNK, CHUNK), 0)
        dst = jax.lax.broadcasted_iota(jnp.int32, (CHUNK, CHUNK), 1)
        a_b = -jnp.exp(alog_ref[HEADS:])
        pairs = []
        for q in range(n_chunks):
            pairs += _fwd_chunk_slices(q, kxs_ref, kb_ref, kc_ref, kdt_ref, kcum_ref, dskip_ref,
                                       ypart_ref, h_ref, a_b, src <= dst, src == dst)
        sweep = [pairs[0][0]]
        for p in range(len(pairs)):
            nxt = pairs[p + 1][0] if p + 1 < len(pairs) else (lambda: None)
            sweep.append(functools.partial(lambda a, w: (a(), w()), pairs[p][1], nxt))

    stages[0][0]()
    done = 0
    for k, (_, epilogue) in enumerate(stages):
        if k + 1 < len(stages):
            stages[k + 1][0]()
        epilogue()
        upto = -(-len(sweep) * (k + 1) // len(stages))
        for piece in sweep[done:upto]:
            piece()
        done = upto

    if full:
        for kept, ref in ((kxs_ref, xs_t_ref), (kb_ref, b_ref), (kc_ref, c_t_ref),
                          (kdt_ref, dt_ref), (kcum_ref, cum_ref)):
            kept[...] = ref[...]
    else:
        a_b = -jnp.exp(alog_ref[HEADS:])
        for reverse, out_ref in ((False, hf_ref), (True, hb_ref)):
            h_ref[...] = jnp.zeros(h_ref.shape, h_ref.dtype)
            for q in (range(n_chunks - 1, -1, -1) if reverse else range(n_chunks)):
                if reverse:
                    dt_b, cum_b = dt_ref[q, HEADS:], cum_ref[q, HEADS:]
                    scale_in = dt_b * jnp.exp(cum_b - dt_b * a_b)
                    decay = jnp.exp(cum_b[:, CHUNK - 1:CHUNK])
                else:
                    dt_f, cum_f = dt_ref[q, :HEADS], cum_ref[q, :HEADS]
                    tot_f = cum_f[:, CHUNK - 1:CHUNK]
                    scale_in, decay = dt_f * jnp.exp(tot_f - cum_f), jnp.exp(tot_f)
                _state_update(h_ref, xs_t_ref, b_ref, q, scale_in, decay)
            out_ref[0] = h_ref[...]


def _projection(x2d, norm_w, shift, scale, w_bf, wdt_bf, conv_w, conv_b, alog_col, bias_col,
                seq_len, tm, full, dskip_b=None, h0=None):
    n_tok = x2d.shape[0]
    tiles_per_seq = seq_len // tm
    n_tiles = n_tok // tm
    n_mod = shift.shape[0]
    nct = n_tok // CHUNK
    per = tm // SUBLANES
    last_halo = n_tok // SUBLANES - 1
    kern = functools.partial(_proj_kernel, tm=tm, tiles_per_seq=tiles_per_seq, n_tiles=n_tiles,
                             full=full)
    tile = lambda i: jnp.minimum(i, n_tiles - 1)
    mod_map = (lambda i: (tile(i) // tiles_per_seq, 0, 0)) if n_mod > 1 else (lambda i: (0, 0, 0))
    mod_spec = pl.BlockSpec((1, 1, D_MODEL), mod_map)
    const = lambda i: (0, 0)
    tok = lambda i: (tile(i), 0)
    chunk3 = lambda i: (tile(i), 0, 0)
    q = tm // CHUNK
    xs_t = (jax.ShapeDtypeStruct((nct, W_SSD, CHUNK), jnp.bfloat16),
            pl.BlockSpec((q, W_SSD, CHUNK), chunk3))
    b_tok = (jax.ShapeDtypeStruct((n_tok, GN), jnp.bfloat16), pl.BlockSpec((tm, GN), tok))
    c_t = (jax.ShapeDtypeStruct((nct, GN, CHUNK), jnp.bfloat16), pl.BlockSpec((q, GN, CHUNK), chunk3))
    dt = (jax.ShapeDtypeStruct((nct, 2 * HEADS, CHUNK), jnp.float32),
          pl.BlockSpec((q, 2 * HEADS, CHUNK), chunk3))
    if full:
        zp = (jax.ShapeDtypeStruct((n_tok, POOL_GROUP_W), jnp.bfloat16),
              pl.BlockSpec((tm, POOL_GROUP_W), tok))
        u = zp
        zs = (jax.ShapeDtypeStruct((n_tok, W_SSD), jnp.bfloat16), pl.BlockSpec((tm, W_SSD), tok))
        swept = lambda i: jnp.maximum(i - 1, 0)
        y_part = (jax.ShapeDtypeStruct((nct, W_SSD, CHUNK), jnp.bfloat16),
                  pl.BlockSpec((q, W_SSD, CHUNK), lambda i: (swept(i), 0, 0)))
        outs = [u] * N_POOL_GROUPS + [zp] * N_POOL_GROUPS + [zs, xs_t, b_tok, c_t, dt, dt, y_part]
        extra_in = [pl.BlockSpec((W_SSD, CHUNK), const),
                    pl.BlockSpec((1, W_SSD, D_STATE), lambda i: (swept(i) // tiles_per_seq, 0, 0))]
        extra_args = [dskip_b, h0]
        extra_scratch = [pltpu.VMEM((q, W_SSD, CHUNK), jnp.bfloat16),
                         pltpu.VMEM((tm, GN), jnp.bfloat16),
                         pltpu.VMEM((q, GN, CHUNK), jnp.bfloat16),
                         pltpu.VMEM((q, 2 * HEADS, CHUNK), jnp.float32),
                         pltpu.VMEM((q, 2 * HEADS, CHUNK), jnp.float32),
                         pltpu.VMEM((W_SSD, D_STATE), jnp.float32)]
    else:
        assert tiles_per_seq == 1, "prefix states are computed from one whole sequence per step"
        state = (jax.ShapeDtypeStruct((n_tiles, W_SSD, D_STATE), jnp.float32),
                 pl.BlockSpec((1, W_SSD, D_STATE), chunk3))
        outs = [state, state]
        extra_in, extra_args = [], []
        extra_scratch = [pltpu.VMEM((q, W_SSD, CHUNK), jnp.bfloat16),
                         pltpu.VMEM((tm, GN), jnp.bfloat16),
                         pltpu.VMEM((q, 2 * HEADS, CHUNK), jnp.float32),
                         pltpu.VMEM((q, 2 * HEADS, CHUNK), jnp.float32),
                         pltpu.VMEM((W_SSD, D_STATE), jnp.float32)]
    return pl.pallas_call(
        kern,
        grid=(n_tiles + 1 if full else n_tiles,),
        in_specs=[pl.BlockSpec((tm, D_MODEL), tok),
                  pl.BlockSpec((SUBLANES, D_MODEL), lambda i: (jnp.maximum(tile(i) * per - 1, 0), 0)),
                  pl.BlockSpec((SUBLANES, D_MODEL),
                               lambda i: (jnp.minimum((tile(i) + 1) * per, last_halo), 0)),
                  pl.BlockSpec((1, D_MODEL), const),
                  mod_spec, mod_spec,
                  pl.BlockSpec(w_bf.shape, const),
                  pl.BlockSpec((D_MODEL, DT_PAD), const),
                  pl.BlockSpec((D_CONV, CONV_DIM), const),
                  pl.BlockSpec((1, CONV_DIM), const),
                  pl.BlockSpec((2 * HEADS, 1), const),
                  pl.BlockSpec((2 * HEADS, 1), const)] + extra_in,
        out_specs=[o[1] for o in outs],
        out_shape=[o[0] for o in outs],
        scratch_shapes=[pltpu.VMEM((2, tm + 2 * SUBLANES, CONV_SEG), jnp.float32),
                        pltpu.VMEM((2, CONV_SEG // LANES, tm, LANES), jnp.float32),
                        pltpu.VMEM((D_MODEL // LANES, tm, LANES), jnp.float32)] + extra_scratch,
        compiler_params=pltpu.CompilerParams(
            dimension_semantics=("arbitrary",), vmem_limit_bytes=VMEM_LIMIT),
        name="proj" if full else "proj_ctx",
    )(x2d, x2d, x2d, norm_w.reshape(1, D_MODEL), shift, scale, w_bf, wdt_bf, conv_w, conv_b,
      alog_col, bias_col, *extra_args)


def _tok_rows(q):
    return pl.ds(q * CHUNK, CHUNK)


def _state_update_group(h_ref, xs_t_ref, b_ref, q, g, scale_in, chunk_decay):
    bg = b_ref[_tok_rows(q), g * D_STATE:(g + 1) * D_STATE]
    xd = []
    for r in range(HEADS_PER_GROUP):
        h = g * HEADS_PER_GROUP + r
        x_h = xs_t_ref[q, h * HEADDIM:(h + 1) * HEADDIM].astype(jnp.float32)
        xd.append(_bf(x_h * scale_in[h:h + 1]))
    s_new = _dot(jnp.concatenate(xd, axis=0), bg)
    for r in range(HEADS_PER_GROUP):
        h = g * HEADS_PER_GROUP + r
        hr = slice(h * HEADDIM, (h + 1) * HEADDIM)
        h_ref[hr] = h_ref[hr] * chunk_decay[h:h + 1] + s_new[r * HEADDIM:(r + 1) * HEADDIM]


def _state_update(h_ref, xs_t_ref, b_ref, q, scale_in, chunk_decay):
    for g in range(GROUPS):
        _state_update_group(h_ref, xs_t_ref, b_ref, q, g, scale_in, chunk_decay)


def _fwd_chunk_slices(q, xs_t_ref, b_ref, c_t_ref, dt_ref, cum_ref, dskip_ref, y_ref, h_ref, a_b,
                      causal, is_diag):
    ctx = {}

    def setup():
        dt_f, cum_f = dt_ref[q, :HEADS], cum_ref[q, :HEADS]
        tot_f = cum_f[:, CHUNK - 1:CHUNK]
        dt_b, cum_b = dt_ref[q, HEADS:], cum_ref[q, HEADS:]
        cumx_b = cum_b - dt_b * a_b
        ctx["scale_in"] = dt_f * jnp.exp(tot_f - cum_f)
        ctx["chunk_decay"] = jnp.exp(tot_f)
        ctx["col_terms"] = jnp.concatenate(
            [jnp.log(dt_f) - cum_f, jnp.log(dt_b) + cumx_b,
             jnp.zeros((CHUNK - 2 * HEADS, CHUNK), jnp.float32)], axis=0).T
        ctx["row_f"], ctx["row_b"] = cum_f, -cumx_b
        ctx["decay_out_f"] = jnp.exp(cum_f)
        ctx["dt_b"] = dt_b

    def weights(g):
        if g == 0:
            setup()
        col_terms, row_f, row_b = ctx["col_terms"], ctx["row_f"], ctx["row_b"]
        bg = b_ref[_tok_rows(q), g * D_STATE:(g + 1) * D_STATE]
        cg_t = c_t_ref[q, g * D_STATE:(g + 1) * D_STATE]
        rows = slice(g * HEADS_PER_GROUP * HEADDIM, (g + 1) * HEADS_PER_GROUP * HEADDIM)
        g_t = _dot(bg, cg_t)
        ctx["g_diag", g] = jnp.sum(jnp.where(is_diag, g_t, 0.0), axis=0, keepdims=True)
        ctx["y_off", g] = _dot(_bf(h_ref[rows]), cg_t)
        for r in range(HEADS_PER_GROUP):
            h = g * HEADS_PER_GROUP + r
            col_f = jnp.broadcast_to(col_terms[:, h:h + 1], (CHUNK, CHUNK))
            col_b = jnp.broadcast_to(col_terms[:, HEADS + h:HEADS + h + 1], (CHUNK, CHUNK))
            expo = jnp.where(causal, col_f + row_f[h:h + 1], col_b + row_b[h:h + 1])
            ctx["w_t", h] = _bf(g_t * jnp.exp(expo))

    def apply(g):
        y_off, g_diag = ctx.pop(("y_off", g)), ctx.pop(("g_diag", g))
        for r in range(HEADS_PER_GROUP):
            h = g * HEADS_PER_GROUP + r
            hr = slice(h * HEADDIM, (h + 1) * HEADDIM)
            x_bf = xs_t_ref[q, hr]
            y_h = _dot(x_bf, ctx.pop(("w_t", h)))
            y_h = y_h + y_off[r * HEADDIM:(r + 1) * HEADDIM] * ctx["decay_out_f"][h:h + 1]
            skip = dskip_ref[hr] + g_diag * ctx["dt_b"][h:h + 1]
            y_ref[q, hr] = _bf(y_h + skip * x_bf.astype(jnp.float32))
        _state_update_group(h_ref, xs_t_ref, b_ref, q, g, ctx["scale_in"], ctx["chunk_decay"])

    return [(functools.partial(weights, g), functools.partial(apply, g)) for g in range(GROUPS)]


def _bwd_out_kernel(xs_t_ref, b_ref, dt_ref, cum_ref, alog_ref, h0_ref, c_t_ref, ypart_ref,
                    yp0_ref, yp1_ref, yp2_ref, yp3_ref, zs_ref, x_ref, gate_ref, snw_ref, wout_ref,
                    fnw_ref, o_ref, h_ref, y_ref, *, cps):
    @pl.when(pl.program_id(1) == 0)
    def _():
        h_ref[...] = h0_ref[0]

    a_b = -jnp.exp(alog_ref[HEADS:])

    def chunk(q):
        dt_b, cum_b = dt_ref[q, HEADS:], cum_ref[q, HEADS:]
        tot_b = cum_b[:, CHUNK - 1:CHUNK]
        cumx_b = cum_b - dt_b * a_b
        decay_out = jnp.exp(tot_b - cumx_b)
        y_parts = []
        for g in range(GROUPS):
            cg_t = c_t_ref[q, g * D_STATE:(g + 1) * D_STATE]
            rows = slice(g * HEADS_PER_GROUP * HEADDIM, (g + 1) * HEADS_PER_GROUP * HEADDIM)
            y_off = _dot(_bf(h_ref[rows]), cg_t)
            for r in range(HEADS_PER_GROUP):
                h = g * HEADS_PER_GROUP + r
                hr = slice(h * HEADDIM, (h + 1) * HEADDIM)
                y_parts.append(ypart_ref[q, hr].astype(jnp.float32)
                               + y_off[r * HEADDIM:(r + 1) * HEADDIM] * decay_out[h:h + 1])
        y_ref[_tok_rows(q), :] = jnp.concatenate(y_parts, axis=0).T
        _state_update(h_ref, xs_t_ref, b_ref, q, dt_b * jnp.exp(cumx_b), jnp.exp(tot_b))

    yp_refs = [yp0_ref, yp1_ref, yp2_ref, yp3_ref]
    acc = jnp.zeros((cps * CHUNK, D_MODEL), jnp.float32)
    for i in range(max(cps, N_POOL_GROUPS)):
        if i < cps:
            chunk(cps - 1 - i)
        if i < N_POOL_GROUPS:
            acc = acc + _dot(yp_refs[i][...], wout_ref[i * POOL_GROUP_W:(i + 1) * POOL_GROUP_W])

    gw = W_SSD // GROUPS
    for g in range(GROUPS):
        cols = slice(g * gw, (g + 1) * gw)
        gated = y_ref[:, cols] * zs_ref[:, cols].astype(jnp.float32)
        ms = jnp.mean(gated * gated, axis=-1, keepdims=True)
        yn = gated * jax.lax.rsqrt(ms + EPS) * snw_ref[:, cols]
        acc = acc + _dot(_bf(yn), wout_ref[W_POOL + g * gw:W_POOL + (g + 1) * gw])
    hres = x_ref[...] + gate_ref[0] * acc
    ms = jnp.mean(hres * hres, axis=-1, keepdims=True)
    o_ref[...] = hres * jax.lax.rsqrt(ms + EPS) * fnw_ref[...]


def _backward_output(xs_t, b_tok, dt, cum, c_t, y_part, alog_col, h0, y_pool, gate_ssd, x2d, gate,
                     ssd_norm_w, w_out_bf, final_norm_w, bsz, n_chunks):
    n_tok = b_tok.shape[0]
    cps = min(n_chunks, SSD_CHUNKS_PER_STEP)
    n_steps = n_chunks // cps
    tm = cps * CHUNK
    block_of = lambda b, s: b * n_steps + (n_steps - 1 - s)
    tok = lambda b, s: (block_of(b, s), 0)
    chunk3 = lambda b, s: (block_of(b, s), 0, 0)
    const2 = lambda b, s: (0, 0)
    per_seq = lambda b, s: (b, 0, 0)
    head_spec = pl.BlockSpec((cps, 2 * HEADS, CHUNK), chunk3)
    return pl.pallas_call(
        functools.partial(_bwd_out_kernel, cps=cps),
        grid=(bsz, n_steps),
        in_specs=[pl.BlockSpec((cps, W_SSD, CHUNK), chunk3),
                  pl.BlockSpec((tm, GN), tok),
                  head_spec, head_spec,
                  pl.BlockSpec((2 * HEADS, 1), const2),
                  pl.BlockSpec((1, W_SSD, D_STATE), per_seq),
                  pl.BlockSpec((cps, GN, CHUNK), chunk3),
                  pl.BlockSpec((cps, W_SSD, CHUNK), chunk3)] + [
                  pl.BlockSpec((tm, POOL_GROUP_W), tok)] * N_POOL_GROUPS + [
                  pl.BlockSpec((tm, W_SSD), tok),
                  pl.BlockSpec((tm, D_MODEL), tok),
                  pl.BlockSpec((1, 1, D_MODEL), per_seq),
                  pl.BlockSpec((1, W_SSD), const2),
                  pl.BlockSpec((W_POOL + W_SSD, D_MODEL), const2),
                  pl.BlockSpec((1, D_MODEL), const2)],
        out_specs=pl.BlockSpec((tm, D_MODEL), tok),
        out_shape=jax.ShapeDtypeStruct((n_tok, D_MODEL), jnp.float32),
        scratch_shapes=[pltpu.VMEM((W_SSD, D_STATE), jnp.float32),
                        pltpu.VMEM((tm, W_SSD), jnp.float32)],
        compiler_params=pltpu.CompilerParams(
            dimension_semantics=("arbitrary", "arbitrary"), vmem_limit_bytes=VMEM_LIMIT),
        name="bwd_out",
    )(xs_t, b_tok, dt, cum, alog_col, h0, c_t, y_part, *y_pool, gate_ssd, x2d, gate,
      ssd_norm_w.reshape(1, W_SSD), w_out_bf, final_norm_w.reshape(1, D_MODEL))


POOL_TILE_ROWS = 4
POOL_TILE = POOL_TILE_ROWS * GRID_W


def _pool_constants(window, n_rows):
    lo_off, hi_off = -(window // 2), window - window // 2
    col = np.arange(GRID_W)
    lo = np.clip(col + lo_off, 0, GRID_W)
    hi = np.clip(col + hi_off, 0, GRID_W)
    band = ((col[None, :] >= lo[:, None]) & (col[None, :] < hi[:, None])).astype(np.float32)
    band_tile = np.kron(np.eye(POOL_TILE_ROWS, dtype=np.float32), band)
    row = np.arange(n_rows)
    cnt_r = np.clip(row + hi_off, 0, n_rows) - np.clip(row + lo_off, 0, n_rows)
    inv = 1.0 / (cnt_r[:, None] * (hi - lo)[None, :]).astype(np.float64)
    inv = np.broadcast_to(inv.reshape(-1, 1), (n_rows * GRID_W, 128)).astype(np.float32)
    return jnp.asarray(band_tile, jnp.bfloat16), jnp.asarray(inv)


def _pool_kernel(u_ref, z_ref, band_ref, inv_ref, w_ref, scale_ref, o_ref, *, window, n_rows):
    def grid_row(r):
        return u_ref[r * GRID_W:(r + 1) * GRID_W].astype(jnp.float32)

    def bounds(r):
        return max(r - window // 2, 0), min(r + window - window // 2, n_rows)

    band = band_ref[...]
    rsum, tile_rows = None, []
    for r in range(n_rows):
        lo, hi = bounds(r)
        if r == 0 or window <= 2:
            rsum = grid_row(lo)
            for k in range(lo + 1, hi):
                rsum = rsum + grid_row(k)
        else:
            prev_lo, prev_hi = bounds(r - 1)
            if hi > prev_hi:
                rsum = rsum + grid_row(hi - 1)
            if lo > prev_lo:
                rsum = rsum - grid_row(prev_lo)
        tile_rows.append(rsum)
        if len(tile_rows) < POOL_TILE_ROWS:
            continue
        base = (r + 1 - POOL_TILE_ROWS) * GRID_W
        rows = slice(base, base + POOL_TILE)
        rs = jnp.concatenate(tile_rows, axis=0)
        tile_rows = []
        box = _dot(band, _bf(rs))
        inv = inv_ref[rows]
        mean = box * jnp.concatenate([inv, inv], axis=1)
        d = mean - u_ref[rows].astype(jnp.float32)
        y = _dot(_bf(d), w_ref[0]) * scale_ref[...]
        o_ref[rows] = _bf(y * z_ref[rows].astype(jnp.float32))


def _pool_group(u, gate, pool_w_bf, pool_scale, g, bsz, n_img_tok):
    window = POOL_WINDOWS[g]
    n_rows = n_img_tok // GRID_W
    band, inv = _pool_constants(window, n_rows)
    kern = functools.partial(_pool_kernel, window=window, n_rows=n_rows)
    img = pl.BlockSpec((n_img_tok, POOL_GROUP_W), lambda b: (b, 0))
    return pl.pallas_call(
        kern,
        grid=(bsz,),
        in_specs=[img, img,
                  pl.BlockSpec((POOL_TILE, POOL_TILE), lambda b: (0, 0)),
                  pl.BlockSpec((n_img_tok, 128), lambda b: (0, 0)),
                  pl.BlockSpec((1, POOL_GROUP_W, POOL_GROUP_W), lambda b: (g, 0, 0)),
                  pl.BlockSpec((1, POOL_GROUP_W), lambda b: (0, g))],
        out_specs=img,
        out_shape=jax.ShapeDtypeStruct((bsz * n_img_tok, POOL_GROUP_W), jnp.bfloat16),
        compiler_params=pltpu.CompilerParams(vmem_limit_bytes=VMEM_LIMIT),
        name=f"pool{window}",
    )(u, gate, band, inv, pool_w_bf, pool_scale)


def kernel(x, c, ctx, c_ctx, norm_w, w_ada, b_ada, w_in, conv_w, conv_b, a_log, dt_bias, d_skip,
           ssd_norm_w, pool_w, pool_scale, w_out, final_norm_w):
    bsz, seq, _ = x.shape
    ctx_len = ctx.shape[1]
    depth = norm_w.shape[0]
    assert depth == 1, "single-layer block: the context stream update is never consumed"
    assert seq % 512 == 0 and ctx_len % CHUNK == 0 and seq % GRID_W == 0

    mod_rows = -(-(bsz + 1) // SUBLANES) * SUBLANES
    cond = jnp.concatenate([c, c_ctx[None], jnp.zeros((mod_rows - bsz - 1, D_MODEL), c.dtype)])
    mod = _modulation(cond, w_ada[0], b_ada[0])
    shift = mod[:, :D_MODEL].reshape(mod_rows, 1, D_MODEL)
    scale = mod[:, D_MODEL:2 * D_MODEL].reshape(mod_rows, 1, D_MODEL)
    gate = mod[:, 2 * D_MODEL:].reshape(mod_rows, 1, D_MODEL)

    w_in_bf = _bf(w_in[0])
    w_dt_bf = jnp.pad(_bf(w_in[0, :, OFF_DT:]), ((0, 0), (0, DT_PAD - 2 * HEADS)))
    alog_col = a_log[0].reshape(2 * HEADS, 1)
    bias_col = dt_bias[0].reshape(2 * HEADS, 1)
    dskip_b = jnp.broadcast_to(jnp.repeat(d_skip[0], HEADDIM)[:, None], (W_SSD, CHUNK))
    conv_b2 = conv_b[0].reshape(1, CONV_DIM)

    ctx2d = ctx.reshape(bsz * ctx_len, D_MODEL)
    h_fwd, h_bwd = _projection(
        ctx2d, norm_w[0], shift[bsz:bsz + 1], scale[bsz:bsz + 1], w_in_bf, w_dt_bf, conv_w[0],
        conv_b2, alog_col, bias_col, ctx_len, ctx_len, full=False)

    x2d = x.reshape(bsz * seq, D_MODEL)
    outs = _projection(x2d, norm_w[0], shift, scale, w_in_bf, w_dt_bf, conv_w[0], conv_b2,
                       alog_col, bias_col, seq, 512, full=True, dskip_b=dskip_b, h0=h_fwd)
    u_pool, gate_pool = outs[:N_POOL_GROUPS], outs[N_POOL_GROUPS:2 * N_POOL_GROUPS]
    gate_ssd, xs_t, b_tok, c_t, dt, cum, y_part = outs[2 * N_POOL_GROUPS:]
    nc = seq // CHUNK
    pool_w_bf = _bf(pool_w[0])
    y_pool = [_pool_group(u_pool[g], gate_pool[g], pool_w_bf, pool_scale, g, bsz, seq)
              for g in range(N_POOL_GROUPS)]
    out = _backward_output(xs_t, b_tok, dt, cum, c_t, y_part, alog_col, h_bwd, y_pool, gate_ssd,
                           x2d, gate, ssd_norm_w[0], _bf(w_out[0]), final_norm_w, bsz, nc)
    return out.reshape(bsz, seq, D_MODEL)
```

```python
import functools

import numpy as np
import jax
import jax.numpy as jnp
from jax.experimental import pallas as pl
from jax.experimental.pallas import tpu as pltpu

D_MODEL = 1024
GRID_W = 64
W_POOL = 1024
W_SSD = 1024
POOL_WINDOWS = (2, 4, 8, 16)
N_POOL_GROUPS = len(POOL_WINDOWS)
POOL_GROUP_W = 256
HEADDIM = 64
HEADS = 16
GROUPS = 4
HEADS_PER_GROUP = 4
D_STATE = 128
D_CONV = 4
CONV_LEFT = 2
CHUNK = 128
GN = GROUPS * D_STATE
CONV_DIM = W_SSD + 2 * GN
OFF_POOL_Z = W_POOL
OFF_SSD_Z = 2 * W_POOL
OFF_XBC = 2 * W_POOL + W_SSD
OFF_DT = OFF_XBC + CONV_DIM
DT_PAD = 128
EPS = 1e-6
SUBLANES = 8
LANES = 128
IL_GROUPS = CHUNK // SUBLANES
CONV_SEG = 256
SSD_CHUNKS_PER_STEP = 4
VMEM_LIMIT = 56 * 1024 * 1024


def _silu(v):
    h = 0.5 * v
    return h + h * jnp.tanh(h)


def _softplus(v):
    return jnp.maximum(v, 0.0) + jnp.log1p(jnp.exp(-jnp.abs(v)))


def _bf(v):
    return v.astype(jnp.bfloat16)


def _dot(a, b):
    return jnp.dot(a, b, preferred_element_type=jnp.float32)


def _mod_kernel(c_ref, w_ref, b_ref, o_ref):
    s = _silu(c_ref[...])
    o_ref[...] = jnp.dot(s, w_ref[...], preferred_element_type=jnp.float32,
                         precision=jax.lax.Precision.HIGHEST) + b_ref[...]


def _modulation(cond_rows, w_ada, b_ada):
    rows = cond_rows.shape[0]
    n_out = w_ada.shape[1]
    tn = 1024
    return pl.pallas_call(
        _mod_kernel,
        grid=(n_out // tn,),
        in_specs=[pl.BlockSpec((rows, D_MODEL), lambda j: (0, 0)),
                  pl.BlockSpec((D_MODEL, tn), lambda j: (0, j)),
                  pl.BlockSpec((1, tn), lambda j: (0, j))],
        out_specs=pl.BlockSpec((rows, tn), lambda j: (0, j)),
        out_shape=jax.ShapeDtypeStruct((rows, n_out), jnp.float32),
        compiler_params=pltpu.CompilerParams(vmem_limit_bytes=VMEM_LIMIT),
        name="mod",
    )(cond_rows, w_ada, b_ada.reshape(1, n_out))


def _lane_cumsum(v):
    lane = jax.lax.broadcasted_iota(jnp.int32, v.shape, 1)
    shift = 1
    while shift < CHUNK:
        v = v + jnp.where(lane >= shift, pltpu.roll(v, shift, 1), 0.0)
        shift *= 2
    return v


def _proj_kernel(x_ref, xp_ref, xn_ref, nw_ref, sh_ref, sc_ref, w_ref, wdt_ref, cw_ref, cb_ref,
                 alog_ref, bias_ref, *rest, tm, tiles_per_seq, n_tiles, full):
    if full:
        (dskip_ref, h0_ref, u0, u1, u2, u3, zp0, zp1, zp2, zp3, zs_ref, xs_t_ref, b_ref, c_t_ref,
         dt_ref, cum_ref, ypart_ref, pe_ref, xc_ref, mn_ref, kxs_ref, kb_ref, kc_ref, kdt_ref,
         kcum_ref, h_ref) = rest
        u_refs, zp_refs = (u0, u1, u2, u3), (zp0, zp1, zp2, zp3)
    else:
        hf_ref, hb_ref, pe_ref, xc_ref, mn_ref, xs_t_ref, b_ref, dt_ref, cum_ref, h_ref = rest
    i = pl.program_id(0)
    pos = jnp.minimum(i, n_tiles - 1) % tiles_per_seq
    has_prev = pos > 0
    has_next = pos < tiles_per_seq - 1
    n_chunks = tm // CHUNK
    seg = CONV_SEG

    if full:
        @pl.when(i == 0)
        def _():
            for ref in (kxs_ref, kb_ref, kc_ref, kdt_ref, kcum_ref, h_ref):
                ref[...] = jnp.zeros(ref.shape, ref.dtype)

    def modulated(v):
        ms = jnp.mean(v * v, axis=-1, keepdims=True)
        y = v * jax.lax.rsqrt(ms + EPS) * nw_ref[...]
        return y * (1.0 + sc_ref[0]) + sh_ref[0]

    m_tok = modulated(x_ref[...])
    hm = _bf(m_tok)
    for t in range(D_MODEL // LANES):
        mn_ref[t] = m_tok[:, t * LANES:(t + 1) * LANES]

    rows = [jnp.concatenate([mn_ref[t, pl.ds(q * CHUNK + b, SUBLANES, stride=IL_GROUPS), :]
                             for t in range(D_MODEL // LANES)], axis=1)
            for q in range(n_chunks) for b in range(IL_GROUPS)]
    halo = [jnp.where(has_prev, modulated(xp_ref[...]), 0.0),
            jnp.where(has_next, modulated(xn_ref[...]), 0.0)]
    hm_il = _bf(jnp.concatenate(halo + rows, axis=0))
    sub = jax.lax.broadcasted_iota(jnp.int32, (SUBLANES, seg), 0)

    def conv_stage(j, slot):
        is_x = j < W_SSD
        is_b = W_SSD <= j < W_SSD + GN

        def matmul():
            pe_ref[slot] = _dot(hm_il, w_ref[:, OFF_XBC + j:OFF_XBC + j + seg])

        def group(q, b):
            lo = 2 * SUBLANES + q * CHUNK + b * SUBLANES
            return pe_ref[slot, lo:lo + SUBLANES]

        def shifted(q, b, delta):
            bb = b + delta
            if 0 <= bb < IL_GROUPS:
                return group(q, bb)
            if bb < 0:
                bb += IL_GROUPS
                if q == 0:
                    first = pe_ref[slot, bb - SUBLANES:bb - SUBLANES + 1]
                else:
                    row = 2 * SUBLANES + (q - 1) * CHUNK + bb * SUBLANES + SUBLANES - 1
                    first = pe_ref[slot, row:row + 1]
                return jnp.where(sub == 0, first, pltpu.roll(group(q, bb), 1, 0))
            bb -= IL_GROUPS
            if q == n_chunks - 1:
                last = pe_ref[slot, SUBLANES + bb:SUBLANES + bb + 1]
            else:
                nxt = 2 * SUBLANES + (q + 1) * CHUNK + bb * SUBLANES
                last = pe_ref[slot, nxt:nxt + 1]
            return jnp.where(sub == SUBLANES - 1, last, pltpu.roll(group(q, bb), SUBLANES - 1, 0))

        def epilogue():
            taps = [cw_ref[k:k + 1, j:j + seg] for k in range(D_CONV)]
            bias = cb_ref[:, j:j + seg]
            for q in range(n_chunks):
                for b in range(IL_GROUPS):
                    acc = bias
                    for k in range(D_CONV):
                        acc = acc + shifted(q, b, k - CONV_LEFT) * taps[k]
                    lo = q * CHUNK + b * SUBLANES
                    act = _silu(acc)
                    for t in range(seg // LANES):
                        xc_ref[slot, t, lo:lo + SUBLANES] = act[:, t * LANES:(t + 1) * LANES]
            for q in range(n_chunks):
                xc = jnp.concatenate(
                    [jnp.concatenate(
                        [xc_ref[slot, t, pl.ds(q * CHUNK + (m % 2) * (CHUNK // 2) + m // 2,
                                               SUBLANES, stride=SUBLANES), :]
                         for t in range(seg // LANES)], axis=1)
                     for m in range(IL_GROUPS)], axis=0)
                if is_b:
                    b_ref[q * CHUNK:(q + 1) * CHUNK, j - W_SSD:j - W_SSD + seg] = _bf(xc)
                else:
                    dst, off = (xs_t_ref, j) if is_x else (c_t_ref, j - W_SSD - GN)
                    dst[q, off:off + seg] = _bf(xc.T)

        return matmul, epilogue

    def plain_stage(cols, finish):
        box = []
        return (lambda: box.append(_dot(hm, w_ref[:, cols]))), (lambda: finish(box.pop()))

    def dt_stage():
        box = []

        def epilogue():
            p_dt = box.pop()
            a_col = -jnp.exp(alog_ref[...])
            for q in range(n_chunks):
                dt = _softplus(p_dt[q * CHUNK:(q + 1) * CHUNK].T[:2 * HEADS] + bias_ref[...])
                dt_ref[q] = dt
                cum_ref[q] = _lane_cumsum(dt * a_col)

        return (lambda: box.append(_dot(hm, wdt_ref[...]))), epilogue

    def store_to(ref, cols=None, act=None):
        def finish(v):
            v = v if act is None else _bf(act(v))
            if cols is None:
                ref[...] = v
            else:
                ref[:, cols] = v
        return finish

    def store_pair(refs, act=None):
        def finish(v):
            for n, ref in enumerate(refs):
                part = v[:, n * POOL_GROUP_W:(n + 1) * POOL_GROUP_W]
                ref[...] = _bf(part if act is None else act(part))
        return finish

    light, heavy = [], []
    if full:
        for g in range(0, N_POOL_GROUPS, 2):
            light.append(plain_stage(slice(g * POOL_GROUP_W, (g + 2) * POOL_GROUP_W),
                                     store_pair(u_refs[g:g + 2])))
    light.append(dt_stage())
    for j in range(0, CONV_DIM, seg):
        if full or j < W_SSD + GN:
            heavy.append(conv_stage(j, len(heavy) % 2))
    if full:
        for g in range(0, N_POOL_GROUPS, 2):
            zcols = slice(OFF_POOL_Z + g * POOL_GROUP_W, OFF_POOL_Z + (g + 2) * POOL_GROUP_W)
            light.append(plain_stage(zcols, store_pair(zp_refs[g:g + 2], act=_silu)))
        for j in range(0, W_SSD, seg):
            light.append(plain_stage(slice(OFF_SSD_Z + j, OFF_SSD_Z + j + seg),
                                     store_to(zs_ref, cols=slice(j, j + seg), act=_silu)))
    stages = []
    while light or heavy:
        if light:
            stages.append(light.pop(0))
        if heavy:
            stages.append(heavy.pop(0))

    sweep = []
    if full:
        swept = jnp.maximum(i - 1, 0)
        h_ref[...] = jnp.where(swept % tiles_per_seq == 0, h0_ref[0], h_ref[...])
        src = jax.lax.broadcasted_iota(jnp.int32, (CHUNK, CHUNK), 0)
        dst = jax.lax.broadcasted_iota(jnp.int32, (CHUNK, CHUNK), 1)
        a_b = -jnp.exp(alog_ref[HEADS:])
        pairs = []
        for q in range(n_chunks):
            pairs += _fwd_chunk_slices(q, kxs_ref, kb_ref, kc_ref, kdt_ref, kcum_ref, dskip_ref,
                                       ypart_ref, h_ref, a_b, src <= dst, src == dst)
        sweep = [pairs[0][0]]
        for p in range(len(pairs)):
            nxt = pairs[p + 1][0] if p + 1 < len(pairs) else (lambda: None)
            sweep.append(functools.partial(lambda a, w: (a(), w()), pairs[p][1], nxt))

    stages[0][0]()
    done = 0
    for k, (_, epilogue) in enumerate(stages):
        if k + 1 < len(stages):
            stages[k + 1][0]()
        epilogue()
        upto = -(-len(sweep) * (k + 1) // len(stages))
        for piece in sweep[done:upto]:
            piece()
        done = upto

    if full:
        for kept, ref in ((kxs_ref, xs_t_ref), (kb_ref, b_ref), (kc_ref, c_t_ref),
                          (kdt_ref, dt_ref), (kcum_ref, cum_ref)):
            kept[...] = ref[...]
    else:
        a_b = -jnp.exp(alog_ref[HEADS:])
        for reverse, out_ref in ((False, hf_ref), (True, hb_ref)):
            h_ref[...] = jnp.zeros(h_ref.shape, h_ref.dtype)
            for q in (range(n_chunks - 1, -1, -1) if reverse else range(n_chunks)):
                if reverse:
                    dt_b, cum_b = dt_ref[q, HEADS:], cum_ref[q, HEADS:]
                    scale_in = dt_b * jnp.exp(cum_b - dt_b * a_b)
                    decay = jnp.exp(cum_b[:, CHUNK - 1:CHUNK])
                else:
                    dt_f, cum_f = dt_ref[q, :HEADS], cum_ref[q, :HEADS]
                    tot_f = cum_f[:, CHUNK - 1:CHUNK]
                    scale_in, decay = dt_f * jnp.exp(tot_f - cum_f), jnp.exp(tot_f)
                _state_update(h_ref, xs_t_ref, b_ref, q, scale_in, decay)
            out_ref[0] = h_ref[...]


def _projection(x2d, norm_w, shift, scale, w_bf, wdt_bf, conv_w, conv_b, alog_col, bias_col,
                seq_len, tm, full, dskip_b=None, h0=None):
    n_tok = x2d.shape[0]
    tiles_per_seq = seq_len // tm
    n_tiles = n_tok // tm
    n_mod = shift.shape[0]
    nct = n_tok // CHUNK
    per = tm // SUBLANES
    last_halo = n_tok // SUBLANES - 1
    kern = functools.partial(_proj_kernel, tm=tm, tiles_per_seq=tiles_per_seq, n_tiles=n_tiles,
                             full=full)
    tile = lambda i: jnp.minimum(i, n_tiles - 1)
    mod_map = (lambda i: (tile(i) // tiles_per_seq, 0, 0)) if n_mod > 1 else (lambda i: (0, 0, 0))
    mod_spec = pl.BlockSpec((1, 1, D_MODEL), mod_map)
    const = lambda i: (0, 0)
    tok = lambda i: (tile(i), 0)
    chunk3 = lambda i: (tile(i), 0, 0)
    q = tm // CHUNK
    xs_t = (jax.ShapeDtypeStruct((nct, W_SSD, CHUNK), jnp.bfloat16),
            pl.BlockSpec((q, W_SSD, CHUNK), chunk3))
    b_tok = (jax.ShapeDtypeStruct((n_tok, GN), jnp.bfloat16), pl.BlockSpec((tm, GN), tok))
    c_t = (jax.ShapeDtypeStruct((nct, GN, CHUNK), jnp.bfloat16), pl.BlockSpec((q, GN, CHUNK), chunk3))
    dt = (jax.ShapeDtypeStruct((nct, 2 * HEADS, CHUNK), jnp.float32),
          pl.BlockSpec((q, 2 * HEADS, CHUNK), chunk3))
    if full:
        zp = (jax.ShapeDtypeStruct((n_tok, POOL_GROUP_W), jnp.bfloat16),
              pl.BlockSpec((tm, POOL_GROUP_W), tok))
        u = zp
        zs = (jax.ShapeDtypeStruct((n_tok, W_SSD), jnp.bfloat16), pl.BlockSpec((tm, W_SSD), tok))
        swept = lambda i: jnp.maximum(i - 1, 0)
        y_part = (jax.ShapeDtypeStruct((nct, W_SSD, CHUNK), jnp.bfloat16),
                  pl.BlockSpec((q, W_SSD, CHUNK), lambda i: (swept(i), 0, 0)))
        outs = [u] * N_POOL_GROUPS + [zp] * N_POOL_GROUPS + [zs, xs_t, b_tok, c_t, dt, dt, y_part]
        extra_in = [pl.BlockSpec((W_SSD, CHUNK), const),
                    pl.BlockSpec((1, W_SSD, D_STATE), lambda i: (swept(i) // tiles_per_seq, 0, 0))]
        extra_args = [dskip_b, h0]
        extra_scratch = [pltpu.VMEM((q, W_SSD, CHUNK), jnp.bfloat16),
                         pltpu.VMEM((tm, GN), jnp.bfloat16),
                         pltpu.VMEM((q, GN, CHUNK), jnp.bfloat16),
                         pltpu.VMEM((q, 2 * HEADS, CHUNK), jnp.float32),
                         pltpu.VMEM((q, 2 * HEADS, CHUNK), jnp.float32),
                         pltpu.VMEM((W_SSD, D_STATE), jnp.float32)]
    else:
        assert tiles_per_seq == 1, "prefix states are computed from one whole sequence per step"
        state = (jax.ShapeDtypeStruct((n_tiles, W_SSD, D_STATE), jnp.float32),
                 pl.BlockSpec((1, W_SSD, D_STATE), chunk3))
        outs = [state, state]
        extra_in, extra_args = [], []
        extra_scratch = [pltpu.VMEM((q, W_SSD, CHUNK), jnp.bfloat16),
                         pltpu.VMEM((tm, GN), jnp.bfloat16),
                         pltpu.VMEM((q, 2 * HEADS, CHUNK), jnp.float32),
                         pltpu.VMEM((q, 2 * HEADS, CHUNK), jnp.float32),
                         pltpu.VMEM((W_SSD, D_STATE), jnp.float32)]
    return pl.pallas_call(
        kern,
        grid=(n_tiles + 1 if full else n_tiles,),
        in_specs=[pl.BlockSpec((tm, D_MODEL), tok),
                  pl.BlockSpec((SUBLANES, D_MODEL), lambda i: (jnp.maximum(tile(i) * per - 1, 0), 0)),
                  pl.BlockSpec((SUBLANES, D_MODEL),
                               lambda i: (jnp.minimum((tile(i) + 1) * per, last_halo), 0)),
                  pl.BlockSpec((1, D_MODEL), const),
                  mod_spec, mod_spec,
                  pl.BlockSpec(w_bf.shape, const),
                  pl.BlockSpec((D_MODEL, DT_PAD), const),
                  pl.BlockSpec((D_CONV, CONV_DIM), const),
                  pl.BlockSpec((1, CONV_DIM), const),
                  pl.BlockSpec((2 * HEADS, 1), const),
                  pl.BlockSpec((2 * HEADS, 1), const)] + extra_in,
        out_specs=[o[1] for o in outs],
        out_shape=[o[0] for o in outs],
        scratch_shapes=[pltpu.VMEM((2, tm + 2 * SUBLANES, CONV_SEG), jnp.float32),
                        pltpu.VMEM((2, CONV_SEG // LANES, tm, LANES), jnp.float32),
                        pltpu.VMEM((D_MODEL // LANES, tm, LANES), jnp.float32)] + extra_scratch,
        compiler_params=pltpu.CompilerParams(
            dimension_semantics=("arbitrary",), vmem_limit_bytes=VMEM_LIMIT),
        name="proj" if full else "proj_ctx",
    )(x2d, x2d, x2d, norm_w.reshape(1, D_MODEL), shift, scale, w_bf, wdt_bf, conv_w, conv_b,
      alog_col, bias_col, *extra_args)


def _tok_rows(q):
    return pl.ds(q * CHUNK, CHUNK)


def _state_update_group(h_ref, xs_t_ref, b_ref, q, g, scale_in, chunk_decay):
    bg = b_ref[_tok_rows(q), g * D_STATE:(g + 1) * D_STATE]
    xd = []
    for r in range(HEADS_PER_GROUP):
        h = g * HEADS_PER_GROUP + r
        x_h = xs_t_ref[q, h * HEADDIM:(h + 1) * HEADDIM].astype(jnp.float32)
        xd.append(_bf(x_h * scale_in[h:h + 1]))
    s_new = _dot(jnp.concatenate(xd, axis=0), bg)
    for r in range(HEADS_PER_GROUP):
        h = g * HEADS_PER_GROUP + r
        hr = slice(h * HEADDIM, (h + 1) * HEADDIM)
        h_ref[hr] = h_ref[hr] * chunk_decay[h:h + 1] + s_new[r * HEADDIM:(r + 1) * HEADDIM]


def _state_update(h_ref, xs_t_ref, b_ref, q, scale_in, chunk_decay):
    for g in range(GROUPS):
        _state_update_group(h_ref, xs_t_ref, b_ref, q, g, scale_in, chunk_decay)


def _fwd_chunk_slices(q, xs_t_ref, b_ref, c_t_ref, dt_ref, cum_ref, dskip_ref, y_ref, h_ref, a_b,
                      causal, is_diag):
    ctx = {}

    def setup():
        dt_f, cum_f = dt_ref[q, :HEADS], cum_ref[q, :HEADS]
        tot_f = cum_f[:, CHUNK - 1:CHUNK]
        dt_b, cum_b = dt_ref[q, HEADS:], cum_ref[q, HEADS:]
        cumx_b = cum_b - dt_b * a_b
        ctx["scale_in"] = dt_f * jnp.exp(tot_f - cum_f)
        ctx["chunk_decay"] = jnp.exp(tot_f)
        ctx["col_terms"] = jnp.concatenate(
            [jnp.log(dt_f) - cum_f, jnp.log(dt_b) + cumx_b,
             jnp.zeros((CHUNK - 2 * HEADS, CHUNK), jnp.float32)], axis=0).T
        ctx["row_f"], ctx["row_b"] = cum_f, -cumx_b
        ctx["decay_out_f"] = jnp.exp(cum_f)
        ctx["dt_b"] = dt_b

    def weights(g):
        if g == 0:
            setup()
        col_terms, row_f, row_b = ctx["col_terms"], ctx["row_f"], ctx["row_b"]
        bg = b_ref[_tok_rows(q), g * D_STATE:(g + 1) * D_STATE]
        cg_t = c_t_ref[q, g * D_STATE:(g + 1) * D_STATE]
        rows = slice(g * HEADS_PER_GROUP * HEADDIM, (g + 1) * HEADS_PER_GROUP * HEADDIM)
        g_t = _dot(bg, cg_t)
        ctx["g_diag", g] = jnp.sum(jnp.where(is_diag, g_t, 0.0), axis=0, keepdims=True)
        ctx["y_off", g] = _dot(_bf(h_ref[rows]), cg_t)
        for r in range(HEADS_PER_GROUP):
            h = g * HEADS_PER_GROUP + r
            col_f = jnp.broadcast_to(col_terms[:, h:h + 1], (CHUNK, CHUNK))
            col_b = jnp.broadcast_to(col_terms[:, HEADS + h:HEADS + h + 1], (CHUNK, CHUNK))
            expo = jnp.where(causal, col_f + row_f[h:h + 1], col_b + row_b[h:h + 1])
            ctx["w_t", h] = _bf(g_t * jnp.exp(expo))

    def apply(g):
        y_off, g_diag = ctx.pop(("y_off", g)), ctx.pop(("g_diag", g))
        for r in range(HEADS_PER_GROUP):
            h = g * HEADS_PER_GROUP + r
            hr = slice(h * HEADDIM, (h + 1) * HEADDIM)
            x_bf = xs_t_ref[q, hr]
            y_h = _dot(x_bf, ctx.pop(("w_t", h)))
            y_h = y_h + y_off[r * HEADDIM:(r + 1) * HEADDIM] * ctx["decay_out_f"][h:h + 1]
            skip = dskip_ref[hr] + g_diag * ctx["dt_b"][h:h + 1]
            y_ref[q, hr] = _bf(y_h + skip * x_bf.astype(jnp.float32))
        _state_update_group(h_ref, xs_t_ref, b_ref, q, g, ctx["scale_in"], ctx["chunk_decay"])

    return [(functools.partial(weights, g), functools.partial(apply, g)) for g in range(GROUPS)]


def _bwd_out_kernel(xs_t_ref, b_ref, dt_ref, cum_ref, alog_ref, h0_ref, c_t_ref, ypart_ref,
                    yp0_ref, yp1_ref, yp2_ref, yp3_ref, zs_ref, x_ref, gate_ref, snw_ref, wout_ref,
                    fnw_ref, o_ref, h_ref, y_ref, *, cps):
    @pl.when(pl.program_id(1) == 0)
    def _():
        h_ref[...] = h0_ref[0]

    a_b = -jnp.exp(alog_ref[HEADS:])

    def chunk(q):
        dt_b, cum_b = dt_ref[q, HEADS:], cum_ref[q, HEADS:]
        tot_b = cum_b[:, CHUNK - 1:CHUNK]
        cumx_b = cum_b - dt_b * a_b
        decay_out = jnp.exp(tot_b - cumx_b)
        y_parts = []
        for g in range(GROUPS):
            cg_t = c_t_ref[q, g * D_STATE:(g + 1) * D_STATE]
            rows = slice(g * HEADS_PER_GROUP * HEADDIM, (g + 1) * HEADS_PER_GROUP * HEADDIM)
            y_off = _dot(_bf(h_ref[rows]), cg_t)
            for r in range(HEADS_PER_GROUP):
                h = g * HEADS_PER_GROUP + r
                hr = slice(h * HEADDIM, (h + 1) * HEADDIM)
                y_parts.append(ypart_ref[q, hr].astype(jnp.float32)
                               + y_off[r * HEADDIM:(r + 1) * HEADDIM] * decay_out[h:h + 1])
        y_ref[_tok_rows(q), :] = jnp.concatenate(y_parts, axis=0).T
        _state_update(h_ref, xs_t_ref, b_ref, q, dt_b * jnp.exp(cumx_b), jnp.exp(tot_b))

    yp_refs = [yp0_ref, yp1_ref, yp2_ref, yp3_ref]
    acc = jnp.zeros((cps * CHUNK, D_MODEL), jnp.float32)
    for i in range(max(cps, N_POOL_GROUPS)):
        if i < cps:
            chunk(cps - 1 - i)
        if i < N_POOL_GROUPS:
            acc = acc + _dot(yp_refs[i][...], wout_ref[i * POOL_GROUP_W:(i + 1) * POOL_GROUP_W])

    gw = W_SSD // GROUPS
    for g in range(GROUPS):
        cols = slice(g * gw, (g + 1) * gw)
        gated = y_ref[:, cols] * zs_ref[:, cols].astype(jnp.float32)
        ms = jnp.mean(gated * gated, axis=-1, keepdims=True)
        yn = gated * jax.lax.rsqrt(ms + EPS) * snw_ref[:, cols]
        acc = acc + _dot(_bf(yn), wout_ref[W_POOL + g * gw:W_POOL + (g + 1) * gw])
    hres = x_ref[...] + gate_ref[0] * acc
    ms = jnp.mean(hres * hres, axis=-1, keepdims=True)
    o_ref[...] = hres * jax.lax.rsqrt(ms + EPS) * fnw_ref[...]


def _backward_output(xs_t, b_tok, dt, cum, c_t, y_part, alog_col, h0, y_pool, gate_ssd, x2d, gate,
                     ssd_norm_w, w_out_bf, final_norm_w, bsz, n_chunks):
    n_tok = b_tok.shape[0]
    cps = min(n_chunks, SSD_CHUNKS_PER_STEP)
    n_steps = n_chunks // cps
    tm = cps * CHUNK
    block_of = lambda b, s: b * n_steps + (n_steps - 1 - s)
    tok = lambda b, s: (block_of(b, s), 0)
    chunk3 = lambda b, s: (block_of(b, s), 0, 0)
    const2 = lambda b, s: (0, 0)
    per_seq = lambda b, s: (b, 0, 0)
    head_spec = pl.BlockSpec((cps, 2 * HEADS, CHUNK), chunk3)
    return pl.pallas_call(
        functools.partial(_bwd_out_kernel, cps=cps),
        grid=(bsz, n_steps),
        in_specs=[pl.BlockSpec((cps, W_SSD, CHUNK), chunk3),
                  pl.BlockSpec((tm, GN), tok),
                  head_spec, head_spec,
                  pl.BlockSpec((2 * HEADS, 1), const2),
                  pl.BlockSpec((1, W_SSD, D_STATE), per_seq),
                  pl.BlockSpec((cps, GN, CHUNK), chunk3),
                  pl.BlockSpec((cps, W_SSD, CHUNK), chunk3)] + [
                  pl.BlockSpec((tm, POOL_GROUP_W), tok)] * N_POOL_GROUPS + [
                  pl.BlockSpec((tm, W_SSD), tok),
                  pl.BlockSpec((tm, D_MODEL), tok),
                  pl.BlockSpec((1, 1, D_MODEL), per_seq),
                  pl.BlockSpec((1, W_SSD), const2),
                  pl.BlockSpec((W_POOL + W_SSD, D_MODEL), const2),
                  pl.BlockSpec((1, D_MODEL), const2)],
        out_specs=pl.BlockSpec((tm, D_MODEL), tok),
        out_shape=jax.ShapeDtypeStruct((n_tok, D_MODEL), jnp.float32),
        scratch_shapes=[pltpu.VMEM((W_SSD, D_STATE), jnp.float32),
                        pltpu.VMEM((tm, W_SSD), jnp.float32)],
        compiler_params=pltpu.CompilerParams(
            dimension_semantics=("arbitrary", "arbitrary"), vmem_limit_bytes=VMEM_LIMIT),
        name="bwd_out",
    )(xs_t, b_tok, dt, cum, alog_col, h0, c_t, y_part, *y_pool, gate_ssd, x2d, gate,
      ssd_norm_w.reshape(1, W_SSD), w_out_bf, final_norm_w.reshape(1, D_MODEL))


POOL_TILE_ROWS = 4
POOL_TILE = POOL_TILE_ROWS * GRID_W


def _pool_constants(window, n_rows):
    lo_off, hi_off = -(window // 2), window - window // 2
    col = np.arange(GRID_W)
    lo = np.clip(col + lo_off, 0, GRID_W)
    hi = np.clip(col + hi_off, 0, GRID_W)
    band = ((col[None, :] >= lo[:, None]) & (col[None, :] < hi[:, None])).astype(np.float32)
    band_tile = np.kron(np.eye(POOL_TILE_ROWS, dtype=np.float32), band)
    row = np.arange(n_rows)
    cnt_r = np.clip(row + hi_off, 0, n_rows) - np.clip(row + lo_off, 0, n_rows)
    inv = 1.0 / (cnt_r[:, None] * (hi - lo)[None, :]).astype(np.float64)
    inv = np.broadcast_to(inv.reshape(-1, 1), (n_rows * GRID_W, 128)).astype(np.float32)
    return jnp.asarray(band_tile, jnp.bfloat16), jnp.asarray(inv)


def _pool_kernel(u_ref, z_ref, band_ref, inv_ref, w_ref, scale_ref, o_ref, *, window, n_rows):
    def grid_row(r):
        return u_ref[r * GRID_W:(r + 1) * GRID_W].astype(jnp.float32)

    def bounds(r):
        return max(r - window // 2, 0), min(r + window - window // 2, n_rows)

    band = band_ref[...]
    rsum, tile_rows = None, []
    for r in range(n_rows):
        lo, hi = bounds(r)
        if r == 0 or window <= 2:
            rsum = grid_row(lo)
            for k in range(lo + 1, hi):
                rsum = rsum + grid_row(k)
        else:
            prev_lo, prev_hi = bounds(r - 1)
            if hi > prev_hi:
                rsum = rsum + grid_row(hi - 1)
            if lo > prev_lo:
                rsum = rsum - grid_row(prev_lo)
        tile_rows.append(rsum)
        if len(tile_rows) < POOL_TILE_ROWS:
            continue
        base = (r + 1 - POOL_TILE_ROWS) * GRID_W
        rows = slice(base, base + POOL_TILE)
        rs = jnp.concatenate(tile_rows, axis=0)
        tile_rows = []
        box = _dot(band, _bf(rs))
        inv = inv_ref[rows]
        mean = box * jnp.concatenate([inv, inv], axis=1)
        d = mean - u_ref[rows].astype(jnp.float32)
        y = _dot(_bf(d), w_ref[0]) * scale_ref[...]
        o_ref[rows] = _bf(y * z_ref[rows].astype(jnp.float32))


def _pool_group(u, gate, pool_w_bf, pool_scale, g, bsz, n_img_tok):
    window = POOL_WINDOWS[g]
    n_rows = n_img_tok // GRID_W
    band, inv = _pool_constants(window, n_rows)
    kern = functools.partial(_pool_kernel, window=window, n_rows=n_rows)
    img = pl.BlockSpec((n_img_tok, POOL_GROUP_W), lambda b: (b, 0))
    return pl.pallas_call(
        kern,
        grid=(bsz,),
        in_specs=[img, img,
                  pl.BlockSpec((POOL_TILE, POOL_TILE), lambda b: (0, 0)),
                  pl.BlockSpec((n_img_tok, 128), lambda b: (0, 0)),
                  pl.BlockSpec((1, POOL_GROUP_W, POOL_GROUP_W), lambda b: (g, 0, 0)),
                  pl.BlockSpec((1, POOL_GROUP_W), lambda b: (0, g))],
        out_specs=img,
        out_shape=jax.ShapeDtypeStruct((bsz * n_img_tok, POOL_GROUP_W), jnp.bfloat16),
        compiler_params=pltpu.CompilerParams(vmem_limit_bytes=VMEM_LIMIT),
        name=f"pool{window}",
    )(u, gate, band, inv, pool_w_bf, pool_scale)


def kernel(x, c, ctx, c_ctx, norm_w, w_ada, b_ada, w_in, conv_w, conv_b, a_log, dt_bias, d_skip,
           ssd_norm_w, pool_w, pool_scale, w_out, final_norm_w):
    bsz, seq, _ = x.shape
    ctx_len = ctx.shape[1]
    depth = norm_w.shape[0]
    assert depth == 1, "single-layer block: the context stream update is never consumed"
    assert seq % 512 == 0 and ctx_len % CHUNK == 0 and seq % GRID_W == 0

    mod_rows = -(-(bsz + 1) // SUBLANES) * SUBLANES
    cond = jnp.concatenate([c, c_ctx[None], jnp.zeros((mod_rows - bsz - 1, D_MODEL), c.dtype)])
    mod = _modulation(cond, w_ada[0], b_ada[0])
    shift = mod[:, :D_MODEL].reshape(mod_rows, 1, D_MODEL)
    scale = mod[:, D_MODEL:2 * D_MODEL].reshape(mod_rows, 1, D_MODEL)
    gate = mod[:, 2 * D_MODEL:].reshape(mod_rows, 1, D_MODEL)

    w_in_bf = _bf(w_in[0])
    w_dt_bf = jnp.pad(_bf(w_in[0, :, OFF_DT:]), ((0, 0), (0, DT_PAD - 2 * HEADS)))
    alog_col = a_log[0].reshape(2 * HEADS, 1)
    bias_col = dt_bias[0].reshape(2 * HEADS, 1)
    dskip_b = jnp.broadcast_to(jnp.repeat(d_skip[0], HEADDIM)[:, None], (W_SSD, CHUNK))
    conv_b2 = conv_b[0].reshape(1, CONV_DIM)

    ctx2d = ctx.reshape(bsz * ctx_len, D_MODEL)
    h_fwd, h_bwd = _projection(
        ctx2d, norm_w[0], shift[bsz:bsz + 1], scale[bsz:bsz + 1], w_in_bf, w_dt_bf, conv_w[0],
        conv_b2, alog_col, bias_col, ctx_len, ctx_len, full=False)

    x2d = x.reshape(bsz * seq, D_MODEL)
    outs = _projection(x2d, norm_w[0], shift, scale, w_in_bf, w_dt_bf, conv_w[0], conv_b2,
                       alog_col, bias_col, seq, 512, full=True, dskip_b=dskip_b, h0=h_fwd)
    u_pool, gate_pool = outs[:N_POOL_GROUPS], outs[N_POOL_GROUPS:2 * N_POOL_GROUPS]
    gate_ssd, xs_t, b_tok, c_t, dt, cum, y_part = outs[2 * N_POOL_GROUPS:]
    nc = seq // CHUNK
    pool_w_bf = _bf(pool_w[0])
    y_pool = [_pool_group(u_pool[g], gate_pool[g], pool_w_bf, pool_scale, g, bsz, seq)
              for g in range(N_POOL_GROUPS)]
    out = _backward_output(xs_t, b_tok, dt, cum, c_t, y_part, alog_col, h_bwd, y_pool, gate_ssd,
                           x2d, gate, ssd_norm_w[0], _bf(w_out[0]), final_norm_w, bsz, nc)
    return out.reshape(bsz, seq, D_MODEL)
```

```python
import functools

import numpy as np
import jax
import jax.numpy as jnp
from jax.experimental import pallas as pl
from jax.experimental.pallas import tpu as pltpu

D_MODEL = 1024
GRID_W = 64
W_POOL = 1024
W_SSD = 1024
POOL_WINDOWS = (2, 4, 8, 16)
N_POOL_GROUPS = len(POOL_WINDOWS)
POOL_GROUP_W = 256
HEADDIM = 64
HEADS = 16
GROUPS = 4
HEADS_PER_GROUP = 4
D_STATE = 128
D_CONV = 4
CONV_LEFT = 2
CHUNK = 128
GN = GROUPS * D_STATE
CONV_DIM = W_SSD + 2 * GN
OFF_POOL_Z = W_POOL
OFF_SSD_Z = 2 * W_POOL
OFF_XBC = 2 * W_POOL + W_SSD
OFF_DT = OFF_XBC + CONV_DIM
DT_PAD = 128
EPS = 1e-6
SUBLANES = 8
LANES = 128
IL_GROUPS = CHUNK // SUBLANES
CONV_SEG = 256
SSD_CHUNKS_PER_STEP = 4
VMEM_LIMIT = 56 * 1024 * 1024


def _silu(v):
    h = 0.5 * v
    return h + h * jnp.tanh(h)


def _softplus(v):
    return jnp.maximum(v, 0.0) + jnp.log1p(jnp.exp(-jnp.abs(v)))


def _bf(v):
    return v.astype(jnp.bfloat16)


def _dot(a, b):
    return jnp.dot(a, b, preferred_element_type=jnp.float32)


def _mod_kernel(c_ref, w_ref, b_ref, o_ref):
    s = _silu(c_ref[...])
    o_ref[...] = jnp.dot(s, w_ref[...], preferred_element_type=jnp.float32,
                         precision=jax.lax.Precision.HIGHEST) + b_ref[...]


def _modulation(cond_rows, w_ada, b_ada):
    rows = cond_rows.shape[0]
    n_out = w_ada.shape[1]
    tn = 1024
    return pl.pallas_call(
        _mod_kernel,
        grid=(n_out // tn,),
        in_specs=[pl.BlockSpec((rows, D_MODEL), lambda j: (0, 0)),
                  pl.BlockSpec((D_MODEL, tn), lambda j: (0, j)),
                  pl.BlockSpec((1, tn), lambda j: (0, j))],
        out_specs=pl.BlockSpec((rows, tn), lambda j: (0, j)),
        out_shape=jax.ShapeDtypeStruct((rows, n_out), jnp.float32),
        compiler_params=pltpu.CompilerParams(vmem_limit_bytes=VMEM_LIMIT),
        name="mod",
    )(cond_rows, w_ada, b_ada.reshape(1, n_out))


def _lane_cumsum(v):
    lane = jax.lax.broadcasted_iota(jnp.int32, v.shape, 1)
    shift = 1
    while shift < CHUNK:
        v = v + jnp.where(lane >= shift, pltpu.roll(v, shift, 1), 0.0)
        shift *= 2
    return v


def _proj_kernel(x_ref, xp_ref, xn_ref, nw_ref, sh_ref, sc_ref, w_ref, wdt_ref, cw_ref, cb_ref,
                 alog_ref, bias_ref, *rest, tm, tiles_per_seq, n_tiles, full):
    if full:
        (dskip_ref, h0_ref, u0, u1, u2, u3, zp0, zp1, zp2, zp3, zs_ref, xs_t_ref, b_ref, c_t_ref,
         dt_ref, cum_ref, ypart_ref, pe_ref, xc_ref, mn_ref, kxs_ref, kb_ref, kc_ref, kdt_ref,
         kcum_ref, h_ref) = rest
        u_refs, zp_refs = (u0, u1, u2, u3), (zp0, zp1, zp2, zp3)
    else:
        hf_ref, hb_ref, pe_ref, xc_ref, mn_ref, xs_t_ref, b_ref, dt_ref, cum_ref, h_ref = rest
    i = pl.program_id(0)
    pos = jnp.minimum(i, n_tiles - 1) % tiles_per_seq
    has_prev = pos > 0
    has_next = pos < tiles_per_seq - 1
    n_chunks = tm // CHUNK
    seg = CONV_SEG

    if full:
        @pl.when(i == 0)
        def _():
            for ref in (kxs_ref, kb_ref, kc_ref, kdt_ref, kcum_ref, h_ref):
                ref[...] = jnp.zeros(ref.shape, ref.dtype)

    gain = nw_ref[...] * (1.0 + sc_ref[0])

    def modulated(v):
        ms = jnp.mean(v * v, axis=-1, keepdims=True)
        return v * jax.lax.rsqrt(ms + EPS) * gain + sh_ref[0]

    m_tok = modulated(x_ref[...])
    hm = _bf(m_tok)
    for t in range(D_MODEL // LANES):
        mn_ref[t] = m_tok[:, t * LANES:(t + 1) * LANES]

    rows = [jnp.concatenate([mn_ref[t, pl.ds(q * CHUNK + b, SUBLANES, stride=IL_GROUPS), :]
                             for t in range(D_MODEL // LANES)], axis=1)
            for q in range(n_chunks) for b in range(IL_GROUPS)]
    halo = [jnp.where(has_prev, modulated(xp_ref[...]), 0.0),
            jnp.where(has_next, modulated(xn_ref[...]), 0.0)]
    hm_il = _bf(jnp.concatenate(halo + rows, axis=0))
    sub = jax.lax.broadcasted_iota(jnp.int32, (SUBLANES, seg), 0)

    def conv_stage(j, slot):
        is_x = j < W_SSD
        is_b = W_SSD <= j < W_SSD + GN

        def matmul():
            lo = (OFF_XBC if full else 0) + j
            pe_ref[slot] = _dot(hm_il, w_ref[:, lo:lo + seg])

        def group(q, b):
            lo = 2 * SUBLANES + q * CHUNK + b * SUBLANES
            return pe_ref[slot, lo:lo + SUBLANES]

        def shifted(q, b, delta):
            bb = b + delta
            if 0 <= bb < IL_GROUPS:
                return group(q, bb)
            if bb < 0:
                bb += IL_GROUPS
                if q == 0:
                    first = pe_ref[slot, bb - SUBLANES:bb - SUBLANES + 1]
                else:
                    row = 2 * SUBLANES + (q - 1) * CHUNK + bb * SUBLANES + SUBLANES - 1
                    first = pe_ref[slot, row:row + 1]
                return jnp.where(sub == 0, first, pltpu.roll(group(q, bb), 1, 0))
            bb -= IL_GROUPS
            if q == n_chunks - 1:
                last = pe_ref[slot, SUBLANES + bb:SUBLANES + bb + 1]
            else:
                nxt = 2 * SUBLANES + (q + 1) * CHUNK + bb * SUBLANES
                last = pe_ref[slot, nxt:nxt + 1]
            return jnp.where(sub == SUBLANES - 1, last, pltpu.roll(group(q, bb), SUBLANES - 1, 0))

        def epilogue():
            taps = [cw_ref[k:k + 1, j:j + seg] for k in range(D_CONV)]
            bias = cb_ref[:, j:j + seg]
            for q in range(n_chunks):
                for b in range(IL_GROUPS):
                    acc = bias
                    for k in range(D_CONV):
                        acc = acc + shifted(q, b, k - CONV_LEFT) * taps[k]
                    lo = q * CHUNK + b * SUBLANES
                    act = _silu(acc)
                    for t in range(seg // LANES):
                        xc_ref[slot, t, lo:lo + SUBLANES] = act[:, t * LANES:(t + 1) * LANES]
            for q in range(n_chunks):
                xc = jnp.concatenate(
                    [jnp.concatenate(
                        [xc_ref[slot, t, pl.ds(q * CHUNK + (m % 2) * (CHUNK // 2) + m // 2,
                                               SUBLANES, stride=SUBLANES), :]
                         for t in range(seg // LANES)], axis=1)
                     for m in range(IL_GROUPS)], axis=0)
                if is_b:
                    b_ref[q * CHUNK:(q + 1) * CHUNK, j - W_SSD:j - W_SSD + seg] = _bf(xc)
                else:
                    dst, off = (xs_t_ref, j) if is_x else (c_t_ref, j - W_SSD - GN)
                    dst[q, off:off + seg] = _bf(xc.T)

        return matmul, epilogue

    def plain_stage(cols, finish):
        box = []
        return (lambda: box.append(_dot(hm, w_ref[:, cols]))), (lambda: finish(box.pop()))

    def dt_stage():
        box = []

        def epilogue():
            p_dt = box.pop()
            a_col = -jnp.exp(alog_ref[...])
            for q in range(n_chunks):
                dt = _softplus(p_dt[q * CHUNK:(q + 1) * CHUNK].T[:2 * HEADS] + bias_ref[...])
                dt_ref[q] = dt
                cum_ref[q] = _lane_cumsum(dt * a_col)

        return (lambda: box.append(_dot(hm, wdt_ref[...]))), epilogue

    def store_to(ref, cols=None, act=None):
        def finish(v):
            v = v if act is None else _bf(act(v))
            if cols is None:
                ref[...] = v
            else:
                ref[:, cols] = v
        return finish

    def store_pair(refs, act=None):
        def finish(v):
            for n, ref in enumerate(refs):
                part = v[:, n * POOL_GROUP_W:(n + 1) * POOL_GROUP_W]
                ref[...] = _bf(part if act is None else act(part))
        return finish

    light, heavy = [], []
    if full:
        for g in range(0, N_POOL_GROUPS, 2):
            light.append(plain_stage(slice(g * POOL_GROUP_W, (g + 2) * POOL_GROUP_W),
                                     store_pair(u_refs[g:g + 2])))
    light.append(dt_stage())
    for j in range(0, CONV_DIM, seg):
        if full or j < W_SSD + GN:
            heavy.append(conv_stage(j, len(heavy) % 2))
    if full:
        for g in range(0, N_POOL_GROUPS, 2):
            zcols = slice(OFF_POOL_Z + g * POOL_GROUP_W, OFF_POOL_Z + (g + 2) * POOL_GROUP_W)
            light.append(plain_stage(zcols, store_pair(zp_refs[g:g + 2], act=_silu)))
        for j in range(0, W_SSD, seg):
            light.append(plain_stage(slice(OFF_SSD_Z + j, OFF_SSD_Z + j + seg),
                                     store_to(zs_ref, cols=slice(j, j + seg), act=_silu)))
    stages = []
    while light or heavy:
        if light:
            stages.append(light.pop(0))
        if heavy:
            stages.append(heavy.pop(0))

    sweep = []
    if full:
        swept = jnp.maximum(i - 1, 0)
        h_ref[...] = jnp.where(swept % tiles_per_seq == 0, h0_ref[0], h_ref[...])
        src = jax.lax.broadcasted_iota(jnp.int32, (CHUNK, CHUNK), 0)
        dst = jax.lax.broadcasted_iota(jnp.int32, (CHUNK, CHUNK), 1)
        a_b = -jnp.exp(alog_ref[HEADS:])
        pairs = []
        for q in range(n_chunks):
            pairs += _fwd_chunk_slices(q, kxs_ref, kb_ref, kc_ref, kdt_ref, kcum_ref, dskip_ref,
                                       ypart_ref, h_ref, a_b, src <= dst, src == dst)
        sweep = [pairs[0][0]]
        for p in range(len(pairs)):
            nxt = pairs[p + 1][0] if p + 1 < len(pairs) else (lambda: None)
            sweep.append(functools.partial(lambda a, w: (a(), w()), pairs[p][1], nxt))

    stages[0][0]()
    done = 0
    for k, (_, epilogue) in enumerate(stages):
        if k + 1 < len(stages):
            stages[k + 1][0]()
        epilogue()
        upto = -(-len(sweep) * (k + 1) // len(stages))
        for piece in sweep[done:upto]:
            piece()
        done = upto

    if full:
        for kept, ref in ((kxs_ref, xs_t_ref), (kb_ref, b_ref), (kc_ref, c_t_ref),
                          (kdt_ref, dt_ref), (kcum_ref, cum_ref)):
            kept[...] = ref[...]
    else:
        a_b = -jnp.exp(alog_ref[HEADS:])
        for reverse, out_ref in ((False, hf_ref), (True, hb_ref)):
            h_ref[...] = jnp.zeros(h_ref.shape, h_ref.dtype)
            for q in (range(n_chunks - 1, -1, -1) if reverse else range(n_chunks)):
                if reverse:
                    dt_b, cum_b = dt_ref[q, HEADS:], cum_ref[q, HEADS:]
                    scale_in = dt_b * jnp.exp(cum_b - dt_b * a_b)
                    decay = jnp.exp(cum_b[:, CHUNK - 1:CHUNK])
                else:
                    dt_f, cum_f = dt_ref[q, :HEADS], cum_ref[q, :HEADS]
                    tot_f = cum_f[:, CHUNK - 1:CHUNK]
                    scale_in, decay = dt_f * jnp.exp(tot_f - cum_f), jnp.exp(tot_f)
                _state_update(h_ref, xs_t_ref, b_ref, q, scale_in, decay)
            out_ref[0] = h_ref[...]


def _projection(x2d, norm_w, shift, scale, w_bf, wdt_bf, conv_w, conv_b, alog_col, bias_col,
                seq_len, tm, full, dskip_b=None, h0=None):
    n_tok = x2d.shape[0]
    tiles_per_seq = seq_len // tm
    n_tiles = n_tok // tm
    n_mod = shift.shape[0]
    nct = n_tok // CHUNK
    per = tm // SUBLANES
    last_halo = n_tok // SUBLANES - 1
    kern = functools.partial(_proj_kernel, tm=tm, tiles_per_seq=tiles_per_seq, n_tiles=n_tiles,
                             full=full)
    tile = lambda i: jnp.minimum(i, n_tiles - 1)
    mod_map = (lambda i: (tile(i) // tiles_per_seq, 0, 0)) if n_mod > 1 else (lambda i: (0, 0, 0))
    mod_spec = pl.BlockSpec((1, 1, D_MODEL), mod_map)
    const = lambda i: (0, 0)
    tok = lambda i: (tile(i), 0)
    chunk3 = lambda i: (tile(i), 0, 0)
    q = tm // CHUNK
    xs_t = (jax.ShapeDtypeStruct((nct, W_SSD, CHUNK), jnp.bfloat16),
            pl.BlockSpec((q, W_SSD, CHUNK), chunk3))
    b_tok = (jax.ShapeDtypeStruct((n_tok, GN), jnp.bfloat16), pl.BlockSpec((tm, GN), tok))
    c_t = (jax.ShapeDtypeStruct((nct, GN, CHUNK), jnp.bfloat16), pl.BlockSpec((q, GN, CHUNK), chunk3))
    dt = (jax.ShapeDtypeStruct((nct, 2 * HEADS, CHUNK), jnp.float32),
          pl.BlockSpec((q, 2 * HEADS, CHUNK), chunk3))
    if full:
        zp = (jax.ShapeDtypeStruct((n_tok, POOL_GROUP_W), jnp.bfloat16),
              pl.BlockSpec((tm, POOL_GROUP_W), tok))
        u = zp
        zs = (jax.ShapeDtypeStruct((n_tok, W_SSD), jnp.bfloat16), pl.BlockSpec((tm, W_SSD), tok))
        swept = lambda i: jnp.maximum(i - 1, 0)
        y_part = (jax.ShapeDtypeStruct((nct, W_SSD, CHUNK), jnp.bfloat16),
                  pl.BlockSpec((q, W_SSD, CHUNK), lambda i: (swept(i), 0, 0)))
        outs = [u] * N_POOL_GROUPS + [zp] * N_POOL_GROUPS + [zs, xs_t, b_tok, c_t, dt, dt, y_part]
        extra_in = [pl.BlockSpec((W_SSD, CHUNK), const),
                    pl.BlockSpec((1, W_SSD, D_STATE), lambda i: (swept(i) // tiles_per_seq, 0, 0))]
        extra_args = [dskip_b, h0]
        extra_scratch = [pltpu.VMEM((q, W_SSD, CHUNK), jnp.bfloat16),
                         pltpu.VMEM((tm, GN), jnp.bfloat16),
                         pltpu.VMEM((q, GN, CHUNK), jnp.bfloat16),
                         pltpu.VMEM((q, 2 * HEADS, CHUNK), jnp.float32),
                         pltpu.VMEM((q, 2 * HEADS, CHUNK), jnp.float32),
                         pltpu.VMEM((W_SSD, D_STATE), jnp.float32)]
    else:
        assert tiles_per_seq == 1, "prefix states are computed from one whole sequence per step"
        state = (jax.ShapeDtypeStruct((n_tiles, W_SSD, D_STATE), jnp.float32),
                 pl.BlockSpec((1, W_SSD, D_STATE), chunk3))
        outs = [state, state]
        extra_in, extra_args = [], []
        extra_scratch = [pltpu.VMEM((q, W_SSD, CHUNK), jnp.bfloat16),
                         pltpu.VMEM((tm, GN), jnp.bfloat16),
                         pltpu.VMEM((q, 2 * HEADS, CHUNK), jnp.float32),
                         pltpu.VMEM((q, 2 * HEADS, CHUNK), jnp.float32),
                         pltpu.VMEM((W_SSD, D_STATE), jnp.float32)]
    return pl.pallas_call(
        kern,
        grid=(n_tiles + 1 if full else n_tiles,),
        in_specs=[pl.BlockSpec((tm, D_MODEL), tok),
                  pl.BlockSpec((SUBLANES, D_MODEL), lambda i: (jnp.maximum(tile(i) * per - 1, 0), 0)),
                  pl.BlockSpec((SUBLANES, D_MODEL),
                               lambda i: (jnp.minimum((tile(i) + 1) * per, last_halo), 0)),
                  pl.BlockSpec((1, D_MODEL), const),
                  mod_spec, mod_spec,
                  (pl.BlockSpec(w_bf.shape, const) if full else
                   pl.BlockSpec((D_MODEL, W_SSD + GN), lambda i: (0, OFF_XBC // (W_SSD + GN)))),
                  pl.BlockSpec((D_MODEL, DT_PAD), const),
                  pl.BlockSpec((D_CONV, CONV_DIM), const),
                  pl.BlockSpec((1, CONV_DIM), const),
                  pl.BlockSpec((2 * HEADS, 1), const),
                  pl.BlockSpec((2 * HEADS, 1), const)] + extra_in,
        out_specs=[o[1] for o in outs],
        out_shape=[o[0] for o in outs],
        scratch_shapes=[pltpu.VMEM((2, tm + 2 * SUBLANES, CONV_SEG), jnp.float32),
                        pltpu.VMEM((2, CONV_SEG // LANES, tm, LANES), jnp.float32),
                        pltpu.VMEM((D_MODEL // LANES, tm, LANES), jnp.float32)] + extra_scratch,
        compiler_params=pltpu.CompilerParams(
            dimension_semantics=("arbitrary",), vmem_limit_bytes=VMEM_LIMIT),
        name="proj" if full else "proj_ctx",
    )(x2d, x2d, x2d, norm_w.reshape(1, D_MODEL), shift, scale, w_bf, wdt_bf, conv_w, conv_b,
      alog_col, bias_col, *extra_args)


def _tok_rows(q):
    return pl.ds(q * CHUNK, CHUNK)


def _state_update_group(h_ref, xs_t_ref, b_ref, q, g, scale_in, chunk_decay):
    bg = b_ref[_tok_rows(q), g * D_STATE:(g + 1) * D_STATE]
    xd = []
    for r in range(HEADS_PER_GROUP):
        h = g * HEADS_PER_GROUP + r
        x_h = xs_t_ref[q, h * HEADDIM:(h + 1) * HEADDIM].astype(jnp.float32)
        xd.append(_bf(x_h * scale_in[h:h + 1]))
    s_new = _dot(jnp.concatenate(xd, axis=0), bg)
    for r in range(HEADS_PER_GROUP):
        h = g * HEADS_PER_GROUP + r
        hr = slice(h * HEADDIM, (h + 1) * HEADDIM)
        h_ref[hr] = h_ref[hr] * chunk_decay[h:h + 1] + s_new[r * HEADDIM:(r + 1) * HEADDIM]


def _state_update(h_ref, xs_t_ref, b_ref, q, scale_in, chunk_decay):
    for g in range(GROUPS):
        _state_update_group(h_ref, xs_t_ref, b_ref, q, g, scale_in, chunk_decay)


def _fwd_chunk_slices(q, xs_t_ref, b_ref, c_t_ref, dt_ref, cum_ref, dskip_ref, y_ref, h_ref, a_b,
                      causal, is_diag):
    ctx = {}

    def setup():
        dt_f, cum_f = dt_ref[q, :HEADS], cum_ref[q, :HEADS]
        tot_f = cum_f[:, CHUNK - 1:CHUNK]
        dt_b, cum_b = dt_ref[q, HEADS:], cum_ref[q, HEADS:]
        cumx_b = cum_b - dt_b * a_b
        ctx["scale_in"] = dt_f * jnp.exp(tot_f - cum_f)
        ctx["chunk_decay"] = jnp.exp(tot_f)
        ctx["col_terms"] = jnp.concatenate(
            [jnp.log(dt_f) - cum_f, jnp.log(dt_b) + cumx_b,
             jnp.zeros((CHUNK - 2 * HEADS, CHUNK), jnp.float32)], axis=0).T
        ctx["row_f"], ctx["row_b"] = cum_f, -cumx_b
        ctx["decay_out_f"] = jnp.exp(cum_f)
        ctx["dt_b"] = dt_b

    def weights(g):
        if g == 0:
            setup()
        col_terms, row_f, row_b = ctx["col_terms"], ctx["row_f"], ctx["row_b"]
        bg = b_ref[_tok_rows(q), g * D_STATE:(g + 1) * D_STATE]
        cg_t = c_t_ref[q, g * D_STATE:(g + 1) * D_STATE]
        rows = slice(g * HEADS_PER_GROUP * HEADDIM, (g + 1) * HEADS_PER_GROUP * HEADDIM)
        g_t = _dot(bg, cg_t)
        ctx["g_diag", g] = jnp.sum(jnp.where(is_diag, g_t, 0.0), axis=0, keepdims=True)
        ctx["y_off", g] = _dot(_bf(h_ref[rows]), cg_t)
        for r in range(HEADS_PER_GROUP):
            h = g * HEADS_PER_GROUP + r
            col_f = jnp.broadcast_to(col_terms[:, h:h + 1], (CHUNK, CHUNK))
            col_b = jnp.broadcast_to(col_terms[:, HEADS + h:HEADS + h + 1], (CHUNK, CHUNK))
            expo = jnp.where(causal, col_f + row_f[h:h + 1], col_b + row_b[h:h + 1])
            ctx["w_t", h] = _bf(g_t * jnp.exp(expo))

    def apply(g):
        y_off, g_diag = ctx.pop(("y_off", g)), ctx.pop(("g_diag", g))
        for r in range(HEADS_PER_GROUP):
            h = g * HEADS_PER_GROUP + r
            hr = slice(h * HEADDIM, (h + 1) * HEADDIM)
            x_bf = xs_t_ref[q, hr]
            y_h = _dot(x_bf, ctx.pop(("w_t", h)))
            y_h = y_h + y_off[r * HEADDIM:(r + 1) * HEADDIM] * ctx["decay_out_f"][h:h + 1]
            skip = dskip_ref[hr] + g_diag * ctx["dt_b"][h:h + 1]
            y_ref[q, hr] = _bf(y_h + skip * x_bf.astype(jnp.float32))
        _state_update_group(h_ref, xs_t_ref, b_ref, q, g, ctx["scale_in"], ctx["chunk_decay"])

    return [(functools.partial(weights, g), functools.partial(apply, g)) for g in range(GROUPS)]


def _bwd_out_kernel(xs_t_ref, b_ref, dt_ref, cum_ref, alog_ref, h0_ref, c_t_ref, ypart_ref,
                    yp0_ref, yp1_ref, yp2_ref, yp3_ref, zs_ref, x_ref, gate_ref, snw_ref, wout_ref,
                    fnw_ref, o_ref, h_ref, y_ref, *, cps):
    @pl.when(pl.program_id(1) == 0)
    def _():
        h_ref[...] = h0_ref[0]

    a_b = -jnp.exp(alog_ref[HEADS:])

    def chunk(q):
        dt_b, cum_b = dt_ref[q, HEADS:], cum_ref[q, HEADS:]
        tot_b = cum_b[:, CHUNK - 1:CHUNK]
        cumx_b = cum_b - dt_b * a_b
        decay_out = jnp.exp(tot_b - cumx_b)
        y_parts = []
        for g in range(GROUPS):
            cg_t = c_t_ref[q, g * D_STATE:(g + 1) * D_STATE]
            rows = slice(g * HEADS_PER_GROUP * HEADDIM, (g + 1) * HEADS_PER_GROUP * HEADDIM)
            y_off = _dot(_bf(h_ref[rows]), cg_t)
            for r in range(HEADS_PER_GROUP):
                h = g * HEADS_PER_GROUP + r
                hr = slice(h * HEADDIM, (h + 1) * HEADDIM)
                y_parts.append(ypart_ref[q, hr].astype(jnp.float32)
                               + y_off[r * HEADDIM:(r + 1) * HEADDIM] * decay_out[h:h + 1])
        y_ref[_tok_rows(q), :] = jnp.concatenate(y_parts, axis=0).T
        _state_update(h_ref, xs_t_ref, b_ref, q, dt_b * jnp.exp(cumx_b), jnp.exp(tot_b))

    yp_refs = [yp0_ref, yp1_ref, yp2_ref, yp3_ref]
    acc = jnp.zeros((cps * CHUNK, D_MODEL), jnp.float32)
    for i in range(max(cps, N_POOL_GROUPS)):
        if i < cps:
            chunk(cps - 1 - i)
        if i < N_POOL_GROUPS:
            acc = acc + _dot(yp_refs[i][...], wout_ref[i * POOL_GROUP_W:(i + 1) * POOL_GROUP_W])

    gw = W_SSD // GROUPS
    for g in range(GROUPS):
        cols = slice(g * gw, (g + 1) * gw)
        gated = y_ref[:, cols] * zs_ref[:, cols].astype(jnp.float32)
        ms = jnp.mean(gated * gated, axis=-1, keepdims=True)
        yn = gated * jax.lax.rsqrt(ms + EPS) * snw_ref[:, cols]
        acc = acc + _dot(_bf(yn), wout_ref[W_POOL + g * gw:W_POOL + (g + 1) * gw])
    hres = x_ref[...] + gate_ref[0] * acc
    ms = jnp.mean(hres * hres, axis=-1, keepdims=True)
    o_ref[...] = hres * jax.lax.rsqrt(ms + EPS) * fnw_ref[...]


def _backward_output(xs_t, b_tok, dt, cum, c_t, y_part, alog_col, h0, y_pool, gate_ssd, x2d, gate,
                     ssd_norm_w, w_out_bf, final_norm_w, bsz, n_chunks):
    n_tok = b_tok.shape[0]
    cps = min(n_chunks, SSD_CHUNKS_PER_STEP)
    n_steps = n_chunks // cps
    tm = cps * CHUNK
    block_of = lambda b, s: b * n_steps + (n_steps - 1 - s)
    tok = lambda b, s: (block_of(b, s), 0)
    chunk3 = lambda b, s: (block_of(b, s), 0, 0)
    const2 = lambda b, s: (0, 0)
    per_seq = lambda b, s: (b, 0, 0)
    head_spec = pl.BlockSpec((cps, 2 * HEADS, CHUNK), chunk3)
    return pl.pallas_call(
        functools.partial(_bwd_out_kernel, cps=cps),
        grid=(bsz, n_steps),
        in_specs=[pl.BlockSpec((cps, W_SSD, CHUNK), chunk3),
                  pl.BlockSpec((tm, GN), tok),
                  head_spec, head_spec,
                  pl.BlockSpec((2 * HEADS, 1), const2),
                  pl.BlockSpec((1, W_SSD, D_STATE), per_seq),
                  pl.BlockSpec((cps, GN, CHUNK), chunk3),
                  pl.BlockSpec((cps, W_SSD, CHUNK), chunk3)] + [
                  pl.BlockSpec((tm, POOL_GROUP_W), tok)] * N_POOL_GROUPS + [
                  pl.BlockSpec((tm, W_SSD), tok),
                  pl.BlockSpec((tm, D_MODEL), tok),
                  pl.BlockSpec((1, 1, D_MODEL), per_seq),
                  pl.BlockSpec((1, W_SSD), const2),
                  pl.BlockSpec((W_POOL + W_SSD, D_MODEL), const2),
                  pl.BlockSpec((1, D_MODEL), const2)],
        out_specs=pl.BlockSpec((tm, D_MODEL), tok),
        out_shape=jax.ShapeDtypeStruct((n_tok, D_MODEL), jnp.float32),
        scratch_shapes=[pltpu.VMEM((W_SSD, D_STATE), jnp.float32),
                        pltpu.VMEM((tm, W_SSD), jnp.float32)],
        compiler_params=pltpu.CompilerParams(
            dimension_semantics=("arbitrary", "arbitrary"), vmem_limit_bytes=VMEM_LIMIT),
        name="bwd_out",
    )(xs_t, b_tok, dt, cum, alog_col, h0, c_t, y_part, *y_pool, gate_ssd, x2d, gate,
      ssd_norm_w.reshape(1, W_SSD), w_out_bf, final_norm_w.reshape(1, D_MODEL))


POOL_TILE_ROWS = 4
POOL_TILE = POOL_TILE_ROWS * GRID_W


def _pool_constants(window, n_rows):
    lo_off, hi_off = -(window // 2), window - window // 2
    col = np.arange(GRID_W)
    lo = np.clip(col + lo_off, 0, GRID_W)
    hi = np.clip(col + hi_off, 0, GRID_W)
    band = ((col[None, :] >= lo[:, None]) & (col[None, :] < hi[:, None])).astype(np.float32)
    band_tile = np.kron(np.eye(POOL_TILE_ROWS, dtype=np.float32), band)
    row = np.arange(n_rows)
    cnt_r = np.clip(row + hi_off, 0, n_rows) - np.clip(row + lo_off, 0, n_rows)
    inv = 1.0 / (cnt_r[:, None] * (hi - lo)[None, :]).astype(np.float64)
    inv = np.broadcast_to(inv.reshape(-1, 1), (n_rows * GRID_W, 128)).astype(np.float32)
    return jnp.asarray(band_tile, jnp.bfloat16), jnp.asarray(inv)


def _pool_kernel(u_ref, z_ref, band_ref, inv_ref, w_ref, scale_ref, o_ref, *, window, n_rows):
    def grid_row(r):
        return u_ref[r * GRID_W:(r + 1) * GRID_W].astype(jnp.float32)

    def bounds(r):
        return max(r - window // 2, 0), min(r + window - window // 2, n_rows)

    band = band_ref[...]
    rsum, tile_rows = None, []
    for r in range(n_rows):
        lo, hi = bounds(r)
        if r == 0 or window <= 2:
            rsum = grid_row(lo)
            for k in range(lo + 1, hi):
                rsum = rsum + grid_row(k)
        else:
            prev_lo, prev_hi = bounds(r - 1)
            if hi > prev_hi:
                rsum = rsum + grid_row(hi - 1)
            if lo > prev_lo:
                rsum = rsum - grid_row(prev_lo)
        tile_rows.append(rsum)
        if len(tile_rows) < POOL_TILE_ROWS:
            continue
        base = (r + 1 - POOL_TILE_ROWS) * GRID_W
        rows = slice(base, base + POOL_TILE)
        rs = jnp.concatenate(tile_rows, axis=0)
        tile_rows = []
        box = _dot(band, _bf(rs))
        inv = inv_ref[rows]
        mean = box * jnp.concatenate([inv, inv], axis=1)
        d = mean - u_ref[rows].astype(jnp.float32)
        y = _dot(_bf(d), w_ref[0]) * scale_ref[...]
        o_ref[rows] = _bf(y * z_ref[rows].astype(jnp.float32))


def _pool_group(u, gate, pool_w_bf, pool_scale, g, bsz, n_img_tok):
    window = POOL_WINDOWS[g]
    n_rows = n_img_tok // GRID_W
    band, inv = _pool_constants(window, n_rows)
    kern = functools.partial(_pool_kernel, window=window, n_rows=n_rows)
    img = pl.BlockSpec((n_img_tok, POOL_GROUP_W), lambda b: (b, 0))
    return pl.pallas_call(
        kern,
        grid=(bsz,),
        in_specs=[img, img,
                  pl.BlockSpec((POOL_TILE, POOL_TILE), lambda b: (0, 0)),
                  pl.BlockSpec((n_img_tok, 128), lambda b: (0, 0)),
                  pl.BlockSpec((1, POOL_GROUP_W, POOL_GROUP_W), lambda b: (g, 0, 0)),
                  pl.BlockSpec((1, POOL_GROUP_W), lambda b: (0, g))],
        out_specs=img,
        out_shape=jax.ShapeDtypeStruct((bsz * n_img_tok, POOL_GROUP_W), jnp.bfloat16),
        compiler_params=pltpu.CompilerParams(vmem_limit_bytes=VMEM_LIMIT),
        name=f"pool{window}",
    )(u, gate, band, inv, pool_w_bf, pool_scale)


def kernel(x, c, ctx, c_ctx, norm_w, w_ada, b_ada, w_in, conv_w, conv_b, a_log, dt_bias, d_skip,
           ssd_norm_w, pool_w, pool_scale, w_out, final_norm_w):
    bsz, seq, _ = x.shape
    ctx_len = ctx.shape[1]
    depth = norm_w.shape[0]
    assert depth == 1, "single-layer block: the context stream update is never consumed"
    assert seq % 512 == 0 and ctx_len % CHUNK == 0 and seq % GRID_W == 0

    mod_rows = -(-(bsz + 1) // SUBLANES) * SUBLANES
    cond = jnp.concatenate([c, c_ctx[None], jnp.zeros((mod_rows - bsz - 1, D_MODEL), c.dtype)])
    mod = _modulation(cond, w_ada[0], b_ada[0])
    shift = mod[:, :D_MODEL].reshape(mod_rows, 1, D_MODEL)
    scale = mod[:, D_MODEL:2 * D_MODEL].reshape(mod_rows, 1, D_MODEL)
    gate = mod[:, 2 * D_MODEL:].reshape(mod_rows, 1, D_MODEL)

    w_in_bf = _bf(w_in[0])
    w_dt_bf = jnp.pad(_bf(w_in[0, :, OFF_DT:]), ((0, 0), (0, DT_PAD - 2 * HEADS)))
    alog_col = a_log[0].reshape(2 * HEADS, 1)
    bias_col = dt_bias[0].reshape(2 * HEADS, 1)
    dskip_b = jnp.broadcast_to(jnp.repeat(d_skip[0], HEADDIM)[:, None], (W_SSD, CHUNK))
    conv_b2 = conv_b[0].reshape(1, CONV_DIM)

    ctx2d = ctx.reshape(bsz * ctx_len, D_MODEL)
    h_fwd, h_bwd = _projection(
        ctx2d, norm_w[0], shift[bsz:bsz + 1], scale[bsz:bsz + 1], w_in_bf, w_dt_bf, conv_w[0],
        conv_b2, alog_col, bias_col, ctx_len, ctx_len, full=False)

    x2d = x.reshape(bsz * seq, D_MODEL)
    outs = _projection(x2d, norm_w[0], shift, scale, w_in_bf, w_dt_bf, conv_w[0], conv_b2,
                       alog_col, bias_col, seq, 512, full=True, dskip_b=dskip_b, h0=h_fwd)
    u_pool, gate_pool = outs[:N_POOL_GROUPS], outs[N_POOL_GROUPS:2 * N_POOL_GROUPS]
    gate_ssd, xs_t, b_tok, c_t, dt, cum, y_part = outs[2 * N_POOL_GROUPS:]
    nc = seq // CHUNK
    pool_w_bf = _bf(pool_w[0])
    y_pool = [_pool_group(u_pool[g], gate_pool[g], pool_w_bf, pool_scale, g, bsz, seq)
              for g in range(N_POOL_GROUPS)]
    out = _backward_output(xs_t, b_tok, dt, cum, c_t, y_part, alog_col, h_bwd, y_pool, gate_ssd,
                           x2d, gate, ssd_norm_w[0], _bf(w_out[0]), final_norm_w, bsz, nc)
    return out.reshape(bsz, seq, D_MODEL)
```

```python
import functools

import numpy as np
import jax
import jax.numpy as jnp
from jax.experimental import pallas as pl
from jax.experimental.pallas import tpu as pltpu

D_MODEL = 1024
GRID_W = 64
W_POOL = 1024
W_SSD = 1024
POOL_WINDOWS = (2, 4, 8, 16)
N_POOL_GROUPS = len(POOL_WINDOWS)
POOL_GROUP_W = 256
HEADDIM = 64
HEADS = 16
GROUPS = 4
HEADS_PER_GROUP = 4
D_STATE = 128
D_CONV = 4
CONV_LEFT = 2
CHUNK = 128
GN = GROUPS * D_STATE
CONV_DIM = W_SSD + 2 * GN
OFF_POOL_Z = W_POOL
OFF_SSD_Z = 2 * W_POOL
OFF_XBC = 2 * W_POOL + W_SSD
OFF_DT = OFF_XBC + CONV_DIM
DT_PAD = 128
EPS = 1e-6
SUBLANES = 8
LANES = 128
IL_GROUPS = CHUNK // SUBLANES
CONV_SEG = 256
SSD_CHUNKS_PER_STEP = 4
VMEM_LIMIT = 56 * 1024 * 1024


def _silu(v):
    h = 0.5 * v
    return h + h * jnp.tanh(h)


def _softplus(v):
    return jnp.maximum(v, 0.0) + jnp.log1p(jnp.exp(-jnp.abs(v)))


def _bf(v):
    return v.astype(jnp.bfloat16)


def _dot(a, b):
    return jnp.dot(a, b, preferred_element_type=jnp.float32)


def _mod_kernel(c_ref, w_ref, b_ref, o_ref):
    s = _silu(c_ref[...])
    o_ref[...] = jnp.dot(s, w_ref[...], preferred_element_type=jnp.float32,
                         precision=jax.lax.Precision.HIGHEST) + b_ref[...]


def _modulation(cond_rows, w_ada, b_ada):
    rows = cond_rows.shape[0]
    n_out = w_ada.shape[1]
    tn = 1024
    return pl.pallas_call(
        _mod_kernel,
        grid=(n_out // tn,),
        in_specs=[pl.BlockSpec((rows, D_MODEL), lambda j: (0, 0)),
                  pl.BlockSpec((D_MODEL, tn), lambda j: (0, j)),
                  pl.BlockSpec((1, tn), lambda j: (0, j))],
        out_specs=pl.BlockSpec((rows, tn), lambda j: (0, j)),
        out_shape=jax.ShapeDtypeStruct((rows, n_out), jnp.float32),
        compiler_params=pltpu.CompilerParams(vmem_limit_bytes=VMEM_LIMIT),
        name="mod",
    )(cond_rows, w_ada, b_ada.reshape(1, n_out))


def _lane_cumsum(v):
    lane = jax.lax.broadcasted_iota(jnp.int32, v.shape, 1)
    shift = 1
    while shift < CHUNK:
        v = v + jnp.where(lane >= shift, pltpu.roll(v, shift, 1), 0.0)
        shift *= 2
    return v


def _proj_kernel(x_ref, xp_ref, xn_ref, nw_ref, sh_ref, sc_ref, w_ref, wdt_ref, cw_ref, cb_ref,
                 alog_ref, bias_ref, *rest, tm, tiles_per_seq, n_tiles, full):
    if full:
        (dskip_ref, h0_ref, u_ref, zp_ref, zs_ref, xs_t_ref, b_ref, c_t_ref,
         dt_ref, cum_ref, ypart_ref, pe_ref, xc_ref, mn_ref, kxs_ref, kb_ref, kc_ref, kdt_ref,
         kcum_ref, h_ref) = rest
    else:
        hf_ref, hb_ref, pe_ref, xc_ref, mn_ref, xs_t_ref, b_ref, dt_ref, cum_ref, h_ref = rest
    i = pl.program_id(0)
    pos = jnp.minimum(i, n_tiles - 1) % tiles_per_seq
    has_prev = pos > 0
    has_next = pos < tiles_per_seq - 1
    n_chunks = tm // CHUNK
    seg = CONV_SEG

    if full:
        @pl.when(i == 0)
        def _():
            for ref in (kxs_ref, kb_ref, kc_ref, kdt_ref, kcum_ref, h_ref):
                ref[...] = jnp.zeros(ref.shape, ref.dtype)

    gain = nw_ref[...] * (1.0 + sc_ref[0])

    def modulated(v):
        ms = jnp.mean(v * v, axis=-1, keepdims=True)
        return v * jax.lax.rsqrt(ms + EPS) * gain + sh_ref[0]

    m_tok = modulated(x_ref[...])
    hm = _bf(m_tok)
    for t in range(D_MODEL // LANES):
        mn_ref[t] = m_tok[:, t * LANES:(t + 1) * LANES]

    rows = [jnp.concatenate([mn_ref[t, pl.ds(q * CHUNK + b, SUBLANES, stride=IL_GROUPS), :]
                             for t in range(D_MODEL // LANES)], axis=1)
            for q in range(n_chunks) for b in range(IL_GROUPS)]
    halo = [jnp.where(has_prev, modulated(xp_ref[...]), 0.0),
            jnp.where(has_next, modulated(xn_ref[...]), 0.0)]
    hm_il = _bf(jnp.concatenate(halo + rows, axis=0))
    sub = jax.lax.broadcasted_iota(jnp.int32, (SUBLANES, seg), 0)

    def conv_stage(j, slot):
        is_x = j < W_SSD
        is_b = W_SSD <= j < W_SSD + GN

        def matmul():
            lo = (OFF_XBC if full else 0) + j
            pe_ref[slot] = _dot(hm_il, w_ref[:, lo:lo + seg])

        def group(q, b):
            lo = 2 * SUBLANES + q * CHUNK + b * SUBLANES
            return pe_ref[slot, lo:lo + SUBLANES]

        def shifted(q, b, delta):
            bb = b + delta
            if 0 <= bb < IL_GROUPS:
                return group(q, bb)
            if bb < 0:
                bb += IL_GROUPS
                if q == 0:
                    first = pe_ref[slot, bb - SUBLANES:bb - SUBLANES + 1]
                else:
                    row = 2 * SUBLANES + (q - 1) * CHUNK + bb * SUBLANES + SUBLANES - 1
                    first = pe_ref[slot, row:row + 1]
                return jnp.where(sub == 0, first, pltpu.roll(group(q, bb), 1, 0))
            bb -= IL_GROUPS
            if q == n_chunks - 1:
                last = pe_ref[slot, SUBLANES + bb:SUBLANES + bb + 1]
            else:
                nxt = 2 * SUBLANES + (q + 1) * CHUNK + bb * SUBLANES
                last = pe_ref[slot, nxt:nxt + 1]
            return jnp.where(sub == SUBLANES - 1, last, pltpu.roll(group(q, bb), SUBLANES - 1, 0))

        def epilogue():
            taps = [cw_ref[k:k + 1, j:j + seg] for k in range(D_CONV)]
            bias = cb_ref[:, j:j + seg]
            for q in range(n_chunks):
                for b in range(IL_GROUPS):
                    acc = bias
                    for k in range(D_CONV):
                        acc = acc + shifted(q, b, k - CONV_LEFT) * taps[k]
                    lo = q * CHUNK + b * SUBLANES
                    act = _silu(acc)
                    for t in range(seg // LANES):
                        xc_ref[slot, t, lo:lo + SUBLANES] = act[:, t * LANES:(t + 1) * LANES]
            for q in range(n_chunks):
                xc = jnp.concatenate(
                    [jnp.concatenate(
                        [xc_ref[slot, t, pl.ds(q * CHUNK + (m % 2) * (CHUNK // 2) + m // 2,
                                               SUBLANES, stride=SUBLANES), :]
                         for t in range(seg // LANES)], axis=1)
                     for m in range(IL_GROUPS)], axis=0)
                if is_b:
                    b_ref[q * CHUNK:(q + 1) * CHUNK, j - W_SSD:j - W_SSD + seg] = _bf(xc)
                else:
                    dst, off = (xs_t_ref, j) if is_x else (c_t_ref, j - W_SSD - GN)
                    dst[q, off:off + seg] = _bf(xc.T)

        return matmul, epilogue

    def plain_stage(cols, finish):
        box = []
        return (lambda: box.append(_dot(hm, w_ref[:, cols]))), (lambda: finish(box.pop()))

    def dt_stage():
        box = []

        def epilogue():
            p_dt = box.pop()
            a_col = -jnp.exp(alog_ref[...])
            for q in range(n_chunks):
                dt = _softplus(p_dt[q * CHUNK:(q + 1) * CHUNK].T[:2 * HEADS] + bias_ref[...])
                dt_ref[q] = dt
                cum_ref[q] = _lane_cumsum(dt * a_col)

        return (lambda: box.append(_dot(hm, wdt_ref[...]))), epilogue

    def store_to(ref, cols=None, act=None):
        def finish(v):
            v = v if act is None else _bf(act(v))
            if cols is None:
                ref[...] = v
            else:
                ref[:, cols] = v
        return finish

    def store_pair(ref, g, act=None):
        def finish(v):
            for n in range(2):
                part = v[:, n * POOL_GROUP_W:(n + 1) * POOL_GROUP_W]
                ref[g + n] = _bf(part if act is None else act(part))
        return finish

    light, heavy = [], []
    if full:
        for g in range(0, N_POOL_GROUPS, 2):
            light.append(plain_stage(slice(g * POOL_GROUP_W, (g + 2) * POOL_GROUP_W),
                                     store_pair(u_ref, g)))
    light.append(dt_stage())
    for j in range(0, CONV_DIM, seg):
        if full or j < W_SSD + GN:
            heavy.append(conv_stage(j, len(heavy) % 2))
    if full:
        for g in range(0, N_POOL_GROUPS, 2):
            zcols = slice(OFF_POOL_Z + g * POOL_GROUP_W, OFF_POOL_Z + (g + 2) * POOL_GROUP_W)
            light.append(plain_stage(zcols, store_pair(zp_ref, g, act=_silu)))
        for j in range(0, W_SSD, seg):
            light.append(plain_stage(slice(OFF_SSD_Z + j, OFF_SSD_Z + j + seg),
                                     store_to(zs_ref, cols=slice(j, j + seg), act=_silu)))
    stages = []
    while light or heavy:
        if light:
            stages.append(light.pop(0))
        if heavy:
            stages.append(heavy.pop(0))

    sweep = []
    if full:
        swept = jnp.maximum(i - 1, 0)
        h_ref[...] = jnp.where(swept % tiles_per_seq == 0, h0_ref[0], h_ref[...])
        src = jax.lax.broadcasted_iota(jnp.int32, (CHUNK, CHUNK), 0)
        dst = jax.lax.broadcasted_iota(jnp.int32, (CHUNK, CHUNK), 1)
        a_b = -jnp.exp(alog_ref[HEADS:])
        pairs = []
        for q in range(n_chunks):
            pairs += _fwd_chunk_slices(q, kxs_ref, kb_ref, kc_ref, kdt_ref, kcum_ref, dskip_ref,
                                       ypart_ref, h_ref, a_b, src <= dst, src == dst)
        sweep = [pairs[0][0]]
        for p in range(len(pairs)):
            nxt = pairs[p + 1][0] if p + 1 < len(pairs) else (lambda: None)
            sweep.append(functools.partial(lambda a, w: (a(), w()), pairs[p][1], nxt))

    stages[0][0]()
    done = 0
    for k, (_, epilogue) in enumerate(stages):
        if k + 1 < len(stages):
            stages[k + 1][0]()
        epilogue()
        upto = -(-len(sweep) * (k + 1) // len(stages))
        for piece in sweep[done:upto]:
            piece()
        done = upto

    if full:
        for kept, ref in ((kxs_ref, xs_t_ref), (kb_ref, b_ref), (kc_ref, c_t_ref),
                          (kdt_ref, dt_ref), (kcum_ref, cum_ref)):
            kept[...] = ref[...]
    else:
        a_b = -jnp.exp(alog_ref[HEADS:])
        for reverse, out_ref in ((False, hf_ref), (True, hb_ref)):
            h_ref[...] = jnp.zeros(h_ref.shape, h_ref.dtype)
            for q in (range(n_chunks - 1, -1, -1) if reverse else range(n_chunks)):
                if reverse:
                    dt_b, cum_b = dt_ref[q, HEADS:], cum_ref[q, HEADS:]
                    scale_in = dt_b * jnp.exp(cum_b - dt_b * a_b)
                    decay = jnp.exp(cum_b[:, CHUNK - 1:CHUNK])
                else:
                    dt_f, cum_f = dt_ref[q, :HEADS], cum_ref[q, :HEADS]
                    tot_f = cum_f[:, CHUNK - 1:CHUNK]
                    scale_in, decay = dt_f * jnp.exp(tot_f - cum_f), jnp.exp(tot_f)
                _state_update(h_ref, xs_t_ref, b_ref, q, scale_in, decay)
            out_ref[0] = h_ref[...]


def _projection(x2d, norm_w, shift, scale, w_bf, wdt_bf, conv_w, conv_b, alog_col, bias_col,
                seq_len, tm, full, dskip_b=None, h0=None):
    n_tok = x2d.shape[0]
    tiles_per_seq = seq_len // tm
    n_tiles = n_tok // tm
    n_mod = shift.shape[0]
    nct = n_tok // CHUNK
    per = tm // SUBLANES
    last_halo = n_tok // SUBLANES - 1
    kern = functools.partial(_proj_kernel, tm=tm, tiles_per_seq=tiles_per_seq, n_tiles=n_tiles,
                             full=full)
    tile = lambda i: jnp.minimum(i, n_tiles - 1)
    mod_map = (lambda i: (tile(i) // tiles_per_seq, 0, 0)) if n_mod > 1 else (lambda i: (0, 0, 0))
    mod_spec = pl.BlockSpec((1, 1, D_MODEL), mod_map)
    const = lambda i: (0, 0)
    tok = lambda i: (tile(i), 0)
    chunk3 = lambda i: (tile(i), 0, 0)
    q = tm // CHUNK
    xs_t = (jax.ShapeDtypeStruct((nct, W_SSD, CHUNK), jnp.bfloat16),
            pl.BlockSpec((q, W_SSD, CHUNK), chunk3))
    b_tok = (jax.ShapeDtypeStruct((n_tok, GN), jnp.bfloat16), pl.BlockSpec((tm, GN), tok))
    c_t = (jax.ShapeDtypeStruct((nct, GN, CHUNK), jnp.bfloat16), pl.BlockSpec((q, GN, CHUNK), chunk3))
    dt = (jax.ShapeDtypeStruct((nct, 2 * HEADS, CHUNK), jnp.float32),
          pl.BlockSpec((q, 2 * HEADS, CHUNK), chunk3))
    if full:
        pooled = (jax.ShapeDtypeStruct((N_POOL_GROUPS, n_tok, POOL_GROUP_W), jnp.bfloat16),
                  pl.BlockSpec((N_POOL_GROUPS, tm, POOL_GROUP_W), lambda i: (0, tile(i), 0)))
        zs = (jax.ShapeDtypeStruct((n_tok, W_SSD), jnp.bfloat16), pl.BlockSpec((tm, W_SSD), tok))
        swept = lambda i: jnp.maximum(i - 1, 0)
        y_part = (jax.ShapeDtypeStruct((nct, W_SSD, CHUNK), jnp.bfloat16),
                  pl.BlockSpec((q, W_SSD, CHUNK), lambda i: (swept(i), 0, 0)))
        outs = [pooled, pooled, zs, xs_t, b_tok, c_t, dt, dt, y_part]
        extra_in = [pl.BlockSpec((W_SSD, CHUNK), const),
                    pl.BlockSpec((1, W_SSD, D_STATE), lambda i: (swept(i) // tiles_per_seq, 0, 0))]
        extra_args = [dskip_b, h0]
        extra_scratch = [pltpu.VMEM((q, W_SSD, CHUNK), jnp.bfloat16),
                         pltpu.VMEM((tm, GN), jnp.bfloat16),
                         pltpu.VMEM((q, GN, CHUNK), jnp.bfloat16),
                         pltpu.VMEM((q, 2 * HEADS, CHUNK), jnp.float32),
                         pltpu.VMEM((q, 2 * HEADS, CHUNK), jnp.float32),
                         pltpu.VMEM((W_SSD, D_STATE), jnp.float32)]
    else:
        assert tiles_per_seq == 1, "prefix states are computed from one whole sequence per step"
        state = (jax.ShapeDtypeStruct((n_tiles, W_SSD, D_STATE), jnp.float32),
                 pl.BlockSpec((1, W_SSD, D_STATE), chunk3))
        outs = [state, state]
        extra_in, extra_args = [], []
        extra_scratch = [pltpu.VMEM((q, W_SSD, CHUNK), jnp.bfloat16),
                         pltpu.VMEM((tm, GN), jnp.bfloat16),
                         pltpu.VMEM((q, 2 * HEADS, CHUNK), jnp.float32),
                         pltpu.VMEM((q, 2 * HEADS, CHUNK), jnp.float32),
                         pltpu.VMEM((W_SSD, D_STATE), jnp.float32)]
    return pl.pallas_call(
        kern,
        grid=(n_tiles + 1 if full else n_tiles,),
        in_specs=[pl.BlockSpec((tm, D_MODEL), tok),
                  pl.BlockSpec((SUBLANES, D_MODEL), lambda i: (jnp.maximum(tile(i) * per - 1, 0), 0)),
                  pl.BlockSpec((SUBLANES, D_MODEL),
                               lambda i: (jnp.minimum((tile(i) + 1) * per, last_halo), 0)),
                  pl.BlockSpec((1, D_MODEL), const),
                  mod_spec, mod_spec,
                  (pl.BlockSpec(w_bf.shape, const) if full else
                   pl.BlockSpec((D_MODEL, W_SSD + GN), lambda i: (0, OFF_XBC // (W_SSD + GN)))),
                  pl.BlockSpec((D_MODEL, DT_PAD), const),
                  pl.BlockSpec((D_CONV, CONV_DIM), const),
                  pl.BlockSpec((1, CONV_DIM), const),
                  pl.BlockSpec((2 * HEADS, 1), const),
                  pl.BlockSpec((2 * HEADS, 1), const)] + extra_in,
        out_specs=[o[1] for o in outs],
        out_shape=[o[0] for o in outs],
        scratch_shapes=[pltpu.VMEM((2, tm + 2 * SUBLANES, CONV_SEG), jnp.float32),
                        pltpu.VMEM((2, CONV_SEG // LANES, tm, LANES), jnp.float32),
                        pltpu.VMEM((D_MODEL // LANES, tm, LANES), jnp.float32)] + extra_scratch,
        compiler_params=pltpu.CompilerParams(
            dimension_semantics=("arbitrary",), vmem_limit_bytes=VMEM_LIMIT),
        name="proj" if full else "proj_ctx",
    )(x2d, x2d, x2d, norm_w.reshape(1, D_MODEL), shift, scale, w_bf, wdt_bf, conv_w, conv_b,
      alog_col, bias_col, *extra_args)


def _tok_rows(q):
    return pl.ds(q * CHUNK, CHUNK)


def _state_update_group(h_ref, xs_t_ref, b_ref, q, g, scale_in, chunk_decay):
    bg = b_ref[_tok_rows(q), g * D_STATE:(g + 1) * D_STATE]
    xd = []
    for r in range(HEADS_PER_GROUP):
        h = g * HEADS_PER_GROUP + r
        x_h = xs_t_ref[q, h * HEADDIM:(h + 1) * HEADDIM].astype(jnp.float32)
        xd.append(_bf(x_h * scale_in[h:h + 1]))
    s_new = _dot(jnp.concatenate(xd, axis=0), bg)
    for r in range(HEADS_PER_GROUP):
        h = g * HEADS_PER_GROUP + r
        hr = slice(h * HEADDIM, (h + 1) * HEADDIM)
        h_ref[hr] = h_ref[hr] * chunk_decay[h:h + 1] + s_new[r * HEADDIM:(r + 1) * HEADDIM]


def _state_update(h_ref, xs_t_ref, b_ref, q, scale_in, chunk_decay):
    for g in range(GROUPS):
        _state_update_group(h_ref, xs_t_ref, b_ref, q, g, scale_in, chunk_decay)


def _fwd_chunk_slices(q, xs_t_ref, b_ref, c_t_ref, dt_ref, cum_ref, dskip_ref, y_ref, h_ref, a_b,
                      causal, is_diag):
    ctx = {}

    def setup():
        dt_f, cum_f = dt_ref[q, :HEADS], cum_ref[q, :HEADS]
        tot_f = cum_f[:, CHUNK - 1:CHUNK]
        dt_b, cum_b = dt_ref[q, HEADS:], cum_ref[q, HEADS:]
        cumx_b = cum_b - dt_b * a_b
        ctx["scale_in"] = dt_f * jnp.exp(tot_f - cum_f)
        ctx["chunk_decay"] = jnp.exp(tot_f)
        ctx["col_terms"] = jnp.concatenate(
            [jnp.log(dt_f) - cum_f, jnp.log(dt_b) + cumx_b,
             jnp.zeros((CHUNK - 2 * HEADS, CHUNK), jnp.float32)], axis=0).T
        ctx["row_f"], ctx["row_b"] = cum_f, -cumx_b
        ctx["decay_out_f"] = jnp.exp(cum_f)
        ctx["dt_b"] = dt_b

    def weights(g):
        if g == 0:
            setup()
        col_terms, row_f, row_b = ctx["col_terms"], ctx["row_f"], ctx["row_b"]
        bg = b_ref[_tok_rows(q), g * D_STATE:(g + 1) * D_STATE]
        cg_t = c_t_ref[q, g * D_STATE:(g + 1) * D_STATE]
        rows = slice(g * HEADS_PER_GROUP * HEADDIM, (g + 1) * HEADS_PER_GROUP * HEADDIM)
        g_t = _dot(bg, cg_t)
        ctx["g_diag", g] = jnp.sum(jnp.where(is_diag, g_t, 0.0), axis=0, keepdims=True)
        ctx["y_off", g] = _dot(_bf(h_ref[rows]), cg_t)
        for r in range(HEADS_PER_GROUP):
            h = g * HEADS_PER_GROUP + r
            col_f = jnp.broadcast_to(col_terms[:, h:h + 1], (CHUNK, CHUNK))
            col_b = jnp.broadcast_to(col_terms[:, HEADS + h:HEADS + h + 1], (CHUNK, CHUNK))
            expo = jnp.where(causal, col_f + row_f[h:h + 1], col_b + row_b[h:h + 1])
            ctx["w_t", h] = _bf(g_t * jnp.exp(expo))

    def apply(g):
        y_off, g_diag = ctx.pop(("y_off", g)), ctx.pop(("g_diag", g))
        for r in range(HEADS_PER_GROUP):
            h = g * HEADS_PER_GROUP + r
            hr = slice(h * HEADDIM, (h + 1) * HEADDIM)
            x_bf = xs_t_ref[q, hr]
            y_h = _dot(x_bf, ctx.pop(("w_t", h)))
            y_h = y_h + y_off[r * HEADDIM:(r + 1) * HEADDIM] * ctx["decay_out_f"][h:h + 1]
            skip = dskip_ref[hr] + g_diag * ctx["dt_b"][h:h + 1]
            y_ref[q, hr] = _bf(y_h + skip * x_bf.astype(jnp.float32))
        _state_update_group(h_ref, xs_t_ref, b_ref, q, g, ctx["scale_in"], ctx["chunk_decay"])

    return [(functools.partial(weights, g), functools.partial(apply, g)) for g in range(GROUPS)]


def _bwd_out_kernel(xs_t_ref, b_ref, dt_ref, cum_ref, alog_ref, h0_ref, c_t_ref, ypart_ref,
                    yp_ref, zs_ref, x_ref, gate_ref, snw_ref, wout_ref,
                    fnw_ref, o_ref, h_ref, y_ref, *, cps):
    @pl.when(pl.program_id(1) == 0)
    def _():
        h_ref[...] = h0_ref[0]

    a_b = -jnp.exp(alog_ref[HEADS:])

    def chunk(q):
        dt_b, cum_b = dt_ref[q, HEADS:], cum_ref[q, HEADS:]
        tot_b = cum_b[:, CHUNK - 1:CHUNK]
        cumx_b = cum_b - dt_b * a_b
        decay_out = jnp.exp(tot_b - cumx_b)
        y_parts = []
        for g in range(GROUPS):
            cg_t = c_t_ref[q, g * D_STATE:(g + 1) * D_STATE]
            rows = slice(g * HEADS_PER_GROUP * HEADDIM, (g + 1) * HEADS_PER_GROUP * HEADDIM)
            y_off = _dot(_bf(h_ref[rows]), cg_t)
            for r in range(HEADS_PER_GROUP):
                h = g * HEADS_PER_GROUP + r
                hr = slice(h * HEADDIM, (h + 1) * HEADDIM)
                y_parts.append(ypart_ref[q, hr].astype(jnp.float32)
                               + y_off[r * HEADDIM:(r + 1) * HEADDIM] * decay_out[h:h + 1])
        y_ref[_tok_rows(q), :] = jnp.concatenate(y_parts, axis=0).T
        _state_update(h_ref, xs_t_ref, b_ref, q, dt_b * jnp.exp(cumx_b), jnp.exp(tot_b))

    acc = jnp.zeros((cps * CHUNK, D_MODEL), jnp.float32)
    for i in range(max(cps, N_POOL_GROUPS)):
        if i < cps:
            chunk(cps - 1 - i)
        if i < N_POOL_GROUPS:
            acc = acc + _dot(yp_ref[i], wout_ref[i * POOL_GROUP_W:(i + 1) * POOL_GROUP_W])

    gw = W_SSD // GROUPS
    for g in range(GROUPS):
        cols = slice(g * gw, (g + 1) * gw)
        gated = y_ref[:, cols] * zs_ref[:, cols].astype(jnp.float32)
        ms = jnp.mean(gated * gated, axis=-1, keepdims=True)
        yn = gated * jax.lax.rsqrt(ms + EPS) * snw_ref[:, cols]
        acc = acc + _dot(_bf(yn), wout_ref[W_POOL + g * gw:W_POOL + (g + 1) * gw])
    hres = x_ref[...] + gate_ref[0] * acc
    ms = jnp.mean(hres * hres, axis=-1, keepdims=True)
    o_ref[...] = hres * jax.lax.rsqrt(ms + EPS) * fnw_ref[...]


def _backward_output(xs_t, b_tok, dt, cum, c_t, y_part, alog_col, h0, y_pool, gate_ssd, x2d, gate,
                     ssd_norm_w, w_out_bf, final_norm_w, bsz, n_chunks):
    n_tok = b_tok.shape[0]
    cps = min(n_chunks, SSD_CHUNKS_PER_STEP)
    n_steps = n_chunks // cps
    tm = cps * CHUNK
    block_of = lambda b, s: b * n_steps + (n_steps - 1 - s)
    tok = lambda b, s: (block_of(b, s), 0)
    chunk3 = lambda b, s: (block_of(b, s), 0, 0)
    const2 = lambda b, s: (0, 0)
    per_seq = lambda b, s: (b, 0, 0)
    head_spec = pl.BlockSpec((cps, 2 * HEADS, CHUNK), chunk3)
    return pl.pallas_call(
        functools.partial(_bwd_out_kernel, cps=cps),
        grid=(bsz, n_steps),
        in_specs=[pl.BlockSpec((cps, W_SSD, CHUNK), chunk3),
                  pl.BlockSpec((tm, GN), tok),
                  head_spec, head_spec,
                  pl.BlockSpec((2 * HEADS, 1), const2),
                  pl.BlockSpec((1, W_SSD, D_STATE), per_seq),
                  pl.BlockSpec((cps, GN, CHUNK), chunk3),
                  pl.BlockSpec((cps, W_SSD, CHUNK), chunk3),
                  pl.BlockSpec((N_POOL_GROUPS, tm, POOL_GROUP_W),
                               lambda b, s: (0, block_of(b, s), 0)),
                  pl.BlockSpec((tm, W_SSD), tok),
                  pl.BlockSpec((tm, D_MODEL), tok),
                  pl.BlockSpec((1, 1, D_MODEL), per_seq),
                  pl.BlockSpec((1, W_SSD), const2),
                  pl.BlockSpec((W_POOL + W_SSD, D_MODEL), const2),
                  pl.BlockSpec((1, D_MODEL), const2)],
        out_specs=pl.BlockSpec((tm, D_MODEL), tok),
        out_shape=jax.ShapeDtypeStruct((n_tok, D_MODEL), jnp.float32),
        scratch_shapes=[pltpu.VMEM((W_SSD, D_STATE), jnp.float32),
                        pltpu.VMEM((tm, W_SSD), jnp.float32)],
        compiler_params=pltpu.CompilerParams(
            dimension_semantics=("arbitrary", "arbitrary"), vmem_limit_bytes=VMEM_LIMIT),
        name="bwd_out",
    )(xs_t, b_tok, dt, cum, alog_col, h0, c_t, y_part, y_pool, gate_ssd, x2d, gate,
      ssd_norm_w.reshape(1, W_SSD), w_out_bf, final_norm_w.reshape(1, D_MODEL))


POOL_TILE_ROWS = 4
POOL_TILE = POOL_TILE_ROWS * GRID_W


def _pool_constants(window, n_rows):
    lo_off, hi_off = -(window // 2), window - window // 2
    col = np.arange(GRID_W)
    lo = np.clip(col + lo_off, 0, GRID_W)
    hi = np.clip(col + hi_off, 0, GRID_W)
    band = ((col[None, :] >= lo[:, None]) & (col[None, :] < hi[:, None])).astype(np.float32)
    band_tile = np.kron(np.eye(POOL_TILE_ROWS, dtype=np.float32), band)
    row = np.arange(n_rows)
    cnt_r = np.clip(row + hi_off, 0, n_rows) - np.clip(row + lo_off, 0, n_rows)
    inv = 1.0 / (cnt_r[:, None] * (hi - lo)[None, :]).astype(np.float64)
    inv = np.broadcast_to(inv.reshape(-1, 1), (n_rows * GRID_W, 128)).astype(np.float32)
    return jnp.asarray(band_tile, jnp.bfloat16), jnp.asarray(inv)


def _pool_kernel(u_ref, z_ref, band_ref, inv_ref, w_ref, scale_ref, o_ref, *, n_rows):
    for g, window in enumerate(POOL_WINDOWS):
        @pl.when(pl.program_id(0) == g)
        def _(window=window):
            _pool_image(u_ref.at[0], z_ref.at[0], band_ref.at[0], inv_ref.at[0], w_ref, scale_ref,
                        o_ref.at[0], window, n_rows)


def _pool_image(u_ref, z_ref, band_ref, inv_ref, w_ref, scale_ref, o_ref, window, n_rows):
    def grid_row(r):
        return u_ref[r * GRID_W:(r + 1) * GRID_W].astype(jnp.float32)

    def bounds(r):
        return max(r - window // 2, 0), min(r + window - window // 2, n_rows)

    band = band_ref[...]
    rsum, tile_rows = None, []
    for r in range(n_rows):
        lo, hi = bounds(r)
        if r == 0 or window <= 2:
            rsum = grid_row(lo)
            for k in range(lo + 1, hi):
                rsum = rsum + grid_row(k)
        else:
            prev_lo, prev_hi = bounds(r - 1)
            if hi > prev_hi:
                rsum = rsum + grid_row(hi - 1)
            if lo > prev_lo:
                rsum = rsum - grid_row(prev_lo)
        tile_rows.append(rsum)
        if len(tile_rows) < POOL_TILE_ROWS:
            continue
        base = (r + 1 - POOL_TILE_ROWS) * GRID_W
        rows = slice(base, base + POOL_TILE)
        rs = jnp.concatenate(tile_rows, axis=0)
        tile_rows = []
        box = _dot(band, _bf(rs))
        inv = inv_ref[rows]
        mean = box * jnp.concatenate([inv, inv], axis=1)
        d = mean - u_ref[rows].astype(jnp.float32)
        y = _dot(_bf(d), w_ref[0]) * scale_ref[...]
        o_ref[rows] = _bf(y * z_ref[rows].astype(jnp.float32))


def _pool_mixer(u, gate, pool_w_bf, pool_scale, bsz, n_img_tok):
    n_rows = n_img_tok // GRID_W
    consts = [_pool_constants(window, n_rows) for window in POOL_WINDOWS]
    band = jnp.stack([c[0] for c in consts])
    inv = jnp.stack([c[1] for c in consts])
    img = pl.BlockSpec((1, n_img_tok, POOL_GROUP_W), lambda g, b: (g, b, 0))
    per_group = lambda g, b: (g, 0, 0)
    return pl.pallas_call(
        functools.partial(_pool_kernel, n_rows=n_rows),
        grid=(N_POOL_GROUPS, bsz),
        in_specs=[img, img,
                  pl.BlockSpec((1, POOL_TILE, POOL_TILE), per_group),
                  pl.BlockSpec((1, n_img_tok, 128), per_group),
                  pl.BlockSpec((1, POOL_GROUP_W, POOL_GROUP_W), per_group),
                  pl.BlockSpec((1, POOL_GROUP_W), lambda g, b: (0, g))],
        out_specs=img,
        out_shape=jax.ShapeDtypeStruct(u.shape, jnp.bfloat16),
        compiler_params=pltpu.CompilerParams(
            dimension_semantics=("arbitrary", "arbitrary"), vmem_limit_bytes=VMEM_LIMIT),
        name="pool",
    )(u, gate, band, inv, pool_w_bf, pool_scale)


def kernel(x, c, ctx, c_ctx, norm_w, w_ada, b_ada, w_in, conv_w, conv_b, a_log, dt_bias, d_skip,
           ssd_norm_w, pool_w, pool_scale, w_out, final_norm_w):
    bsz, seq, _ = x.shape
    ctx_len = ctx.shape[1]
    depth = norm_w.shape[0]
    assert depth == 1, "single-layer block: the context stream update is never consumed"
    assert seq % 512 == 0 and ctx_len % CHUNK == 0 and seq % GRID_W == 0

    mod_rows = -(-(bsz + 1) // SUBLANES) * SUBLANES
    cond = jnp.concatenate([c, c_ctx[None], jnp.zeros((mod_rows - bsz - 1, D_MODEL), c.dtype)])
    mod = _modulation(cond, w_ada[0], b_ada[0])
    shift = mod[:, :D_MODEL].reshape(mod_rows, 1, D_MODEL)
    scale = mod[:, D_MODEL:2 * D_MODEL].reshape(mod_rows, 1, D_MODEL)
    gate = mod[:, 2 * D_MODEL:].reshape(mod_rows, 1, D_MODEL)

    w_in_bf = _bf(w_in[0])
    w_dt_bf = jnp.pad(_bf(w_in[0, :, OFF_DT:]), ((0, 0), (0, DT_PAD - 2 * HEADS)))
    alog_col = a_log[0].reshape(2 * HEADS, 1)
    bias_col = dt_bias[0].reshape(2 * HEADS, 1)
    dskip_b = jnp.broadcast_to(jnp.repeat(d_skip[0], HEADDIM)[:, None], (W_SSD, CHUNK))
    conv_b2 = conv_b[0].reshape(1, CONV_DIM)

    ctx2d = ctx.reshape(bsz * ctx_len, D_MODEL)
    h_fwd, h_bwd = _projection(
        ctx2d, norm_w[0], shift[bsz:bsz + 1], scale[bsz:bsz + 1], w_in_bf, w_dt_bf, conv_w[0],
        conv_b2, alog_col, bias_col, ctx_len, ctx_len, full=False)

    x2d = x.reshape(bsz * seq, D_MODEL)
    outs = _projection(x2d, norm_w[0], shift, scale, w_in_bf, w_dt_bf, conv_w[0], conv_b2,
                       alog_col, bias_col, seq, 512, full=True, dskip_b=dskip_b, h0=h_fwd)
    u_pool, gate_pool, gate_ssd, xs_t, b_tok, c_t, dt, cum, y_part = outs
    nc = seq // CHUNK
    y_pool = _pool_mixer(u_pool, gate_pool, _bf(pool_w[0]), pool_scale, bsz, seq)
    out = _backward_output(xs_t, b_tok, dt, cum, c_t, y_part, alog_col, h_bwd, y_pool, gate_ssd,
                           x2d, gate, ssd_norm_w[0], _bf(w_out[0]), final_norm_w, bsz, nc)
    return out.reshape(bsz, seq, D_MODEL)
```

```python
import functools

import numpy as np
import jax
import jax.numpy as jnp
from jax.experimental import pallas as pl
from jax.experimental.pallas import tpu as pltpu

D_MODEL = 1024
GRID_W = 64
W_POOL = 1024
W_SSD = 1024
POOL_WINDOWS = (2, 4, 8, 16)
N_POOL_GROUPS = len(POOL_WINDOWS)
POOL_GROUP_W = 256
HEADDIM = 64
HEADS = 16
GROUPS = 4
HEADS_PER_GROUP = 4
D_STATE = 128
D_CONV = 4
CONV_LEFT = 2
CHUNK = 128
GN = GROUPS * D_STATE
CONV_DIM = W_SSD + 2 * GN
OFF_POOL_Z = W_POOL
OFF_SSD_Z = 2 * W_POOL
OFF_XBC = 2 * W_POOL + W_SSD
OFF_DT = OFF_XBC + CONV_DIM
DT_PAD = 128
EPS = 1e-6
SUBLANES = 8
LANES = 128
IL_GROUPS = CHUNK // SUBLANES
CONV_SEG = 256
SSD_CHUNKS_PER_STEP = 4
PROJ_TILE = 512
VMEM_LIMIT = 56 * 1024 * 1024


def _silu(v):
    h = 0.5 * v
    return h + h * jnp.tanh(h)


def _softplus(v):
    return jnp.maximum(v, 0.0) + jnp.log1p(jnp.exp(-jnp.abs(v)))


def _bf(v):
    return v.astype(jnp.bfloat16)


def _dot(a, b):
    return jnp.dot(a, b, preferred_element_type=jnp.float32)


def _mod_kernel(c_ref, w_ref, b_ref, o_ref):
    s, w = _silu(c_ref[...]), w_ref[...]
    s_hi, w_hi = _bf(s), _bf(w)
    s_lo = _bf(s - s_hi.astype(jnp.float32))
    w_lo = _bf(w - w_hi.astype(jnp.float32))
    both = _dot(jnp.concatenate([s_hi, s_lo], axis=0), w_hi)
    rows = s.shape[0]
    o_ref[...] = both[:rows] + both[rows:] + _dot(s_hi, w_lo) + b_ref[...]


def _modulation(cond_rows, w_ada, b_ada):
    rows = cond_rows.shape[0]
    n_out = w_ada.shape[1]
    tn = 1024
    return pl.pallas_call(
        _mod_kernel,
        grid=(n_out // tn,),
        in_specs=[pl.BlockSpec((rows, D_MODEL), lambda j: (0, 0)),
                  pl.BlockSpec((D_MODEL, tn), lambda j: (0, j)),
                  pl.BlockSpec((1, tn), lambda j: (0, j))],
        out_specs=pl.BlockSpec((rows, tn), lambda j: (0, j)),
        out_shape=jax.ShapeDtypeStruct((rows, n_out), jnp.float32),
        compiler_params=pltpu.CompilerParams(vmem_limit_bytes=VMEM_LIMIT),
        name="mod",
    )(cond_rows, w_ada, b_ada.reshape(1, n_out))


def _lane_cumsum(v):
    lane = jax.lax.broadcasted_iota(jnp.int32, v.shape, 1)
    shift = 1
    while shift < CHUNK:
        v = v + jnp.where(lane >= shift, pltpu.roll(v, shift, 1), 0.0)
        shift *= 2
    return v


def _proj_kernel(x_ref, xp_ref, xn_ref, nw_ref, sh_ref, sc_ref, w_ref, wdt_ref, cw_ref, cb_ref,
                 alog_ref, bias_ref, *rest, tm, tiles_per_seq, n_tiles, full):
    if full:
        (dskip_ref, h0_ref, u_ref, zp_ref, zs_ref, xs_t_ref, b_ref, c_t_ref,
         dt_ref, cum_ref, ypart_ref, pe_ref, xc_ref, mn_ref, kxs_ref, kb_ref, kc_ref, kdt_ref,
         kcum_ref, h_ref) = rest
    else:
        hf_ref, hb_ref, pe_ref, xc_ref, mn_ref, xs_t_ref, b_ref, dt_ref, cum_ref, h_ref = rest
    i = pl.program_id(0)
    pos = jnp.minimum(i, n_tiles - 1) % tiles_per_seq
    has_prev = pos > 0
    has_next = pos < tiles_per_seq - 1
    n_chunks = tm // CHUNK
    seg = CONV_SEG

    if full:
        @pl.when(i == 0)
        def _():
            for ref in (kxs_ref, kb_ref, kc_ref, kdt_ref, kcum_ref, h_ref):
                ref[...] = jnp.zeros(ref.shape, ref.dtype)

    gain = nw_ref[...] * (1.0 + sc_ref[0])

    def modulated(v):
        ms = jnp.mean(v * v, axis=-1, keepdims=True)
        return v * jax.lax.rsqrt(ms + EPS) * gain + sh_ref[0]

    m_tok = modulated(x_ref[...])
    hm = _bf(m_tok)
    for t in range(D_MODEL // LANES):
        mn_ref[t] = m_tok[:, t * LANES:(t + 1) * LANES]

    rows = [jnp.concatenate([mn_ref[t, pl.ds(q * CHUNK + b, SUBLANES, stride=IL_GROUPS), :]
                             for t in range(D_MODEL // LANES)], axis=1)
            for q in range(n_chunks) for b in range(IL_GROUPS)]
    halo = [jnp.where(has_prev, modulated(xp_ref[...]), 0.0),
            jnp.where(has_next, modulated(xn_ref[...]), 0.0)]
    hm_il = _bf(jnp.concatenate(halo + rows, axis=0))
    sub = jax.lax.broadcasted_iota(jnp.int32, (SUBLANES, seg), 0)

    def conv_stage(j, slot):
        is_x = j < W_SSD
        is_b = W_SSD <= j < W_SSD + GN

        def matmul():
            lo = (OFF_XBC if full else 0) + j
            pe_ref[slot] = _dot(hm_il, w_ref[:, lo:lo + seg])

        def group(q, b):
            lo = 2 * SUBLANES + q * CHUNK + b * SUBLANES
            return pe_ref[slot, lo:lo + SUBLANES]

        def shifted(q, b, delta):
            bb = b + delta
            if 0 <= bb < IL_GROUPS:
                return group(q, bb)
            if bb < 0:
                bb += IL_GROUPS
                if q == 0:
                    first = pe_ref[slot, bb - SUBLANES:bb - SUBLANES + 1]
                else:
                    row = 2 * SUBLANES + (q - 1) * CHUNK + bb * SUBLANES + SUBLANES - 1
                    first = pe_ref[slot, row:row + 1]
                return jnp.where(sub == 0, first, pltpu.roll(group(q, bb), 1, 0))
            bb -= IL_GROUPS
            if q == n_chunks - 1:
                last = pe_ref[slot, SUBLANES + bb:SUBLANES + bb + 1]
            else:
                nxt = 2 * SUBLANES + (q + 1) * CHUNK + bb * SUBLANES
                last = pe_ref[slot, nxt:nxt + 1]
            return jnp.where(sub == SUBLANES - 1, last, pltpu.roll(group(q, bb), SUBLANES - 1, 0))

        def epilogue():
            taps = [cw_ref[k:k + 1, j:j + seg] for k in range(D_CONV)]
            bias = cb_ref[:, j:j + seg]
            for q in range(n_chunks):
                for b in range(IL_GROUPS):
                    acc = bias
                    for k in range(D_CONV):
                        acc = acc + shifted(q, b, k - CONV_LEFT) * taps[k]
                    lo = q * CHUNK + b * SUBLANES
                    act = _silu(acc)
                    for t in range(seg // LANES):
                        xc_ref[slot, t, lo:lo + SUBLANES] = act[:, t * LANES:(t + 1) * LANES]
            for q in range(n_chunks):
                xc = jnp.concatenate(
                    [jnp.concatenate(
                        [xc_ref[slot, t, pl.ds(q * CHUNK + (m % 2) * (CHUNK // 2) + m // 2,
                                               SUBLANES, stride=SUBLANES), :]
                         for t in range(seg // LANES)], axis=1)
                     for m in range(IL_GROUPS)], axis=0)
                if is_b:
                    b_ref[q * CHUNK:(q + 1) * CHUNK, j - W_SSD:j - W_SSD + seg] = _bf(xc)
                else:
                    dst, off = (xs_t_ref, j) if is_x else (c_t_ref, j - W_SSD - GN)
                    dst[q, off:off + seg] = _bf(xc.T)

        return matmul, epilogue

    def plain_stage(cols, finish):
        box = []
        return (lambda: box.append(_dot(hm, w_ref[:, cols]))), (lambda: finish(box.pop()))

    def dt_stage():
        box = []

        def epilogue():
            p_dt = box.pop()
            a_col = -jnp.exp(alog_ref[...])
            for q in range(n_chunks):
                dt = _softplus(p_dt[q * CHUNK:(q + 1) * CHUNK].T[:2 * HEADS] + bias_ref[...])
                dt_ref[q] = dt
                cum_ref[q] = _lane_cumsum(dt * a_col)

        return (lambda: box.append(_dot(hm, wdt_ref[...]))), epilogue

    def store_to(ref, cols=None, act=None):
        def finish(v):
            v = v if act is None else _bf(act(v))
            if cols is None:
                ref[...] = v
            else:
                ref[:, cols] = v
        return finish

    def store_pair(ref, g, act=None):
        def finish(v):
            for n in range(2):
                part = v[:, n * POOL_GROUP_W:(n + 1) * POOL_GROUP_W]
                ref[g + n] = _bf(part if act is None else act(part))
        return finish

    light, heavy = [], []
    if full:
        for g in range(0, N_POOL_GROUPS, 2):
            light.append(plain_stage(slice(g * POOL_GROUP_W, (g + 2) * POOL_GROUP_W),
                                     store_pair(u_ref, g)))
    light.append(dt_stage())
    for j in range(0, CONV_DIM, seg):
        if full or j < W_SSD + GN:
            heavy.append(conv_stage(j, len(heavy) % 2))
    if full:
        for g in range(0, N_POOL_GROUPS, 2):
            zcols = slice(OFF_POOL_Z + g * POOL_GROUP_W, OFF_POOL_Z + (g + 2) * POOL_GROUP_W)
            light.append(plain_stage(zcols, store_pair(zp_ref, g, act=_silu)))
        for j in range(0, W_SSD, seg):
            light.append(plain_stage(slice(OFF_SSD_Z + j, OFF_SSD_Z + j + seg),
                                     store_to(zs_ref, cols=slice(j, j + seg), act=_silu)))
    stages = []
    while light or heavy:
        if light:
            stages.append(light.pop(0))
        if heavy:
            stages.append(heavy.pop(0))

    sweep = []
    if full:
        swept = jnp.maximum(i - 1, 0)
        h_ref[...] = jnp.where(swept % tiles_per_seq == 0, h0_ref[0], h_ref[...])
        src = jax.lax.broadcasted_iota(jnp.int32, (CHUNK, CHUNK), 0)
        dst = jax.lax.broadcasted_iota(jnp.int32, (CHUNK, CHUNK), 1)
        a_b = -jnp.exp(alog_ref[HEADS:])
        pairs = []
        for q in range(n_chunks):
            pairs += _fwd_chunk_slices(q, kxs_ref, kb_ref, kc_ref, kdt_ref, kcum_ref, dskip_ref,
                                       ypart_ref, h_ref, a_b, src <= dst, src == dst)
        sweep = [pairs[0][0]]
        for p in range(len(pairs)):
            nxt = pairs[p + 1][0] if p + 1 < len(pairs) else (lambda: None)
            sweep.append(functools.partial(lambda a, w: (a(), w()), pairs[p][1], nxt))

    stages[0][0]()
    done = 0
    for k, (_, epilogue) in enumerate(stages):
        if k + 1 < len(stages):
            stages[k + 1][0]()
        epilogue()
        upto = -(-len(sweep) * (k + 1) // len(stages))
        for piece in sweep[done:upto]:
            piece()
        done = upto

    if full:
        for kept, ref in ((kxs_ref, xs_t_ref), (kb_ref, b_ref), (kc_ref, c_t_ref),
                          (kdt_ref, dt_ref), (kcum_ref, cum_ref)):
            kept[...] = ref[...]
    else:
        a_b = -jnp.exp(alog_ref[HEADS:])
        for reverse, out_ref in ((False, hf_ref), (True, hb_ref)):
            h_ref[...] = jnp.zeros(h_ref.shape, h_ref.dtype)
            for q in (range(n_chunks - 1, -1, -1) if reverse else range(n_chunks)):
                if reverse:
                    dt_b, cum_b = dt_ref[q, HEADS:], cum_ref[q, HEADS:]
                    scale_in = dt_b * jnp.exp(cum_b - dt_b * a_b)
                    decay = jnp.exp(cum_b[:, CHUNK - 1:CHUNK])
                else:
                    dt_f, cum_f = dt_ref[q, :HEADS], cum_ref[q, :HEADS]
                    tot_f = cum_f[:, CHUNK - 1:CHUNK]
                    scale_in, decay = dt_f * jnp.exp(tot_f - cum_f), jnp.exp(tot_f)
                _state_update(h_ref, xs_t_ref, b_ref, q, scale_in, decay)
            out_ref[0] = h_ref[...]


def _projection(x2d, norm_w, shift, scale, w_bf, wdt_bf, conv_w, conv_b, alog_col, bias_col,
                seq_len, tm, full, dskip_b=None, h0=None):
    n_tok = x2d.shape[0]
    tiles_per_seq = seq_len // tm
    n_tiles = n_tok // tm
    n_mod = shift.shape[0]
    nct = n_tok // CHUNK
    per = tm // SUBLANES
    last_halo = n_tok // SUBLANES - 1
    kern = functools.partial(_proj_kernel, tm=tm, tiles_per_seq=tiles_per_seq, n_tiles=n_tiles,
                             full=full)
    tile = lambda i: jnp.minimum(i, n_tiles - 1)
    mod_map = (lambda i: (tile(i) // tiles_per_seq, 0, 0)) if n_mod > 1 else (lambda i: (0, 0, 0))
    mod_spec = pl.BlockSpec((1, 1, D_MODEL), mod_map)
    const = lambda i: (0, 0)
    tok = lambda i: (tile(i), 0)
    chunk3 = lambda i: (tile(i), 0, 0)
    q = tm // CHUNK
    xs_t = (jax.ShapeDtypeStruct((nct, W_SSD, CHUNK), jnp.bfloat16),
            pl.BlockSpec((q, W_SSD, CHUNK), chunk3))
    b_tok = (jax.ShapeDtypeStruct((n_tok, GN), jnp.bfloat16), pl.BlockSpec((tm, GN), tok))
    c_t = (jax.ShapeDtypeStruct((nct, GN, CHUNK), jnp.bfloat16), pl.BlockSpec((q, GN, CHUNK), chunk3))
    dt = (jax.ShapeDtypeStruct((nct, 2 * HEADS, CHUNK), jnp.float32),
          pl.BlockSpec((q, 2 * HEADS, CHUNK), chunk3))
    if full:
        pooled = (jax.ShapeDtypeStruct((N_POOL_GROUPS, n_tok, POOL_GROUP_W), jnp.bfloat16),
                  pl.BlockSpec((N_POOL_GROUPS, tm, POOL_GROUP_W), lambda i: (0, tile(i), 0)))
        zs = (jax.ShapeDtypeStruct((n_tok, W_SSD), jnp.bfloat16), pl.BlockSpec((tm, W_SSD), tok))
        swept = lambda i: jnp.maximum(i - 1, 0)
        y_part = (jax.ShapeDtypeStruct((nct, W_SSD, CHUNK), jnp.bfloat16),
                  pl.BlockSpec((q, W_SSD, CHUNK), lambda i: (swept(i), 0, 0)))
        outs = [pooled, pooled, zs, xs_t, b_tok, c_t, dt, dt, y_part]
        extra_in = [pl.BlockSpec((W_SSD, CHUNK), const),
                    pl.BlockSpec((1, W_SSD, D_STATE), lambda i: (swept(i) // tiles_per_seq, 0, 0))]
        extra_args = [dskip_b, h0]
        extra_scratch = [pltpu.VMEM((q, W_SSD, CHUNK), jnp.bfloat16),
                         pltpu.VMEM((tm, GN), jnp.bfloat16),
                         pltpu.VMEM((q, GN, CHUNK), jnp.bfloat16),
                         pltpu.VMEM((q, 2 * HEADS, CHUNK), jnp.float32),
                         pltpu.VMEM((q, 2 * HEADS, CHUNK), jnp.float32),
                         pltpu.VMEM((W_SSD, D_STATE), jnp.float32)]
    else:
        assert tiles_per_seq == 1, "prefix states are computed from one whole sequence per step"
        state = (jax.ShapeDtypeStruct((n_tiles, W_SSD, D_STATE), jnp.float32),
                 pl.BlockSpec((1, W_SSD, D_STATE), chunk3))
        outs = [state, state]
        extra_in, extra_args = [], []
        extra_scratch = [pltpu.VMEM((q, W_SSD, CHUNK), jnp.bfloat16),
                         pltpu.VMEM((tm, GN), jnp.bfloat16),
                         pltpu.VMEM((q, 2 * HEADS, CHUNK), jnp.float32),
                         pltpu.VMEM((q, 2 * HEADS, CHUNK), jnp.float32),
                         pltpu.VMEM((W_SSD, D_STATE), jnp.float32)]
    return pl.pallas_call(
        kern,
        grid=(n_tiles + 1 if full else n_tiles,),
        in_specs=[pl.BlockSpec((tm, D_MODEL), tok),
                  pl.BlockSpec((SUBLANES, D_MODEL), lambda i: (jnp.maximum(tile(i) * per - 1, 0), 0)),
                  pl.BlockSpec((SUBLANES, D_MODEL),
                               lambda i: (jnp.minimum((tile(i) + 1) * per, last_halo), 0)),
                  pl.BlockSpec((1, D_MODEL), const),
                  mod_spec, mod_spec,
                  (pl.BlockSpec(w_bf.shape, const) if full else
                   pl.BlockSpec((D_MODEL, W_SSD + GN), lambda i: (0, OFF_XBC // (W_SSD + GN)))),
                  pl.BlockSpec((D_MODEL, DT_PAD), const),
                  pl.BlockSpec((D_CONV, CONV_DIM), const),
                  pl.BlockSpec((1, CONV_DIM), const),
                  pl.BlockSpec((2 * HEADS, 1), const),
                  pl.BlockSpec((2 * HEADS, 1), const)] + extra_in,
        out_specs=[o[1] for o in outs],
        out_shape=[o[0] for o in outs],
        scratch_shapes=[pltpu.VMEM((2, tm + 2 * SUBLANES, CONV_SEG), jnp.float32),
                        pltpu.VMEM((2, CONV_SEG // LANES, tm, LANES), jnp.float32),
                        pltpu.VMEM((D_MODEL // LANES, tm, LANES), jnp.float32)] + extra_scratch,
        compiler_params=pltpu.CompilerParams(
            dimension_semantics=("arbitrary",), vmem_limit_bytes=VMEM_LIMIT),
        name="proj" if full else "proj_ctx",
    )(x2d, x2d, x2d, norm_w.reshape(1, D_MODEL), shift, scale, w_bf, wdt_bf, conv_w, conv_b,
      alog_col, bias_col, *extra_args)


def _tok_rows(q):
    return pl.ds(q * CHUNK, CHUNK)


def _state_update_group(h_ref, xs_t_ref, b_ref, q, g, scale_in, chunk_decay):
    bg = b_ref[_tok_rows(q), g * D_STATE:(g + 1) * D_STATE]
    xd = []
    for r in range(HEADS_PER_GROUP):
        h = g * HEADS_PER_GROUP + r
        x_h = xs_t_ref[q, h * HEADDIM:(h + 1) * HEADDIM].astype(jnp.float32)
        xd.append(_bf(x_h * scale_in[h:h + 1]))
    s_new = _dot(jnp.concatenate(xd, axis=0), bg)
    for r in range(HEADS_PER_GROUP):
        h = g * HEADS_PER_GROUP + r
        hr = slice(h * HEADDIM, (h + 1) * HEADDIM)
        h_ref[hr] = h_ref[hr] * chunk_decay[h:h + 1] + s_new[r * HEADDIM:(r + 1) * HEADDIM]


def _state_update(h_ref, xs_t_ref, b_ref, q, scale_in, chunk_decay):
    for g in range(GROUPS):
        _state_update_group(h_ref, xs_t_ref, b_ref, q, g, scale_in, chunk_decay)


def _fwd_chunk_slices(q, xs_t_ref, b_ref, c_t_ref, dt_ref, cum_ref, dskip_ref, y_ref, h_ref, a_b,
                      causal, is_diag):
    ctx = {}

    def setup():
        dt_f, cum_f = dt_ref[q, :HEADS], cum_ref[q, :HEADS]
        tot_f = cum_f[:, CHUNK - 1:CHUNK]
        dt_b, cum_b = dt_ref[q, HEADS:], cum_ref[q, HEADS:]
        cumx_b = cum_b - dt_b * a_b
        ctx["scale_in"] = dt_f * jnp.exp(tot_f - cum_f)
        ctx["chunk_decay"] = jnp.exp(tot_f)
        ctx["col_terms"] = jnp.concatenate(
            [jnp.log(dt_f) - cum_f, jnp.log(dt_b) + cumx_b,
             jnp.zeros((CHUNK - 2 * HEADS, CHUNK), jnp.float32)], axis=0).T
        ctx["row_f"], ctx["row_b"] = cum_f, -cumx_b
        ctx["decay_out_f"] = jnp.exp(cum_f)
        ctx["dt_b"] = dt_b

    def weights(g):
        if g == 0:
            setup()
        col_terms, row_f, row_b = ctx["col_terms"], ctx["row_f"], ctx["row_b"]
        bg = b_ref[_tok_rows(q), g * D_STATE:(g + 1) * D_STATE]
        cg_t = c_t_ref[q, g * D_STATE:(g + 1) * D_STATE]
        rows = slice(g * HEADS_PER_GROUP * HEADDIM, (g + 1) * HEADS_PER_GROUP * HEADDIM)
        g_t = _dot(bg, cg_t)
        ctx["g_diag", g] = jnp.sum(jnp.where(is_diag, g_t, 0.0), axis=0, keepdims=True)
        ctx["y_off", g] = _dot(_bf(h_ref[rows]), cg_t)
        for r in range(HEADS_PER_GROUP):
            h = g * HEADS_PER_GROUP + r
            col_f = jnp.broadcast_to(col_terms[:, h:h + 1], (CHUNK, CHUNK))
            col_b = jnp.broadcast_to(col_terms[:, HEADS + h:HEADS + h + 1], (CHUNK, CHUNK))
            expo = jnp.where(causal, col_f + row_f[h:h + 1], col_b + row_b[h:h + 1])
            ctx["w_t", h] = _bf(g_t * jnp.exp(expo))

    def apply(g):
        y_off, g_diag = ctx.pop(("y_off", g)), ctx.pop(("g_diag", g))
        for r in range(HEADS_PER_GROUP):
            h = g * HEADS_PER_GROUP + r
            hr = slice(h * HEADDIM, (h + 1) * HEADDIM)
            x_bf = xs_t_ref[q, hr]
            y_h = _dot(x_bf, ctx.pop(("w_t", h)))
            y_h = y_h + y_off[r * HEADDIM:(r + 1) * HEADDIM] * ctx["decay_out_f"][h:h + 1]
            skip = dskip_ref[hr] + g_diag * ctx["dt_b"][h:h + 1]
            y_ref[q, hr] = _bf(y_h + skip * x_bf.astype(jnp.float32))
        _state_update_group(h_ref, xs_t_ref, b_ref, q, g, ctx["scale_in"], ctx["chunk_decay"])

    return [(functools.partial(weights, g), functools.partial(apply, g)) for g in range(GROUPS)]


def _bwd_out_kernel(xs_t_ref, b_ref, dt_ref, cum_ref, alog_ref, h0_ref, c_t_ref, ypart_ref,
                    yp_ref, zs_ref, x_ref, gate_ref, snw_ref, wout_ref,
                    fnw_ref, o_ref, h_ref, y_ref, *, cps):
    @pl.when(pl.program_id(1) == 0)
    def _():
        h_ref[...] = h0_ref[0]

    a_b = -jnp.exp(alog_ref[HEADS:])

    def chunk(q):
        dt_b, cum_b = dt_ref[q, HEADS:], cum_ref[q, HEADS:]
        tot_b = cum_b[:, CHUNK - 1:CHUNK]
        cumx_b = cum_b - dt_b * a_b
        decay_out = jnp.exp(tot_b - cumx_b)
        y_parts = []
        for g in range(GROUPS):
            cg_t = c_t_ref[q, g * D_STATE:(g + 1) * D_STATE]
            rows = slice(g * HEADS_PER_GROUP * HEADDIM, (g + 1) * HEADS_PER_GROUP * HEADDIM)
            y_off = _dot(_bf(h_ref[rows]), cg_t)
            for r in range(HEADS_PER_GROUP):
                h = g * HEADS_PER_GROUP + r
                hr = slice(h * HEADDIM, (h + 1) * HEADDIM)
                y_parts.append(ypart_ref[q, hr].astype(jnp.float32)
                               + y_off[r * HEADDIM:(r + 1) * HEADDIM] * decay_out[h:h + 1])
        y_ref[_tok_rows(q), :] = jnp.concatenate(y_parts, axis=0).T
        _state_update(h_ref, xs_t_ref, b_ref, q, dt_b * jnp.exp(cumx_b), jnp.exp(tot_b))

    acc = jnp.zeros((cps * CHUNK, D_MODEL), jnp.float32)
    for i in range(max(cps, N_POOL_GROUPS)):
        if i < cps:
            chunk(cps - 1 - i)
        if i < N_POOL_GROUPS:
            acc = acc + _dot(yp_ref[i], wout_ref[i * POOL_GROUP_W:(i + 1) * POOL_GROUP_W])

    gw = W_SSD // GROUPS
    for g in range(GROUPS):
        cols = slice(g * gw, (g + 1) * gw)
        gated = y_ref[:, cols] * zs_ref[:, cols].astype(jnp.float32)
        ms = jnp.mean(gated * gated, axis=-1, keepdims=True)
        yn = gated * jax.lax.rsqrt(ms + EPS) * snw_ref[:, cols]
        acc = acc + _dot(_bf(yn), wout_ref[W_POOL + g * gw:W_POOL + (g + 1) * gw])
    hres = x_ref[...] + gate_ref[0] * acc
    ms = jnp.mean(hres * hres, axis=-1, keepdims=True)
    o_ref[...] = hres * jax.lax.rsqrt(ms + EPS) * fnw_ref[...]


def _backward_output(xs_t, b_tok, dt, cum, c_t, y_part, alog_col, h0, y_pool, gate_ssd, x2d, gate,
                     ssd_norm_w, w_out_bf, final_norm_w, bsz, n_chunks):
    n_tok = b_tok.shape[0]
    cps = min(n_chunks, SSD_CHUNKS_PER_STEP)
    n_steps = n_chunks // cps
    tm = cps * CHUNK
    block_of = lambda b, s: b * n_steps + (n_steps - 1 - s)
    tok = lambda b, s: (block_of(b, s), 0)
    chunk3 = lambda b, s: (block_of(b, s), 0, 0)
    const2 = lambda b, s: (0, 0)
    per_seq = lambda b, s: (b, 0, 0)
    head_spec = pl.BlockSpec((cps, 2 * HEADS, CHUNK), chunk3)
    return pl.pallas_call(
        functools.partial(_bwd_out_kernel, cps=cps),
        grid=(bsz, n_steps),
        in_specs=[pl.BlockSpec((cps, W_SSD, CHUNK), chunk3),
                  pl.BlockSpec((tm, GN), tok),
                  head_spec, head_spec,
                  pl.BlockSpec((2 * HEADS, 1), const2),
                  pl.BlockSpec((1, W_SSD, D_STATE), per_seq),
                  pl.BlockSpec((cps, GN, CHUNK), chunk3),
                  pl.BlockSpec((cps, W_SSD, CHUNK), chunk3),
                  pl.BlockSpec((N_POOL_GROUPS, tm, POOL_GROUP_W),
                               lambda b, s: (0, block_of(b, s), 0)),
                  pl.BlockSpec((tm, W_SSD), tok),
                  pl.BlockSpec((tm, D_MODEL), tok),
                  pl.BlockSpec((1, 1, D_MODEL), per_seq),
                  pl.BlockSpec((1, W_SSD), const2),
                  pl.BlockSpec((W_POOL + W_SSD, D_MODEL), const2),
                  pl.BlockSpec((1, D_MODEL), const2)],
        out_specs=pl.BlockSpec((tm, D_MODEL), tok),
        out_shape=jax.ShapeDtypeStruct((n_tok, D_MODEL), jnp.float32),
        scratch_shapes=[pltpu.VMEM((W_SSD, D_STATE), jnp.float32),
                        pltpu.VMEM((tm, W_SSD), jnp.float32)],
        compiler_params=pltpu.CompilerParams(
            dimension_semantics=("arbitrary", "arbitrary"), vmem_limit_bytes=VMEM_LIMIT),
        name="bwd_out",
    )(xs_t, b_tok, dt, cum, alog_col, h0, c_t, y_part, y_pool, gate_ssd, x2d, gate,
      ssd_norm_w.reshape(1, W_SSD), w_out_bf, final_norm_w.reshape(1, D_MODEL))


POOL_TILE_ROWS = 4
POOL_TILE = POOL_TILE_ROWS * GRID_W


def _pool_constants(window, n_rows):
    lo_off, hi_off = -(window // 2), window - window // 2
    col = np.arange(GRID_W)
    lo = np.clip(col + lo_off, 0, GRID_W)
    hi = np.clip(col + hi_off, 0, GRID_W)
    band = ((col[None, :] >= lo[:, None]) & (col[None, :] < hi[:, None])).astype(np.float32)
    band_tile = np.kron(np.eye(POOL_TILE_ROWS, dtype=np.float32), band)
    row = np.arange(n_rows)
    cnt_r = np.clip(row + hi_off, 0, n_rows) - np.clip(row + lo_off, 0, n_rows)
    inv = 1.0 / (cnt_r[:, None] * (hi - lo)[None, :]).astype(np.float64)
    inv = np.broadcast_to(inv.reshape(-1, 1), (n_rows * GRID_W, 128)).astype(np.float32)
    return jnp.asarray(band_tile, jnp.bfloat16), jnp.asarray(inv)


def _pool_kernel(u_ref, z_ref, band_ref, inv_ref, w_ref, scale_ref, o_ref, *, n_rows):
    for g, window in enumerate(POOL_WINDOWS):
        @pl.when(pl.program_id(0) == g)
        def _(window=window):
            _pool_image(u_ref.at[0], z_ref.at[0], band_ref.at[0], inv_ref.at[0], w_ref, scale_ref,
                        o_ref.at[0], window, n_rows)


def _pool_image(u_ref, z_ref, band_ref, inv_ref, w_ref, scale_ref, o_ref, window, n_rows):
    def grid_row(r):
        return u_ref[r * GRID_W:(r + 1) * GRID_W].astype(jnp.float32)

    def bounds(r):
        return max(r - window // 2, 0), min(r + window - window // 2, n_rows)

    band = band_ref[...]
    rsum, tile_rows = None, []
    for r in range(n_rows):
        lo, hi = bounds(r)
        if r == 0 or window <= 2:
            rsum = grid_row(lo)
            for k in range(lo + 1, hi):
                rsum = rsum + grid_row(k)
        else:
            prev_lo, prev_hi = bounds(r - 1)
            if hi > prev_hi:
                rsum = rsum + grid_row(hi - 1)
            if lo > prev_lo:
                rsum = rsum - grid_row(prev_lo)
        tile_rows.append(rsum)
        if len(tile_rows) < POOL_TILE_ROWS:
            continue
        base = (r + 1 - POOL_TILE_ROWS) * GRID_W
        rows = slice(base, base + POOL_TILE)
        rs = jnp.concatenate(tile_rows, axis=0)
        tile_rows = []
        box = _dot(band, _bf(rs))
        inv = inv_ref[rows]
        mean = box * jnp.concatenate([inv, inv], axis=1)
        d = mean - u_ref[rows].astype(jnp.float32)
        y = _dot(_bf(d), w_ref[0]) * scale_ref[...]
        o_ref[rows] = _bf(y * z_ref[rows].astype(jnp.float32))


def _pool_mixer(u, gate, pool_w_bf, pool_scale, bsz, n_img_tok):
    n_rows = n_img_tok // GRID_W
    consts = [_pool_constants(window, n_rows) for window in POOL_WINDOWS]
    band = jnp.stack([c[0] for c in consts])
    inv = jnp.stack([c[1] for c in consts])
    img = pl.BlockSpec((1, n_img_tok, POOL_GROUP_W), lambda g, b: (g, b, 0))
    per_group = lambda g, b: (g, 0, 0)
    return pl.pallas_call(
        functools.partial(_pool_kernel, n_rows=n_rows),
        grid=(N_POOL_GROUPS, bsz),
        in_specs=[img, img,
                  pl.BlockSpec((1, POOL_TILE, POOL_TILE), per_group),
                  pl.BlockSpec((1, n_img_tok, 128), per_group),
                  pl.BlockSpec((1, POOL_GROUP_W, POOL_GROUP_W), per_group),
                  pl.BlockSpec((1, POOL_GROUP_W), lambda g, b: (0, g))],
        out_specs=img,
        out_shape=jax.ShapeDtypeStruct(u.shape, jnp.bfloat16),
        compiler_params=pltpu.CompilerParams(
            dimension_semantics=("arbitrary", "arbitrary"), vmem_limit_bytes=VMEM_LIMIT),
        name="pool",
    )(u, gate, band, inv, pool_w_bf, pool_scale)


def kernel(x, c, ctx, c_ctx, norm_w, w_ada, b_ada, w_in, conv_w, conv_b, a_log, dt_bias, d_skip,
           ssd_norm_w, pool_w, pool_scale, w_out, final_norm_w):
    bsz, seq, _ = x.shape
    ctx_len = ctx.shape[1]
    depth = norm_w.shape[0]
    assert depth == 1, "single-layer block: the context stream update is never consumed"
    assert seq % PROJ_TILE == 0 and ctx_len % CHUNK == 0 and seq % POOL_TILE == 0
    assert OFF_XBC % (W_SSD + GN) == 0

    mod_rows = -(-(bsz + 1) // SUBLANES) * SUBLANES
    cond = jnp.concatenate([c, c_ctx[None], jnp.zeros((mod_rows - bsz - 1, D_MODEL), c.dtype)])
    mod = _modulation(cond, w_ada[0], b_ada[0])
    shift = mod[:, :D_MODEL].reshape(mod_rows, 1, D_MODEL)
    scale = mod[:, D_MODEL:2 * D_MODEL].reshape(mod_rows, 1, D_MODEL)
    gate = mod[:, 2 * D_MODEL:].reshape(mod_rows, 1, D_MODEL)

    w_in_bf = _bf(w_in[0])
    w_dt_bf = jnp.pad(_bf(w_in[0, :, OFF_DT:]), ((0, 0), (0, DT_PAD - 2 * HEADS)))
    alog_col = a_log[0].reshape(2 * HEADS, 1)
    bias_col = dt_bias[0].reshape(2 * HEADS, 1)
    dskip_b = jnp.broadcast_to(jnp.repeat(d_skip[0], HEADDIM)[:, None], (W_SSD, CHUNK))
    conv_b2 = conv_b[0].reshape(1, CONV_DIM)

    ctx2d = ctx.reshape(bsz * ctx_len, D_MODEL)
    h_fwd, h_bwd = _projection(
        ctx2d, norm_w[0], shift[bsz:bsz + 1], scale[bsz:bsz + 1], w_in_bf, w_dt_bf, conv_w[0],
        conv_b2, alog_col, bias_col, ctx_len, ctx_len, full=False)

    x2d = x.reshape(bsz * seq, D_MODEL)
    outs = _projection(x2d, norm_w[0], shift, scale, w_in_bf, w_dt_bf, conv_w[0], conv_b2,
                       alog_col, bias_col, seq, PROJ_TILE, full=True, dskip_b=dskip_b, h0=h_fwd)
    u_pool, gate_pool, gate_ssd, xs_t, b_tok, c_t, dt, cum, y_part = outs
    nc = seq // CHUNK
    y_pool = _pool_mixer(u_pool, gate_pool, _bf(pool_w[0]), pool_scale, bsz, seq)
    out = _backward_output(xs_t, b_tok, dt, cum, c_t, y_part, alog_col, h_bwd, y_pool, gate_ssd,
                           x2d, gate, ssd_norm_w[0], _bf(w_out[0]), final_norm_w, bsz, nc)
    return out.reshape(bsz, seq, D_MODEL)
```

```python
import functools

import numpy as np
import jax
import jax.numpy as jnp
from jax.experimental import pallas as pl
from jax.experimental.pallas import tpu as pltpu

D_MODEL = 1024
GRID_W = 64
W_POOL = 1024
W_SSD = 1024
POOL_WINDOWS = (2, 4, 8, 16)
N_POOL_GROUPS = len(POOL_WINDOWS)
POOL_GROUP_W = 256
HEADDIM = 64
HEADS = 16
GROUPS = 4
HEADS_PER_GROUP = 4
D_STATE = 128
D_CONV = 4
CONV_LEFT = 2
CHUNK = 128
GN = GROUPS * D_STATE
CONV_DIM = W_SSD + 2 * GN
OFF_POOL_Z = W_POOL
OFF_SSD_Z = 2 * W_POOL
OFF_XBC = 2 * W_POOL + W_SSD
OFF_DT = OFF_XBC + CONV_DIM
DT_PAD = 128
EPS = 1e-6
SUBLANES = 8
LANES = 128
IL_GROUPS = CHUNK // SUBLANES
CONV_SEG = 256
SSD_CHUNKS_PER_STEP = 4
PROJ_TILE = 512
SWEEP_LAG = 2
VMEM_LIMIT = 56 * 1024 * 1024


def _silu(v):
    h = 0.5 * v
    return h + h * jnp.tanh(h)


def _softplus(v):
    return jnp.maximum(v, 0.0) + jnp.log1p(jnp.exp(-jnp.abs(v)))


def _bf(v):
    return v.astype(jnp.bfloat16)


def _dot(a, b):
    return jnp.dot(a, b, preferred_element_type=jnp.float32)


def _mod_kernel(c_ref, w_ref, b_ref, o_ref):
    s, w = _silu(c_ref[...]), w_ref[...]
    s_hi, w_hi = _bf(s), _bf(w)
    s_lo = _bf(s - s_hi.astype(jnp.float32))
    w_lo = _bf(w - w_hi.astype(jnp.float32))
    both = _dot(jnp.concatenate([s_hi, s_lo], axis=0), w_hi)
    rows = s.shape[0]
    o_ref[...] = both[:rows] + both[rows:] + _dot(s_hi, w_lo) + b_ref[...]


def _modulation(cond_rows, w_ada, b_ada):
    rows = cond_rows.shape[0]
    n_out = w_ada.shape[1]
    tn = 1024
    return pl.pallas_call(
        _mod_kernel,
        grid=(n_out // tn,),
        in_specs=[pl.BlockSpec((rows, D_MODEL), lambda j: (0, 0)),
                  pl.BlockSpec((D_MODEL, tn), lambda j: (0, j)),
                  pl.BlockSpec((1, tn), lambda j: (0, j))],
        out_specs=pl.BlockSpec((rows, tn), lambda j: (0, j)),
        out_shape=jax.ShapeDtypeStruct((rows, n_out), jnp.float32),
        compiler_params=pltpu.CompilerParams(vmem_limit_bytes=VMEM_LIMIT),
        name="mod",
    )(cond_rows, w_ada, b_ada.reshape(1, n_out))


def _lane_cumsum(v):
    lane = jax.lax.broadcasted_iota(jnp.int32, v.shape, 1)
    shift = 1
    while shift < CHUNK:
        v = v + jnp.where(lane >= shift, pltpu.roll(v, shift, 1), 0.0)
        shift *= 2
    return v


def _proj_kernel(x_ref, xp_ref, xn_ref, nw_ref, sh_ref, sc_ref, w_ref, wdt_ref, cw_ref, cb_ref,
                 alog_ref, bias_ref, *rest, tm, tiles_per_seq, n_tiles, full):
    if full:
        (dskip_ref, h0_ref, u_ref, zp_ref, zs_ref, xs_t_ref, b_ref, c_t_ref,
         dt_ref, cum_ref, ypart_ref, pe_ref, xc_ref, mn_ref, kxs_ref, kb_ref, kc_ref, kdt_ref,
         kcum_ref, h_ref) = rest
    else:
        hf_ref, hb_ref, pe_ref, xc_ref, mn_ref, xs_t_ref, b_ref, dt_ref, cum_ref, h_ref = rest
    i = pl.program_id(0)
    pos = jnp.minimum(i, n_tiles - 1) % tiles_per_seq
    has_prev = pos > 0
    has_next = pos < tiles_per_seq - 1
    n_chunks = tm // CHUNK
    seg = CONV_SEG

    if full:
        @pl.when(i == 0)
        def _():
            for ref in (kxs_ref, kb_ref, kc_ref, kdt_ref, kcum_ref, h_ref):
                ref[...] = jnp.zeros(ref.shape, ref.dtype)

    gain = nw_ref[...] * (1.0 + sc_ref[0])

    def modulated(v):
        ms = jnp.mean(v * v, axis=-1, keepdims=True)
        return v * jax.lax.rsqrt(ms + EPS) * gain + sh_ref[0]

    m_tok = modulated(x_ref[...])
    hm = _bf(m_tok)
    for t in range(D_MODEL // LANES):
        mn_ref[t] = m_tok[:, t * LANES:(t + 1) * LANES]

    rows = [jnp.concatenate([mn_ref[t, pl.ds(q * CHUNK + b, SUBLANES, stride=IL_GROUPS), :]
                             for t in range(D_MODEL // LANES)], axis=1)
            for q in range(n_chunks) for b in range(IL_GROUPS)]
    halo = [jnp.where(has_prev, modulated(xp_ref[...]), 0.0),
            jnp.where(has_next, modulated(xn_ref[...]), 0.0)]
    hm_il = _bf(jnp.concatenate(halo + rows, axis=0))
    sub = jax.lax.broadcasted_iota(jnp.int32, (SUBLANES, seg), 0)

    def conv_stage(j, slot):
        is_x = j < W_SSD
        is_b = W_SSD <= j < W_SSD + GN

        def matmul():
            lo = (OFF_XBC if full else 0) + j
            pe_ref[slot] = _dot(hm_il, w_ref[:, lo:lo + seg])

        def group(q, b):
            lo = 2 * SUBLANES + q * CHUNK + b * SUBLANES
            return pe_ref[slot, lo:lo + SUBLANES]

        def shifted(q, b, delta):
            bb = b + delta
            if 0 <= bb < IL_GROUPS:
                return group(q, bb)
            if bb < 0:
                bb += IL_GROUPS
                if q == 0:
                    first = pe_ref[slot, bb - SUBLANES:bb - SUBLANES + 1]
                else:
                    row = 2 * SUBLANES + (q - 1) * CHUNK + bb * SUBLANES + SUBLANES - 1
                    first = pe_ref[slot, row:row + 1]
                return jnp.where(sub == 0, first, pltpu.roll(group(q, bb), 1, 0))
            bb -= IL_GROUPS
            if q == n_chunks - 1:
                last = pe_ref[slot, SUBLANES + bb:SUBLANES + bb + 1]
            else:
                nxt = 2 * SUBLANES + (q + 1) * CHUNK + bb * SUBLANES
                last = pe_ref[slot, nxt:nxt + 1]
            return jnp.where(sub == SUBLANES - 1, last, pltpu.roll(group(q, bb), SUBLANES - 1, 0))

        def epilogue():
            taps = [cw_ref[k:k + 1, j:j + seg] for k in range(D_CONV)]
            bias = cb_ref[:, j:j + seg]
            for q in range(n_chunks):
                for b in range(IL_GROUPS):
                    acc = bias
                    for k in range(D_CONV):
                        acc = acc + shifted(q, b, k - CONV_LEFT) * taps[k]
                    lo = q * CHUNK + b * SUBLANES
                    act = _silu(acc)
                    for t in range(seg // LANES):
                        xc_ref[slot, t, lo:lo + SUBLANES] = act[:, t * LANES:(t + 1) * LANES]
            for q in range(n_chunks):
                xc = jnp.concatenate(
                    [jnp.concatenate(
                        [xc_ref[slot, t, pl.ds(q * CHUNK + (m % 2) * (CHUNK // 2) + m // 2,
                                               SUBLANES, stride=SUBLANES), :]
                         for t in range(seg // LANES)], axis=1)
                     for m in range(IL_GROUPS)], axis=0)
                if is_b:
                    b_ref[q * CHUNK:(q + 1) * CHUNK, j - W_SSD:j - W_SSD + seg] = _bf(xc)
                else:
                    dst, off = (xs_t_ref, j) if is_x else (c_t_ref, j - W_SSD - GN)
                    dst[q, off:off + seg] = _bf(xc.T)

        return matmul, epilogue

    def plain_stage(cols, finish):
        box = []
        return (lambda: box.append(_dot(hm, w_ref[:, cols]))), (lambda: finish(box.pop()))

    def dt_stage():
        box = []

        def epilogue():
            p_dt = box.pop()
            a_col = -jnp.exp(alog_ref[...])
            for q in range(n_chunks):
                dt = _softplus(p_dt[q * CHUNK:(q + 1) * CHUNK].T[:2 * HEADS] + bias_ref[...])
                dt_ref[q] = dt
                cum_ref[q] = _lane_cumsum(dt * a_col)

        return (lambda: box.append(_dot(hm, wdt_ref[...]))), epilogue

    def store_to(ref, cols=None, act=None):
        def finish(v):
            v = v if act is None else _bf(act(v))
            if cols is None:
                ref[...] = v
            else:
                ref[:, cols] = v
        return finish

    def store_pair(ref, g, act=None):
        def finish(v):
            for n in range(2):
                part = v[:, n * POOL_GROUP_W:(n + 1) * POOL_GROUP_W]
                ref[g + n] = _bf(part if act is None else act(part))
        return finish

    light, heavy = [], []
    if full:
        for g in range(0, N_POOL_GROUPS, 2):
            light.append(plain_stage(slice(g * POOL_GROUP_W, (g + 2) * POOL_GROUP_W),
                                     store_pair(u_ref, g)))
    light.append(dt_stage())
    for j in range(0, CONV_DIM, seg):
        if full or j < W_SSD + GN:
            heavy.append(conv_stage(j, len(heavy) % 2))
    if full:
        for g in range(0, N_POOL_GROUPS, 2):
            zcols = slice(OFF_POOL_Z + g * POOL_GROUP_W, OFF_POOL_Z + (g + 2) * POOL_GROUP_W)
            light.append(plain_stage(zcols, store_pair(zp_ref, g, act=_silu)))
        for j in range(0, W_SSD, seg):
            light.append(plain_stage(slice(OFF_SSD_Z + j, OFF_SSD_Z + j + seg),
                                     store_to(zs_ref, cols=slice(j, j + seg), act=_silu)))
    stages = []
    while light or heavy:
        if light:
            stages.append(light.pop(0))
        if heavy:
            stages.append(heavy.pop(0))

    sweep = []
    if full:
        swept = jnp.maximum(i - 1, 0)
        h_ref[...] = jnp.where(swept % tiles_per_seq == 0, h0_ref[0], h_ref[...])
        src = jax.lax.broadcasted_iota(jnp.int32, (CHUNK, CHUNK), 0)
        dst = jax.lax.broadcasted_iota(jnp.int32, (CHUNK, CHUNK), 1)
        a_b = -jnp.exp(alog_ref[HEADS:])
        pairs = []
        for q in range(n_chunks):
            pairs += _fwd_chunk_slices(q, kxs_ref, kb_ref, kc_ref, kdt_ref, kcum_ref, dskip_ref,
                                       ypart_ref, h_ref, a_b, src <= dst, src == dst)
        lag = SWEEP_LAG
        assert lag < GROUPS, "weights(p) reads the state written by apply(p - GROUPS)"
        sweep = []
        for s in range(len(pairs) + lag):
            todo = []
            if s >= lag:
                todo.append(pairs[s - lag][1])
            if s < len(pairs):
                todo.append(pairs[s][0])
            sweep.append(functools.partial(lambda fs: [f() for f in fs], todo))

    stages[0][0]()
    done = 0
    for k, (_, epilogue) in enumerate(stages):
        if k + 1 < len(stages):
            stages[k + 1][0]()
        epilogue()
        upto = -(-len(sweep) * (k + 1) // len(stages))
        for piece in sweep[done:upto]:
            piece()
        done = upto

    if full:
        for kept, ref in ((kxs_ref, xs_t_ref), (kb_ref, b_ref), (kc_ref, c_t_ref),
                          (kdt_ref, dt_ref), (kcum_ref, cum_ref)):
            kept[...] = ref[...]
    else:
        a_b = -jnp.exp(alog_ref[HEADS:])
        for reverse, out_ref in ((False, hf_ref), (True, hb_ref)):
            h_ref[...] = jnp.zeros(h_ref.shape, h_ref.dtype)
            for q in (range(n_chunks - 1, -1, -1) if reverse else range(n_chunks)):
                if reverse:
                    dt_b, cum_b = dt_ref[q, HEADS:], cum_ref[q, HEADS:]
                    scale_in = dt_b * jnp.exp(cum_b - dt_b * a_b)
                    decay = jnp.exp(cum_b[:, CHUNK - 1:CHUNK])
                else:
                    dt_f, cum_f = dt_ref[q, :HEADS], cum_ref[q, :HEADS]
                    tot_f = cum_f[:, CHUNK - 1:CHUNK]
                    scale_in, decay = dt_f * jnp.exp(tot_f - cum_f), jnp.exp(tot_f)
                _state_update(h_ref, xs_t_ref, b_ref, q, scale_in, decay)
            out_ref[0] = h_ref[...]


def _projection(x2d, norm_w, shift, scale, w_bf, wdt_bf, conv_w, conv_b, alog_col, bias_col,
                seq_len, tm, full, dskip_b=None, h0=None):
    n_tok = x2d.shape[0]
    tiles_per_seq = seq_len // tm
    n_tiles = n_tok // tm
    n_mod = shift.shape[0]
    nct = n_tok // CHUNK
    per = tm // SUBLANES
    last_halo = n_tok // SUBLANES - 1
    kern = functools.partial(_proj_kernel, tm=tm, tiles_per_seq=tiles_per_seq, n_tiles=n_tiles,
                             full=full)
    tile = lambda i: jnp.minimum(i, n_tiles - 1)
    mod_map = (lambda i: (tile(i) // tiles_per_seq, 0, 0)) if n_mod > 1 else (lambda i: (0, 0, 0))
    mod_spec = pl.BlockSpec((1, 1, D_MODEL), mod_map)
    const = lambda i: (0, 0)
    tok = lambda i: (tile(i), 0)
    chunk3 = lambda i: (tile(i), 0, 0)
    q = tm // CHUNK
    xs_t = (jax.ShapeDtypeStruct((nct, W_SSD, CHUNK), jnp.bfloat16),
            pl.BlockSpec((q, W_SSD, CHUNK), chunk3))
    b_tok = (jax.ShapeDtypeStruct((n_tok, GN), jnp.bfloat16), pl.BlockSpec((tm, GN), tok))
    c_t = (jax.ShapeDtypeStruct((nct, GN, CHUNK), jnp.bfloat16), pl.BlockSpec((q, GN, CHUNK), chunk3))
    dt = (jax.ShapeDtypeStruct((nct, 2 * HEADS, CHUNK), jnp.float32),
          pl.BlockSpec((q, 2 * HEADS, CHUNK), chunk3))
    if full:
        pooled = (jax.ShapeDtypeStruct((N_POOL_GROUPS, n_tok, POOL_GROUP_W), jnp.bfloat16),
                  pl.BlockSpec((N_POOL_GROUPS, tm, POOL_GROUP_W), lambda i: (0, tile(i), 0)))
        zs = (jax.ShapeDtypeStruct((n_tok, W_SSD), jnp.bfloat16), pl.BlockSpec((tm, W_SSD), tok))
        swept = lambda i: jnp.maximum(i - 1, 0)
        y_part = (jax.ShapeDtypeStruct((nct, W_SSD, CHUNK), jnp.bfloat16),
                  pl.BlockSpec((q, W_SSD, CHUNK), lambda i: (swept(i), 0, 0)))
        outs = [pooled, pooled, zs, xs_t, b_tok, c_t, dt, dt, y_part]
        extra_in = [pl.BlockSpec((W_SSD, CHUNK), const),
                    pl.BlockSpec((1, W_SSD, D_STATE), lambda i: (swept(i) // tiles_per_seq, 0, 0))]
        extra_args = [dskip_b, h0]
        extra_scratch = [pltpu.VMEM((q, W_SSD, CHUNK), jnp.bfloat16),
                         pltpu.VMEM((tm, GN), jnp.bfloat16),
                         pltpu.VMEM((q, GN, CHUNK), jnp.bfloat16),
                         pltpu.VMEM((q, 2 * HEADS, CHUNK), jnp.float32),
                         pltpu.VMEM((q, 2 * HEADS, CHUNK), jnp.float32),
                         pltpu.VMEM((W_SSD, D_STATE), jnp.float32)]
    else:
        assert tiles_per_seq == 1, "prefix states are computed from one whole sequence per step"
        state = (jax.ShapeDtypeStruct((n_tiles, W_SSD, D_STATE), jnp.float32),
                 pl.BlockSpec((1, W_SSD, D_STATE), chunk3))
        outs = [state, state]
        extra_in, extra_args = [], []
        extra_scratch = [pltpu.VMEM((q, W_SSD, CHUNK), jnp.bfloat16),
                         pltpu.VMEM((tm, GN), jnp.bfloat16),
                         pltpu.VMEM((q, 2 * HEADS, CHUNK), jnp.float32),
                         pltpu.VMEM((q, 2 * HEADS, CHUNK), jnp.float32),
                         pltpu.VMEM((W_SSD, D_STATE), jnp.float32)]
    return pl.pallas_call(
        kern,
        grid=(n_tiles + 1 if full else n_tiles,),
        in_specs=[pl.BlockSpec((tm, D_MODEL), tok),
                  pl.BlockSpec((SUBLANES, D_MODEL), lambda i: (jnp.maximum(tile(i) * per - 1, 0), 0)),
                  pl.BlockSpec((SUBLANES, D_MODEL),
                               lambda i: (jnp.minimum((tile(i) + 1) * per, last_halo), 0)),
                  pl.BlockSpec((1, D_MODEL), const),
                  mod_spec, mod_spec,
                  (pl.BlockSpec(w_bf.shape, const) if full else
                   pl.BlockSpec((D_MODEL, W_SSD + GN), lambda i: (0, OFF_XBC // (W_SSD + GN)))),
                  pl.BlockSpec((D_MODEL, DT_PAD), const),
                  pl.BlockSpec((D_CONV, CONV_DIM), const),
                  pl.BlockSpec((1, CONV_DIM), const),
                  pl.BlockSpec((2 * HEADS, 1), const),
                  pl.BlockSpec((2 * HEADS, 1), const)] + extra_in,
        out_specs=[o[1] for o in outs],
        out_shape=[o[0] for o in outs],
        scratch_shapes=[pltpu.VMEM((2, tm + 2 * SUBLANES, CONV_SEG), jnp.float32),
                        pltpu.VMEM((2, CONV_SEG // LANES, tm, LANES), jnp.float32),
                        pltpu.VMEM((D_MODEL // LANES, tm, LANES), jnp.float32)] + extra_scratch,
        compiler_params=pltpu.CompilerParams(
            dimension_semantics=("arbitrary",), vmem_limit_bytes=VMEM_LIMIT),
        name="proj" if full else "proj_ctx",
    )(x2d, x2d, x2d, norm_w.reshape(1, D_MODEL), shift, scale, w_bf, wdt_bf, conv_w, conv_b,
      alog_col, bias_col, *extra_args)


def _tok_rows(q):
    return pl.ds(q * CHUNK, CHUNK)


def _state_update_group(h_ref, xs_t_ref, b_ref, q, g, scale_in, chunk_decay):
    bg = b_ref[_tok_rows(q), g * D_STATE:(g + 1) * D_STATE]
    xd = []
    for r in range(HEADS_PER_GROUP):
        h = g * HEADS_PER_GROUP + r
        x_h = xs_t_ref[q, h * HEADDIM:(h + 1) * HEADDIM].astype(jnp.float32)
        xd.append(_bf(x_h * scale_in[h:h + 1]))
    s_new = _dot(jnp.concatenate(xd, axis=0), bg)
    for r in range(HEADS_PER_GROUP):
        h = g * HEADS_PER_GROUP + r
        hr = slice(h * HEADDIM, (h + 1) * HEADDIM)
        h_ref[hr] = h_ref[hr] * chunk_decay[h:h + 1] + s_new[r * HEADDIM:(r + 1) * HEADDIM]


def _state_update(h_ref, xs_t_ref, b_ref, q, scale_in, chunk_decay):
    for g in range(GROUPS):
        _state_update_group(h_ref, xs_t_ref, b_ref, q, g, scale_in, chunk_decay)


def _fwd_chunk_slices(q, xs_t_ref, b_ref, c_t_ref, dt_ref, cum_ref, dskip_ref, y_ref, h_ref, a_b,
                      causal, is_diag):
    ctx = {}

    def setup():
        dt_f, cum_f = dt_ref[q, :HEADS], cum_ref[q, :HEADS]
        tot_f = cum_f[:, CHUNK - 1:CHUNK]
        dt_b, cum_b = dt_ref[q, HEADS:], cum_ref[q, HEADS:]
        cumx_b = cum_b - dt_b * a_b
        ctx["scale_in"] = dt_f * jnp.exp(tot_f - cum_f)
        ctx["chunk_decay"] = jnp.exp(tot_f)
        ctx["col_terms"] = jnp.concatenate(
            [jnp.log(dt_f) - cum_f, jnp.log(dt_b) + cumx_b,
             jnp.zeros((CHUNK - 2 * HEADS, CHUNK), jnp.float32)], axis=0).T
        ctx["row_f"], ctx["row_b"] = cum_f, -cumx_b
        ctx["decay_out_f"] = jnp.exp(cum_f)
        ctx["dt_b"] = dt_b

    def weights(g):
        if g == 0:
            setup()
        col_terms, row_f, row_b = ctx["col_terms"], ctx["row_f"], ctx["row_b"]
        bg = b_ref[_tok_rows(q), g * D_STATE:(g + 1) * D_STATE]
        cg_t = c_t_ref[q, g * D_STATE:(g + 1) * D_STATE]
        rows = slice(g * HEADS_PER_GROUP * HEADDIM, (g + 1) * HEADS_PER_GROUP * HEADDIM)
        g_t = _dot(bg, cg_t)
        ctx["g_diag", g] = jnp.sum(jnp.where(is_diag, g_t, 0.0), axis=0, keepdims=True)
        ctx["y_off", g] = _dot(_bf(h_ref[rows]), cg_t)
        for r in range(HEADS_PER_GROUP):
            h = g * HEADS_PER_GROUP + r
            col_f = jnp.broadcast_to(col_terms[:, h:h + 1], (CHUNK, CHUNK))
            col_b = jnp.broadcast_to(col_terms[:, HEADS + h:HEADS + h + 1], (CHUNK, CHUNK))
            expo = jnp.where(causal, col_f + row_f[h:h + 1], col_b + row_b[h:h + 1])
            ctx["w_t", h] = _bf(g_t * jnp.exp(expo))

    def apply(g):
        y_off, g_diag = ctx.pop(("y_off", g)), ctx.pop(("g_diag", g))
        for r in range(HEADS_PER_GROUP):
            h = g * HEADS_PER_GROUP + r
            hr = slice(h * HEADDIM, (h + 1) * HEADDIM)
            x_bf = xs_t_ref[q, hr]
            y_h = _dot(x_bf, ctx.pop(("w_t", h)))
            y_h = y_h + y_off[r * HEADDIM:(r + 1) * HEADDIM] * ctx["decay_out_f"][h:h + 1]
            skip = dskip_ref[hr] + g_diag * ctx["dt_b"][h:h + 1]
            y_ref[q, hr] = _bf(y_h + skip * x_bf.astype(jnp.float32))
        _state_update_group(h_ref, xs_t_ref, b_ref, q, g, ctx["scale_in"], ctx["chunk_decay"])

    return [(functools.partial(weights, g), functools.partial(apply, g)) for g in range(GROUPS)]


def _bwd_out_kernel(xs_t_ref, b_ref, dt_ref, cum_ref, alog_ref, h0_ref, c_t_ref, ypart_ref,
                    yp_ref, zs_ref, x_ref, gate_ref, snw_ref, wout_ref,
                    fnw_ref, o_ref, h_ref, y_ref, *, cps):
    @pl.when(pl.program_id(1) == 0)
    def _():
        h_ref[...] = h0_ref[0]

    a_b = -jnp.exp(alog_ref[HEADS:])

    def chunk(q):
        dt_b, cum_b = dt_ref[q, HEADS:], cum_ref[q, HEADS:]
        tot_b = cum_b[:, CHUNK - 1:CHUNK]
        cumx_b = cum_b - dt_b * a_b
        decay_out = jnp.exp(tot_b - cumx_b)
        y_parts = []
        for g in range(GROUPS):
            cg_t = c_t_ref[q, g * D_STATE:(g + 1) * D_STATE]
            rows = slice(g * HEADS_PER_GROUP * HEADDIM, (g + 1) * HEADS_PER_GROUP * HEADDIM)
            y_off = _dot(_bf(h_ref[rows]), cg_t)
            for r in range(HEADS_PER_GROUP):
                h = g * HEADS_PER_GROUP + r
                hr = slice(h * HEADDIM, (h + 1) * HEADDIM)
                y_parts.append(ypart_ref[q, hr].astype(jnp.float32)
                               + y_off[r * HEADDIM:(r + 1) * HEADDIM] * decay_out[h:h + 1])
        y_ref[_tok_rows(q), :] = jnp.concatenate(y_parts, axis=0).T
        _state_update(h_ref, xs_t_ref, b_ref, q, dt_b * jnp.exp(cumx_b), jnp.exp(tot_b))

    acc = jnp.zeros((cps * CHUNK, D_MODEL), jnp.float32)
    for i in range(max(cps, N_POOL_GROUPS)):
        if i < cps:
            chunk(cps - 1 - i)
        if i < N_POOL_GROUPS:
            acc = acc + _dot(yp_ref[i], wout_ref[i * POOL_GROUP_W:(i + 1) * POOL_GROUP_W])

    gw = W_SSD // GROUPS
    for g in range(GROUPS):
        cols = slice(g * gw, (g + 1) * gw)
        gated = y_ref[:, cols] * zs_ref[:, cols].astype(jnp.float32)
        ms = jnp.mean(gated * gated, axis=-1, keepdims=True)
        yn = gated * jax.lax.rsqrt(ms + EPS) * snw_ref[:, cols]
        acc = acc + _dot(_bf(yn), wout_ref[W_POOL + g * gw:W_POOL + (g + 1) * gw])
    hres = x_ref[...] + gate_ref[0] * acc
    ms = jnp.mean(hres * hres, axis=-1, keepdims=True)
    o_ref[...] = hres * jax.lax.rsqrt(ms + EPS) * fnw_ref[...]


def _backward_output(xs_t, b_tok, dt, cum, c_t, y_part, alog_col, h0, y_pool, gate_ssd, x2d, gate,
                     ssd_norm_w, w_out_bf, final_norm_w, bsz, n_chunks):
    n_tok = b_tok.shape[0]
    cps = min(n_chunks, SSD_CHUNKS_PER_STEP)
    n_steps = n_chunks // cps
    tm = cps * CHUNK
    block_of = lambda b, s: b * n_steps + (n_steps - 1 - s)
    tok = lambda b, s: (block_of(b, s), 0)
    chunk3 = lambda b, s: (block_of(b, s), 0, 0)
    const2 = lambda b, s: (0, 0)
    per_seq = lambda b, s: (b, 0, 0)
    head_spec = pl.BlockSpec((cps, 2 * HEADS, CHUNK), chunk3)
    return pl.pallas_call(
        functools.partial(_bwd_out_kernel, cps=cps),
        grid=(bsz, n_steps),
        in_specs=[pl.BlockSpec((cps, W_SSD, CHUNK), chunk3),
                  pl.BlockSpec((tm, GN), tok),
                  head_spec, head_spec,
                  pl.BlockSpec((2 * HEADS, 1), const2),
                  pl.BlockSpec((1, W_SSD, D_STATE), per_seq),
                  pl.BlockSpec((cps, GN, CHUNK), chunk3),
                  pl.BlockSpec((cps, W_SSD, CHUNK), chunk3),
                  pl.BlockSpec((N_POOL_GROUPS, tm, POOL_GROUP_W),
                               lambda b, s: (0, block_of(b, s), 0)),
                  pl.BlockSpec((tm, W_SSD), tok),
                  pl.BlockSpec((tm, D_MODEL), tok),
                  pl.BlockSpec((1, 1, D_MODEL), per_seq),
                  pl.BlockSpec((1, W_SSD), const2),
                  pl.BlockSpec((W_POOL + W_SSD, D_MODEL), const2),
                  pl.BlockSpec((1, D_MODEL), const2)],
        out_specs=pl.BlockSpec((tm, D_MODEL), tok),
        out_shape=jax.ShapeDtypeStruct((n_tok, D_MODEL), jnp.float32),
        scratch_shapes=[pltpu.VMEM((W_SSD, D_STATE), jnp.float32),
                        pltpu.VMEM((tm, W_SSD), jnp.float32)],
        compiler_params=pltpu.CompilerParams(
            dimension_semantics=("arbitrary", "arbitrary"), vmem_limit_bytes=VMEM_LIMIT),
        name="bwd_out",
    )(xs_t, b_tok, dt, cum, alog_col, h0, c_t, y_part, y_pool, gate_ssd, x2d, gate,
      ssd_norm_w.reshape(1, W_SSD), w_out_bf, final_norm_w.reshape(1, D_MODEL))


POOL_TILE_ROWS = 4
POOL_TILE = POOL_TILE_ROWS * GRID_W


def _pool_constants(window, n_rows):
    lo_off, hi_off = -(window // 2), window - window // 2
    col = np.arange(GRID_W)
    lo = np.clip(col + lo_off, 0, GRID_W)
    hi = np.clip(col + hi_off, 0, GRID_W)
    band = ((col[None, :] >= lo[:, None]) & (col[None, :] < hi[:, None])).astype(np.float32)
    band_tile = np.kron(np.eye(POOL_TILE_ROWS, dtype=np.float32), band)
    row = np.arange(n_rows)
    cnt_r = np.clip(row + hi_off, 0, n_rows) - np.clip(row + lo_off, 0, n_rows)
    inv = 1.0 / (cnt_r[:, None] * (hi - lo)[None, :]).astype(np.float64)
    inv = np.broadcast_to(inv.reshape(-1, 1), (n_rows * GRID_W, 128)).astype(np.float32)
    return jnp.asarray(band_tile, jnp.bfloat16), jnp.asarray(inv)


def _pool_kernel(u_ref, z_ref, band_ref, inv_ref, w_ref, scale_ref, o_ref, *, n_rows):
    for g, window in enumerate(POOL_WINDOWS):
        @pl.when(pl.program_id(0) == g)
        def _(window=window):
            _pool_image(u_ref.at[0], z_ref.at[0], band_ref.at[0], inv_ref.at[0], w_ref, scale_ref,
                        o_ref.at[0], window, n_rows)


def _pool_image(u_ref, z_ref, band_ref, inv_ref, w_ref, scale_ref, o_ref, window, n_rows):
    def grid_row(r):
        return u_ref[r * GRID_W:(r + 1) * GRID_W].astype(jnp.float32)

    def bounds(r):
        return max(r - window // 2, 0), min(r + window - window // 2, n_rows)

    band = band_ref[...]
    rsum, tile_rows = None, []
    for r in range(n_rows):
        lo, hi = bounds(r)
        if r == 0 or window <= 2:
            rsum = grid_row(lo)
            for k in range(lo + 1, hi):
                rsum = rsum + grid_row(k)
        else:
            prev_lo, prev_hi = bounds(r - 1)
            if hi > prev_hi:
                rsum = rsum + grid_row(hi - 1)
            if lo > prev_lo:
                rsum = rsum - grid_row(prev_lo)
        tile_rows.append(rsum)
        if len(tile_rows) < POOL_TILE_ROWS:
            continue
        base = (r + 1 - POOL_TILE_ROWS) * GRID_W
        rows = slice(base, base + POOL_TILE)
        rs = jnp.concatenate(tile_rows, axis=0)
        tile_rows = []
        box = _dot(band, _bf(rs))
        inv = inv_ref[rows]
        mean = box * jnp.concatenate([inv, inv], axis=1)
        d = mean - u_ref[rows].astype(jnp.float32)
        y = _dot(_bf(d), w_ref[0]) * scale_ref[...]
        o_ref[rows] = _bf(y * z_ref[rows].astype(jnp.float32))


def _pool_mixer(u, gate, pool_w_bf, pool_scale, bsz, n_img_tok):
    n_rows = n_img_tok // GRID_W
    consts = [_pool_constants(window, n_rows) for window in POOL_WINDOWS]
    band = jnp.stack([c[0] for c in consts])
    inv = jnp.stack([c[1] for c in consts])
    img = pl.BlockSpec((1, n_img_tok, POOL_GROUP_W), lambda g, b: (g, b, 0))
    per_group = lambda g, b: (g, 0, 0)
    return pl.pallas_call(
        functools.partial(_pool_kernel, n_rows=n_rows),
        grid=(N_POOL_GROUPS, bsz),
        in_specs=[img, img,
                  pl.BlockSpec((1, POOL_TILE, POOL_TILE), per_group),
                  pl.BlockSpec((1, n_img_tok, 128), per_group),
                  pl.BlockSpec((1, POOL_GROUP_W, POOL_GROUP_W), per_group),
                  pl.BlockSpec((1, POOL_GROUP_W), lambda g, b: (0, g))],
        out_specs=img,
        out_shape=jax.ShapeDtypeStruct(u.shape, jnp.bfloat16),
        compiler_params=pltpu.CompilerParams(
            dimension_semantics=("arbitrary", "arbitrary"), vmem_limit_bytes=VMEM_LIMIT),
        name="pool",
    )(u, gate, band, inv, pool_w_bf, pool_scale)


def kernel(x, c, ctx, c_ctx, norm_w, w_ada, b_ada, w_in, conv_w, conv_b, a_log, dt_bias, d_skip,
           ssd_norm_w, pool_w, pool_scale, w_out, final_norm_w):
    bsz, seq, _ = x.shape
    ctx_len = ctx.shape[1]
    depth = norm_w.shape[0]
    assert depth == 1, "single-layer block: the context stream update is never consumed"
    assert seq % PROJ_TILE == 0 and ctx_len % CHUNK == 0 and seq % POOL_TILE == 0
    assert OFF_XBC % (W_SSD + GN) == 0

    mod_rows = -(-(bsz + 1) // SUBLANES) * SUBLANES
    cond = jnp.concatenate([c, c_ctx[None], jnp.zeros((mod_rows - bsz - 1, D_MODEL), c.dtype)])
    mod = _modulation(cond, w_ada[0], b_ada[0])
    shift = mod[:, :D_MODEL].reshape(mod_rows, 1, D_MODEL)
    scale = mod[:, D_MODEL:2 * D_MODEL].reshape(mod_rows, 1, D_MODEL)
    gate = mod[:, 2 * D_MODEL:].reshape(mod_rows, 1, D_MODEL)

    w_in_bf = _bf(w_in[0])
    w_dt_bf = jnp.pad(_bf(w_in[0, :, OFF_DT:]), ((0, 0), (0, DT_PAD - 2 * HEADS)))
    alog_col = a_log[0].reshape(2 * HEADS, 1)
    bias_col = dt_bias[0].reshape(2 * HEADS, 1)
    dskip_b = jnp.broadcast_to(jnp.repeat(d_skip[0], HEADDIM)[:, None], (W_SSD, CHUNK))
    conv_b2 = conv_b[0].reshape(1, CONV_DIM)

    ctx2d = ctx.reshape(bsz * ctx_len, D_MODEL)
    h_fwd, h_bwd = _projection(
        ctx2d, norm_w[0], shift[bsz:bsz + 1], scale[bsz:bsz + 1], w_in_bf, w_dt_bf, conv_w[0],
        conv_b2, alog_col, bias_col, ctx_len, ctx_len, full=False)

    x2d = x.reshape(bsz * seq, D_MODEL)
    outs = _projection(x2d, norm_w[0], shift, scale, w_in_bf, w_dt_bf, conv_w[0], conv_b2,
                       alog_col, bias_col, seq, PROJ_TILE, full=True, dskip_b=dskip_b, h0=h_fwd)
    u_pool, gate_pool, gate_ssd, xs_t, b_tok, c_t, dt, cum, y_part = outs
    nc = seq // CHUNK
    y_pool = _pool_mixer(u_pool, gate_pool, _bf(pool_w[0]), pool_scale, bsz, seq)
    out = _backward_output(xs_t, b_tok, dt, cum, c_t, y_part, alog_col, h_bwd, y_pool, gate_ssd,
                           x2d, gate, ssd_norm_w[0], _bf(w_out[0]), final_norm_w, bsz, nc)
    return out.reshape(bsz, seq, D_MODEL)
```

```python
import functools

import numpy as np
import jax
import jax.numpy as jnp
from jax.experimental import pallas as pl
from jax.experimental.pallas import tpu as pltpu

D_MODEL = 1024
GRID_W = 64
W_POOL = 1024
W_SSD = 1024
POOL_WINDOWS = (2, 4, 8, 16)
N_POOL_GROUPS = len(POOL_WINDOWS)
POOL_GROUP_W = 256
HEADDIM = 64
HEADS = 16
GROUPS = 4
HEADS_PER_GROUP = 4
D_STATE = 128
D_CONV = 4
CONV_LEFT = 2
CHUNK = 128
GN = GROUPS * D_STATE
CONV_DIM = W_SSD + 2 * GN
OFF_POOL_Z = W_POOL
OFF_SSD_Z = 2 * W_POOL
OFF_XBC = 2 * W_POOL + W_SSD
OFF_DT = OFF_XBC + CONV_DIM
DT_PAD = 128
EPS = 1e-6
SUBLANES = 8
LANES = 128
IL_GROUPS = CHUNK // SUBLANES
CONV_SEG = 256
SSD_CHUNKS_PER_STEP = 4
PROJ_TILE = 512
SWEEP_LAG = 2
VMEM_LIMIT = 56 * 1024 * 1024


def _silu(v):
    h = 0.5 * v
    return h + h * jnp.tanh(h)


def _softplus(v):
    return jnp.maximum(v, 0.0) + jnp.log1p(jnp.exp(-jnp.abs(v)))


def _bf(v):
    return v.astype(jnp.bfloat16)


def _dot(a, b):
    return jnp.dot(a, b, preferred_element_type=jnp.float32)


def _mod_kernel(c_ref, w_ref, b_ref, o_ref):
    s, w = _silu(c_ref[...]), w_ref[...]
    s_hi, w_hi = _bf(s), _bf(w)
    s_lo = _bf(s - s_hi.astype(jnp.float32))
    w_lo = _bf(w - w_hi.astype(jnp.float32))
    both = _dot(jnp.concatenate([s_hi, s_lo], axis=0), w_hi)
    rows = s.shape[0]
    o_ref[...] = both[:rows] + both[rows:] + _dot(s_hi, w_lo) + b_ref[...]


def _modulation(cond_rows, w_ada, b_ada):
    rows = cond_rows.shape[0]
    n_out = w_ada.shape[1]
    tn = 1024
    return pl.pallas_call(
        _mod_kernel,
        grid=(n_out // tn,),
        in_specs=[pl.BlockSpec((rows, D_MODEL), lambda j: (0, 0)),
                  pl.BlockSpec((D_MODEL, tn), lambda j: (0, j)),
                  pl.BlockSpec((1, tn), lambda j: (0, j))],
        out_specs=pl.BlockSpec((rows, tn), lambda j: (0, j)),
        out_shape=jax.ShapeDtypeStruct((rows, n_out), jnp.float32),
        compiler_params=pltpu.CompilerParams(vmem_limit_bytes=VMEM_LIMIT),
        name="mod",
    )(cond_rows, w_ada, b_ada.reshape(1, n_out))


def _lane_cumsum(v):
    lane = jax.lax.broadcasted_iota(jnp.int32, v.shape, 1)
    shift = 1
    while shift < CHUNK:
        v = v + jnp.where(lane >= shift, pltpu.roll(v, shift, 1), 0.0)
        shift *= 2
    return v


def _proj_kernel(x_ref, xp_ref, xn_ref, nw_ref, sh_ref, sc_ref, w_ref, wdt_ref, cw_ref, cb_ref,
                 alog_ref, bias_ref, *rest, tm, tiles_per_seq, n_tiles, full):
    if full:
        (dskip_ref, h0_ref, u_ref, zp_ref, zs_ref, xs_t_ref, b_ref, c_t_ref,
         dt_ref, cum_ref, ypart_ref, pe_ref, xc_ref, mn_ref, kxs_ref, kb_ref, kc_ref, kdt_ref,
         kcum_ref, h_ref) = rest
    else:
        hf_ref, hb_ref, pe_ref, xc_ref, mn_ref, xs_t_ref, b_ref, dt_ref, cum_ref, h_ref = rest
    i = pl.program_id(0)
    pos = jnp.minimum(i, n_tiles - 1) % tiles_per_seq
    has_prev = pos > 0
    has_next = pos < tiles_per_seq - 1
    n_chunks = tm // CHUNK
    seg = CONV_SEG

    if full:
        @pl.when(i == 0)
        def _():
            for ref in (kxs_ref, kb_ref, kc_ref, kdt_ref, kcum_ref, h_ref):
                ref[...] = jnp.zeros(ref.shape, ref.dtype)

    gain = nw_ref[...] * (1.0 + sc_ref[0])

    def modulated(v):
        ms = jnp.mean(v * v, axis=-1, keepdims=True)
        return v * jax.lax.rsqrt(ms + EPS) * gain + sh_ref[0]

    m_tok = modulated(x_ref[...])
    hm = _bf(m_tok)
    for t in range(D_MODEL // LANES):
        mn_ref[t] = m_tok[:, t * LANES:(t + 1) * LANES]

    rows = [jnp.concatenate([mn_ref[t, pl.ds(q * CHUNK + b, SUBLANES, stride=IL_GROUPS), :]
                             for t in range(D_MODEL // LANES)], axis=1)
            for q in range(n_chunks) for b in range(IL_GROUPS)]
    halo = [jnp.where(has_prev, modulated(xp_ref[...]), 0.0),
            jnp.where(has_next, modulated(xn_ref[...]), 0.0)]
    hm_il = _bf(jnp.concatenate(halo + rows, axis=0))
    sub = jax.lax.broadcasted_iota(jnp.int32, (SUBLANES, seg), 0)

    def conv_stage(j, slot):
        is_x = j < W_SSD
        is_b = W_SSD <= j < W_SSD + GN

        def matmul():
            lo = (OFF_XBC if full else 0) + j
            pe_ref[slot] = _dot(hm_il, w_ref[:, lo:lo + seg])

        def group(q, b):
            lo = 2 * SUBLANES + q * CHUNK + b * SUBLANES
            return pe_ref[slot, lo:lo + SUBLANES]

        def shifted(q, b, delta):
            bb = b + delta
            if 0 <= bb < IL_GROUPS:
                return group(q, bb)
            if bb < 0:
                bb += IL_GROUPS
                if q == 0:
                    first = pe_ref[slot, bb - SUBLANES:bb - SUBLANES + 1]
                else:
                    row = 2 * SUBLANES + (q - 1) * CHUNK + bb * SUBLANES + SUBLANES - 1
                    first = pe_ref[slot, row:row + 1]
                return jnp.where(sub == 0, first, pltpu.roll(group(q, bb), 1, 0))
            bb -= IL_GROUPS
            if q == n_chunks - 1:
                last = pe_ref[slot, SUBLANES + bb:SUBLANES + bb + 1]
            else:
                nxt = 2 * SUBLANES + (q + 1) * CHUNK + bb * SUBLANES
                last = pe_ref[slot, nxt:nxt + 1]
            return jnp.where(sub == SUBLANES - 1, last, pltpu.roll(group(q, bb), SUBLANES - 1, 0))

        def epilogue():
            taps = [cw_ref[k:k + 1, j:j + seg] for k in range(D_CONV)]
            bias = cb_ref[:, j:j + seg]
            for q in range(n_chunks):
                for b in range(IL_GROUPS):
                    acc = bias
                    for k in range(D_CONV):
                        acc = acc + shifted(q, b, k - CONV_LEFT) * taps[k]
                    lo = q * CHUNK + b * SUBLANES
                    act = _silu(acc)
                    for t in range(seg // LANES):
                        xc_ref[slot, t, lo:lo + SUBLANES] = act[:, t * LANES:(t + 1) * LANES]
            for q in range(n_chunks):
                xc = jnp.concatenate(
                    [jnp.concatenate(
                        [xc_ref[slot, t, pl.ds(q * CHUNK + (m % 2) * (CHUNK // 2) + m // 2,
                                               SUBLANES, stride=SUBLANES), :]
                         for t in range(seg // LANES)], axis=1)
                     for m in range(IL_GROUPS)], axis=0)
                if is_b:
                    b_ref[q * CHUNK:(q + 1) * CHUNK, j - W_SSD:j - W_SSD + seg] = _bf(xc)
                else:
                    dst, off = (xs_t_ref, j) if is_x else (c_t_ref, j - W_SSD - GN)
                    dst[q, off:off + seg] = _bf(xc.T)

        return matmul, epilogue

    def plain_stage(cols, finish):
        box = []
        return (lambda: box.append(_dot(hm, w_ref[:, cols]))), (lambda: finish(box.pop()))

    def dt_stage():
        box = []

        def epilogue():
            p_dt = box.pop()
            a_col = -jnp.exp(alog_ref[...])
            for q in range(n_chunks):
                dt = _softplus(p_dt[q * CHUNK:(q + 1) * CHUNK].T[:2 * HEADS] + bias_ref[...])
                dt_ref[q] = dt
                cum_ref[q] = _lane_cumsum(dt * a_col)

        return (lambda: box.append(_dot(hm, wdt_ref[...]))), epilogue

    def store_to(ref, cols=None, act=None):
        def finish(v):
            v = v if act is None else _bf(act(v))
            if cols is None:
                ref[...] = v
            else:
                ref[:, cols] = v
        return finish

    def store_pair(ref, g, act=None):
        def finish(v):
            for n in range(2):
                part = v[:, n * POOL_GROUP_W:(n + 1) * POOL_GROUP_W]
                ref[g + n] = _bf(part if act is None else act(part))
        return finish

    light, heavy = [], []
    if full:
        for g in range(0, N_POOL_GROUPS, 2):
            light.append(plain_stage(slice(g * POOL_GROUP_W, (g + 2) * POOL_GROUP_W),
                                     store_pair(u_ref, g)))
    light.append(dt_stage())
    for j in range(0, CONV_DIM, seg):
        if full or j < W_SSD + GN:
            heavy.append(conv_stage(j, len(heavy) % 2))
    if full:
        for g in range(0, N_POOL_GROUPS, 2):
            zcols = slice(OFF_POOL_Z + g * POOL_GROUP_W, OFF_POOL_Z + (g + 2) * POOL_GROUP_W)
            light.append(plain_stage(zcols, store_pair(zp_ref, g, act=_silu)))
        for j in range(0, W_SSD, seg):
            light.append(plain_stage(slice(OFF_SSD_Z + j, OFF_SSD_Z + j + seg),
                                     store_to(zs_ref, cols=slice(j, j + seg), act=_silu)))
    stages = []
    light.reverse()
    heavy.reverse()
    while light or heavy:
        if light:
            stages.append(light.pop(0))
        if heavy:
            stages.append(heavy.pop(0))

    sweep = []
    if full:
        swept = jnp.maximum(i - 1, 0)
        h_ref[...] = jnp.where(swept % tiles_per_seq == 0, h0_ref[0], h_ref[...])
        src = jax.lax.broadcasted_iota(jnp.int32, (CHUNK, CHUNK), 0)
        dst = jax.lax.broadcasted_iota(jnp.int32, (CHUNK, CHUNK), 1)
        a_b = -jnp.exp(alog_ref[HEADS:])
        pairs = []
        for q in range(n_chunks):
            pairs += _fwd_chunk_slices(q, kxs_ref, kb_ref, kc_ref, kdt_ref, kcum_ref, dskip_ref,
                                       ypart_ref, h_ref, a_b, src <= dst, src == dst)
        lag = SWEEP_LAG
        assert lag < GROUPS, "weights(p) reads the state written by apply(p - GROUPS)"
        sweep = []
        for s in range(len(pairs) + lag):
            todo = []
            if s >= lag:
                todo.append(pairs[s - lag][1])
            if s < len(pairs):
                todo.append(pairs[s][0])
            sweep.append(functools.partial(lambda fs: [f() for f in fs], todo))

    stages[0][0]()
    done = 0
    for k, (_, epilogue) in enumerate(stages):
        if k + 1 < len(stages):
            stages[k + 1][0]()
        epilogue()
        upto = -(-len(sweep) * (k + 1) // len(stages))
        for piece in sweep[done:upto]:
            piece()
        done = upto

    if full:
        for kept, ref in ((kxs_ref, xs_t_ref), (kb_ref, b_ref), (kc_ref, c_t_ref),
                          (kdt_ref, dt_ref), (kcum_ref, cum_ref)):
            kept[...] = ref[...]
    else:
        a_b = -jnp.exp(alog_ref[HEADS:])
        for reverse, out_ref in ((False, hf_ref), (True, hb_ref)):
            h_ref[...] = jnp.zeros(h_ref.shape, h_ref.dtype)
            for q in (range(n_chunks - 1, -1, -1) if reverse else range(n_chunks)):
                if reverse:
                    dt_b, cum_b = dt_ref[q, HEADS:], cum_ref[q, HEADS:]
                    scale_in = dt_b * jnp.exp(cum_b - dt_b * a_b)
                    decay = jnp.exp(cum_b[:, CHUNK - 1:CHUNK])
                else:
                    dt_f, cum_f = dt_ref[q, :HEADS], cum_ref[q, :HEADS]
                    tot_f = cum_f[:, CHUNK - 1:CHUNK]
                    scale_in, decay = dt_f * jnp.exp(tot_f - cum_f), jnp.exp(tot_f)
                _state_update(h_ref, xs_t_ref, b_ref, q, scale_in, decay)
            out_ref[0] = h_ref[...]


def _projection(x2d, norm_w, shift, scale, w_bf, wdt_bf, conv_w, conv_b, alog_col, bias_col,
                seq_len, tm, full, dskip_b=None, h0=None):
    n_tok = x2d.shape[0]
    tiles_per_seq = seq_len // tm
    n_tiles = n_tok // tm
    n_mod = shift.shape[0]
    nct = n_tok // CHUNK
    per = tm // SUBLANES
    last_halo = n_tok // SUBLANES - 1
    kern = functools.partial(_proj_kernel, tm=tm, tiles_per_seq=tiles_per_seq, n_tiles=n_tiles,
                             full=full)
    tile = lambda i: jnp.minimum(i, n_tiles - 1)
    mod_map = (lambda i: (tile(i) // tiles_per_seq, 0, 0)) if n_mod > 1 else (lambda i: (0, 0, 0))
    mod_spec = pl.BlockSpec((1, 1, D_MODEL), mod_map)
    const = lambda i: (0, 0)
    tok = lambda i: (tile(i), 0)
    chunk3 = lambda i: (tile(i), 0, 0)
    q = tm // CHUNK
    xs_t = (jax.ShapeDtypeStruct((nct, W_SSD, CHUNK), jnp.bfloat16),
            pl.BlockSpec((q, W_SSD, CHUNK), chunk3))
    b_tok = (jax.ShapeDtypeStruct((n_tok, GN), jnp.bfloat16), pl.BlockSpec((tm, GN), tok))
    c_t = (jax.ShapeDtypeStruct((nct, GN, CHUNK), jnp.bfloat16), pl.BlockSpec((q, GN, CHUNK), chunk3))
    dt = (jax.ShapeDtypeStruct((nct, 2 * HEADS, CHUNK), jnp.float32),
          pl.BlockSpec((q, 2 * HEADS, CHUNK), chunk3))
    if full:
        pooled = (jax.ShapeDtypeStruct((N_POOL_GROUPS, n_tok, POOL_GROUP_W), jnp.bfloat16),
                  pl.BlockSpec((N_POOL_GROUPS, tm, POOL_GROUP_W), lambda i: (0, tile(i), 0)))
        zs = (jax.ShapeDtypeStruct((n_tok, W_SSD), jnp.bfloat16), pl.BlockSpec((tm, W_SSD), tok))
        swept = lambda i: jnp.maximum(i - 1, 0)
        y_part = (jax.ShapeDtypeStruct((nct, W_SSD, CHUNK), jnp.bfloat16),
                  pl.BlockSpec((q, W_SSD, CHUNK), lambda i: (swept(i), 0, 0)))
        outs = [pooled, pooled, zs, xs_t, b_tok, c_t, dt, dt, y_part]
        extra_in = [pl.BlockSpec((W_SSD, CHUNK), const),
                    pl.BlockSpec((1, W_SSD, D_STATE), lambda i: (swept(i) // tiles_per_seq, 0, 0))]
        extra_args = [dskip_b, h0]
        extra_scratch = [pltpu.VMEM((q, W_SSD, CHUNK), jnp.bfloat16),
                         pltpu.VMEM((tm, GN), jnp.bfloat16),
                         pltpu.VMEM((q, GN, CHUNK), jnp.bfloat16),
                         pltpu.VMEM((q, 2 * HEADS, CHUNK), jnp.float32),
                         pltpu.VMEM((q, 2 * HEADS, CHUNK), jnp.float32),
                         pltpu.VMEM((W_SSD, D_STATE), jnp.float32)]
    else:
        assert tiles_per_seq == 1, "prefix states are computed from one whole sequence per step"
        state = (jax.ShapeDtypeStruct((n_tiles, W_SSD, D_STATE), jnp.float32),
                 pl.BlockSpec((1, W_SSD, D_STATE), chunk3))
        outs = [state, state]
        extra_in, extra_args = [], []
        extra_scratch = [pltpu.VMEM((q, W_SSD, CHUNK), jnp.bfloat16),
                         pltpu.VMEM((tm, GN), jnp.bfloat16),
                         pltpu.VMEM((q, 2 * HEADS, CHUNK), jnp.float32),
                         pltpu.VMEM((q, 2 * HEADS, CHUNK), jnp.float32),
                         pltpu.VMEM((W_SSD, D_STATE), jnp.float32)]
    return pl.pallas_call(
        kern,
        grid=(n_tiles + 1 if full else n_tiles,),
        in_specs=[pl.BlockSpec((tm, D_MODEL), tok),
                  pl.BlockSpec((SUBLANES, D_MODEL), lambda i: (jnp.maximum(tile(i) * per - 1, 0), 0)),
                  pl.BlockSpec((SUBLANES, D_MODEL),
                               lambda i: (jnp.minimum((tile(i) + 1) * per, last_halo), 0)),
                  pl.BlockSpec((1, D_MODEL), const),
                  mod_spec, mod_spec,
                  (pl.BlockSpec(w_bf.shape, const) if full else
                   pl.BlockSpec((D_MODEL, W_SSD + GN), lambda i: (0, OFF_XBC // (W_SSD + GN)))),
                  pl.BlockSpec((D_MODEL, DT_PAD), const),
                  pl.BlockSpec((D_CONV, CONV_DIM), const),
                  pl.BlockSpec((1, CONV_DIM), const),
                  pl.BlockSpec((2 * HEADS, 1), const),
                  pl.BlockSpec((2 * HEADS, 1), const)] + extra_in,
        out_specs=[o[1] for o in outs],
        out_shape=[o[0] for o in outs],
        scratch_shapes=[pltpu.VMEM((2, tm + 2 * SUBLANES, CONV_SEG), jnp.float32),
                        pltpu.VMEM((2, CONV_SEG // LANES, tm, LANES), jnp.float32),
                        pltpu.VMEM((D_MODEL // LANES, tm, LANES), jnp.float32)] + extra_scratch,
        compiler_params=pltpu.CompilerParams(
            dimension_semantics=("arbitrary",), vmem_limit_bytes=VMEM_LIMIT),
        name="proj" if full else "proj_ctx",
    )(x2d, x2d, x2d, norm_w.reshape(1, D_MODEL), shift, scale, w_bf, wdt_bf, conv_w, conv_b,
      alog_col, bias_col, *extra_args)


def _tok_rows(q):
    return pl.ds(q * CHUNK, CHUNK)


def _state_update_group(h_ref, xs_t_ref, b_ref, q, g, scale_in, chunk_decay):
    bg = b_ref[_tok_rows(q), g * D_STATE:(g + 1) * D_STATE]
    xd = []
    for r in range(HEADS_PER_GROUP):
        h = g * HEADS_PER_GROUP + r
        x_h = xs_t_ref[q, h * HEADDIM:(h + 1) * HEADDIM].astype(jnp.float32)
        xd.append(_bf(x_h * scale_in[h:h + 1]))
    s_new = _dot(jnp.concatenate(xd, axis=0), bg)
    for r in range(HEADS_PER_GROUP):
        h = g * HEADS_PER_GROUP + r
        hr = slice(h * HEADDIM, (h + 1) * HEADDIM)
        h_ref[hr] = h_ref[hr] * chunk_decay[h:h + 1] + s_new[r * HEADDIM:(r + 1) * HEADDIM]


def _state_update(h_ref, xs_t_ref, b_ref, q, scale_in, chunk_decay):
    for g in range(GROUPS):
        _state_update_group(h_ref, xs_t_ref, b_ref, q, g, scale_in, chunk_decay)


def _fwd_chunk_slices(q, xs_t_ref, b_ref, c_t_ref, dt_ref, cum_ref, dskip_ref, y_ref, h_ref, a_b,
                      causal, is_diag):
    ctx = {}

    def setup():
        dt_f, cum_f = dt_ref[q, :HEADS], cum_ref[q, :HEADS]
        tot_f = cum_f[:, CHUNK - 1:CHUNK]
        dt_b, cum_b = dt_ref[q, HEADS:], cum_ref[q, HEADS:]
        cumx_b = cum_b - dt_b * a_b
        ctx["scale_in"] = dt_f * jnp.exp(tot_f - cum_f)
        ctx["chunk_decay"] = jnp.exp(tot_f)
        ctx["col_terms"] = jnp.concatenate(
            [jnp.log(dt_f) - cum_f, jnp.log(dt_b) + cumx_b,
             jnp.zeros((CHUNK - 2 * HEADS, CHUNK), jnp.float32)], axis=0).T
        ctx["row_f"], ctx["row_b"] = cum_f, -cumx_b
        ctx["decay_out_f"] = jnp.exp(cum_f)
        ctx["dt_b"] = dt_b

    def weights(g):
        if g == 0:
            setup()
        col_terms, row_f, row_b = ctx["col_terms"], ctx["row_f"], ctx["row_b"]
        bg = b_ref[_tok_rows(q), g * D_STATE:(g + 1) * D_STATE]
        cg_t = c_t_ref[q, g * D_STATE:(g + 1) * D_STATE]
        rows = slice(g * HEADS_PER_GROUP * HEADDIM, (g + 1) * HEADS_PER_GROUP * HEADDIM)
        g_t = _dot(bg, cg_t)
        ctx["g_diag", g] = jnp.sum(jnp.where(is_diag, g_t, 0.0), axis=0, keepdims=True)
        ctx["y_off", g] = _dot(_bf(h_ref[rows]), cg_t)
        for r in range(HEADS_PER_GROUP):
            h = g * HEADS_PER_GROUP + r
            col_f = jnp.broadcast_to(col_terms[:, h:h + 1], (CHUNK, CHUNK))
            col_b = jnp.broadcast_to(col_terms[:, HEADS + h:HEADS + h + 1], (CHUNK, CHUNK))
            expo = jnp.where(causal, col_f + row_f[h:h + 1], col_b + row_b[h:h + 1])
            ctx["w_t", h] = _bf(g_t * jnp.exp(expo))

    def apply(g):
        y_off, g_diag = ctx.pop(("y_off", g)), ctx.pop(("g_diag", g))
        for r in range(HEADS_PER_GROUP):
            h = g * HEADS_PER_GROUP + r
            hr = slice(h * HEADDIM, (h + 1) * HEADDIM)
            x_bf = xs_t_ref[q, hr]
            y_h = _dot(x_bf, ctx.pop(("w_t", h)))
            y_h = y_h + y_off[r * HEADDIM:(r + 1) * HEADDIM] * ctx["decay_out_f"][h:h + 1]
            skip = dskip_ref[hr] + g_diag * ctx["dt_b"][h:h + 1]
            y_ref[q, hr] = _bf(y_h + skip * x_bf.astype(jnp.float32))
        _state_update_group(h_ref, xs_t_ref, b_ref, q, g, ctx["scale_in"], ctx["chunk_decay"])

    return [(functools.partial(weights, g), functools.partial(apply, g)) for g in range(GROUPS)]


def _bwd_out_kernel(xs_t_ref, b_ref, dt_ref, cum_ref, alog_ref, h0_ref, c_t_ref, ypart_ref,
                    yp_ref, zs_ref, x_ref, gate_ref, snw_ref, wout_ref,
                    fnw_ref, o_ref, h_ref, y_ref, *, cps):
    @pl.when(pl.program_id(1) == 0)
    def _():
        h_ref[...] = h0_ref[0]

    a_b = -jnp.exp(alog_ref[HEADS:])

    def chunk(q):
        dt_b, cum_b = dt_ref[q, HEADS:], cum_ref[q, HEADS:]
        tot_b = cum_b[:, CHUNK - 1:CHUNK]
        cumx_b = cum_b - dt_b * a_b
        decay_out = jnp.exp(tot_b - cumx_b)
        y_parts = []
        for g in range(GROUPS):
            cg_t = c_t_ref[q, g * D_STATE:(g + 1) * D_STATE]
            rows = slice(g * HEADS_PER_GROUP * HEADDIM, (g + 1) * HEADS_PER_GROUP * HEADDIM)
            y_off = _dot(_bf(h_ref[rows]), cg_t)
            for r in range(HEADS_PER_GROUP):
                h = g * HEADS_PER_GROUP + r
                hr = slice(h * HEADDIM, (h + 1) * HEADDIM)
                y_parts.append(ypart_ref[q, hr].astype(jnp.float32)
                               + y_off[r * HEADDIM:(r + 1) * HEADDIM] * decay_out[h:h + 1])
        y_ref[_tok_rows(q), :] = jnp.concatenate(y_parts, axis=0).T
        _state_update(h_ref, xs_t_ref, b_ref, q, dt_b * jnp.exp(cumx_b), jnp.exp(tot_b))

    acc = jnp.zeros((cps * CHUNK, D_MODEL), jnp.float32)
    for i in range(max(cps, N_POOL_GROUPS)):
        if i < cps:
            chunk(cps - 1 - i)
        if i < N_POOL_GROUPS:
            acc = acc + _dot(yp_ref[i], wout_ref[i * POOL_GROUP_W:(i + 1) * POOL_GROUP_W])

    gw = W_SSD // GROUPS
    for g in range(GROUPS):
        cols = slice(g * gw, (g + 1) * gw)
        gated = y_ref[:, cols] * zs_ref[:, cols].astype(jnp.float32)
        ms = jnp.mean(gated * gated, axis=-1, keepdims=True)
        yn = gated * jax.lax.rsqrt(ms + EPS) * snw_ref[:, cols]
        acc = acc + _dot(_bf(yn), wout_ref[W_POOL + g * gw:W_POOL + (g + 1) * gw])
    hres = x_ref[...] + gate_ref[0] * acc
    ms = jnp.mean(hres * hres, axis=-1, keepdims=True)
    o_ref[...] = hres * jax.lax.rsqrt(ms + EPS) * fnw_ref[...]


def _backward_output(xs_t, b_tok, dt, cum, c_t, y_part, alog_col, h0, y_pool, gate_ssd, x2d, gate,
                     ssd_norm_w, w_out_bf, final_norm_w, bsz, n_chunks):
    n_tok = b_tok.shape[0]
    cps = min(n_chunks, SSD_CHUNKS_PER_STEP)
    n_steps = n_chunks // cps
    tm = cps * CHUNK
    block_of = lambda b, s: b * n_steps + (n_steps - 1 - s)
    tok = lambda b, s: (block_of(b, s), 0)
    chunk3 = lambda b, s: (block_of(b, s), 0, 0)
    const2 = lambda b, s: (0, 0)
    per_seq = lambda b, s: (b, 0, 0)
    head_spec = pl.BlockSpec((cps, 2 * HEADS, CHUNK), chunk3)
    return pl.pallas_call(
        functools.partial(_bwd_out_kernel, cps=cps),
        grid=(bsz, n_steps),
        in_specs=[pl.BlockSpec((cps, W_SSD, CHUNK), chunk3),
                  pl.BlockSpec((tm, GN), tok),
                  head_spec, head_spec,
                  pl.BlockSpec((2 * HEADS, 1), const2),
                  pl.BlockSpec((1, W_SSD, D_STATE), per_seq),
                  pl.BlockSpec((cps, GN, CHUNK), chunk3),
                  pl.BlockSpec((cps, W_SSD, CHUNK), chunk3),
                  pl.BlockSpec((N_POOL_GROUPS, tm, POOL_GROUP_W),
                               lambda b, s: (0, block_of(b, s), 0)),
                  pl.BlockSpec((tm, W_SSD), tok),
                  pl.BlockSpec((tm, D_MODEL), tok),
                  pl.BlockSpec((1, 1, D_MODEL), per_seq),
                  pl.BlockSpec((1, W_SSD), const2),
                  pl.BlockSpec((W_POOL + W_SSD, D_MODEL), const2),
                  pl.BlockSpec((1, D_MODEL), const2)],
        out_specs=pl.BlockSpec((tm, D_MODEL), tok),
        out_shape=jax.ShapeDtypeStruct((n_tok, D_MODEL), jnp.float32),
        scratch_shapes=[pltpu.VMEM((W_SSD, D_STATE), jnp.float32),
                        pltpu.VMEM((tm, W_SSD), jnp.float32)],
        compiler_params=pltpu.CompilerParams(
            dimension_semantics=("arbitrary", "arbitrary"), vmem_limit_bytes=VMEM_LIMIT),
        name="bwd_out",
    )(xs_t, b_tok, dt, cum, alog_col, h0, c_t, y_part, y_pool, gate_ssd, x2d, gate,
      ssd_norm_w.reshape(1, W_SSD), w_out_bf, final_norm_w.reshape(1, D_MODEL))


POOL_TILE_ROWS = 4
POOL_TILE = POOL_TILE_ROWS * GRID_W


def _pool_constants(window, n_rows):
    lo_off, hi_off = -(window // 2), window - window // 2
    col = np.arange(GRID_W)
    lo = np.clip(col + lo_off, 0, GRID_W)
    hi = np.clip(col + hi_off, 0, GRID_W)
    band = ((col[None, :] >= lo[:, None]) & (col[None, :] < hi[:, None])).astype(np.float32)
    band_tile = np.kron(np.eye(POOL_TILE_ROWS, dtype=np.float32), band)
    row = np.arange(n_rows)
    cnt_r = np.clip(row + hi_off, 0, n_rows) - np.clip(row + lo_off, 0, n_rows)
    inv = 1.0 / (cnt_r[:, None] * (hi - lo)[None, :]).astype(np.float64)
    inv = np.broadcast_to(inv.reshape(-1, 1), (n_rows * GRID_W, 128)).astype(np.float32)
    return jnp.asarray(band_tile, jnp.bfloat16), jnp.asarray(inv)


def _pool_kernel(u_ref, z_ref, band_ref, inv_ref, w_ref, scale_ref, o_ref, *, n_rows):
    for g, window in enumerate(POOL_WINDOWS):
        @pl.when(pl.program_id(0) == g)
        def _(window=window):
            _pool_image(u_ref.at[0], z_ref.at[0], band_ref.at[0], inv_ref.at[0], w_ref, scale_ref,
                        o_ref.at[0], window, n_rows)


def _pool_image(u_ref, z_ref, band_ref, inv_ref, w_ref, scale_ref, o_ref, window, n_rows):
    def grid_row(r):
        return u_ref[r * GRID_W:(r + 1) * GRID_W].astype(jnp.float32)

    def bounds(r):
        return max(r - window // 2, 0), min(r + window - window // 2, n_rows)

    band = band_ref[...]
    rsum, tile_rows = None, []
    for r in range(n_rows):
        lo, hi = bounds(r)
        if r == 0 or window <= 2:
            rsum = grid_row(lo)
            for k in range(lo + 1, hi):
                rsum = rsum + grid_row(k)
        else:
            prev_lo, prev_hi = bounds(r - 1)
            if hi > prev_hi:
                rsum = rsum + grid_row(hi - 1)
            if lo > prev_lo:
                rsum = rsum - grid_row(prev_lo)
        tile_rows.append(rsum)
        if len(tile_rows) < POOL_TILE_ROWS:
            continue
        base = (r + 1 - POOL_TILE_ROWS) * GRID_W
        rows = slice(base, base + POOL_TILE)
        rs = jnp.concatenate(tile_rows, axis=0)
        tile_rows = []
        box = _dot(band, _bf(rs))
        inv = inv_ref[rows]
        mean = box * jnp.concatenate([inv, inv], axis=1)
        d = mean - u_ref[rows].astype(jnp.float32)
        y = _dot(_bf(d), w_ref[0]) * scale_ref[...]
        o_ref[rows] = _bf(y * z_ref[rows].astype(jnp.float32))


def _pool_mixer(u, gate, pool_w_bf, pool_scale, bsz, n_img_tok):
    n_rows = n_img_tok // GRID_W
    consts = [_pool_constants(window, n_rows) for window in POOL_WINDOWS]
    band = jnp.stack([c[0] for c in consts])
    inv = jnp.stack([c[1] for c in consts])
    img = pl.BlockSpec((1, n_img_tok, POOL_GROUP_W), lambda g, b: (g, b, 0))
    per_group = lambda g, b: (g, 0, 0)
    return pl.pallas_call(
        functools.partial(_pool_kernel, n_rows=n_rows),
        grid=(N_POOL_GROUPS, bsz),
        in_specs=[img, img,
                  pl.BlockSpec((1, POOL_TILE, POOL_TILE), per_group),
                  pl.BlockSpec((1, n_img_tok, 128), per_group),
                  pl.BlockSpec((1, POOL_GROUP_W, POOL_GROUP_W), per_group),
                  pl.BlockSpec((1, POOL_GROUP_W), lambda g, b: (0, g))],
        out_specs=img,
        out_shape=jax.ShapeDtypeStruct(u.shape, jnp.bfloat16),
        compiler_params=pltpu.CompilerParams(
            dimension_semantics=("arbitrary", "arbitrary"), vmem_limit_bytes=VMEM_LIMIT),
        name="pool",
    )(u, gate, band, inv, pool_w_bf, pool_scale)


def kernel(x, c, ctx, c_ctx, norm_w, w_ada, b_ada, w_in, conv_w, conv_b, a_log, dt_bias, d_skip,
           ssd_norm_w, pool_w, pool_scale, w_out, final_norm_w):
    bsz, seq, _ = x.shape
    ctx_len = ctx.shape[1]
    depth = norm_w.shape[0]
    assert depth == 1, "single-layer block: the context stream update is never consumed"
    assert seq % PROJ_TILE == 0 and ctx_len % CHUNK == 0 and seq % POOL_TILE == 0
    assert OFF_XBC % (W_SSD + GN) == 0

    mod_rows = -(-(bsz + 1) // SUBLANES) * SUBLANES
    cond = jnp.concatenate([c, c_ctx[None], jnp.zeros((mod_rows - bsz - 1, D_MODEL), c.dtype)])
    mod = _modulation(cond, w_ada[0], b_ada[0])
    shift = mod[:, :D_MODEL].reshape(mod_rows, 1, D_MODEL)
    scale = mod[:, D_MODEL:2 * D_MODEL].reshape(mod_rows, 1, D_MODEL)
    gate = mod[:, 2 * D_MODEL:].reshape(mod_rows, 1, D_MODEL)

    w_in_bf = _bf(w_in[0])
    w_dt_bf = jnp.pad(_bf(w_in[0, :, OFF_DT:]), ((0, 0), (0, DT_PAD - 2 * HEADS)))
    alog_col = a_log[0].reshape(2 * HEADS, 1)
    bias_col = dt_bias[0].reshape(2 * HEADS, 1)
    dskip_b = jnp.broadcast_to(jnp.repeat(d_skip[0], HEADDIM)[:, None], (W_SSD, CHUNK))
    conv_b2 = conv_b[0].reshape(1, CONV_DIM)

    ctx2d = ctx.reshape(bsz * ctx_len, D_MODEL)
    h_fwd, h_bwd = _projection(
        ctx2d, norm_w[0], shift[bsz:bsz + 1], scale[bsz:bsz + 1], w_in_bf, w_dt_bf, conv_w[0],
        conv_b2, alog_col, bias_col, ctx_len, ctx_len, full=False)

    x2d = x.reshape(bsz * seq, D_MODEL)
    outs = _projection(x2d, norm_w[0], shift, scale, w_in_bf, w_dt_bf, conv_w[0], conv_b2,
                       alog_col, bias_col, seq, PROJ_TILE, full=True, dskip_b=dskip_b, h0=h_fwd)
    u_pool, gate_pool, gate_ssd, xs_t, b_tok, c_t, dt, cum, y_part = outs
    nc = seq // CHUNK
    y_pool = _pool_mixer(u_pool, gate_pool, _bf(pool_w[0]), pool_scale, bsz, seq)
    out = _backward_output(xs_t, b_tok, dt, cum, c_t, y_part, alog_col, h_bwd, y_pool, gate_ssd,
                           x2d, gate, ssd_norm_w[0], _bf(w_out[0]), final_norm_w, bsz, nc)
    return out.reshape(bsz, seq, D_MODEL)
```

```python
import functools

import numpy as np
import jax
import jax.numpy as jnp
from jax.experimental import pallas as pl
from jax.experimental.pallas import tpu as pltpu

D_MODEL = 1024
GRID_W = 64
W_POOL = 1024
W_SSD = 1024
POOL_WINDOWS = (2, 4, 8, 16)
N_POOL_GROUPS = len(POOL_WINDOWS)
POOL_GROUP_W = 256
HEADDIM = 64
HEADS = 16
GROUPS = 4
HEADS_PER_GROUP = 4
D_STATE = 128
D_CONV = 4
CONV_LEFT = 2
CHUNK = 128
GN = GROUPS * D_STATE
CONV_DIM = W_SSD + 2 * GN
OFF_POOL_Z = W_POOL
OFF_SSD_Z = 2 * W_POOL
OFF_XBC = 2 * W_POOL + W_SSD
OFF_DT = OFF_XBC + CONV_DIM
DT_PAD = 128
EPS = 1e-6
SUBLANES = 8
LANES = 128
IL_GROUPS = CHUNK // SUBLANES
CONV_SEG = 256
SSD_CHUNKS_PER_STEP = 4
PROJ_TILE = 512
SWEEP_LAG = 2
VMEM_LIMIT = 56 * 1024 * 1024


def _silu(v):
    h = 0.5 * v
    return h + h * jnp.tanh(h)


def _softplus(v):
    return jnp.maximum(v, 0.0) + jnp.log1p(jnp.exp(-jnp.abs(v)))


def _bf(v):
    return v.astype(jnp.bfloat16)


def _dot(a, b):
    return jnp.dot(a, b, preferred_element_type=jnp.float32)


def _mod_kernel(c_ref, w_ref, b_ref, o_ref):
    s, w = _silu(c_ref[...]), w_ref[...]
    s_hi, w_hi = _bf(s), _bf(w)
    s_lo = _bf(s - s_hi.astype(jnp.float32))
    w_lo = _bf(w - w_hi.astype(jnp.float32))
    both = _dot(jnp.concatenate([s_hi, s_lo], axis=0), w_hi)
    rows = s.shape[0]
    o_ref[...] = both[:rows] + both[rows:] + _dot(s_hi, w_lo) + b_ref[...]


def _modulation(cond_rows, w_ada, b_ada):
    rows = cond_rows.shape[0]
    n_out = w_ada.shape[1]
    tn = 1024
    return pl.pallas_call(
        _mod_kernel,
        grid=(n_out // tn,),
        in_specs=[pl.BlockSpec((rows, D_MODEL), lambda j: (0, 0)),
                  pl.BlockSpec((D_MODEL, tn), lambda j: (0, j)),
                  pl.BlockSpec((1, tn), lambda j: (0, j))],
        out_specs=pl.BlockSpec((rows, tn), lambda j: (0, j)),
        out_shape=jax.ShapeDtypeStruct((rows, n_out), jnp.float32),
        compiler_params=pltpu.CompilerParams(vmem_limit_bytes=VMEM_LIMIT),
        name="mod",
    )(cond_rows, w_ada, b_ada.reshape(1, n_out))


def _lane_cumsum(v):
    lane = jax.lax.broadcasted_iota(jnp.int32, v.shape, 1)
    shift = 1
    while shift < CHUNK:
        v = v + jnp.where(lane >= shift, pltpu.roll(v, shift, 1), 0.0)
        shift *= 2
    return v


def _proj_kernel(x_ref, xp_ref, xn_ref, nw_ref, sh_ref, sc_ref, w_ref, wdt_ref, cw_ref, cb_ref,
                 alog_ref, bias_ref, *rest, tm, tiles_per_seq, n_tiles, full):
    if full:
        (dskip_ref, h0_ref, u_ref, zp_ref, zs_ref, xs_t_ref, b_ref, c_t_ref,
         dt_ref, cum_ref, ypart_ref, pe_ref, xc_ref, mn_ref, kxs_ref, kb_ref, kc_ref, kdt_ref,
         kcum_ref, h_ref) = rest
    else:
        hf_ref, hb_ref, pe_ref, xc_ref, mn_ref, xs_t_ref, b_ref, dt_ref, cum_ref, h_ref = rest
    i = pl.program_id(0)
    pos = jnp.minimum(i, n_tiles - 1) % tiles_per_seq
    has_prev = pos > 0
    has_next = pos < tiles_per_seq - 1
    n_chunks = tm // CHUNK
    seg = CONV_SEG

    if full:
        @pl.when(i == 0)
        def _():
            for ref in (kxs_ref, kb_ref, kc_ref, kdt_ref, kcum_ref, h_ref):
                ref[...] = jnp.zeros(ref.shape, ref.dtype)

    gain = nw_ref[...] * (1.0 + sc_ref[0, 0])

    def modulated(v):
        ms = jnp.mean(v * v, axis=-1, keepdims=True)
        return v * jax.lax.rsqrt(ms + EPS) * gain + sh_ref[0, 0]

    m_tok = modulated(x_ref[...])
    hm = _bf(m_tok)
    for t in range(D_MODEL // LANES):
        mn_ref[t] = m_tok[:, t * LANES:(t + 1) * LANES]

    rows = [jnp.concatenate([mn_ref[t, pl.ds(q * CHUNK + b, SUBLANES, stride=IL_GROUPS), :]
                             for t in range(D_MODEL // LANES)], axis=1)
            for q in range(n_chunks) for b in range(IL_GROUPS)]
    halo = [jnp.where(has_prev, modulated(xp_ref[...]), 0.0),
            jnp.where(has_next, modulated(xn_ref[...]), 0.0)]
    hm_il = _bf(jnp.concatenate(halo + rows, axis=0))
    sub = jax.lax.broadcasted_iota(jnp.int32, (SUBLANES, seg), 0)

    def conv_stage(j, slot):
        is_x = j < W_SSD
        is_b = W_SSD <= j < W_SSD + GN

        def matmul():
            lo = (OFF_XBC if full else 0) + j
            pe_ref[slot] = _dot(hm_il, w_ref[:, lo:lo + seg])

        def group(q, b):
            lo = 2 * SUBLANES + q * CHUNK + b * SUBLANES
            return pe_ref[slot, lo:lo + SUBLANES]

        def shifted(q, b, delta):
            bb = b + delta
            if 0 <= bb < IL_GROUPS:
                return group(q, bb)
            if bb < 0:
                bb += IL_GROUPS
                if q == 0:
                    first = pe_ref[slot, bb - SUBLANES:bb - SUBLANES + 1]
                else:
                    row = 2 * SUBLANES + (q - 1) * CHUNK + bb * SUBLANES + SUBLANES - 1
                    first = pe_ref[slot, row:row + 1]
                return jnp.where(sub == 0, first, pltpu.roll(group(q, bb), 1, 0))
            bb -= IL_GROUPS
            if q == n_chunks - 1:
                last = pe_ref[slot, SUBLANES + bb:SUBLANES + bb + 1]
            else:
                nxt = 2 * SUBLANES + (q + 1) * CHUNK + bb * SUBLANES
                last = pe_ref[slot, nxt:nxt + 1]
            return jnp.where(sub == SUBLANES - 1, last, pltpu.roll(group(q, bb), SUBLANES - 1, 0))

        def epilogue():
            taps = [cw_ref[k:k + 1, j:j + seg] for k in range(D_CONV)]
            bias = cb_ref[:, j:j + seg]
            for q in range(n_chunks):
                for b in range(IL_GROUPS):
                    acc = bias
                    for k in range(D_CONV):
                        acc = acc + shifted(q, b, k - CONV_LEFT) * taps[k]
                    lo = q * CHUNK + b * SUBLANES
                    act = _silu(acc)
                    for t in range(seg // LANES):
                        xc_ref[slot, t, lo:lo + SUBLANES] = act[:, t * LANES:(t + 1) * LANES]
            for q in range(n_chunks):
                xc = jnp.concatenate(
                    [jnp.concatenate(
                        [xc_ref[slot, t, pl.ds(q * CHUNK + (m % 2) * (CHUNK // 2) + m // 2,
                                               SUBLANES, stride=SUBLANES), :]
                         for t in range(seg // LANES)], axis=1)
                     for m in range(IL_GROUPS)], axis=0)
                if is_b:
                    b_ref[q * CHUNK:(q + 1) * CHUNK, j - W_SSD:j - W_SSD + seg] = _bf(xc)
                else:
                    dst, off = (xs_t_ref, j) if is_x else (c_t_ref, j - W_SSD - GN)
                    dst[q, off:off + seg] = _bf(xc.T)

        return matmul, epilogue

    def plain_stage(cols, finish):
        box = []
        return (lambda: box.append(_dot(hm, w_ref[:, cols]))), (lambda: finish(box.pop()))

    def dt_stage():
        box = []

        def epilogue():
            p_dt = box.pop()
            a_col = -jnp.exp(alog_ref[...])
            for q in range(n_chunks):
                dt = _softplus(p_dt[q * CHUNK:(q + 1) * CHUNK].T[:2 * HEADS] + bias_ref[...])
                dt_ref[q] = dt
                cum_ref[q] = _lane_cumsum(dt * a_col)

        return (lambda: box.append(_dot(hm, wdt_ref[...]))), epilogue

    def store_to(ref, cols=None, act=None):
        def finish(v):
            v = v if act is None else _bf(act(v))
            if cols is None:
                ref[...] = v
            else:
                ref[:, cols] = v
        return finish

    def store_pair(ref, g, act=None):
        def finish(v):
            for n in range(2):
                part = v[:, n * POOL_GROUP_W:(n + 1) * POOL_GROUP_W]
                ref[g + n] = _bf(part if act is None else act(part))
        return finish

    light, heavy = [], []
    if full:
        for g in range(0, N_POOL_GROUPS, 2):
            light.append(plain_stage(slice(g * POOL_GROUP_W, (g + 2) * POOL_GROUP_W),
                                     store_pair(u_ref, g)))
    light.append(dt_stage())
    for j in range(0, CONV_DIM, seg):
        if full or j < W_SSD + GN:
            heavy.append(conv_stage(j, len(heavy) % 2))
    if full:
        for g in range(0, N_POOL_GROUPS, 2):
            zcols = slice(OFF_POOL_Z + g * POOL_GROUP_W, OFF_POOL_Z + (g + 2) * POOL_GROUP_W)
            light.append(plain_stage(zcols, store_pair(zp_ref, g, act=_silu)))
        for j in range(0, W_SSD, seg):
            light.append(plain_stage(slice(OFF_SSD_Z + j, OFF_SSD_Z + j + seg),
                                     store_to(zs_ref, cols=slice(j, j + seg), act=_silu)))
    stages = []
    light.reverse()
    heavy.reverse()
    while light or heavy:
        if light:
            stages.append(light.pop(0))
        if heavy:
            stages.append(heavy.pop(0))

    sweep = []
    if full:
        swept = jnp.maximum(i - 1, 0)
        h_ref[...] = jnp.where(swept % tiles_per_seq == 0, h0_ref[0], h_ref[...])
        src = jax.lax.broadcasted_iota(jnp.int32, (CHUNK, CHUNK), 0)
        dst = jax.lax.broadcasted_iota(jnp.int32, (CHUNK, CHUNK), 1)
        a_b = -jnp.exp(alog_ref[HEADS:])
        pairs = []
        for q in range(n_chunks):
            pairs += _fwd_chunk_slices(q, kxs_ref, kb_ref, kc_ref, kdt_ref, kcum_ref, dskip_ref,
                                       ypart_ref, h_ref, a_b, src <= dst, src == dst)
        lag = SWEEP_LAG
        assert lag < GROUPS, "weights(p) reads the state written by apply(p - GROUPS)"
        sweep = []
        for s in range(len(pairs) + lag):
            todo = []
            if s >= lag:
                todo.append(pairs[s - lag][1])
            if s < len(pairs):
                todo.append(pairs[s][0])
            sweep.append(functools.partial(lambda fs: [f() for f in fs], todo))

    stages[0][0]()
    done = 0
    for k, (_, epilogue) in enumerate(stages):
        if k + 1 < len(stages):
            stages[k + 1][0]()
        epilogue()
        upto = -(-len(sweep) * (k + 1) // len(stages))
        for piece in sweep[done:upto]:
            piece()
        done = upto

    if full:
        for kept, ref in ((kxs_ref, xs_t_ref), (kb_ref, b_ref), (kc_ref, c_t_ref),
                          (kdt_ref, dt_ref), (kcum_ref, cum_ref)):
            kept[...] = ref[...]
    else:
        a_b = -jnp.exp(alog_ref[HEADS:])
        for reverse, out_ref in ((False, hf_ref), (True, hb_ref)):
            h_ref[...] = jnp.zeros(h_ref.shape, h_ref.dtype)
            for q in (range(n_chunks - 1, -1, -1) if reverse else range(n_chunks)):
                if reverse:
                    dt_b, cum_b = dt_ref[q, HEADS:], cum_ref[q, HEADS:]
                    scale_in = dt_b * jnp.exp(cum_b - dt_b * a_b)
                    decay = jnp.exp(cum_b[:, CHUNK - 1:CHUNK])
                else:
                    dt_f, cum_f = dt_ref[q, :HEADS], cum_ref[q, :HEADS]
                    tot_f = cum_f[:, CHUNK - 1:CHUNK]
                    scale_in, decay = dt_f * jnp.exp(tot_f - cum_f), jnp.exp(tot_f)
                _state_update(h_ref, xs_t_ref, b_ref, q, scale_in, decay)
            out_ref[0] = h_ref[...]


def _projection(x2d, norm_w, mod, mod_row, w_bf, wdt_bf, conv_w, conv_b, alog_col, bias_col,
                seq_len, tm, full, dskip_b=None, h0=None):
    n_tok = x2d.shape[0]
    tiles_per_seq = seq_len // tm
    n_tiles = n_tok // tm
    nct = n_tok // CHUNK
    per = tm // SUBLANES
    last_halo = n_tok // SUBLANES - 1
    kern = functools.partial(_proj_kernel, tm=tm, tiles_per_seq=tiles_per_seq, n_tiles=n_tiles,
                             full=full)
    tile = lambda i: jnp.minimum(i, n_tiles - 1)
    row = (lambda i: tile(i) // tiles_per_seq) if mod_row is None else (lambda i: mod_row)
    shift_spec = pl.BlockSpec((1, 1, 1, D_MODEL), lambda i: (row(i), 0, 0, 0))
    scale_spec = pl.BlockSpec((1, 1, 1, D_MODEL), lambda i: (row(i), 1, 0, 0))
    const = lambda i: (0, 0)
    tok = lambda i: (tile(i), 0)
    chunk3 = lambda i: (tile(i), 0, 0)
    q = tm // CHUNK
    xs_t = (jax.ShapeDtypeStruct((nct, W_SSD, CHUNK), jnp.bfloat16),
            pl.BlockSpec((q, W_SSD, CHUNK), chunk3))
    b_tok = (jax.ShapeDtypeStruct((n_tok, GN), jnp.bfloat16), pl.BlockSpec((tm, GN), tok))
    c_t = (jax.ShapeDtypeStruct((nct, GN, CHUNK), jnp.bfloat16), pl.BlockSpec((q, GN, CHUNK), chunk3))
    dt = (jax.ShapeDtypeStruct((nct, 2 * HEADS, CHUNK), jnp.float32),
          pl.BlockSpec((q, 2 * HEADS, CHUNK), chunk3))
    if full:
        pooled = (jax.ShapeDtypeStruct((N_POOL_GROUPS, n_tok, POOL_GROUP_W), jnp.bfloat16),
                  pl.BlockSpec((N_POOL_GROUPS, tm, POOL_GROUP_W), lambda i: (0, tile(i), 0)))
        zs = (jax.ShapeDtypeStruct((n_tok, W_SSD), jnp.bfloat16), pl.BlockSpec((tm, W_SSD), tok))
        swept = lambda i: jnp.maximum(i - 1, 0)
        y_part = (jax.ShapeDtypeStruct((nct, W_SSD, CHUNK), jnp.bfloat16),
                  pl.BlockSpec((q, W_SSD, CHUNK), lambda i: (swept(i), 0, 0)))
        outs = [pooled, pooled, zs, xs_t, b_tok, c_t, dt, dt, y_part]
        extra_in = [pl.BlockSpec((W_SSD, CHUNK), const),
                    pl.BlockSpec((1, W_SSD, D_STATE), lambda i: (swept(i) // tiles_per_seq, 0, 0))]
        extra_args = [dskip_b, h0]
        extra_scratch = [pltpu.VMEM((q, W_SSD, CHUNK), jnp.bfloat16),
                         pltpu.VMEM((tm, GN), jnp.bfloat16),
                         pltpu.VMEM((q, GN, CHUNK), jnp.bfloat16),
                         pltpu.VMEM((q, 2 * HEADS, CHUNK), jnp.float32),
                         pltpu.VMEM((q, 2 * HEADS, CHUNK), jnp.float32),
                         pltpu.VMEM((W_SSD, D_STATE), jnp.float32)]
    else:
        assert tiles_per_seq == 1, "prefix states are computed from one whole sequence per step"
        state = (jax.ShapeDtypeStruct((n_tiles, W_SSD, D_STATE), jnp.float32),
                 pl.BlockSpec((1, W_SSD, D_STATE), chunk3))
        outs = [state, state]
        extra_in, extra_args = [], []
        extra_scratch = [pltpu.VMEM((q, W_SSD, CHUNK), jnp.bfloat16),
                         pltpu.VMEM((tm, GN), jnp.bfloat16),
                         pltpu.VMEM((q, 2 * HEADS, CHUNK), jnp.float32),
                         pltpu.VMEM((q, 2 * HEADS, CHUNK), jnp.float32),
                         pltpu.VMEM((W_SSD, D_STATE), jnp.float32)]
    return pl.pallas_call(
        kern,
        grid=(n_tiles + 1 if full else n_tiles,),
        in_specs=[pl.BlockSpec((tm, D_MODEL), tok),
                  pl.BlockSpec((SUBLANES, D_MODEL), lambda i: (jnp.maximum(tile(i) * per - 1, 0), 0)),
                  pl.BlockSpec((SUBLANES, D_MODEL),
                               lambda i: (jnp.minimum((tile(i) + 1) * per, last_halo), 0)),
                  pl.BlockSpec((1, D_MODEL), const),
                  shift_spec, scale_spec,
                  (pl.BlockSpec(w_bf.shape, const) if full else
                   pl.BlockSpec((D_MODEL, W_SSD + GN), lambda i: (0, OFF_XBC // (W_SSD + GN)))),
                  pl.BlockSpec((D_MODEL, DT_PAD), const),
                  pl.BlockSpec((D_CONV, CONV_DIM), const),
                  pl.BlockSpec((1, CONV_DIM), const),
                  pl.BlockSpec((2 * HEADS, 1), const),
                  pl.BlockSpec((2 * HEADS, 1), const)] + extra_in,
        out_specs=[o[1] for o in outs],
        out_shape=[o[0] for o in outs],
        scratch_shapes=[pltpu.VMEM((2, tm + 2 * SUBLANES, CONV_SEG), jnp.float32),
                        pltpu.VMEM((2, CONV_SEG // LANES, tm, LANES), jnp.float32),
                        pltpu.VMEM((D_MODEL // LANES, tm, LANES), jnp.float32)] + extra_scratch,
        compiler_params=pltpu.CompilerParams(
            dimension_semantics=("arbitrary",), vmem_limit_bytes=VMEM_LIMIT),
        name="proj" if full else "proj_ctx",
    )(x2d, x2d, x2d, norm_w.reshape(1, D_MODEL), mod, mod, w_bf, wdt_bf, conv_w, conv_b,
      alog_col, bias_col, *extra_args)


def _tok_rows(q):
    return pl.ds(q * CHUNK, CHUNK)


def _state_update_group(h_ref, xs_t_ref, b_ref, q, g, scale_in, chunk_decay):
    bg = b_ref[_tok_rows(q), g * D_STATE:(g + 1) * D_STATE]
    xd = []
    for r in range(HEADS_PER_GROUP):
        h = g * HEADS_PER_GROUP + r
        x_h = xs_t_ref[q, h * HEADDIM:(h + 1) * HEADDIM].astype(jnp.float32)
        xd.append(_bf(x_h * scale_in[h:h + 1]))
    s_new = _dot(jnp.concatenate(xd, axis=0), bg)
    for r in range(HEADS_PER_GROUP):
        h = g * HEADS_PER_GROUP + r
        hr = slice(h * HEADDIM, (h + 1) * HEADDIM)
        h_ref[hr] = h_ref[hr] * chunk_decay[h:h + 1] + s_new[r * HEADDIM:(r + 1) * HEADDIM]


def _state_update(h_ref, xs_t_ref, b_ref, q, scale_in, chunk_decay):
    for g in range(GROUPS):
        _state_update_group(h_ref, xs_t_ref, b_ref, q, g, scale_in, chunk_decay)


def _fwd_chunk_slices(q, xs_t_ref, b_ref, c_t_ref, dt_ref, cum_ref, dskip_ref, y_ref, h_ref, a_b,
                      causal, is_diag):
    ctx = {}

    def setup():
        dt_f, cum_f = dt_ref[q, :HEADS], cum_ref[q, :HEADS]
        tot_f = cum_f[:, CHUNK - 1:CHUNK]
        dt_b, cum_b = dt_ref[q, HEADS:], cum_ref[q, HEADS:]
        cumx_b = cum_b - dt_b * a_b
        ctx["scale_in"] = dt_f * jnp.exp(tot_f - cum_f)
        ctx["chunk_decay"] = jnp.exp(tot_f)
        ctx["col_terms"] = jnp.concatenate(
            [jnp.log(dt_f) - cum_f, jnp.log(dt_b) + cumx_b,
             jnp.zeros((CHUNK - 2 * HEADS, CHUNK), jnp.float32)], axis=0).T
        ctx["row_f"], ctx["row_b"] = cum_f, -cumx_b
        ctx["decay_out_f"] = jnp.exp(cum_f)
        ctx["dt_b"] = dt_b

    def weights(g):
        if g == 0:
            setup()
        col_terms, row_f, row_b = ctx["col_terms"], ctx["row_f"], ctx["row_b"]
        bg = b_ref[_tok_rows(q), g * D_STATE:(g + 1) * D_STATE]
        cg_t = c_t_ref[q, g * D_STATE:(g + 1) * D_STATE]
        rows = slice(g * HEADS_PER_GROUP * HEADDIM, (g + 1) * HEADS_PER_GROUP * HEADDIM)
        g_t = _dot(bg, cg_t)
        ctx["g_diag", g] = jnp.sum(jnp.where(is_diag, g_t, 0.0), axis=0, keepdims=True)
        ctx["y_off", g] = _dot(_bf(h_ref[rows]), cg_t)
        for r in range(HEADS_PER_GROUP):
            h = g * HEADS_PER_GROUP + r
            col_f = jnp.broadcast_to(col_terms[:, h:h + 1], (CHUNK, CHUNK))
            col_b = jnp.broadcast_to(col_terms[:, HEADS + h:HEADS + h + 1], (CHUNK, CHUNK))
            expo = jnp.where(causal, col_f + row_f[h:h + 1], col_b + row_b[h:h + 1])
            ctx["w_t", h] = _bf(g_t * jnp.exp(expo))

    def apply(g):
        y_off, g_diag = ctx.pop(("y_off", g)), ctx.pop(("g_diag", g))
        for r in range(HEADS_PER_GROUP):
            h = g * HEADS_PER_GROUP + r
            hr = slice(h * HEADDIM, (h + 1) * HEADDIM)
            x_bf = xs_t_ref[q, hr]
            y_h = _dot(x_bf, ctx.pop(("w_t", h)))
            y_h = y_h + y_off[r * HEADDIM:(r + 1) * HEADDIM] * ctx["decay_out_f"][h:h + 1]
            skip = dskip_ref[hr] + g_diag * ctx["dt_b"][h:h + 1]
            y_ref[q, hr] = _bf(y_h + skip * x_bf.astype(jnp.float32))
        _state_update_group(h_ref, xs_t_ref, b_ref, q, g, ctx["scale_in"], ctx["chunk_decay"])

    return [(functools.partial(weights, g), functools.partial(apply, g)) for g in range(GROUPS)]


def _bwd_out_kernel(xs_t_ref, b_ref, dt_ref, cum_ref, alog_ref, h0_ref, c_t_ref, ypart_ref,
                    yp_ref, zs_ref, x_ref, gate_ref, snw_ref, wout_ref,
                    fnw_ref, o_ref, h_ref, y_ref, *, cps):
    @pl.when(pl.program_id(1) == 0)
    def _():
        h_ref[...] = h0_ref[0]

    a_b = -jnp.exp(alog_ref[HEADS:])

    def chunk(q):
        dt_b, cum_b = dt_ref[q, HEADS:], cum_ref[q, HEADS:]
        tot_b = cum_b[:, CHUNK - 1:CHUNK]
        cumx_b = cum_b - dt_b * a_b
        decay_out = jnp.exp(tot_b - cumx_b)
        y_parts = []
        for g in range(GROUPS):
            cg_t = c_t_ref[q, g * D_STATE:(g + 1) * D_STATE]
            rows = slice(g * HEADS_PER_GROUP * HEADDIM, (g + 1) * HEADS_PER_GROUP * HEADDIM)
            y_off = _dot(_bf(h_ref[rows]), cg_t)
            for r in range(HEADS_PER_GROUP):
                h = g * HEADS_PER_GROUP + r
                hr = slice(h * HEADDIM, (h + 1) * HEADDIM)
                y_parts.append(ypart_ref[q, hr].astype(jnp.float32)
                               + y_off[r * HEADDIM:(r + 1) * HEADDIM] * decay_out[h:h + 1])
        y_ref[_tok_rows(q), :] = jnp.concatenate(y_parts, axis=0).T
        _state_update(h_ref, xs_t_ref, b_ref, q, dt_b * jnp.exp(cumx_b), jnp.exp(tot_b))

    acc = jnp.zeros((cps * CHUNK, D_MODEL), jnp.float32)
    for i in range(max(cps, N_POOL_GROUPS)):
        if i < cps:
            chunk(cps - 1 - i)
        if i < N_POOL_GROUPS:
            acc = acc + _dot(yp_ref[i], wout_ref[i * POOL_GROUP_W:(i + 1) * POOL_GROUP_W])

    gw = W_SSD // GROUPS
    for g in range(GROUPS):
        cols = slice(g * gw, (g + 1) * gw)
        gated = y_ref[:, cols] * zs_ref[:, cols].astype(jnp.float32)
        ms = jnp.mean(gated * gated, axis=-1, keepdims=True)
        yn = gated * jax.lax.rsqrt(ms + EPS) * snw_ref[:, cols]
        acc = acc + _dot(_bf(yn), wout_ref[W_POOL + g * gw:W_POOL + (g + 1) * gw])
    hres = x_ref[...] + gate_ref[0, 0] * acc
    ms = jnp.mean(hres * hres, axis=-1, keepdims=True)
    o_ref[...] = hres * jax.lax.rsqrt(ms + EPS) * fnw_ref[...]


def _backward_output(xs_t, b_tok, dt, cum, c_t, y_part, alog_col, h0, y_pool, gate_ssd, x2d, gate,
                     ssd_norm_w, w_out_bf, final_norm_w, bsz, n_chunks):
    n_tok = b_tok.shape[0]
    cps = min(n_chunks, SSD_CHUNKS_PER_STEP)
    n_steps = n_chunks // cps
    tm = cps * CHUNK
    block_of = lambda b, s: b * n_steps + (n_steps - 1 - s)
    tok = lambda b, s: (block_of(b, s), 0)
    chunk3 = lambda b, s: (block_of(b, s), 0, 0)
    const2 = lambda b, s: (0, 0)
    per_seq = lambda b, s: (b, 0, 0)
    head_spec = pl.BlockSpec((cps, 2 * HEADS, CHUNK), chunk3)
    return pl.pallas_call(
        functools.partial(_bwd_out_kernel, cps=cps),
        grid=(bsz, n_steps),
        in_specs=[pl.BlockSpec((cps, W_SSD, CHUNK), chunk3),
                  pl.BlockSpec((tm, GN), tok),
                  head_spec, head_spec,
                  pl.BlockSpec((2 * HEADS, 1), const2),
                  pl.BlockSpec((1, W_SSD, D_STATE), per_seq),
                  pl.BlockSpec((cps, GN, CHUNK), chunk3),
                  pl.BlockSpec((cps, W_SSD, CHUNK), chunk3),
                  pl.BlockSpec((N_POOL_GROUPS, tm, POOL_GROUP_W),
                               lambda b, s: (0, block_of(b, s), 0)),
                  pl.BlockSpec((tm, W_SSD), tok),
                  pl.BlockSpec((tm, D_MODEL), tok),
                  pl.BlockSpec((1, 1, 1, D_MODEL), lambda b, s: (b, 2, 0, 0)),
                  pl.BlockSpec((1, W_SSD), const2),
                  pl.BlockSpec((W_POOL + W_SSD, D_MODEL), const2),
                  pl.BlockSpec((1, D_MODEL), const2)],
        out_specs=pl.BlockSpec((tm, D_MODEL), tok),
        out_shape=jax.ShapeDtypeStruct((n_tok, D_MODEL), jnp.float32),
        scratch_shapes=[pltpu.VMEM((W_SSD, D_STATE), jnp.float32),
                        pltpu.VMEM((tm, W_SSD), jnp.float32)],
        compiler_params=pltpu.CompilerParams(
            dimension_semantics=("arbitrary", "arbitrary"), vmem_limit_bytes=VMEM_LIMIT),
        name="bwd_out",
    )(xs_t, b_tok, dt, cum, alog_col, h0, c_t, y_part, y_pool, gate_ssd, x2d, gate,
      ssd_norm_w.reshape(1, W_SSD), w_out_bf, final_norm_w.reshape(1, D_MODEL))


POOL_TILE_ROWS = 4
POOL_TILE = POOL_TILE_ROWS * GRID_W


def _pool_constants(window, n_rows):
    lo_off, hi_off = -(window // 2), window - window // 2
    col = np.arange(GRID_W)
    lo = np.clip(col + lo_off, 0, GRID_W)
    hi = np.clip(col + hi_off, 0, GRID_W)
    band = ((col[None, :] >= lo[:, None]) & (col[None, :] < hi[:, None])).astype(np.float32)
    band_tile = np.kron(np.eye(POOL_TILE_ROWS, dtype=np.float32), band)
    row = np.arange(n_rows)
    cnt_r = np.clip(row + hi_off, 0, n_rows) - np.clip(row + lo_off, 0, n_rows)
    inv = 1.0 / (cnt_r[:, None] * (hi - lo)[None, :]).astype(np.float64)
    inv = np.broadcast_to(inv.reshape(-1, 1), (n_rows * GRID_W, 128)).astype(np.float32)
    return jnp.asarray(band_tile, jnp.bfloat16), jnp.asarray(inv)


def _pool_kernel(u_ref, z_ref, band_ref, inv_ref, w_ref, scale_ref, o_ref, *, n_rows):
    for g, window in enumerate(POOL_WINDOWS):
        @pl.when(pl.program_id(0) == g)
        def _(window=window):
            _pool_image(u_ref.at[0], z_ref.at[0], band_ref.at[0], inv_ref.at[0], w_ref, scale_ref,
                        o_ref.at[0], window, n_rows)


def _pool_image(u_ref, z_ref, band_ref, inv_ref, w_ref, scale_ref, o_ref, window, n_rows):
    def grid_row(r):
        return u_ref[r * GRID_W:(r + 1) * GRID_W].astype(jnp.float32)

    def bounds(r):
        return max(r - window // 2, 0), min(r + window - window // 2, n_rows)

    band = band_ref[...]
    rsum, tile_rows = None, []
    for r in range(n_rows):
        lo, hi = bounds(r)
        if r == 0 or window <= 2:
            rsum = grid_row(lo)
            for k in range(lo + 1, hi):
                rsum = rsum + grid_row(k)
        else:
            prev_lo, prev_hi = bounds(r - 1)
            if hi > prev_hi:
                rsum = rsum + grid_row(hi - 1)
            if lo > prev_lo:
                rsum = rsum - grid_row(prev_lo)
        tile_rows.append(rsum)
        if len(tile_rows) < POOL_TILE_ROWS:
            continue
        base = (r + 1 - POOL_TILE_ROWS) * GRID_W
        rows = slice(base, base + POOL_TILE)
        rs = jnp.concatenate(tile_rows, axis=0)
        tile_rows = []
        box = _dot(band, _bf(rs))
        inv = inv_ref[rows]
        mean = box * jnp.concatenate([inv, inv], axis=1)
        d = mean - u_ref[rows].astype(jnp.float32)
        y = _dot(_bf(d), w_ref[0]) * scale_ref[...]
        o_ref[rows] = _bf(y * z_ref[rows].astype(jnp.float32))


def _pool_mixer(u, gate, pool_w_bf, pool_scale, bsz, n_img_tok):
    n_rows = n_img_tok // GRID_W
    consts = [_pool_constants(window, n_rows) for window in POOL_WINDOWS]
    band = jnp.stack([c[0] for c in consts])
    inv = jnp.stack([c[1] for c in consts])
    img = pl.BlockSpec((1, n_img_tok, POOL_GROUP_W), lambda g, b: (g, b, 0))
    per_group = lambda g, b: (g, 0, 0)
    return pl.pallas_call(
        functools.partial(_pool_kernel, n_rows=n_rows),
        grid=(N_POOL_GROUPS, bsz),
        in_specs=[img, img,
                  pl.BlockSpec((1, POOL_TILE, POOL_TILE), per_group),
                  pl.BlockSpec((1, n_img_tok, 128), per_group),
                  pl.BlockSpec((1, POOL_GROUP_W, POOL_GROUP_W), per_group),
                  pl.BlockSpec((1, POOL_GROUP_W), lambda g, b: (0, g))],
        out_specs=img,
        out_shape=jax.ShapeDtypeStruct(u.shape, jnp.bfloat16),
        compiler_params=pltpu.CompilerParams(
            dimension_semantics=("arbitrary", "arbitrary"), vmem_limit_bytes=VMEM_LIMIT),
        name="pool",
    )(u, gate, band, inv, pool_w_bf, pool_scale)


def kernel(x, c, ctx, c_ctx, norm_w, w_ada, b_ada, w_in, conv_w, conv_b, a_log, dt_bias, d_skip,
           ssd_norm_w, pool_w, pool_scale, w_out, final_norm_w):
    bsz, seq, _ = x.shape
    ctx_len = ctx.shape[1]
    depth = norm_w.shape[0]
    assert depth == 1, "single-layer block: the context stream update is never consumed"
    assert seq % PROJ_TILE == 0 and ctx_len % CHUNK == 0 and seq % POOL_TILE == 0
    assert OFF_XBC % (W_SSD + GN) == 0

    mod_rows = -(-(bsz + 1) // SUBLANES) * SUBLANES
    cond = jnp.concatenate([c, c_ctx[None], jnp.zeros((mod_rows - bsz - 1, D_MODEL), c.dtype)])
    mod = _modulation(cond, w_ada[0], b_ada[0]).reshape(mod_rows, 3, 1, D_MODEL)

    w_in_bf = _bf(w_in[0])
    w_dt_bf = jnp.pad(w_in_bf[:, OFF_DT:], ((0, 0), (0, DT_PAD - 2 * HEADS)))
    alog_col = a_log[0].reshape(2 * HEADS, 1)
    bias_col = dt_bias[0].reshape(2 * HEADS, 1)
    dskip_b = jnp.broadcast_to(jnp.repeat(d_skip[0], HEADDIM)[:, None], (W_SSD, CHUNK))
    conv_b2 = conv_b[0].reshape(1, CONV_DIM)

    ctx2d = ctx.reshape(bsz * ctx_len, D_MODEL)
    h_fwd, h_bwd = _projection(
        ctx2d, norm_w[0], mod, bsz, w_in_bf, w_dt_bf, conv_w[0], conv_b2, alog_col, bias_col,
        ctx_len, ctx_len, full=False)

    x2d = x.reshape(bsz * seq, D_MODEL)
    outs = _projection(x2d, norm_w[0], mod, None, w_in_bf, w_dt_bf, conv_w[0], conv_b2,
                       alog_col, bias_col, seq, PROJ_TILE, full=True, dskip_b=dskip_b, h0=h_fwd)
    u_pool, gate_pool, gate_ssd, xs_t, b_tok, c_t, dt, cum, y_part = outs
    nc = seq // CHUNK
    y_pool = _pool_mixer(u_pool, gate_pool, _bf(pool_w[0]), pool_scale, bsz, seq)
    out = _backward_output(xs_t, b_tok, dt, cum, c_t, y_part, alog_col, h_bwd, y_pool, gate_ssd,
                           x2d, mod, ssd_norm_w[0], _bf(w_out[0]), final_norm_w, bsz, nc)
    return out.reshape(bsz, seq, D_MODEL)
```

```python
import functools

import numpy as np
import jax
import jax.numpy as jnp
from jax.experimental import pallas as pl
from jax.experimental.pallas import tpu as pltpu

D_MODEL = 1024
GRID_W = 64
W_POOL = 1024
W_SSD = 1024
POOL_WINDOWS = (2, 4, 8, 16)
N_POOL_GROUPS = len(POOL_WINDOWS)
POOL_GROUP_W = 256
HEADDIM = 64
HEADS = 16
GROUPS = 4
HEADS_PER_GROUP = 4
D_STATE = 128
D_CONV = 4
CONV_LEFT = 2
CHUNK = 128
GN = GROUPS * D_STATE
CONV_DIM = W_SSD + 2 * GN
OFF_POOL_Z = W_POOL
OFF_SSD_Z = 2 * W_POOL
OFF_XBC = 2 * W_POOL + W_SSD
OFF_DT = OFF_XBC + CONV_DIM
DT_PAD = 128
EPS = 1e-6
SUBLANES = 8
LANES = 128
IL_GROUPS = CHUNK // SUBLANES
CONV_SEG = 256
SSD_CHUNKS_PER_STEP = 4
PROJ_TILE = 512
SWEEP_LAG = 2
VMEM_LIMIT = 56 * 1024 * 1024


def _silu(v):
    h = 0.5 * v
    return h + h * jnp.tanh(h)


def _softplus(v):
    return jnp.maximum(v, 0.0) + jnp.log1p(jnp.exp(-jnp.abs(v)))


def _bf(v):
    return v.astype(jnp.bfloat16)


def _dot(a, b):
    return jnp.dot(a, b, preferred_element_type=jnp.float32)


def _mod_kernel(c_ref, w_ref, b_ref, o_ref):
    s, w = _silu(c_ref[...]), w_ref[...]
    s_hi, w_hi = _bf(s), _bf(w)
    s_lo = _bf(s - s_hi.astype(jnp.float32))
    w_lo = _bf(w - w_hi.astype(jnp.float32))
    both = _dot(jnp.concatenate([s_hi, s_lo], axis=0), w_hi)
    rows = s.shape[0]
    o_ref[:, 0, 0] = both[:rows] + both[rows:] + _dot(s_hi, w_lo) + b_ref[...]


def _modulation(cond_rows, w_ada, b_ada):
    rows = cond_rows.shape[0]
    n_kinds = w_ada.shape[1] // D_MODEL
    return pl.pallas_call(
        _mod_kernel,
        grid=(n_kinds,),
        in_specs=[pl.BlockSpec((rows, D_MODEL), lambda j: (0, 0)),
                  pl.BlockSpec((D_MODEL, D_MODEL), lambda j: (0, j)),
                  pl.BlockSpec((1, D_MODEL), lambda j: (0, j))],
        out_specs=pl.BlockSpec((rows, 1, 1, D_MODEL), lambda j: (0, j, 0, 0)),
        out_shape=jax.ShapeDtypeStruct((rows, n_kinds, 1, D_MODEL), jnp.float32),
        compiler_params=pltpu.CompilerParams(vmem_limit_bytes=VMEM_LIMIT),
        name="mod",
    )(cond_rows, w_ada, b_ada.reshape(1, n_kinds * D_MODEL))


def _lane_cumsum(v):
    lane = jax.lax.broadcasted_iota(jnp.int32, v.shape, 1)
    shift = 1
    while shift < CHUNK:
        v = v + jnp.where(lane >= shift, pltpu.roll(v, shift, 1), 0.0)
        shift *= 2
    return v


def _proj_kernel(x_ref, xp_ref, xn_ref, nw_ref, sh_ref, sc_ref, w_ref, wdt_ref, cw_ref, cb_ref,
                 alog_ref, bias_ref, *rest, tm, tiles_per_seq, n_tiles, full):
    if full:
        (dskip_ref, h0_ref, u_ref, zp_ref, zs_ref, xs_t_ref, b_ref, c_t_ref,
         dt_ref, cum_ref, ypart_ref, pe_ref, xc_ref, mn_ref, kxs_ref, kb_ref, kc_ref, kdt_ref,
         kcum_ref, h_ref) = rest
    else:
        hf_ref, hb_ref, pe_ref, xc_ref, mn_ref, xs_t_ref, b_ref, dt_ref, cum_ref, h_ref = rest
    i = pl.program_id(0)
    pos = jnp.minimum(i, n_tiles - 1) % tiles_per_seq
    has_prev = pos > 0
    has_next = pos < tiles_per_seq - 1
    n_chunks = tm // CHUNK
    seg = CONV_SEG

    if full:
        @pl.when(i == 0)
        def _():
            for ref in (kxs_ref, kb_ref, kc_ref, kdt_ref, kcum_ref, h_ref):
                ref[...] = jnp.zeros(ref.shape, ref.dtype)

    gain = nw_ref[...] * (1.0 + sc_ref[0, 0])

    def modulated(v):
        ms = jnp.mean(v * v, axis=-1, keepdims=True)
        return v * jax.lax.rsqrt(ms + EPS) * gain + sh_ref[0, 0]

    m_tok = modulated(x_ref[...])
    hm = _bf(m_tok)
    for t in range(D_MODEL // LANES):
        mn_ref[t] = m_tok[:, t * LANES:(t + 1) * LANES]

    rows = [jnp.concatenate([mn_ref[t, pl.ds(q * CHUNK + b, SUBLANES, stride=IL_GROUPS), :]
                             for t in range(D_MODEL // LANES)], axis=1)
            for q in range(n_chunks) for b in range(IL_GROUPS)]
    halo = [jnp.where(has_prev, modulated(xp_ref[...]), 0.0),
            jnp.where(has_next, modulated(xn_ref[...]), 0.0)]
    hm_il = _bf(jnp.concatenate(halo + rows, axis=0))
    sub = jax.lax.broadcasted_iota(jnp.int32, (SUBLANES, seg), 0)

    def conv_stage(j, slot):
        is_x = j < W_SSD
        is_b = W_SSD <= j < W_SSD + GN

        def matmul():
            lo = (OFF_XBC if full else 0) + j
            pe_ref[slot] = _dot(hm_il, w_ref[:, lo:lo + seg])

        def group(q, b):
            lo = 2 * SUBLANES + q * CHUNK + b * SUBLANES
            return pe_ref[slot, lo:lo + SUBLANES]

        def shifted(q, b, delta):
            bb = b + delta
            if 0 <= bb < IL_GROUPS:
                return group(q, bb)
            if bb < 0:
                bb += IL_GROUPS
                if q == 0:
                    first = pe_ref[slot, bb - SUBLANES:bb - SUBLANES + 1]
                else:
                    row = 2 * SUBLANES + (q - 1) * CHUNK + bb * SUBLANES + SUBLANES - 1
                    first = pe_ref[slot, row:row + 1]
                return jnp.where(sub == 0, first, pltpu.roll(group(q, bb), 1, 0))
            bb -= IL_GROUPS
            if q == n_chunks - 1:
                last = pe_ref[slot, SUBLANES + bb:SUBLANES + bb + 1]
            else:
                nxt = 2 * SUBLANES + (q + 1) * CHUNK + bb * SUBLANES
                last = pe_ref[slot, nxt:nxt + 1]
            return jnp.where(sub == SUBLANES - 1, last, pltpu.roll(group(q, bb), SUBLANES - 1, 0))

        def epilogue():
            taps = [cw_ref[k:k + 1, j:j + seg] for k in range(D_CONV)]
            bias = cb_ref[:, j:j + seg]
            for q in range(n_chunks):
                for b in range(IL_GROUPS):
                    acc = bias
                    for k in range(D_CONV):
                        acc = acc + shifted(q, b, k - CONV_LEFT) * taps[k]
                    lo = q * CHUNK + b * SUBLANES
                    act = _silu(acc)
                    for t in range(seg // LANES):
                        xc_ref[slot, t, lo:lo + SUBLANES] = act[:, t * LANES:(t + 1) * LANES]
            for q in range(n_chunks):
                xc = jnp.concatenate(
                    [jnp.concatenate(
                        [xc_ref[slot, t, pl.ds(q * CHUNK + (m % 2) * (CHUNK // 2) + m // 2,
                                               SUBLANES, stride=SUBLANES), :]
                         for t in range(seg // LANES)], axis=1)
                     for m in range(IL_GROUPS)], axis=0)
                if is_b:
                    b_ref[q * CHUNK:(q + 1) * CHUNK, j - W_SSD:j - W_SSD + seg] = _bf(xc)
                else:
                    dst, off = (xs_t_ref, j) if is_x else (c_t_ref, j - W_SSD - GN)
                    dst[q, off:off + seg] = _bf(xc.T)

        return matmul, epilogue

    def plain_stage(cols, finish):
        box = []
        return (lambda: box.append(_dot(hm, w_ref[:, cols]))), (lambda: finish(box.pop()))

    def dt_stage():
        box = []

        def epilogue():
            p_dt = box.pop()
            a_col = -jnp.exp(alog_ref[...])
            for q in range(n_chunks):
                dt = _softplus(p_dt[q * CHUNK:(q + 1) * CHUNK].T[:2 * HEADS] + bias_ref[...])
                dt_ref[q] = dt
                cum_ref[q] = _lane_cumsum(dt * a_col)

        return (lambda: box.append(_dot(hm, wdt_ref[...]))), epilogue

    def store_to(ref, cols=None, act=None):
        def finish(v):
            v = v if act is None else _bf(act(v))
            if cols is None:
                ref[...] = v
            else:
                ref[:, cols] = v
        return finish

    def store_pair(ref, g, act=None):
        def finish(v):
            for n in range(2):
                part = v[:, n * POOL_GROUP_W:(n + 1) * POOL_GROUP_W]
                ref[g + n] = _bf(part if act is None else act(part))
        return finish

    light, heavy = [], []
    if full:
        for g in range(0, N_POOL_GROUPS, 2):
            light.append(plain_stage(slice(g * POOL_GROUP_W, (g + 2) * POOL_GROUP_W),
                                     store_pair(u_ref, g)))
    light.append(dt_stage())
    for j in range(0, CONV_DIM, seg):
        if full or j < W_SSD + GN:
            heavy.append(conv_stage(j, len(heavy) % 2))
    if full:
        for g in range(0, N_POOL_GROUPS, 2):
            zcols = slice(OFF_POOL_Z + g * POOL_GROUP_W, OFF_POOL_Z + (g + 2) * POOL_GROUP_W)
            light.append(plain_stage(zcols, store_pair(zp_ref, g, act=_silu)))
        for j in range(0, W_SSD, seg):
            light.append(plain_stage(slice(OFF_SSD_Z + j, OFF_SSD_Z + j + seg),
                                     store_to(zs_ref, cols=slice(j, j + seg), act=_silu)))
    stages = []
    light.reverse()
    heavy.reverse()
    while light or heavy:
        if light:
            stages.append(light.pop(0))
        if heavy:
            stages.append(heavy.pop(0))

    sweep = []
    if full:
        swept = jnp.maximum(i - 1, 0)
        h_ref[...] = jnp.where(swept % tiles_per_seq == 0, h0_ref[0], h_ref[...])
        src = jax.lax.broadcasted_iota(jnp.int32, (CHUNK, CHUNK), 0)
        dst = jax.lax.broadcasted_iota(jnp.int32, (CHUNK, CHUNK), 1)
        a_b = -jnp.exp(alog_ref[HEADS:])
        pairs = []
        for q in range(n_chunks):
            pairs += _fwd_chunk_slices(q, kxs_ref, kb_ref, kc_ref, kdt_ref, kcum_ref, dskip_ref,
                                       ypart_ref, h_ref, a_b, src <= dst, src == dst)
        lag = SWEEP_LAG
        assert lag < GROUPS, "weights(p) reads the state written by apply(p - GROUPS)"
        sweep = []
        for s in range(len(pairs) + lag):
            todo = []
            if s >= lag:
                todo.append(pairs[s - lag][1])
            if s < len(pairs):
                todo.append(pairs[s][0])
            sweep.append(functools.partial(lambda fs: [f() for f in fs], todo))

    stages[0][0]()
    done = 0
    for k, (_, epilogue) in enumerate(stages):
        if k + 1 < len(stages):
            stages[k + 1][0]()
        epilogue()
        upto = -(-len(sweep) * (k + 1) // len(stages))
        for piece in sweep[done:upto]:
            piece()
        done = upto

    if full:
        for kept, ref in ((kxs_ref, xs_t_ref), (kb_ref, b_ref), (kc_ref, c_t_ref),
                          (kdt_ref, dt_ref), (kcum_ref, cum_ref)):
            kept[...] = ref[...]
    else:
        a_b = -jnp.exp(alog_ref[HEADS:])
        for reverse, out_ref in ((False, hf_ref), (True, hb_ref)):
            h_ref[...] = jnp.zeros(h_ref.shape, h_ref.dtype)
            for q in (range(n_chunks - 1, -1, -1) if reverse else range(n_chunks)):
                if reverse:
                    dt_b, cum_b = dt_ref[q, HEADS:], cum_ref[q, HEADS:]
                    scale_in = dt_b * jnp.exp(cum_b - dt_b * a_b)
                    decay = jnp.exp(cum_b[:, CHUNK - 1:CHUNK])
                else:
                    dt_f, cum_f = dt_ref[q, :HEADS], cum_ref[q, :HEADS]
                    tot_f = cum_f[:, CHUNK - 1:CHUNK]
                    scale_in, decay = dt_f * jnp.exp(tot_f - cum_f), jnp.exp(tot_f)
                _state_update(h_ref, xs_t_ref, b_ref, q, scale_in, decay)
            out_ref[0] = h_ref[...]


def _projection(x2d, norm_w, mod, mod_row, w_bf, wdt_bf, conv_w, conv_b, alog_col, bias_col,
                seq_len, tm, full, dskip_b=None, h0=None):
    n_tok = x2d.shape[0]
    tiles_per_seq = seq_len // tm
    n_tiles = n_tok // tm
    nct = n_tok // CHUNK
    per = tm // SUBLANES
    last_halo = n_tok // SUBLANES - 1
    kern = functools.partial(_proj_kernel, tm=tm, tiles_per_seq=tiles_per_seq, n_tiles=n_tiles,
                             full=full)
    tile = lambda i: jnp.minimum(i, n_tiles - 1)
    row = (lambda i: tile(i) // tiles_per_seq) if mod_row is None else (lambda i: mod_row)
    shift_spec = pl.BlockSpec((1, 1, 1, D_MODEL), lambda i: (row(i), 0, 0, 0))
    scale_spec = pl.BlockSpec((1, 1, 1, D_MODEL), lambda i: (row(i), 1, 0, 0))
    const = lambda i: (0, 0)
    tok = lambda i: (tile(i), 0)
    chunk3 = lambda i: (tile(i), 0, 0)
    q = tm // CHUNK
    xs_t = (jax.ShapeDtypeStruct((nct, W_SSD, CHUNK), jnp.bfloat16),
            pl.BlockSpec((q, W_SSD, CHUNK), chunk3))
    b_tok = (jax.ShapeDtypeStruct((n_tok, GN), jnp.bfloat16), pl.BlockSpec((tm, GN), tok))
    c_t = (jax.ShapeDtypeStruct((nct, GN, CHUNK), jnp.bfloat16),
           pl.BlockSpec((q, GN, CHUNK), chunk3))
    dt = (jax.ShapeDtypeStruct((nct, 2 * HEADS, CHUNK), jnp.float32),
          pl.BlockSpec((q, 2 * HEADS, CHUNK), chunk3))
    if full:
        pooled = (jax.ShapeDtypeStruct((N_POOL_GROUPS, n_tok, POOL_GROUP_W), jnp.bfloat16),
                  pl.BlockSpec((N_POOL_GROUPS, tm, POOL_GROUP_W), lambda i: (0, tile(i), 0)))
        zs = (jax.ShapeDtypeStruct((n_tok, W_SSD), jnp.bfloat16), pl.BlockSpec((tm, W_SSD), tok))
        swept = lambda i: jnp.maximum(i - 1, 0)
        y_part = (jax.ShapeDtypeStruct((nct, W_SSD, CHUNK), jnp.bfloat16),
                  pl.BlockSpec((q, W_SSD, CHUNK), lambda i: (swept(i), 0, 0)))
        outs = [pooled, pooled, zs, xs_t, b_tok, c_t, dt, dt, y_part]
        extra_in = [pl.BlockSpec((W_SSD, CHUNK), const),
                    pl.BlockSpec((1, W_SSD, D_STATE), lambda i: (swept(i) // tiles_per_seq, 0, 0))]
        extra_args = [dskip_b, h0]
        extra_scratch = [pltpu.VMEM((q, W_SSD, CHUNK), jnp.bfloat16),
                         pltpu.VMEM((tm, GN), jnp.bfloat16),
                         pltpu.VMEM((q, GN, CHUNK), jnp.bfloat16),
                         pltpu.VMEM((q, 2 * HEADS, CHUNK), jnp.float32),
                         pltpu.VMEM((q, 2 * HEADS, CHUNK), jnp.float32),
                         pltpu.VMEM((W_SSD, D_STATE), jnp.float32)]
    else:
        assert tiles_per_seq == 1, "prefix states are computed from one whole sequence per step"
        state = (jax.ShapeDtypeStruct((n_tiles, W_SSD, D_STATE), jnp.float32),
                 pl.BlockSpec((1, W_SSD, D_STATE), chunk3))
        outs = [state, state]
        extra_in, extra_args = [], []
        extra_scratch = [pltpu.VMEM((q, W_SSD, CHUNK), jnp.bfloat16),
                         pltpu.VMEM((tm, GN), jnp.bfloat16),
                         pltpu.VMEM((q, 2 * HEADS, CHUNK), jnp.float32),
                         pltpu.VMEM((q, 2 * HEADS, CHUNK), jnp.float32),
                         pltpu.VMEM((W_SSD, D_STATE), jnp.float32)]
    return pl.pallas_call(
        kern,
        grid=(n_tiles + 1 if full else n_tiles,),
        in_specs=[pl.BlockSpec((tm, D_MODEL), tok),
                  pl.BlockSpec((SUBLANES, D_MODEL),
                               lambda i: (jnp.maximum(tile(i) * per - 1, 0), 0)),
                  pl.BlockSpec((SUBLANES, D_MODEL),
                               lambda i: (jnp.minimum((tile(i) + 1) * per, last_halo), 0)),
                  pl.BlockSpec((1, D_MODEL), const),
                  shift_spec, scale_spec,
                  (pl.BlockSpec(w_bf.shape, const) if full else
                   pl.BlockSpec((D_MODEL, W_SSD + GN), lambda i: (0, OFF_XBC // (W_SSD + GN)))),
                  pl.BlockSpec((D_MODEL, DT_PAD), const),
                  pl.BlockSpec((D_CONV, CONV_DIM), const),
                  pl.BlockSpec((1, CONV_DIM), const),
                  pl.BlockSpec((2 * HEADS, 1), const),
                  pl.BlockSpec((2 * HEADS, 1), const)] + extra_in,
        out_specs=[o[1] for o in outs],
        out_shape=[o[0] for o in outs],
        scratch_shapes=[pltpu.VMEM((2, tm + 2 * SUBLANES, CONV_SEG), jnp.float32),
                        pltpu.VMEM((2, CONV_SEG // LANES, tm, LANES), jnp.float32),
                        pltpu.VMEM((D_MODEL // LANES, tm, LANES), jnp.float32)] + extra_scratch,
        compiler_params=pltpu.CompilerParams(
            dimension_semantics=("arbitrary",), vmem_limit_bytes=VMEM_LIMIT),
        name="proj" if full else "proj_ctx",
    )(x2d, x2d, x2d, norm_w.reshape(1, D_MODEL), mod, mod, w_bf, wdt_bf, conv_w, conv_b,
      alog_col, bias_col, *extra_args)


def _tok_rows(q):
    return pl.ds(q * CHUNK, CHUNK)


def _state_update_group(h_ref, xs_t_ref, b_ref, q, g, scale_in, chunk_decay):
    bg = b_ref[_tok_rows(q), g * D_STATE:(g + 1) * D_STATE]
    xd = []
    for r in range(HEADS_PER_GROUP):
        h = g * HEADS_PER_GROUP + r
        x_h = xs_t_ref[q, h * HEADDIM:(h + 1) * HEADDIM].astype(jnp.float32)
        xd.append(_bf(x_h * scale_in[h:h + 1]))
    s_new = _dot(jnp.concatenate(xd, axis=0), bg)
    for r in range(HEADS_PER_GROUP):
        h = g * HEADS_PER_GROUP + r
        hr = slice(h * HEADDIM, (h + 1) * HEADDIM)
        h_ref[hr] = h_ref[hr] * chunk_decay[h:h + 1] + s_new[r * HEADDIM:(r + 1) * HEADDIM]


def _state_update(h_ref, xs_t_ref, b_ref, q, scale_in, chunk_decay):
    for g in range(GROUPS):
        _state_update_group(h_ref, xs_t_ref, b_ref, q, g, scale_in, chunk_decay)


def _fwd_chunk_slices(q, xs_t_ref, b_ref, c_t_ref, dt_ref, cum_ref, dskip_ref, y_ref, h_ref, a_b,
                      causal, is_diag):
    ctx = {}

    def setup():
        dt_f, cum_f = dt_ref[q, :HEADS], cum_ref[q, :HEADS]
        tot_f = cum_f[:, CHUNK - 1:CHUNK]
        dt_b, cum_b = dt_ref[q, HEADS:], cum_ref[q, HEADS:]
        cumx_b = cum_b - dt_b * a_b
        ctx["scale_in"] = dt_f * jnp.exp(tot_f - cum_f)
        ctx["chunk_decay"] = jnp.exp(tot_f)
        ctx["col_terms"] = jnp.concatenate(
            [jnp.log(dt_f) - cum_f, jnp.log(dt_b) + cumx_b,
             jnp.zeros((CHUNK - 2 * HEADS, CHUNK), jnp.float32)], axis=0).T
        ctx["row_f"], ctx["row_b"] = cum_f, -cumx_b
        ctx["decay_out_f"] = jnp.exp(cum_f)
        ctx["dt_b"] = dt_b

    def weights(g):
        if g == 0:
            setup()
        col_terms, row_f, row_b = ctx["col_terms"], ctx["row_f"], ctx["row_b"]
        bg = b_ref[_tok_rows(q), g * D_STATE:(g + 1) * D_STATE]
        cg_t = c_t_ref[q, g * D_STATE:(g + 1) * D_STATE]
        rows = slice(g * HEADS_PER_GROUP * HEADDIM, (g + 1) * HEADS_PER_GROUP * HEADDIM)
        g_t = _dot(bg, cg_t)
        ctx["g_diag", g] = jnp.sum(jnp.where(is_diag, g_t, 0.0), axis=0, keepdims=True)
        ctx["y_off", g] = _dot(_bf(h_ref[rows]), cg_t)
        for r in range(HEADS_PER_GROUP):
            h = g * HEADS_PER_GROUP + r
            col_f = jnp.broadcast_to(col_terms[:, h:h + 1], (CHUNK, CHUNK))
            col_b = jnp.broadcast_to(col_terms[:, HEADS + h:HEADS + h + 1], (CHUNK, CHUNK))
            expo = jnp.where(causal, col_f + row_f[h:h + 1], col_b + row_b[h:h + 1])
            ctx["w_t", h] = _bf(g_t * jnp.exp(expo))

    def apply(g):
        y_off, g_diag = ctx.pop(("y_off", g)), ctx.pop(("g_diag", g))
        for r in range(HEADS_PER_GROUP):
            h = g * HEADS_PER_GROUP + r
            hr = slice(h * HEADDIM, (h + 1) * HEADDIM)
            x_bf = xs_t_ref[q, hr]
            y_h = _dot(x_bf, ctx.pop(("w_t", h)))
            y_h = y_h + y_off[r * HEADDIM:(r + 1) * HEADDIM] * ctx["decay_out_f"][h:h + 1]
            skip = dskip_ref[hr] + g_diag * ctx["dt_b"][h:h + 1]
            y_ref[q, hr] = _bf(y_h + skip * x_bf.astype(jnp.float32))
        _state_update_group(h_ref, xs_t_ref, b_ref, q, g, ctx["scale_in"], ctx["chunk_decay"])

    return [(functools.partial(weights, g), functools.partial(apply, g)) for g in range(GROUPS)]


def _bwd_out_kernel(xs_t_ref, b_ref, dt_ref, cum_ref, alog_ref, h0_ref, c_t_ref, ypart_ref,
                    yp_ref, zs_ref, x_ref, gate_ref, snw_ref, wout_ref,
                    fnw_ref, o_ref, h_ref, y_ref, *, cps):
    @pl.when(pl.program_id(1) == 0)
    def _():
        h_ref[...] = h0_ref[0]

    a_b = -jnp.exp(alog_ref[HEADS:])

    def chunk(q):
        dt_b, cum_b = dt_ref[q, HEADS:], cum_ref[q, HEADS:]
        tot_b = cum_b[:, CHUNK - 1:CHUNK]
        cumx_b = cum_b - dt_b * a_b
        decay_out = jnp.exp(tot_b - cumx_b)
        y_parts = []
        for g in range(GROUPS):
            cg_t = c_t_ref[q, g * D_STATE:(g + 1) * D_STATE]
            rows = slice(g * HEADS_PER_GROUP * HEADDIM, (g + 1) * HEADS_PER_GROUP * HEADDIM)
            y_off = _dot(_bf(h_ref[rows]), cg_t)
            for r in range(HEADS_PER_GROUP):
                h = g * HEADS_PER_GROUP + r
                hr = slice(h * HEADDIM, (h + 1) * HEADDIM)
                y_parts.append(ypart_ref[q, hr].astype(jnp.float32)
                               + y_off[r * HEADDIM:(r + 1) * HEADDIM] * decay_out[h:h + 1])
        y_ref[_tok_rows(q), :] = jnp.concatenate(y_parts, axis=0).T
        _state_update(h_ref, xs_t_ref, b_ref, q, dt_b * jnp.exp(cumx_b), jnp.exp(tot_b))

    acc = jnp.zeros((cps * CHUNK, D_MODEL), jnp.float32)
    for i in range(max(cps, N_POOL_GROUPS)):
        if i < cps:
            chunk(cps - 1 - i)
        if i < N_POOL_GROUPS:
            acc = acc + _dot(yp_ref[i], wout_ref[i * POOL_GROUP_W:(i + 1) * POOL_GROUP_W])

    gw = W_SSD // GROUPS
    for g in range(GROUPS):
        cols = slice(g * gw, (g + 1) * gw)
        gated = y_ref[:, cols] * zs_ref[:, cols].astype(jnp.float32)
        ms = jnp.mean(gated * gated, axis=-1, keepdims=True)
        yn = gated * jax.lax.rsqrt(ms + EPS) * snw_ref[:, cols]
        acc = acc + _dot(_bf(yn), wout_ref[W_POOL + g * gw:W_POOL + (g + 1) * gw])
    hres = x_ref[...] + gate_ref[0, 0] * acc
    ms = jnp.mean(hres * hres, axis=-1, keepdims=True)
    o_ref[...] = hres * jax.lax.rsqrt(ms + EPS) * fnw_ref[...]


def _backward_output(xs_t, b_tok, dt, cum, c_t, y_part, alog_col, h0, y_pool, gate_ssd, x2d, gate,
                     ssd_norm_w, w_out_bf, final_norm_w, bsz, n_chunks):
    n_tok = b_tok.shape[0]
    cps = min(n_chunks, SSD_CHUNKS_PER_STEP)
    n_steps = n_chunks // cps
    tm = cps * CHUNK
    block_of = lambda b, s: b * n_steps + (n_steps - 1 - s)
    tok = lambda b, s: (block_of(b, s), 0)
    chunk3 = lambda b, s: (block_of(b, s), 0, 0)
    const2 = lambda b, s: (0, 0)
    per_seq = lambda b, s: (b, 0, 0)
    head_spec = pl.BlockSpec((cps, 2 * HEADS, CHUNK), chunk3)
    return pl.pallas_call(
        functools.partial(_bwd_out_kernel, cps=cps),
        grid=(bsz, n_steps),
        in_specs=[pl.BlockSpec((cps, W_SSD, CHUNK), chunk3),
                  pl.BlockSpec((tm, GN), tok),
                  head_spec, head_spec,
                  pl.BlockSpec((2 * HEADS, 1), const2),
                  pl.BlockSpec((1, W_SSD, D_STATE), per_seq),
                  pl.BlockSpec((cps, GN, CHUNK), chunk3),
                  pl.BlockSpec((cps, W_SSD, CHUNK), chunk3),
                  pl.BlockSpec((N_POOL_GROUPS, tm, POOL_GROUP_W),
                               lambda b, s: (0, block_of(b, s), 0)),
                  pl.BlockSpec((tm, W_SSD), tok),
                  pl.BlockSpec((tm, D_MODEL), tok),
                  pl.BlockSpec((1, 1, 1, D_MODEL), lambda b, s: (b, 2, 0, 0)),
                  pl.BlockSpec((1, W_SSD), const2),
                  pl.BlockSpec((W_POOL + W_SSD, D_MODEL), const2),
                  pl.BlockSpec((1, D_MODEL), const2)],
        out_specs=pl.BlockSpec((tm, D_MODEL), tok),
        out_shape=jax.ShapeDtypeStruct((n_tok, D_MODEL), jnp.float32),
        scratch_shapes=[pltpu.VMEM((W_SSD, D_STATE), jnp.float32),
                        pltpu.VMEM((tm, W_SSD), jnp.float32)],
        compiler_params=pltpu.CompilerParams(
            dimension_semantics=("arbitrary", "arbitrary"), vmem_limit_bytes=VMEM_LIMIT),
        name="bwd_out",
    )(xs_t, b_tok, dt, cum, alog_col, h0, c_t, y_part, y_pool, gate_ssd, x2d, gate,
      ssd_norm_w.reshape(1, W_SSD), w_out_bf, final_norm_w.reshape(1, D_MODEL))


POOL_TILE_ROWS = 4
POOL_TILE = POOL_TILE_ROWS * GRID_W


def _pool_constants(window, n_rows):
    lo_off, hi_off = -(window // 2), window - window // 2
    col = np.arange(GRID_W)
    lo = np.clip(col + lo_off, 0, GRID_W)
    hi = np.clip(col + hi_off, 0, GRID_W)
    band = ((col[None, :] >= lo[:, None]) & (col[None, :] < hi[:, None])).astype(np.float32)
    band_tile = np.kron(np.eye(POOL_TILE_ROWS, dtype=np.float32), band)
    row = np.arange(n_rows)
    cnt_r = np.clip(row + hi_off, 0, n_rows) - np.clip(row + lo_off, 0, n_rows)
    inv = 1.0 / (cnt_r[:, None] * (hi - lo)[None, :]).astype(np.float64)
    inv = np.broadcast_to(inv.reshape(-1, 1), (n_rows * GRID_W, 128)).astype(np.float32)
    return jnp.asarray(band_tile, jnp.bfloat16), jnp.asarray(inv)


def _pool_kernel(u_ref, z_ref, band_ref, inv_ref, w_ref, scale_ref, o_ref, *, n_rows):
    for g, window in enumerate(POOL_WINDOWS):
        @pl.when(pl.program_id(0) == g)
        def _(window=window):
            _pool_image(u_ref.at[0], z_ref.at[0], band_ref.at[0], inv_ref.at[0], w_ref, scale_ref,
                        o_ref.at[0], window, n_rows)


def _pool_image(u_ref, z_ref, band_ref, inv_ref, w_ref, scale_ref, o_ref, window, n_rows):
    def grid_row(r):
        return u_ref[r * GRID_W:(r + 1) * GRID_W].astype(jnp.float32)

    def bounds(r):
        return max(r - window // 2, 0), min(r + window - window // 2, n_rows)

    band = band_ref[...]
    rsum, tile_rows = None, []
    for r in range(n_rows):
        lo, hi = bounds(r)
        if r == 0 or window <= 2:
            rsum = grid_row(lo)
            for k in range(lo + 1, hi):
                rsum = rsum + grid_row(k)
        else:
            prev_lo, prev_hi = bounds(r - 1)
            if hi > prev_hi:
                rsum = rsum + grid_row(hi - 1)
            if lo > prev_lo:
                rsum = rsum - grid_row(prev_lo)
        tile_rows.append(rsum)
        if len(tile_rows) < POOL_TILE_ROWS:
            continue
        base = (r + 1 - POOL_TILE_ROWS) * GRID_W
        rows = slice(base, base + POOL_TILE)
        rs = jnp.concatenate(tile_rows, axis=0)
        tile_rows = []
        box = _dot(band, _bf(rs))
        inv = inv_ref[rows]
        mean = box * jnp.concatenate([inv, inv], axis=1)
        d = mean - u_ref[rows].astype(jnp.float32)
        y = _dot(_bf(d), w_ref[0]) * scale_ref[...]
        o_ref[rows] = _bf(y * z_ref[rows].astype(jnp.float32))


def _pool_mixer(u, gate, pool_w_bf, pool_scale, bsz, n_img_tok):
    n_rows = n_img_tok // GRID_W
    consts = [_pool_constants(window, n_rows) for window in POOL_WINDOWS]
    band = jnp.stack([c[0] for c in consts])
    inv = jnp.stack([c[1] for c in consts])
    img = pl.BlockSpec((1, n_img_tok, POOL_GROUP_W), lambda g, b: (g, b, 0))
    per_group = lambda g, b: (g, 0, 0)
    return pl.pallas_call(
        functools.partial(_pool_kernel, n_rows=n_rows),
        grid=(N_POOL_GROUPS, bsz),
        in_specs=[img, img,
                  pl.BlockSpec((1, POOL_TILE, POOL_TILE), per_group),
                  pl.BlockSpec((1, n_img_tok, 128), per_group),
                  pl.BlockSpec((1, POOL_GROUP_W, POOL_GROUP_W), per_group),
                  pl.BlockSpec((1, POOL_GROUP_W), lambda g, b: (0, g))],
        out_specs=img,
        out_shape=jax.ShapeDtypeStruct(u.shape, jnp.bfloat16),
        compiler_params=pltpu.CompilerParams(
            dimension_semantics=("arbitrary", "arbitrary"), vmem_limit_bytes=VMEM_LIMIT),
        name="pool",
    )(u, gate, band, inv, pool_w_bf, pool_scale)


def kernel(x, c, ctx, c_ctx, norm_w, w_ada, b_ada, w_in, conv_w, conv_b, a_log, dt_bias, d_skip,
           ssd_norm_w, pool_w, pool_scale, w_out, final_norm_w):
    bsz, seq, _ = x.shape
    ctx_len = ctx.shape[1]
    depth = norm_w.shape[0]
    assert depth == 1, "single-layer block: the context stream update is never consumed"
    assert seq % PROJ_TILE == 0 and ctx_len % CHUNK == 0 and seq % POOL_TILE == 0
    assert OFF_XBC % (W_SSD + GN) == 0

    mod_rows = -(-(bsz + 1) // SUBLANES) * SUBLANES
    cond = jnp.concatenate([c, c_ctx[None], jnp.zeros((mod_rows - bsz - 1, D_MODEL), c.dtype)])
    mod = _modulation(cond, w_ada[0], b_ada[0])

    w_in_bf = _bf(w_in[0])
    w_dt_bf = jnp.pad(w_in_bf[:, OFF_DT:], ((0, 0), (0, DT_PAD - 2 * HEADS)))
    alog_col = a_log[0].reshape(2 * HEADS, 1)
    bias_col = dt_bias[0].reshape(2 * HEADS, 1)
    dskip_b = jnp.broadcast_to(jnp.repeat(d_skip[0], HEADDIM)[:, None], (W_SSD, CHUNK))
    conv_b2 = conv_b[0].reshape(1, CONV_DIM)

    ctx2d = ctx.reshape(bsz * ctx_len, D_MODEL)
    h_fwd, h_bwd = _projection(
        ctx2d, norm_w[0], mod, bsz, w_in_bf, w_dt_bf, conv_w[0], conv_b2, alog_col, bias_col,
        ctx_len, ctx_len, full=False)

    x2d = x.reshape(bsz * seq, D_MODEL)
    outs = _projection(x2d, norm_w[0], mod, None, w_in_bf, w_dt_bf, conv_w[0], conv_b2,
                       alog_col, bias_col, seq, PROJ_TILE, full=True, dskip_b=dskip_b, h0=h_fwd)
    u_pool, gate_pool, gate_ssd, xs_t, b_tok, c_t, dt, cum, y_part = outs
    nc = seq // CHUNK
    y_pool = _pool_mixer(u_pool, gate_pool, _bf(pool_w[0]), pool_scale, bsz, seq)
    out = _backward_output(xs_t, b_tok, dt, cum, c_t, y_part, alog_col, h_bwd, y_pool, gate_ssd,
                           x2d, mod, ssd_norm_w[0], _bf(w_out[0]), final_norm_w, bsz, nc)
    return out.reshape(bsz, seq, D_MODEL)
```

```python
import functools

import numpy as np
import jax
import jax.numpy as jnp
from jax.experimental import pallas as pl
from jax.experimental.pallas import tpu as pltpu

D_MODEL = 1024
GRID_W = 64
W_POOL = 1024
W_SSD = 1024
POOL_WINDOWS = (2, 4, 8, 16)
N_POOL_GROUPS = len(POOL_WINDOWS)
POOL_GROUP_W = 256
HEADDIM = 64
HEADS = 16
GROUPS = 4
HEADS_PER_GROUP = 4
D_STATE = 128
D_CONV = 4
CONV_LEFT = 2
CHUNK = 128
GN = GROUPS * D_STATE
CONV_DIM = W_SSD + 2 * GN
OFF_POOL_Z = W_POOL
OFF_SSD_Z = 2 * W_POOL
OFF_XBC = 2 * W_POOL + W_SSD
OFF_DT = OFF_XBC + CONV_DIM
DT_PAD = 128
EPS = 1e-6
SUBLANES = 8
LANES = 128
IL_GROUPS = CHUNK // SUBLANES
CONV_SEG = 256
SSD_CHUNKS_PER_STEP = 4
PROJ_TILE = 512
SWEEP_LAG = 2
VMEM_LIMIT = 56 * 1024 * 1024


def _silu(v):
    h = 0.5 * v
    return h + h * jnp.tanh(h)


def _softplus(v):
    return jnp.maximum(v, 0.0) + jnp.log1p(jnp.exp(-jnp.abs(v)))


def _bf(v):
    return v.astype(jnp.bfloat16)


def _dot(a, b):
    return jnp.dot(a, b, preferred_element_type=jnp.float32)


def _mod_kernel(c_ref, w_ref, b_ref, o_ref):
    s, w = _silu(c_ref[...]), w_ref[...]
    s_hi, w_hi = _bf(s), _bf(w)
    s_lo = _bf(s - s_hi.astype(jnp.float32))
    w_lo = _bf(w - w_hi.astype(jnp.float32))
    both = _dot(jnp.concatenate([s_hi, s_lo], axis=0), w_hi)
    rows = s.shape[0]
    o_ref[:, 0, 0] = both[:rows] + both[rows:] + _dot(s_hi, w_lo) + b_ref[...]


def _modulation(cond_rows, w_ada, b_ada):
    rows = cond_rows.shape[0]
    n_kinds = w_ada.shape[1] // D_MODEL
    return pl.pallas_call(
        _mod_kernel,
        grid=(n_kinds,),
        in_specs=[pl.BlockSpec((rows, D_MODEL), lambda j: (0, 0)),
                  pl.BlockSpec((D_MODEL, D_MODEL), lambda j: (0, j)),
                  pl.BlockSpec((1, D_MODEL), lambda j: (0, j))],
        out_specs=pl.BlockSpec((rows, 1, 1, D_MODEL), lambda j: (0, j, 0, 0)),
        out_shape=jax.ShapeDtypeStruct((rows, n_kinds, 1, D_MODEL), jnp.float32),
        compiler_params=pltpu.CompilerParams(vmem_limit_bytes=VMEM_LIMIT),
        name="mod",
    )(cond_rows, w_ada, b_ada.reshape(1, n_kinds * D_MODEL))


def _lane_cumsum(v):
    lane = jax.lax.broadcasted_iota(jnp.int32, v.shape, 1)
    shift = 1
    while shift < CHUNK:
        v = v + jnp.where(lane >= shift, pltpu.roll(v, shift, 1), 0.0)
        shift *= 2
    return v


def _proj_kernel(x_ref, xp_ref, xn_ref, nw_ref, sh_ref, sc_ref, w_ref, wdt_ref, cw_ref, cb_ref,
                 alog_ref, bias_ref, *rest, tm, tiles_per_seq, n_tiles, full):
    if full:
        (dskip_ref, h0_ref, u_ref, zp_ref, zs_ref, xs_t_ref, b_ref, c_t_ref,
         dt_ref, cum_ref, ypart_ref, pe_ref, xc_ref, mn_ref, kxs_ref, kb_ref, kc_ref, kdt_ref,
         kcum_ref, h_ref) = rest
    else:
        hf_ref, hb_ref, pe_ref, xc_ref, mn_ref, xs_t_ref, b_ref, dt_ref, cum_ref, h_ref = rest
    i = pl.program_id(0)
    pos = jnp.minimum(i, n_tiles - 1) % tiles_per_seq
    has_prev = pos > 0
    has_next = pos < tiles_per_seq - 1
    n_chunks = tm // CHUNK
    seg = CONV_SEG

    if full:
        @pl.when(i == 0)
        def _():
            for ref in (kxs_ref, kb_ref, kc_ref, kdt_ref, kcum_ref, h_ref):
                ref[...] = jnp.zeros(ref.shape, ref.dtype)

    gain = nw_ref[...] * (1.0 + sc_ref[0, 0])

    def modulated(v):
        ms = jnp.mean(v * v, axis=-1, keepdims=True)
        return v * jax.lax.rsqrt(ms + EPS) * gain + sh_ref[0, 0]

    m_tok = modulated(x_ref[...])
    hm = _bf(m_tok)
    for t in range(D_MODEL // LANES):
        mn_ref[t] = m_tok[:, t * LANES:(t + 1) * LANES]

    rows = [jnp.concatenate([mn_ref[t, pl.ds(q * CHUNK + b, SUBLANES, stride=IL_GROUPS), :]
                             for t in range(D_MODEL // LANES)], axis=1)
            for q in range(n_chunks) for b in range(IL_GROUPS)]
    halo = [jnp.where(has_prev, modulated(xp_ref[...]), 0.0),
            jnp.where(has_next, modulated(xn_ref[...]), 0.0)]
    hm_il = _bf(jnp.concatenate(halo + rows, axis=0))
    sub = jax.lax.broadcasted_iota(jnp.int32, (SUBLANES, seg), 0)

    def conv_stage(j, slot):
        is_x = j < W_SSD
        is_b = W_SSD <= j < W_SSD + GN

        def matmul():
            lo = (OFF_XBC if full else 0) + j
            pe_ref[slot] = _dot(hm_il, w_ref[:, lo:lo + seg])

        def group(q, b):
            lo = 2 * SUBLANES + q * CHUNK + b * SUBLANES
            return pe_ref[slot, lo:lo + SUBLANES]

        def shifted(q, b, delta):
            bb = b + delta
            if 0 <= bb < IL_GROUPS:
                return group(q, bb)
            if bb < 0:
                bb += IL_GROUPS
                if q == 0:
                    first = pe_ref[slot, bb - SUBLANES:bb - SUBLANES + 1]
                else:
                    row = 2 * SUBLANES + (q - 1) * CHUNK + bb * SUBLANES + SUBLANES - 1
                    first = pe_ref[slot, row:row + 1]
                return jnp.where(sub == 0, first, pltpu.roll(group(q, bb), 1, 0))
            bb -= IL_GROUPS
            if q == n_chunks - 1:
                last = pe_ref[slot, SUBLANES + bb:SUBLANES + bb + 1]
            else:
                nxt = 2 * SUBLANES + (q + 1) * CHUNK + bb * SUBLANES
                last = pe_ref[slot, nxt:nxt + 1]
            return jnp.where(sub == SUBLANES - 1, last, pltpu.roll(group(q, bb), SUBLANES - 1, 0))

        def epilogue():
            taps = [cw_ref[k:k + 1, j:j + seg] for k in range(D_CONV)]
            bias = cb_ref[:, j:j + seg]
            for q in range(n_chunks):
                for b in range(IL_GROUPS):
                    acc = bias
                    for k in range(D_CONV):
                        acc = acc + shifted(q, b, k - CONV_LEFT) * taps[k]
                    lo = q * CHUNK + b * SUBLANES
                    act = _silu(acc)
                    for t in range(seg // LANES):
                        xc_ref[slot, t, lo:lo + SUBLANES] = act[:, t * LANES:(t + 1) * LANES]
            for q in range(n_chunks):
                xc = jnp.concatenate(
                    [jnp.concatenate(
                        [xc_ref[slot, t, pl.ds(q * CHUNK + (m % 2) * (CHUNK // 2) + m // 2,
                                               SUBLANES, stride=SUBLANES), :]
                         for t in range(seg // LANES)], axis=1)
                     for m in range(IL_GROUPS)], axis=0)
                if is_b:
                    b_ref[q * CHUNK:(q + 1) * CHUNK, j - W_SSD:j - W_SSD + seg] = _bf(xc)
                else:
                    dst, off = (xs_t_ref, j) if is_x else (c_t_ref, j - W_SSD - GN)
                    dst[q, off:off + seg] = _bf(xc.T)

        return matmul, epilogue

    def plain_stage(cols, finish):
        box = []
        return (lambda: box.append(_dot(hm, w_ref[:, cols]))), (lambda: finish(box.pop()))

    def dt_stage():
        box = []

        def epilogue():
            p_dt = box.pop()
            a_col = -jnp.exp(alog_ref[...])
            for q in range(n_chunks):
                dt = _softplus(p_dt[q * CHUNK:(q + 1) * CHUNK].T[:2 * HEADS] + bias_ref[...])
                dt_ref[q] = dt
                cum_ref[q] = _lane_cumsum(dt * a_col)

        return (lambda: box.append(_dot(hm, wdt_ref[...]))), epilogue

    def store_to(ref, cols=None, act=None):
        def finish(v):
            v = v if act is None else _bf(act(v))
            if cols is None:
                ref[...] = v
            else:
                ref[:, cols] = v
        return finish

    def store_pair(ref, g, act=None):
        def finish(v):
            for n in range(2):
                part = v[:, n * POOL_GROUP_W:(n + 1) * POOL_GROUP_W]
                ref[g + n] = _bf(part if act is None else act(part))
        return finish

    light, heavy = [], []
    if full:
        for g in range(0, N_POOL_GROUPS, 2):
            light.append(plain_stage(slice(g * POOL_GROUP_W, (g + 2) * POOL_GROUP_W),
                                     store_pair(u_ref, g)))
    light.append(dt_stage())
    for j in range(0, CONV_DIM, seg):
        if full or j < W_SSD + GN:
            heavy.append(conv_stage(j, len(heavy) % 2))
    if full:
        for g in range(0, N_POOL_GROUPS, 2):
            zcols = slice(OFF_POOL_Z + g * POOL_GROUP_W, OFF_POOL_Z + (g + 2) * POOL_GROUP_W)
            light.append(plain_stage(zcols, store_pair(zp_ref, g, act=_silu)))
        for j in range(0, W_SSD, seg):
            light.append(plain_stage(slice(OFF_SSD_Z + j, OFF_SSD_Z + j + seg),
                                     store_to(zs_ref, cols=slice(j, j + seg), act=_silu)))
    stages = []
    light.reverse()
    heavy.reverse()
    while light or heavy:
        if light:
            stages.append(light.pop(0))
        if heavy:
            stages.append(heavy.pop(0))

    sweep = []
    if full:
        swept = jnp.maximum(i - 1, 0)
        h_ref[...] = jnp.where(swept % tiles_per_seq == 0, h0_ref[0], h_ref[...])
        src = jax.lax.broadcasted_iota(jnp.int32, (CHUNK, CHUNK), 0)
        dst = jax.lax.broadcasted_iota(jnp.int32, (CHUNK, CHUNK), 1)
        a_b = -jnp.exp(alog_ref[HEADS:])
        pairs = []
        for q in range(n_chunks):
            pairs += _fwd_chunk_slices(q, kxs_ref, kb_ref, kc_ref, kdt_ref, kcum_ref, dskip_ref,
                                       ypart_ref, h_ref, a_b, src <= dst, src == dst)
        lag = SWEEP_LAG
        assert lag < GROUPS, "weights(p) reads the state written by apply(p - GROUPS)"
        sweep = []
        for s in range(len(pairs) + lag):
            todo = []
            if s >= lag:
                todo.append(pairs[s - lag][1])
            if s < len(pairs):
                todo.append(pairs[s][0])
            sweep.append(functools.partial(lambda fs: [f() for f in fs], todo))

    stages[0][0]()
    done = 0
    for k, (_, epilogue) in enumerate(stages):
        if k + 1 < len(stages):
            stages[k + 1][0]()
        epilogue()
        upto = -(-len(sweep) * (k + 1) // len(stages))
        for piece in sweep[done:upto]:
            piece()
        done = upto

    if full:
        for kept, ref in ((kxs_ref, xs_t_ref), (kb_ref, b_ref), (kc_ref, c_t_ref),
                          (kdt_ref, dt_ref), (kcum_ref, cum_ref)):
            kept[...] = ref[...]
    else:
        a_b = -jnp.exp(alog_ref[HEADS:])
        for reverse, out_ref in ((False, hf_ref), (True, hb_ref)):
            h_ref[...] = jnp.zeros(h_ref.shape, h_ref.dtype)
            for q in (range(n_chunks - 1, -1, -1) if reverse else range(n_chunks)):
                if reverse:
                    dt_b, cum_b = dt_ref[q, HEADS:], cum_ref[q, HEADS:]
                    scale_in = dt_b * jnp.exp(cum_b - dt_b * a_b)
                    decay = jnp.exp(cum_b[:, CHUNK - 1:CHUNK])
                else:
                    dt_f, cum_f = dt_ref[q, :HEADS], cum_ref[q, :HEADS]
                    tot_f = cum_f[:, CHUNK - 1:CHUNK]
                    scale_in, decay = dt_f * jnp.exp(tot_f - cum_f), jnp.exp(tot_f)
                _state_update(h_ref, xs_t_ref, b_ref, q, scale_in, decay)
            out_ref[0] = h_ref[...]


def _projection(x2d, norm_w, mod, mod_row, w_bf, wdt_bf, conv_w, conv_b, alog_col, bias_col,
                seq_len, tm, full, dskip_b=None, h0=None):
    n_tok = x2d.shape[0]
    tiles_per_seq = seq_len // tm
    n_tiles = n_tok // tm
    nct = n_tok // CHUNK
    per = tm // SUBLANES
    last_halo = n_tok // SUBLANES - 1
    kern = functools.partial(_proj_kernel, tm=tm, tiles_per_seq=tiles_per_seq, n_tiles=n_tiles,
                             full=full)
    tile = lambda i: jnp.minimum(i, n_tiles - 1)
    row = (lambda i: tile(i) // tiles_per_seq) if mod_row is None else (lambda i: mod_row)
    shift_spec = pl.BlockSpec((1, 1, 1, D_MODEL), lambda i: (row(i), 0, 0, 0))
    scale_spec = pl.BlockSpec((1, 1, 1, D_MODEL), lambda i: (row(i), 1, 0, 0))
    const = lambda i: (0, 0)
    tok = lambda i: (tile(i), 0)
    chunk3 = lambda i: (tile(i), 0, 0)
    q = tm // CHUNK
    xs_t = (jax.ShapeDtypeStruct((nct, W_SSD, CHUNK), jnp.bfloat16),
            pl.BlockSpec((q, W_SSD, CHUNK), chunk3))
    b_tok = (jax.ShapeDtypeStruct((n_tok, GN), jnp.bfloat16), pl.BlockSpec((tm, GN), tok))
    c_t = (jax.ShapeDtypeStruct((nct, GN, CHUNK), jnp.bfloat16),
           pl.BlockSpec((q, GN, CHUNK), chunk3))
    dt = (jax.ShapeDtypeStruct((nct, 2 * HEADS, CHUNK), jnp.float32),
          pl.BlockSpec((q, 2 * HEADS, CHUNK), chunk3))
    if full:
        pooled = (jax.ShapeDtypeStruct((N_POOL_GROUPS, n_tok, POOL_GROUP_W), jnp.bfloat16),
                  pl.BlockSpec((N_POOL_GROUPS, tm, POOL_GROUP_W), lambda i: (0, tile(i), 0)))
        zs = (jax.ShapeDtypeStruct((n_tok, W_SSD), jnp.bfloat16), pl.BlockSpec((tm, W_SSD), tok))
        swept = lambda i: jnp.maximum(i - 1, 0)
        y_part = (jax.ShapeDtypeStruct((nct, W_SSD, CHUNK), jnp.bfloat16),
                  pl.BlockSpec((q, W_SSD, CHUNK), lambda i: (swept(i), 0, 0)))
        outs = [pooled, pooled, zs, xs_t, b_tok, c_t, dt, dt, y_part]
        extra_in = [pl.BlockSpec((W_SSD, CHUNK), const),
                    pl.BlockSpec((1, W_SSD, D_STATE), lambda i: (swept(i) // tiles_per_seq, 0, 0))]
        extra_args = [dskip_b, h0]
        extra_scratch = [pltpu.VMEM((q, W_SSD, CHUNK), jnp.bfloat16),
                         pltpu.VMEM((tm, GN), jnp.bfloat16),
                         pltpu.VMEM((q, GN, CHUNK), jnp.bfloat16),
                         pltpu.VMEM((q, 2 * HEADS, CHUNK), jnp.float32),
                         pltpu.VMEM((q, 2 * HEADS, CHUNK), jnp.float32),
                         pltpu.VMEM((W_SSD, D_STATE), jnp.float32)]
    else:
        assert tiles_per_seq == 1, "prefix states are computed from one whole sequence per step"
        state = (jax.ShapeDtypeStruct((n_tiles, W_SSD, D_STATE), jnp.float32),
                 pl.BlockSpec((1, W_SSD, D_STATE), chunk3))
        outs = [state, state]
        extra_in, extra_args = [], []
        extra_scratch = [pltpu.VMEM((q, W_SSD, CHUNK), jnp.bfloat16),
                         pltpu.VMEM((tm, GN), jnp.bfloat16),
                         pltpu.VMEM((q, 2 * HEADS, CHUNK), jnp.float32),
                         pltpu.VMEM((q, 2 * HEADS, CHUNK), jnp.float32),
                         pltpu.VMEM((W_SSD, D_STATE), jnp.float32)]
    return pl.pallas_call(
        kern,
        grid=(n_tiles + 1 if full else n_tiles,),
        in_specs=[pl.BlockSpec((tm, D_MODEL), tok),
                  pl.BlockSpec((SUBLANES, D_MODEL),
                               lambda i: (jnp.maximum(tile(i) * per - 1, 0), 0)),
                  pl.BlockSpec((SUBLANES, D_MODEL),
                               lambda i: (jnp.minimum((tile(i) + 1) * per, last_halo), 0)),
                  pl.BlockSpec((1, D_MODEL), const),
                  shift_spec, scale_spec,
                  (pl.BlockSpec(w_bf.shape, const) if full else
                   pl.BlockSpec((D_MODEL, W_SSD + GN), lambda i: (0, OFF_XBC // (W_SSD + GN)))),
                  pl.BlockSpec((D_MODEL, DT_PAD), const),
                  pl.BlockSpec((D_CONV, CONV_DIM), const),
                  pl.BlockSpec((1, CONV_DIM), const),
                  pl.BlockSpec((2 * HEADS, 1), const),
                  pl.BlockSpec((2 * HEADS, 1), const)] + extra_in,
        out_specs=[o[1] for o in outs],
        out_shape=[o[0] for o in outs],
        scratch_shapes=[pltpu.VMEM((2, tm + 2 * SUBLANES, CONV_SEG), jnp.float32),
                        pltpu.VMEM((2, CONV_SEG // LANES, tm, LANES), jnp.float32),
                        pltpu.VMEM((D_MODEL // LANES, tm, LANES), jnp.float32)] + extra_scratch,
        compiler_params=pltpu.CompilerParams(
            dimension_semantics=("arbitrary",), vmem_limit_bytes=VMEM_LIMIT),
        name="proj" if full else "proj_ctx",
    )(x2d, x2d, x2d, norm_w.reshape(1, D_MODEL), mod, mod, w_bf, wdt_bf, conv_w, conv_b,
      alog_col, bias_col, *extra_args)


def _tok_rows(q):
    return pl.ds(q * CHUNK, CHUNK)


def _state_update_group(h_ref, xs_t_ref, b_ref, q, g, scale_in, chunk_decay):
    bg = b_ref[_tok_rows(q), g * D_STATE:(g + 1) * D_STATE]
    xd = []
    for r in range(HEADS_PER_GROUP):
        h = g * HEADS_PER_GROUP + r
        x_h = xs_t_ref[q, h * HEADDIM:(h + 1) * HEADDIM].astype(jnp.float32)
        xd.append(_bf(x_h * scale_in[h:h + 1]))
    s_new = _dot(jnp.concatenate(xd, axis=0), bg)
    for r in range(HEADS_PER_GROUP):
        h = g * HEADS_PER_GROUP + r
        hr = slice(h * HEADDIM, (h + 1) * HEADDIM)
        h_ref[hr] = h_ref[hr] * chunk_decay[h:h + 1] + s_new[r * HEADDIM:(r + 1) * HEADDIM]


def _state_update(h_ref, xs_t_ref, b_ref, q, scale_in, chunk_decay):
    for g in range(GROUPS):
        _state_update_group(h_ref, xs_t_ref, b_ref, q, g, scale_in, chunk_decay)


def _fwd_chunk_slices(q, xs_t_ref, b_ref, c_t_ref, dt_ref, cum_ref, dskip_ref, y_ref, h_ref, a_b,
                      causal, is_diag):
    ctx = {}

    def setup():
        dt_f, cum_f = dt_ref[q, :HEADS], cum_ref[q, :HEADS]
        tot_f = cum_f[:, CHUNK - 1:CHUNK]
        dt_b, cum_b = dt_ref[q, HEADS:], cum_ref[q, HEADS:]
        cumx_b = cum_b - dt_b * a_b
        ctx["scale_in"] = dt_f * jnp.exp(tot_f - cum_f)
        ctx["chunk_decay"] = jnp.exp(tot_f)
        ctx["col_terms"] = jnp.concatenate(
            [jnp.log(dt_f) - cum_f, jnp.log(dt_b) + cumx_b,
             jnp.zeros((CHUNK - 2 * HEADS, CHUNK), jnp.float32)], axis=0).T
        ctx["row_f"], ctx["row_b"] = cum_f, -cumx_b
        ctx["decay_out_f"] = jnp.exp(cum_f)
        ctx["dt_b"] = dt_b

    def weights(g):
        if g == 0:
            setup()
        col_terms, row_f, row_b = ctx["col_terms"], ctx["row_f"], ctx["row_b"]
        bg = b_ref[_tok_rows(q), g * D_STATE:(g + 1) * D_STATE]
        cg_t = c_t_ref[q, g * D_STATE:(g + 1) * D_STATE]
        rows = slice(g * HEADS_PER_GROUP * HEADDIM, (g + 1) * HEADS_PER_GROUP * HEADDIM)
        g_t = _dot(bg, cg_t)
        ctx["g_diag", g] = jnp.sum(jnp.where(is_diag, g_t, 0.0), axis=0, keepdims=True)
        ctx["y_off", g] = _dot(_bf(h_ref[rows]), cg_t)
        for r in range(HEADS_PER_GROUP):
            h = g * HEADS_PER_GROUP + r
            col_f = jnp.broadcast_to(col_terms[:, h:h + 1], (CHUNK, CHUNK))
            col_b = jnp.broadcast_to(col_terms[:, HEADS + h:HEADS + h + 1], (CHUNK, CHUNK))
            expo = jnp.where(causal, col_f + row_f[h:h + 1], col_b + row_b[h:h + 1])
            ctx["w_t", h] = _bf(g_t * jnp.exp(expo))

    def apply(g):
        y_off, g_diag = ctx.pop(("y_off", g)), ctx.pop(("g_diag", g))
        for r in range(HEADS_PER_GROUP):
            h = g * HEADS_PER_GROUP + r
            hr = slice(h * HEADDIM, (h + 1) * HEADDIM)
            x_bf = xs_t_ref[q, hr]
            y_h = _dot(x_bf, ctx.pop(("w_t", h)))
            y_h = y_h + y_off[r * HEADDIM:(r + 1) * HEADDIM] * ctx["decay_out_f"][h:h + 1]
            skip = dskip_ref[hr] + g_diag * ctx["dt_b"][h:h + 1]
            y_ref[q, hr] = _bf(y_h + skip * x_bf.astype(jnp.float32))
        _state_update_group(h_ref, xs_t_ref, b_ref, q, g, ctx["scale_in"], ctx["chunk_decay"])

    return [(functools.partial(weights, g), functools.partial(apply, g)) for g in range(GROUPS)]


def _bwd_out_kernel(xs_t_ref, b_ref, dt_ref, cum_ref, alog_ref, h0_ref, c_t_ref, ypart_ref,
                    yp_ref, zs_ref, x_ref, gate_ref, snw_ref, wout_ref,
                    fnw_ref, o_ref, h_ref, y_ref, *, cps):
    @pl.when(pl.program_id(1) == 0)
    def _():
        h_ref[...] = h0_ref[0]

    a_b = -jnp.exp(alog_ref[HEADS:])

    def chunk(q):
        dt_b, cum_b = dt_ref[q, HEADS:], cum_ref[q, HEADS:]
        tot_b = cum_b[:, CHUNK - 1:CHUNK]
        cumx_b = cum_b - dt_b * a_b
        decay_out = jnp.exp(tot_b - cumx_b)
        y_parts = []
        for g in range(GROUPS):
            cg_t = c_t_ref[q, g * D_STATE:(g + 1) * D_STATE]
            rows = slice(g * HEADS_PER_GROUP * HEADDIM, (g + 1) * HEADS_PER_GROUP * HEADDIM)
            y_off = _dot(_bf(h_ref[rows]), cg_t)
            for r in range(HEADS_PER_GROUP):
                h = g * HEADS_PER_GROUP + r
                hr = slice(h * HEADDIM, (h + 1) * HEADDIM)
                y_parts.append(ypart_ref[q, hr].astype(jnp.float32)
                               + y_off[r * HEADDIM:(r + 1) * HEADDIM] * decay_out[h:h + 1])
        y_ref[_tok_rows(q), :] = jnp.concatenate(y_parts, axis=0).T
        _state_update(h_ref, xs_t_ref, b_ref, q, dt_b * jnp.exp(cumx_b), jnp.exp(tot_b))

    acc = jnp.zeros((cps * CHUNK, D_MODEL), jnp.float32)
    for i in range(max(cps, N_POOL_GROUPS)):
        if i < cps:
            chunk(cps - 1 - i)
        if i < N_POOL_GROUPS:
            acc = acc + _dot(yp_ref[i], wout_ref[i * POOL_GROUP_W:(i + 1) * POOL_GROUP_W])

    gw = W_SSD // GROUPS
    for g in range(GROUPS):
        cols = slice(g * gw, (g + 1) * gw)
        gated = y_ref[:, cols] * zs_ref[:, cols].astype(jnp.float32)
        ms = jnp.mean(gated * gated, axis=-1, keepdims=True)
        yn = gated * jax.lax.rsqrt(ms + EPS) * snw_ref[:, cols]
        acc = acc + _dot(_bf(yn), wout_ref[W_POOL + g * gw:W_POOL + (g + 1) * gw])
    hres = x_ref[...] + gate_ref[0, 0] * acc
    ms = jnp.mean(hres * hres, axis=-1, keepdims=True)
    o_ref[...] = hres * jax.lax.rsqrt(ms + EPS) * fnw_ref[...]


def _backward_output(xs_t, b_tok, dt, cum, c_t, y_part, alog_col, h0, y_pool, gate_ssd, x2d, gate,
                     ssd_norm_w, w_out_bf, final_norm_w, bsz, n_chunks):
    n_tok = b_tok.shape[0]
    cps = min(n_chunks, SSD_CHUNKS_PER_STEP)
    n_steps = n_chunks // cps
    tm = cps * CHUNK
    block_of = lambda b, s: b * n_steps + (n_steps - 1 - s)
    tok = lambda b, s: (block_of(b, s), 0)
    chunk3 = lambda b, s: (block_of(b, s), 0, 0)
    const2 = lambda b, s: (0, 0)
    per_seq = lambda b, s: (b, 0, 0)
    head_spec = pl.BlockSpec((cps, 2 * HEADS, CHUNK), chunk3)
    return pl.pallas_call(
        functools.partial(_bwd_out_kernel, cps=cps),
        grid=(bsz, n_steps),
        in_specs=[pl.BlockSpec((cps, W_SSD, CHUNK), chunk3),
                  pl.BlockSpec((tm, GN), tok),
                  head_spec, head_spec,
                  pl.BlockSpec((2 * HEADS, 1), const2),
                  pl.BlockSpec((1, W_SSD, D_STATE), per_seq),
                  pl.BlockSpec((cps, GN, CHUNK), chunk3),
                  pl.BlockSpec((cps, W_SSD, CHUNK), chunk3),
                  pl.BlockSpec((N_POOL_GROUPS, tm, POOL_GROUP_W),
                               lambda b, s: (0, block_of(b, s), 0)),
                  pl.BlockSpec((tm, W_SSD), tok),
                  pl.BlockSpec((tm, D_MODEL), tok),
                  pl.BlockSpec((1, 1, 1, D_MODEL), lambda b, s: (b, 2, 0, 0)),
                  pl.BlockSpec((1, W_SSD), const2),
                  pl.BlockSpec((W_POOL + W_SSD, D_MODEL), const2),
                  pl.BlockSpec((1, D_MODEL), const2)],
        out_specs=pl.BlockSpec((tm, D_MODEL), tok),
        out_shape=jax.ShapeDtypeStruct((n_tok, D_MODEL), jnp.float32),
        scratch_shapes=[pltpu.VMEM((W_SSD, D_STATE), jnp.float32),
                        pltpu.VMEM((tm, W_SSD), jnp.float32)],
        compiler_params=pltpu.CompilerParams(
            dimension_semantics=("arbitrary", "arbitrary"), vmem_limit_bytes=VMEM_LIMIT),
        name="bwd_out",
    )(xs_t, b_tok, dt, cum, alog_col, h0, c_t, y_part, y_pool, gate_ssd, x2d, gate,
      ssd_norm_w.reshape(1, W_SSD), w_out_bf, final_norm_w.reshape(1, D_MODEL))


POOL_TILE_ROWS = 4
POOL_TILE = POOL_TILE_ROWS * GRID_W


def _pool_constants(window, n_rows):
    lo_off, hi_off = -(window // 2), window - window // 2
    col = np.arange(GRID_W)
    lo = np.clip(col + lo_off, 0, GRID_W)
    hi = np.clip(col + hi_off, 0, GRID_W)
    band = ((col[None, :] >= lo[:, None]) & (col[None, :] < hi[:, None])).astype(np.float32)
    band_tile = np.kron(np.eye(POOL_TILE_ROWS, dtype=np.float32), band)
    row = np.arange(n_rows)
    cnt_r = np.clip(row + hi_off, 0, n_rows) - np.clip(row + lo_off, 0, n_rows)
    inv = 1.0 / (cnt_r[:, None] * (hi - lo)[None, :]).astype(np.float64)
    inv = np.broadcast_to(inv.reshape(-1, 1), (n_rows * GRID_W, 128)).astype(np.float32)
    return jnp.asarray(band_tile, jnp.bfloat16), jnp.asarray(inv)


def _pool_kernel(u_ref, z_ref, band_ref, inv_ref, w_ref, scale_ref, o_ref, *, n_rows):
    for g, window in enumerate(POOL_WINDOWS):
        @pl.when(pl.program_id(0) == g)
        def _(window=window):
            _pool_image(u_ref.at[0], z_ref.at[0], band_ref.at[0], inv_ref.at[0], w_ref, scale_ref,
                        o_ref.at[0], window, n_rows)


def _pool_image(u_ref, z_ref, band_ref, inv_ref, w_ref, scale_ref, o_ref, window, n_rows):
    def grid_row(r):
        return u_ref[r * GRID_W:(r + 1) * GRID_W].astype(jnp.float32)

    def bounds(r):
        return max(r - window // 2, 0), min(r + window - window // 2, n_rows)

    band = band_ref[...]
    w_lin = _bf(w_ref[0])
    rsum, tile_rows = None, []
    for r in range(n_rows):
        lo, hi = bounds(r)
        if r == 0 or window <= 2:
            rsum = grid_row(lo)
            for k in range(lo + 1, hi):
                rsum = rsum + grid_row(k)
        else:
            prev_lo, prev_hi = bounds(r - 1)
            if hi > prev_hi:
                rsum = rsum + grid_row(hi - 1)
            if lo > prev_lo:
                rsum = rsum - grid_row(prev_lo)
        tile_rows.append(rsum)
        if len(tile_rows) < POOL_TILE_ROWS:
            continue
        base = (r + 1 - POOL_TILE_ROWS) * GRID_W
        rows = slice(base, base + POOL_TILE)
        rs = jnp.concatenate(tile_rows, axis=0)
        tile_rows = []
        box = _dot(band, _bf(rs))
        inv = inv_ref[rows]
        mean = box * jnp.concatenate([inv, inv], axis=1)
        d = mean - u_ref[rows].astype(jnp.float32)
        y = _dot(_bf(d), w_lin) * scale_ref[...]
        o_ref[rows] = _bf(y * z_ref[rows].astype(jnp.float32))


def _pool_mixer(u, gate, pool_w, pool_scale, bsz, n_img_tok):
    n_rows = n_img_tok // GRID_W
    consts = [_pool_constants(window, n_rows) for window in POOL_WINDOWS]
    band = jnp.stack([c[0] for c in consts])
    inv = jnp.stack([c[1] for c in consts])
    img = pl.BlockSpec((1, n_img_tok, POOL_GROUP_W), lambda g, b: (g, b, 0))
    per_group = lambda g, b: (g, 0, 0)
    return pl.pallas_call(
        functools.partial(_pool_kernel, n_rows=n_rows),
        grid=(N_POOL_GROUPS, bsz),
        in_specs=[img, img,
                  pl.BlockSpec((1, POOL_TILE, POOL_TILE), per_group),
                  pl.BlockSpec((1, n_img_tok, 128), per_group),
                  pl.BlockSpec((1, POOL_GROUP_W, POOL_GROUP_W), per_group),
                  pl.BlockSpec((1, POOL_GROUP_W), lambda g, b: (0, g))],
        out_specs=img,
        out_shape=jax.ShapeDtypeStruct(u.shape, jnp.bfloat16),
        compiler_params=pltpu.CompilerParams(
            dimension_semantics=("arbitrary", "arbitrary"), vmem_limit_bytes=VMEM_LIMIT),
        name="pool",
    )(u, gate, band, inv, pool_w, pool_scale)


def kernel(x, c, ctx, c_ctx, norm_w, w_ada, b_ada, w_in, conv_w, conv_b, a_log, dt_bias, d_skip,
           ssd_norm_w, pool_w, pool_scale, w_out, final_norm_w):
    bsz, seq, _ = x.shape
    ctx_len = ctx.shape[1]
    depth = norm_w.shape[0]
    assert depth == 1, "single-layer block: the context stream update is never consumed"
    assert seq % PROJ_TILE == 0 and ctx_len % CHUNK == 0 and seq % POOL_TILE == 0
    assert OFF_XBC % (W_SSD + GN) == 0

    mod_rows = -(-(bsz + 1) // SUBLANES) * SUBLANES
    cond = jnp.concatenate([c, c_ctx[None], jnp.zeros((mod_rows - bsz - 1, D_MODEL), c.dtype)])
    mod = _modulation(cond, w_ada[0], b_ada[0])

    w_in_bf = _bf(w_in[0])
    w_dt_bf = jnp.pad(w_in_bf[:, OFF_DT:], ((0, 0), (0, DT_PAD - 2 * HEADS)))
    alog_col = a_log[0].reshape(2 * HEADS, 1)
    bias_col = dt_bias[0].reshape(2 * HEADS, 1)
    dskip_b = jnp.broadcast_to(jnp.repeat(d_skip[0], HEADDIM)[:, None], (W_SSD, CHUNK))
    conv_b2 = conv_b[0].reshape(1, CONV_DIM)

    ctx2d = ctx.reshape(bsz * ctx_len, D_MODEL)
    h_fwd, h_bwd = _projection(
        ctx2d, norm_w[0], mod, bsz, w_in_bf, w_dt_bf, conv_w[0], conv_b2, alog_col, bias_col,
        ctx_len, ctx_len, full=False)

    x2d = x.reshape(bsz * seq, D_MODEL)
    outs = _projection(x2d, norm_w[0], mod, None, w_in_bf, w_dt_bf, conv_w[0], conv_b2,
                       alog_col, bias_col, seq, PROJ_TILE, full=True, dskip_b=dskip_b, h0=h_fwd)
    u_pool, gate_pool, gate_ssd, xs_t, b_tok, c_t, dt, cum, y_part = outs
    nc = seq // CHUNK
    y_pool = _pool_mixer(u_pool, gate_pool, pool_w[0], pool_scale, bsz, seq)
    out = _backward_output(xs_t, b_tok, dt, cum, c_t, y_part, alog_col, h_bwd, y_pool, gate_ssd,
                           x2d, mod, ssd_norm_w[0], _bf(w_out[0]), final_norm_w, bsz, nc)
    return out.reshape(bsz, seq, D_MODEL)
```

```python
import functools

import numpy as np
import jax
import jax.numpy as jnp
from jax.experimental import pallas as pl
from jax.experimental.pallas import tpu as pltpu

D_MODEL = 1024
GRID_W = 64
W_POOL = 1024
W_SSD = 1024
POOL_WINDOWS = (2, 4, 8, 16)
N_POOL_GROUPS = len(POOL_WINDOWS)
POOL_GROUP_W = 256
HEADDIM = 64
HEADS = 16
GROUPS = 4
HEADS_PER_GROUP = 4
D_STATE = 128
D_CONV = 4
CONV_LEFT = 2
CHUNK = 128
GN = GROUPS * D_STATE
CONV_DIM = W_SSD + 2 * GN
OFF_POOL_Z = W_POOL
OFF_SSD_Z = 2 * W_POOL
OFF_XBC = 2 * W_POOL + W_SSD
OFF_DT = OFF_XBC + CONV_DIM
DT_PAD = 128
EPS = 1e-6
SUBLANES = 8
LANES = 128
IL_GROUPS = CHUNK // SUBLANES
CONV_SEG = 256
SSD_CHUNKS_PER_STEP = 4
PROJ_TILE = 512
SWEEP_LAG = 2
VMEM_LIMIT = 56 * 1024 * 1024


def _silu(v):
    h = 0.5 * v
    return h + h * jnp.tanh(h)


def _softplus(v):
    return jnp.maximum(v, 0.0) + jnp.log1p(jnp.exp(-jnp.abs(v)))


def _bf(v):
    return v.astype(jnp.bfloat16)


def _dot(a, b):
    return jnp.dot(a, b, preferred_element_type=jnp.float32)


def _mod_kernel(c_ref, w_ref, b_ref, o_ref):
    s, w = _silu(c_ref[...]), w_ref[...]
    s_hi, w_hi = _bf(s), _bf(w)
    s_lo = _bf(s - s_hi.astype(jnp.float32))
    w_lo = _bf(w - w_hi.astype(jnp.float32))
    both = _dot(jnp.concatenate([s_hi, s_lo], axis=0), w_hi)
    rows = s.shape[0]
    o_ref[:, 0, 0] = both[:rows] + both[rows:] + _dot(s_hi, w_lo) + b_ref[...]


def _modulation(cond_rows, w_ada, b_ada):
    rows = cond_rows.shape[0]
    n_kinds = w_ada.shape[1] // D_MODEL
    return pl.pallas_call(
        _mod_kernel,
        grid=(n_kinds,),
        in_specs=[pl.BlockSpec((rows, D_MODEL), lambda j: (0, 0)),
                  pl.BlockSpec((D_MODEL, D_MODEL), lambda j: (0, j)),
                  pl.BlockSpec((1, D_MODEL), lambda j: (0, j))],
        out_specs=pl.BlockSpec((rows, 1, 1, D_MODEL), lambda j: (0, j, 0, 0)),
        out_shape=jax.ShapeDtypeStruct((rows, n_kinds, 1, D_MODEL), jnp.float32),
        compiler_params=pltpu.CompilerParams(vmem_limit_bytes=VMEM_LIMIT),
        name="mod",
    )(cond_rows, w_ada, b_ada.reshape(1, n_kinds * D_MODEL))


def _lane_cumsum(v):
    lane = jax.lax.broadcasted_iota(jnp.int32, v.shape, 1)
    shift = 1
    while shift < CHUNK:
        v = v + jnp.where(lane >= shift, pltpu.roll(v, shift, 1), 0.0)
        shift *= 2
    return v


def _proj_kernel(x_ref, xp_ref, xn_ref, nw_ref, sh_ref, sc_ref, w_ref, wdt_ref, cw_ref, cb_ref,
                 alog_ref, bias_ref, *rest, tm, tiles_per_seq, n_tiles, full):
    if full:
        (dskip_ref, h0_ref, u_ref, zp_ref, zs_ref, xs_t_ref, b_ref, c_t_ref,
         dt_ref, cum_ref, ypart_ref, pe_ref, xc_ref, mn_ref, kxs_ref, kb_ref, kc_ref, kdt_ref,
         kcum_ref, h_ref) = rest
    else:
        hf_ref, hb_ref, pe_ref, xc_ref, mn_ref, xs_t_ref, b_ref, dt_ref, cum_ref, h_ref = rest
    i = pl.program_id(0)
    pos = jnp.minimum(i, n_tiles - 1) % tiles_per_seq
    has_prev = pos > 0
    has_next = pos < tiles_per_seq - 1
    n_chunks = tm // CHUNK
    seg = CONV_SEG

    if full:
        @pl.when(i == 0)
        def _():
            for ref in (kxs_ref, kb_ref, kc_ref, kdt_ref, kcum_ref, h_ref):
                ref[...] = jnp.zeros(ref.shape, ref.dtype)

    gain = nw_ref[...] * (1.0 + sc_ref[0, 0])

    def modulated(v):
        ms = jnp.mean(v * v, axis=-1, keepdims=True)
        return v * jax.lax.rsqrt(ms + EPS) * gain + sh_ref[0, 0]

    m_tok = modulated(x_ref[...])
    hm = _bf(m_tok)
    for t in range(D_MODEL // LANES):
        mn_ref[t] = m_tok[:, t * LANES:(t + 1) * LANES]

    rows = [jnp.concatenate([mn_ref[t, pl.ds(q * CHUNK + b, SUBLANES, stride=IL_GROUPS), :]
                             for t in range(D_MODEL // LANES)], axis=1)
            for q in range(n_chunks) for b in range(IL_GROUPS)]
    halo = [jnp.where(has_prev, modulated(xp_ref[...]), 0.0),
            jnp.where(has_next, modulated(xn_ref[...]), 0.0)]
    hm_il = _bf(jnp.concatenate(halo + rows, axis=0))
    sub = jax.lax.broadcasted_iota(jnp.int32, (SUBLANES, seg), 0)

    def conv_stage(j, slot):
        is_x = j < W_SSD
        is_b = W_SSD <= j < W_SSD + GN

        def matmul():
            lo = (OFF_XBC if full else 0) + j
            pe_ref[slot] = _dot(hm_il, w_ref[:, lo:lo + seg])

        def group(q, b):
            lo = 2 * SUBLANES + q * CHUNK + b * SUBLANES
            return pe_ref[slot, lo:lo + SUBLANES]

        def shifted(q, b, delta):
            bb = b + delta
            if 0 <= bb < IL_GROUPS:
                return group(q, bb)
            if bb < 0:
                bb += IL_GROUPS
                if q == 0:
                    first = pe_ref[slot, bb - SUBLANES:bb - SUBLANES + 1]
                else:
                    row = 2 * SUBLANES + (q - 1) * CHUNK + bb * SUBLANES + SUBLANES - 1
                    first = pe_ref[slot, row:row + 1]
                return jnp.where(sub == 0, first, pltpu.roll(group(q, bb), 1, 0))
            bb -= IL_GROUPS
            if q == n_chunks - 1:
                last = pe_ref[slot, SUBLANES + bb:SUBLANES + bb + 1]
            else:
                nxt = 2 * SUBLANES + (q + 1) * CHUNK + bb * SUBLANES
                last = pe_ref[slot, nxt:nxt + 1]
            return jnp.where(sub == SUBLANES - 1, last, pltpu.roll(group(q, bb), SUBLANES - 1, 0))

        def epilogue():
            taps = [cw_ref[k:k + 1, j:j + seg] for k in range(D_CONV)]
            bias = cb_ref[:, j:j + seg]
            for q in range(n_chunks):
                for b in range(IL_GROUPS):
                    acc = bias
                    for k in range(D_CONV):
                        acc = acc + shifted(q, b, k - CONV_LEFT) * taps[k]
                    lo = q * CHUNK + b * SUBLANES
                    act = _silu(acc)
                    for t in range(seg // LANES):
                        xc_ref[slot, t, lo:lo + SUBLANES] = act[:, t * LANES:(t + 1) * LANES]
            for q in range(n_chunks):
                xc = jnp.concatenate(
                    [jnp.concatenate(
                        [xc_ref[slot, t, pl.ds(q * CHUNK + (m % 2) * (CHUNK // 2) + m // 2,
                                               SUBLANES, stride=SUBLANES), :]
                         for t in range(seg // LANES)], axis=1)
                     for m in range(IL_GROUPS)], axis=0)
                if is_b:
                    b_ref[q * CHUNK:(q + 1) * CHUNK, j - W_SSD:j - W_SSD + seg] = _bf(xc)
                else:
                    dst, off = (xs_t_ref, j) if is_x else (c_t_ref, j - W_SSD - GN)
                    dst[q, off:off + seg] = _bf(xc.T)

        return matmul, epilogue

    def plain_stage(cols, finish):
        box = []
        return (lambda: box.append(_dot(hm, w_ref[:, cols]))), (lambda: finish(box.pop()))

    def dt_stage():
        box = []

        def epilogue():
            p_dt = box.pop()
            a_col = -jnp.exp(alog_ref[...])
            for q in range(n_chunks):
                dt = _softplus(p_dt[q * CHUNK:(q + 1) * CHUNK].T[:2 * HEADS] + bias_ref[...])
                dt_ref[q] = dt
                cum_ref[q] = _lane_cumsum(dt * a_col)

        return (lambda: box.append(_dot(hm, wdt_ref[...]))), epilogue

    def store_to(ref, cols=None, act=None):
        def finish(v):
            v = v if act is None else _bf(act(v))
            if cols is None:
                ref[...] = v
            else:
                ref[:, cols] = v
        return finish

    def store_pair(ref, g, act=None):
        def finish(v):
            for n in range(2):
                part = v[:, n * POOL_GROUP_W:(n + 1) * POOL_GROUP_W]
                ref[g + n] = _bf(part if act is None else act(part))
        return finish

    light, heavy = [], []
    if full:
        for g in range(0, N_POOL_GROUPS, 2):
            light.append(plain_stage(slice(g * POOL_GROUP_W, (g + 2) * POOL_GROUP_W),
                                     store_pair(u_ref, g)))
    light.append(dt_stage())
    for j in range(0, CONV_DIM, seg):
        if full or j < W_SSD + GN:
            heavy.append(conv_stage(j, len(heavy) % 2))
    if full:
        for g in range(0, N_POOL_GROUPS, 2):
            zcols = slice(OFF_POOL_Z + g * POOL_GROUP_W, OFF_POOL_Z + (g + 2) * POOL_GROUP_W)
            light.append(plain_stage(zcols, store_pair(zp_ref, g, act=_silu)))
        for j in range(0, W_SSD, seg):
            light.append(plain_stage(slice(OFF_SSD_Z + j, OFF_SSD_Z + j + seg),
                                     store_to(zs_ref, cols=slice(j, j + seg), act=_silu)))
    stages = []
    light.reverse()
    heavy.reverse()
    while light or heavy:
        if light:
            stages.append(light.pop(0))
        if heavy:
            stages.append(heavy.pop(0))

    sweep = []
    if full:
        swept = jnp.maximum(i - 1, 0)
        h_ref[...] = jnp.where(swept % tiles_per_seq == 0, h0_ref[0], h_ref[...])
        src = jax.lax.broadcasted_iota(jnp.int32, (CHUNK, CHUNK), 0)
        dst = jax.lax.broadcasted_iota(jnp.int32, (CHUNK, CHUNK), 1)
        a_b = -jnp.exp(alog_ref[HEADS:])
        pairs = []
        for q in range(n_chunks):
            pairs += _fwd_chunk_slices(q, kxs_ref, kb_ref, kc_ref, kdt_ref, kcum_ref, dskip_ref,
                                       ypart_ref, h_ref, a_b, src <= dst, src == dst)
        lag = SWEEP_LAG
        assert lag < GROUPS, "weights(p) reads the state written by apply(p - GROUPS)"
        sweep = []
        for s in range(len(pairs) + lag):
            todo = []
            if s >= lag:
                todo.append(pairs[s - lag][1])
            if s < len(pairs):
                todo.append(pairs[s][0])
            sweep.append(functools.partial(lambda fs: [f() for f in fs], todo))

    stages[0][0]()
    done = 0
    for k, (_, epilogue) in enumerate(stages):
        if k + 1 < len(stages):
            stages[k + 1][0]()
        epilogue()
        upto = -(-len(sweep) * (k + 1) // len(stages))
        for piece in sweep[done:upto]:
            piece()
        done = upto

    if full:
        for kept, ref in ((kxs_ref, xs_t_ref), (kb_ref, b_ref), (kc_ref, c_t_ref),
                          (kdt_ref, dt_ref), (kcum_ref, cum_ref)):
            kept[...] = ref[...]
    else:
        a_b = -jnp.exp(alog_ref[HEADS:])
        for reverse, out_ref in ((False, hf_ref), (True, hb_ref)):
            h_ref[...] = jnp.zeros(h_ref.shape, h_ref.dtype)
            for q in (range(n_chunks - 1, -1, -1) if reverse else range(n_chunks)):
                if reverse:
                    dt_b, cum_b = dt_ref[q, HEADS:], cum_ref[q, HEADS:]
                    scale_in = dt_b * jnp.exp(cum_b - dt_b * a_b)
                    decay = jnp.exp(cum_b[:, CHUNK - 1:CHUNK])
                else:
                    dt_f, cum_f = dt_ref[q, :HEADS], cum_ref[q, :HEADS]
                    tot_f = cum_f[:, CHUNK - 1:CHUNK]
                    scale_in, decay = dt_f * jnp.exp(tot_f - cum_f), jnp.exp(tot_f)
                _state_update(h_ref, xs_t_ref, b_ref, q, scale_in, decay)
            out_ref[0] = h_ref[...]


def _projection(x2d, norm_w, mod, mod_row, w_bf, wdt_bf, conv_w, conv_b, alog_col, bias_col,
                seq_len, tm, full, dskip_b=None, h0=None):
    n_tok = x2d.shape[0]
    tiles_per_seq = seq_len // tm
    n_tiles = n_tok // tm
    nct = n_tok // CHUNK
    per = tm // SUBLANES
    last_halo = n_tok // SUBLANES - 1
    kern = functools.partial(_proj_kernel, tm=tm, tiles_per_seq=tiles_per_seq, n_tiles=n_tiles,
                             full=full)
    tile = lambda i: jnp.minimum(i, n_tiles - 1)
    row = (lambda i: tile(i) // tiles_per_seq) if mod_row is None else (lambda i: mod_row)
    shift_spec = pl.BlockSpec((1, 1, 1, D_MODEL), lambda i: (row(i), 0, 0, 0))
    scale_spec = pl.BlockSpec((1, 1, 1, D_MODEL), lambda i: (row(i), 1, 0, 0))
    const = lambda i: (0, 0)
    tok = lambda i: (tile(i), 0)
    chunk3 = lambda i: (tile(i), 0, 0)
    q = tm // CHUNK
    xs_t = (jax.ShapeDtypeStruct((nct, W_SSD, CHUNK), jnp.bfloat16),
            pl.BlockSpec((q, W_SSD, CHUNK), chunk3))
    b_tok = (jax.ShapeDtypeStruct((n_tok, GN), jnp.bfloat16), pl.BlockSpec((tm, GN), tok))
    c_t = (jax.ShapeDtypeStruct((nct, GN, CHUNK), jnp.bfloat16),
           pl.BlockSpec((q, GN, CHUNK), chunk3))
    dt = (jax.ShapeDtypeStruct((nct, 2 * HEADS, CHUNK), jnp.float32),
          pl.BlockSpec((q, 2 * HEADS, CHUNK), chunk3))
    if full:
        pooled = (jax.ShapeDtypeStruct((N_POOL_GROUPS, n_tok, POOL_GROUP_W), jnp.bfloat16),
                  pl.BlockSpec((N_POOL_GROUPS, tm, POOL_GROUP_W), lambda i: (0, tile(i), 0)))
        zs = (jax.ShapeDtypeStruct((n_tok, W_SSD), jnp.bfloat16), pl.BlockSpec((tm, W_SSD), tok))
        swept = lambda i: jnp.maximum(i - 1, 0)
        y_part = (jax.ShapeDtypeStruct((nct, W_SSD, CHUNK), jnp.bfloat16),
                  pl.BlockSpec((q, W_SSD, CHUNK), lambda i: (swept(i), 0, 0)))
        outs = [pooled, pooled, zs, xs_t, b_tok, c_t, dt, dt, y_part]
        extra_in = [pl.BlockSpec((W_SSD, CHUNK), const),
                    pl.BlockSpec((1, W_SSD, D_STATE), lambda i: (swept(i) // tiles_per_seq, 0, 0))]
        extra_args = [dskip_b, h0]
        extra_scratch = [pltpu.VMEM((q, W_SSD, CHUNK), jnp.bfloat16),
                         pltpu.VMEM((tm, GN), jnp.bfloat16),
                         pltpu.VMEM((q, GN, CHUNK), jnp.bfloat16),
                         pltpu.VMEM((q, 2 * HEADS, CHUNK), jnp.float32),
                         pltpu.VMEM((q, 2 * HEADS, CHUNK), jnp.float32),
                         pltpu.VMEM((W_SSD, D_STATE), jnp.float32)]
    else:
        assert tiles_per_seq == 1, "prefix states are computed from one whole sequence per step"
        state = (jax.ShapeDtypeStruct((n_tiles, W_SSD, D_STATE), jnp.float32),
                 pl.BlockSpec((1, W_SSD, D_STATE), chunk3))
        outs = [state, state]
        extra_in, extra_args = [], []
        extra_scratch = [pltpu.VMEM((q, W_SSD, CHUNK), jnp.bfloat16),
                         pltpu.VMEM((tm, GN), jnp.bfloat16),
                         pltpu.VMEM((q, 2 * HEADS, CHUNK), jnp.float32),
                         pltpu.VMEM((q, 2 * HEADS, CHUNK), jnp.float32),
                         pltpu.VMEM((W_SSD, D_STATE), jnp.float32)]
    return pl.pallas_call(
        kern,
        grid=(n_tiles + 1 if full else n_tiles,),
        in_specs=[pl.BlockSpec((tm, D_MODEL), tok),
                  pl.BlockSpec((SUBLANES, D_MODEL),
                               lambda i: (jnp.maximum(tile(i) * per - 1, 0), 0)),
                  pl.BlockSpec((SUBLANES, D_MODEL),
                               lambda i: (jnp.minimum((tile(i) + 1) * per, last_halo), 0)),
                  pl.BlockSpec((1, D_MODEL), const),
                  shift_spec, scale_spec,
                  (pl.BlockSpec(w_bf.shape, const) if full else
                   pl.BlockSpec((D_MODEL, W_SSD + GN), lambda i: (0, OFF_XBC // (W_SSD + GN)))),
                  pl.BlockSpec((D_MODEL, DT_PAD), const),
                  pl.BlockSpec((D_CONV, CONV_DIM), const),
                  pl.BlockSpec((1, CONV_DIM), const),
                  pl.BlockSpec((2 * HEADS, 1), const),
                  pl.BlockSpec((2 * HEADS, 1), const)] + extra_in,
        out_specs=[o[1] for o in outs],
        out_shape=[o[0] for o in outs],
        scratch_shapes=[pltpu.VMEM((2, tm + 2 * SUBLANES, CONV_SEG), jnp.float32),
                        pltpu.VMEM((2, CONV_SEG // LANES, tm, LANES), jnp.float32),
                        pltpu.VMEM((D_MODEL // LANES, tm, LANES), jnp.float32)] + extra_scratch,
        compiler_params=pltpu.CompilerParams(
            dimension_semantics=("arbitrary",), vmem_limit_bytes=VMEM_LIMIT),
        name="proj" if full else "proj_ctx",
    )(x2d, x2d, x2d, norm_w.reshape(1, D_MODEL), mod, mod, w_bf, wdt_bf, conv_w, conv_b,
      alog_col, bias_col, *extra_args)


def _tok_rows(q):
    return pl.ds(q * CHUNK, CHUNK)


def _state_update_group(h_ref, xs_t_ref, b_ref, q, g, scale_in, chunk_decay):
    bg = b_ref[_tok_rows(q), g * D_STATE:(g + 1) * D_STATE]
    xd = []
    for r in range(HEADS_PER_GROUP):
        h = g * HEADS_PER_GROUP + r
        x_h = xs_t_ref[q, h * HEADDIM:(h + 1) * HEADDIM].astype(jnp.float32)
        xd.append(_bf(x_h * scale_in[h:h + 1]))
    s_new = _dot(jnp.concatenate(xd, axis=0), bg)
    for r in range(HEADS_PER_GROUP):
        h = g * HEADS_PER_GROUP + r
        hr = slice(h * HEADDIM, (h + 1) * HEADDIM)
        h_ref[hr] = h_ref[hr] * chunk_decay[h:h + 1] + s_new[r * HEADDIM:(r + 1) * HEADDIM]


def _state_update(h_ref, xs_t_ref, b_ref, q, scale_in, chunk_decay):
    for g in range(GROUPS):
        _state_update_group(h_ref, xs_t_ref, b_ref, q, g, scale_in, chunk_decay)


def _fwd_chunk_slices(q, xs_t_ref, b_ref, c_t_ref, dt_ref, cum_ref, dskip_ref, y_ref, h_ref, a_b,
                      causal, is_diag):
    ctx = {}

    def setup():
        dt_f, cum_f = dt_ref[q, :HEADS], cum_ref[q, :HEADS]
        tot_f = cum_f[:, CHUNK - 1:CHUNK]
        dt_b, cum_b = dt_ref[q, HEADS:], cum_ref[q, HEADS:]
        cumx_b = cum_b - dt_b * a_b
        ctx["scale_in"] = dt_f * jnp.exp(tot_f - cum_f)
        ctx["chunk_decay"] = jnp.exp(tot_f)
        ctx["col_terms"] = jnp.concatenate(
            [jnp.log(dt_f) - cum_f, jnp.log(dt_b) + cumx_b,
             jnp.zeros((CHUNK - 2 * HEADS, CHUNK), jnp.float32)], axis=0).T
        ctx["row_f"], ctx["row_b"] = cum_f, -cumx_b
        ctx["decay_out_f"] = jnp.exp(cum_f)
        ctx["dt_b"] = dt_b

    def weights(g):
        if g == 0:
            setup()
        col_terms, row_f, row_b = ctx["col_terms"], ctx["row_f"], ctx["row_b"]
        bg = b_ref[_tok_rows(q), g * D_STATE:(g + 1) * D_STATE]
        cg_t = c_t_ref[q, g * D_STATE:(g + 1) * D_STATE]
        rows = slice(g * HEADS_PER_GROUP * HEADDIM, (g + 1) * HEADS_PER_GROUP * HEADDIM)
        g_t = _dot(bg, cg_t)
        ctx["g_diag", g] = jnp.sum(jnp.where(is_diag, g_t, 0.0), axis=0, keepdims=True)
        ctx["y_off", g] = _dot(_bf(h_ref[rows]), cg_t)
        for r in range(HEADS_PER_GROUP):
            h = g * HEADS_PER_GROUP + r
            col_f = jnp.broadcast_to(col_terms[:, h:h + 1], (CHUNK, CHUNK))
            col_b = jnp.broadcast_to(col_terms[:, HEADS + h:HEADS + h + 1], (CHUNK, CHUNK))
            expo = jnp.where(causal, col_f + row_f[h:h + 1], col_b + row_b[h:h + 1])
            ctx["w_t", h] = _bf(g_t * jnp.exp(expo))

    def apply(g):
        y_off, g_diag = ctx.pop(("y_off", g)), ctx.pop(("g_diag", g))
        for r in range(HEADS_PER_GROUP):
            h = g * HEADS_PER_GROUP + r
            hr = slice(h * HEADDIM, (h + 1) * HEADDIM)
            x_bf = xs_t_ref[q, hr]
            y_h = _dot(x_bf, ctx.pop(("w_t", h)))
            y_h = y_h + y_off[r * HEADDIM:(r + 1) * HEADDIM] * ctx["decay_out_f"][h:h + 1]
            skip = dskip_ref[hr] + g_diag * ctx["dt_b"][h:h + 1]
            y_ref[q, hr] = _bf(y_h + skip * x_bf.astype(jnp.float32))
        _state_update_group(h_ref, xs_t_ref, b_ref, q, g, ctx["scale_in"], ctx["chunk_decay"])

    return [(functools.partial(weights, g), functools.partial(apply, g)) for g in range(GROUPS)]


def _bwd_out_kernel(xs_t_ref, b_ref, dt_ref, cum_ref, alog_ref, h0_ref, c_t_ref, ypart_ref,
                    yp_ref, zp_ref, zs_ref, x_ref, gate_ref, snw_ref, wout_ref,
                    fnw_ref, o_ref, h_ref, y_ref, *, cps):
    @pl.when(pl.program_id(1) == 0)
    def _():
        h_ref[...] = h0_ref[0]

    a_b = -jnp.exp(alog_ref[HEADS:])

    def chunk(q):
        dt_b, cum_b = dt_ref[q, HEADS:], cum_ref[q, HEADS:]
        tot_b = cum_b[:, CHUNK - 1:CHUNK]
        cumx_b = cum_b - dt_b * a_b
        decay_out = jnp.exp(tot_b - cumx_b)
        y_parts = []
        for g in range(GROUPS):
            cg_t = c_t_ref[q, g * D_STATE:(g + 1) * D_STATE]
            rows = slice(g * HEADS_PER_GROUP * HEADDIM, (g + 1) * HEADS_PER_GROUP * HEADDIM)
            y_off = _dot(_bf(h_ref[rows]), cg_t)
            for r in range(HEADS_PER_GROUP):
                h = g * HEADS_PER_GROUP + r
                hr = slice(h * HEADDIM, (h + 1) * HEADDIM)
                y_parts.append(ypart_ref[q, hr].astype(jnp.float32)
                               + y_off[r * HEADDIM:(r + 1) * HEADDIM] * decay_out[h:h + 1])
        y_ref[_tok_rows(q), :] = jnp.concatenate(y_parts, axis=0).T
        _state_update(h_ref, xs_t_ref, b_ref, q, dt_b * jnp.exp(cumx_b), jnp.exp(tot_b))

    acc = jnp.zeros((cps * CHUNK, D_MODEL), jnp.float32)
    for i in range(max(cps, N_POOL_GROUPS)):
        if i < cps:
            chunk(cps - 1 - i)
        if i < N_POOL_GROUPS:
            gated = yp_ref[i] * zp_ref[i]
            acc = acc + _dot(gated, wout_ref[i * POOL_GROUP_W:(i + 1) * POOL_GROUP_W])

    gw = W_SSD // GROUPS
    for g in range(GROUPS):
        cols = slice(g * gw, (g + 1) * gw)
        gated = y_ref[:, cols] * zs_ref[:, cols].astype(jnp.float32)
        ms = jnp.mean(gated * gated, axis=-1, keepdims=True)
        yn = gated * jax.lax.rsqrt(ms + EPS) * snw_ref[:, cols]
        acc = acc + _dot(_bf(yn), wout_ref[W_POOL + g * gw:W_POOL + (g + 1) * gw])
    hres = x_ref[...] + gate_ref[0, 0] * acc
    ms = jnp.mean(hres * hres, axis=-1, keepdims=True)
    o_ref[...] = hres * jax.lax.rsqrt(ms + EPS) * fnw_ref[...]


def _backward_output(xs_t, b_tok, dt, cum, c_t, y_part, alog_col, h0, y_pool, gate_pool, gate_ssd,
                     x2d, gate, ssd_norm_w, w_out_bf, final_norm_w, bsz, n_chunks):
    n_tok = b_tok.shape[0]
    cps = min(n_chunks, SSD_CHUNKS_PER_STEP)
    n_steps = n_chunks // cps
    tm = cps * CHUNK
    block_of = lambda b, s: b * n_steps + (n_steps - 1 - s)
    tok = lambda b, s: (block_of(b, s), 0)
    chunk3 = lambda b, s: (block_of(b, s), 0, 0)
    const2 = lambda b, s: (0, 0)
    per_seq = lambda b, s: (b, 0, 0)
    head_spec = pl.BlockSpec((cps, 2 * HEADS, CHUNK), chunk3)
    return pl.pallas_call(
        functools.partial(_bwd_out_kernel, cps=cps),
        grid=(bsz, n_steps),
        in_specs=[pl.BlockSpec((cps, W_SSD, CHUNK), chunk3),
                  pl.BlockSpec((tm, GN), tok),
                  head_spec, head_spec,
                  pl.BlockSpec((2 * HEADS, 1), const2),
                  pl.BlockSpec((1, W_SSD, D_STATE), per_seq),
                  pl.BlockSpec((cps, GN, CHUNK), chunk3),
                  pl.BlockSpec((cps, W_SSD, CHUNK), chunk3),
                  pl.BlockSpec((N_POOL_GROUPS, tm, POOL_GROUP_W),
                               lambda b, s: (0, block_of(b, s), 0)),
                  pl.BlockSpec((N_POOL_GROUPS, tm, POOL_GROUP_W),
                               lambda b, s: (0, block_of(b, s), 0)),
                  pl.BlockSpec((tm, W_SSD), tok),
                  pl.BlockSpec((tm, D_MODEL), tok),
                  pl.BlockSpec((1, 1, 1, D_MODEL), lambda b, s: (b, 2, 0, 0)),
                  pl.BlockSpec((1, W_SSD), const2),
                  pl.BlockSpec((W_POOL + W_SSD, D_MODEL), const2),
                  pl.BlockSpec((1, D_MODEL), const2)],
        out_specs=pl.BlockSpec((tm, D_MODEL), tok),
        out_shape=jax.ShapeDtypeStruct((n_tok, D_MODEL), jnp.float32),
        scratch_shapes=[pltpu.VMEM((W_SSD, D_STATE), jnp.float32),
                        pltpu.VMEM((tm, W_SSD), jnp.float32)],
        compiler_params=pltpu.CompilerParams(
            dimension_semantics=("arbitrary", "arbitrary"), vmem_limit_bytes=VMEM_LIMIT),
        name="bwd_out",
    )(xs_t, b_tok, dt, cum, alog_col, h0, c_t, y_part, y_pool, gate_pool, gate_ssd, x2d, gate,
      ssd_norm_w.reshape(1, W_SSD), w_out_bf, final_norm_w.reshape(1, D_MODEL))


POOL_TILE_ROWS = 4
POOL_TILE = POOL_TILE_ROWS * GRID_W


def _pool_constants(window, n_rows):
    lo_off, hi_off = -(window // 2), window - window // 2
    col = np.arange(GRID_W)
    lo = np.clip(col + lo_off, 0, GRID_W)
    hi = np.clip(col + hi_off, 0, GRID_W)
    band = ((col[None, :] >= lo[:, None]) & (col[None, :] < hi[:, None])).astype(np.float32)
    band_tile = np.kron(np.eye(POOL_TILE_ROWS, dtype=np.float32), band)
    row = np.arange(n_rows)
    cnt_r = np.clip(row + hi_off, 0, n_rows) - np.clip(row + lo_off, 0, n_rows)
    inv = 1.0 / (cnt_r[:, None] * (hi - lo)[None, :]).astype(np.float64)
    inv = np.broadcast_to(inv.reshape(-1, 1), (n_rows * GRID_W, 128)).astype(np.float32)
    return jnp.asarray(band_tile, jnp.bfloat16), jnp.asarray(inv)


def _pool_kernel(u_ref, band_ref, inv_ref, w_ref, scale_ref, o_ref, *, n_rows):
    for g, window in enumerate(POOL_WINDOWS):
        @pl.when(pl.program_id(0) == g)
        def _(window=window):
            _pool_image(u_ref.at[0], band_ref.at[0], inv_ref.at[0], w_ref, scale_ref, o_ref.at[0],
                        window, n_rows)


def _pool_image(u_ref, band_ref, inv_ref, w_ref, scale_ref, o_ref, window, n_rows):
    def grid_row(r):
        return u_ref[r * GRID_W:(r + 1) * GRID_W].astype(jnp.float32)

    def bounds(r):
        return max(r - window // 2, 0), min(r + window - window // 2, n_rows)

    band = band_ref[...]
    rsum, tile_rows = None, []
    for r in range(n_rows):
        lo, hi = bounds(r)
        if r == 0 or window <= 2:
            rsum = grid_row(lo)
            for k in range(lo + 1, hi):
                rsum = rsum + grid_row(k)
        else:
            prev_lo, prev_hi = bounds(r - 1)
            if hi > prev_hi:
                rsum = rsum + grid_row(hi - 1)
            if lo > prev_lo:
                rsum = rsum - grid_row(prev_lo)
        tile_rows.append(rsum)
        if len(tile_rows) < POOL_TILE_ROWS:
            continue
        base = (r + 1 - POOL_TILE_ROWS) * GRID_W
        rows = slice(base, base + POOL_TILE)
        rs = jnp.concatenate(tile_rows, axis=0)
        tile_rows = []
        box = _dot(band, _bf(rs))
        inv = inv_ref[rows]
        mean = box * jnp.concatenate([inv, inv], axis=1)
        d = mean - u_ref[rows].astype(jnp.float32)
        y = _dot(_bf(d), w_ref[0]) * scale_ref[...]
        o_ref[rows] = _bf(y)


def _pool_mixer(u, pool_w_bf, pool_scale, bsz, n_img_tok):
    n_rows = n_img_tok // GRID_W
    consts = [_pool_constants(window, n_rows) for window in POOL_WINDOWS]
    band = jnp.stack([c[0] for c in consts])
    inv = jnp.stack([c[1] for c in consts])
    img = pl.BlockSpec((1, n_img_tok, POOL_GROUP_W), lambda g, b: (g, b, 0))
    per_group = lambda g, b: (g, 0, 0)
    return pl.pallas_call(
        functools.partial(_pool_kernel, n_rows=n_rows),
        grid=(N_POOL_GROUPS, bsz),
        in_specs=[img,
                  pl.BlockSpec((1, POOL_TILE, POOL_TILE), per_group),
                  pl.BlockSpec((1, n_img_tok, 128), per_group),
                  pl.BlockSpec((1, POOL_GROUP_W, POOL_GROUP_W), per_group),
                  pl.BlockSpec((1, POOL_GROUP_W), lambda g, b: (0, g))],
        out_specs=img,
        out_shape=jax.ShapeDtypeStruct(u.shape, jnp.bfloat16),
        compiler_params=pltpu.CompilerParams(
            dimension_semantics=("arbitrary", "arbitrary"), vmem_limit_bytes=VMEM_LIMIT),
        name="pool",
    )(u, band, inv, pool_w_bf, pool_scale)


def kernel(x, c, ctx, c_ctx, norm_w, w_ada, b_ada, w_in, conv_w, conv_b, a_log, dt_bias, d_skip,
           ssd_norm_w, pool_w, pool_scale, w_out, final_norm_w):
    bsz, seq, _ = x.shape
    ctx_len = ctx.shape[1]
    depth = norm_w.shape[0]
    assert depth == 1, "single-layer block: the context stream update is never consumed"
    assert seq % PROJ_TILE == 0 and ctx_len % CHUNK == 0 and seq % POOL_TILE == 0
    assert OFF_XBC % (W_SSD + GN) == 0

    mod_rows = -(-(bsz + 1) // SUBLANES) * SUBLANES
    cond = jnp.concatenate([c, c_ctx[None], jnp.zeros((mod_rows - bsz - 1, D_MODEL), c.dtype)])
    mod = _modulation(cond, w_ada[0], b_ada[0])

    w_in_bf = _bf(w_in[0])
    w_dt_bf = jnp.pad(w_in_bf[:, OFF_DT:], ((0, 0), (0, DT_PAD - 2 * HEADS)))
    alog_col = a_log[0].reshape(2 * HEADS, 1)
    bias_col = dt_bias[0].reshape(2 * HEADS, 1)
    dskip_b = jnp.broadcast_to(jnp.repeat(d_skip[0], HEADDIM)[:, None], (W_SSD, CHUNK))
    conv_b2 = conv_b[0].reshape(1, CONV_DIM)

    ctx2d = ctx.reshape(bsz * ctx_len, D_MODEL)
    h_fwd, h_bwd = _projection(
        ctx2d, norm_w[0], mod, bsz, w_in_bf, w_dt_bf, conv_w[0], conv_b2, alog_col, bias_col,
        ctx_len, ctx_len, full=False)

    x2d = x.reshape(bsz * seq, D_MODEL)
    outs = _projection(x2d, norm_w[0], mod, None, w_in_bf, w_dt_bf, conv_w[0], conv_b2,
                       alog_col, bias_col, seq, PROJ_TILE, full=True, dskip_b=dskip_b, h0=h_fwd)
    u_pool, gate_pool, gate_ssd, xs_t, b_tok, c_t, dt, cum, y_part = outs
    nc = seq // CHUNK
    y_pool = _pool_mixer(u_pool, _bf(pool_w[0]), pool_scale, bsz, seq)
    out = _backward_output(xs_t, b_tok, dt, cum, c_t, y_part, alog_col, h_bwd, y_pool, gate_pool,
                           gate_ssd, x2d, mod, ssd_norm_w[0], _bf(w_out[0]), final_norm_w, bsz, nc)
    return out.reshape(bsz, seq, D_MODEL)
```
